```python
import math
import jax
import jax.numpy as jnp
from jax import lax
import numpy as np

D_MODEL = 2048
BATCH = 8
SEQ = 4096
DEPTH = 2
DEC_BATCH = 2
DEC_SEQ = 4096
PAST_LEN = 128

RMS_EPS = 1e-6
GRID_W = 64
N_BRANCH = 4
BRANCH_W = 512

NA_HEADS = 8
NA_HEAD_DIM = 64
NA_WIN_R = 8
NA_WIN_C = 16
NA_QBLK = 16
NA_KBLK = 32

HY_WIDTH = 512
HY_SHORT = 3
HY_POS_BANDS = 16
HY_POS_DIM = 1 + 2 * HY_POS_BANDS
HY_FILT_HIDDEN = 64
HY_FAST_DECAY = 0.3
HY_SLOW_DECAY = 1.5
HY_DECAY_TARGET = 1e-2

MLA_HEADS = 4
MLA_Q_RANK = 512
MLA_KV_RANK = 256
MLA_NOPE = 128
MLA_ROPE = 64
MLA_V = 128
ROPE_THETA = 10000.0
ATT_QBLK = 128

DN_HEADS = 4
DN_DK = 128
DN_DV = 128
DN_CONV = 3
DN_CHUNK = 64

D_FF = 4 * D_MODEL

NA_QKV_COLS = 3 * NA_HEADS * NA_HEAD_DIM
HY_IN_COLS = 3 * HY_WIDTH
DN_QKV_COLS = DN_HEADS * (2 * DN_DK + DN_DV)
DN_GATE_COLS = DN_HEADS * DN_DV
DN_AB_COLS = 4 * DN_HEADS
GATE_COLS = N_BRANCH * D_MODEL
SPLIT_SIZES = (NA_QKV_COLS, HY_IN_COLS, MLA_Q_RANK, MLA_KV_RANK, MLA_ROPE, DN_QKV_COLS, DN_GATE_COLS, DN_AB_COLS)
IN_COLS = NA_QKV_COLS + HY_IN_COLS + MLA_Q_RANK + MLA_KV_RANK + MLA_ROPE + DN_QKV_COLS + DN_GATE_COLS + DN_AB_COLS + GATE_COLS

kernel_name = 'hybrid_bidir_encoder_gated_merge'


def rms_norm(x, g):
    xf = x.astype(jnp.float32)
    y = xf * lax.rsqrt(jnp.mean(xf * xf, axis=-1, keepdims=True) + RMS_EPS)
    return (y * g.astype(jnp.float32)).astype(x.dtype)


def l2norm(x):
    return x * lax.rsqrt(jnp.sum(x * x, axis=-1, keepdims=True) + RMS_EPS)


def centred_dwconv(x, w):
    width = w.shape[0]
    pad = width // 2
    L = x.shape[1]
    xp = jnp.pad(x, ((0, 0), (pad, pad), (0, 0)))
    return sum(xp[:, i:i + L] * w[i] for i in range(width))


def rope(x):
    L = x.shape[1]
    half = x.shape[-1] // 2
    inv = ROPE_THETA ** (-jnp.arange(half, dtype=jnp.float32) / half)
    ang = jnp.arange(L, dtype=jnp.float32)[:, None] * inv[None, :]
    shape = (1, L) + (1,) * (x.ndim - 3) + (half,)
    cos = jnp.cos(ang).reshape(shape)
    sin = jnp.sin(ang).reshape(shape)
    xf = x.astype(jnp.float32)
    x1, x2 = xf[..., :half], xf[..., half:]
    return jnp.concatenate([x1 * cos - x2 * sin, x2 * cos + x1 * sin], axis=-1).astype(x.dtype)


def neighbourhood_attention(q, k, v, rpb):
    B, L, H, dh = q.shape
    rows = L // GRID_W
    wr = min(NA_WIN_R, rows)
    ncb = GRID_W // NA_QBLK
    qcol = np.arange(GRID_W).reshape(ncb, NA_QBLK)
    cstart = np.clip(qcol - NA_WIN_C // 2, 0, GRID_W - NA_WIN_C)
    kstart = np.clip(np.arange(ncb) * NA_QBLK - NA_WIN_C // 2, 0, GRID_W - NA_KBLK)
    kcol = kstart[:, None] + np.arange(NA_KBLK)[None, :]
    kc = kcol[:, None, :]
    cs = cstart[:, :, None]
    col_ok = jnp.asarray((kc >= cs) & (kc < cs + NA_WIN_C))
    dcol = np.clip(kc - qcol[:, :, None] + NA_WIN_C - 1, 0, 2 * NA_WIN_C - 2)
    rpb_c = rpb.astype(jnp.float32)[:, :, dcol]
    qg = q.reshape(B, rows, ncb, NA_QBLK, H, dh)
    kg = k.reshape(B, rows, GRID_W, H, dh)
    vg = v.reshape(B, rows, GRID_W, H, dh)
    scale = dh ** -0.5

    def row_block(r):
        rs = jnp.clip(r - wr // 2, 0, rows - wr)
        k_blk = lax.dynamic_slice_in_dim(kg, rs, wr, axis=1)[:, :, kcol]
        v_blk = lax.dynamic_slice_in_dim(vg, rs, wr, axis=1)[:, :, kcol]
        q_row = lax.dynamic_index_in_dim(qg, r, axis=1, keepdims=False)
        s = jnp.einsum('bjqhd,bijkhd->bhjqik', q_row, k_blk, preferred_element_type=jnp.float32) * scale
        drow = rs + jnp.arange(wr) - r + NA_WIN_R - 1
        s = s + jnp.transpose(rpb_c[:, drow], (0, 2, 3, 1, 4))[None]
        s = jnp.where(col_ok[:, :, None, :], s, -jnp.inf)
        p = jax.nn.softmax(s.reshape(B, H, ncb, NA_QBLK, wr * NA_KBLK), axis=-1).reshape(s.shape)
        o = jnp.einsum('bhjqik,bijkhd->bjqhd', p.astype(v.dtype), v_blk)
        return o.reshape(B, GRID_W, H, dh)

    out = lax.map(row_block, jnp.arange(rows))
    return jnp.moveaxis(out, 0, 1).reshape(B, L, H * dh)


def hyena_filters(L, w1, b1, w2, b2, w3):
    f32 = jnp.float32
    t = jnp.linspace(0.0, 1.0, L, dtype=f32)[:, None]
    w = 2.0 * math.pi * jnp.arange(L, dtype=f32)[:, None] / L
    bands = jnp.linspace(1e-4, HY_POS_BANDS - 1, HY_POS_BANDS, dtype=f32)[None, :]
    z = jnp.concatenate([t, jnp.cos(bands * w), -jnp.sin(bands * w)], axis=-1)
    hid = jnp.sin(z @ w1.astype(f32) + b1.astype(f32))
    hid = jnp.sin(hid @ w2.astype(f32) + b2.astype(f32))
    h = (hid @ w3.astype(f32)).reshape(L, 2, HY_WIDTH)
    max_decay = math.log(HY_DECAY_TARGET) / HY_FAST_DECAY
    min_decay = math.log(HY_DECAY_TARGET) / HY_SLOW_DECAY
    deltas = jnp.abs(jnp.linspace(min_decay, max_decay, HY_WIDTH, dtype=f32))
    h = h * jnp.exp(-t * deltas[None, :])[:, None, :]
    h = h / (jnp.sum(jnp.abs(h), axis=(0, 1), keepdims=True) + RMS_EPS)
    return h[:, 0], h[:, 1]


def bidir_fftconv(v, h_f, h_b):
    L = v.shape[1]
    n = 2 * L
    k_two = jnp.concatenate([h_f[:1] + h_b[:1], h_f[1:], jnp.zeros_like(h_f[:1]), h_b[:0:-1]], axis=0)
    kf = jnp.fft.rfft(k_two, n=n, axis=0)
    vf = jnp.fft.rfft(v.astype(jnp.float32), n=n, axis=1)
    y = jnp.fft.irfft(vf * kf[None], n=n, axis=1)[:, :L]
    return y.astype(v.dtype)


def hyena_mixer(u, w_short, skip, w1, b1, w2, b2, w3):
    L = u.shape[1]
    uc = centred_dwconv(u, w_short)
    x1, x2, v = jnp.split(uc, 3, axis=-1)
    h_f, h_b = hyena_filters(L, w1, b1, w2, b2, w3)
    v = v * x1
    y = bidir_fftconv(v, h_f, h_b) + v * skip
    return y * x2


def mla_mixer(c_q, c_kv, k_r, g_q, g_kv, w_uq, w_ukv):
    B, L, _ = c_q.shape
    q = (rms_norm(c_q, g_q) @ w_uq).reshape(B, L, MLA_HEADS, MLA_NOPE + MLA_ROPE)
    kv = (rms_norm(c_kv, g_kv) @ w_ukv).reshape(B, L, MLA_HEADS, MLA_NOPE + MLA_V)
    q_nope, q_rope = q[..., :MLA_NOPE], rope(q[..., MLA_NOPE:])
    k_nope, v = kv[..., :MLA_NOPE], kv[..., MLA_NOPE:]
    k_rope = rope(k_r)
    scale = (MLA_NOPE + MLA_ROPE) ** -0.5
    nblk = L // ATT_QBLK

    def blocks(t):
        return jnp.moveaxis(t.reshape((B, nblk, ATT_QBLK) + t.shape[2:]), 1, 0)

    def attend(qs):
        qn, qr = qs
        s = jnp.einsum('bqhd,bkhd->bhqk', qn, k_nope, preferred_element_type=jnp.float32)
        s = s + jnp.einsum('bqhd,bkd->bhqk', qr, k_rope, preferred_element_type=jnp.float32)
        p = jax.nn.softmax(s * scale, axis=-1)
        return jnp.einsum('bhqk,bkhd->bqhd', p.astype(v.dtype), v)

    o = lax.map(attend, (blocks(q_nope), blocks(q_rope)))
    return jnp.moveaxis(o, 0, 1).reshape(B, L, MLA_HEADS * MLA_V)


def gated_delta_chunked(q, k, v, g, beta):
    B, L, H, dk = q.shape
    dv = v.shape[-1]
    C = DN_CHUNK
    N = L // C

    def to_chunks(t):
        return jnp.moveaxis(t.reshape((B, N, C, H) + t.shape[3:]), 3, 1)

    q = to_chunks(q) * dk ** -0.5
    k = to_chunks(k)
    v = to_chunks(v)
    g = to_chunks(g)
    beta = to_chunks(beta)
    gc = jnp.cumsum(g, axis=-1)
    tri = jnp.tril(jnp.ones((C, C), dtype=bool))
    strict = jnp.tril(jnp.ones((C, C), dtype=bool), -1)
    decay = jnp.exp(jnp.where(tri, gc[..., :, None] - gc[..., None, :], -jnp.inf))
    kb = k * beta[..., None]
    a_kk = jnp.where(strict, jnp.einsum('bhncd,bhnsd->bhncs', kb, k) * decay, 0.0)
    eye = jnp.eye(C, dtype=jnp.float32)
    t_inv = lax.linalg.triangular_solve(eye + a_kk, jnp.broadcast_to(eye, a_kk.shape),
                                        left_side=True, lower=True, unit_diagonal=True)
    u = t_inv @ (v * beta[..., None])
    w = t_inv @ (kb * jnp.exp(gc)[..., None])
    a_qk = jnp.where(tri, jnp.einsum('bhncd,bhnsd->bhncs', q, k) * decay, 0.0)
    q_dec = q * jnp.exp(gc)[..., None]
    k_dec = k * jnp.exp(gc[..., -1:] - gc)[..., None]
    g_last = jnp.exp(gc[..., -1])

    def step(S, xs):
        u_i, w_i, qd_i, a_i, kd_i, gl_i = xs
        v_new = u_i - w_i @ S
        o_i = qd_i @ S + a_i @ v_new
        S = S * gl_i[..., None, None] + jnp.swapaxes(kd_i, -1, -2) @ v_new
        return S, o_i

    xs = tuple(jnp.moveaxis(t, 2, 0) for t in (u, w, q_dec, a_qk, k_dec, g_last))
    s0 = jnp.zeros((B, H, dk, dv), jnp.float32)
    _, o = lax.scan(step, s0, xs)
    return jnp.transpose(o, (1, 0, 3, 2, 4)).reshape(B, L, H, dv)


def deltanet_mixer(qkv, gate, ab, w_conv, a_log, dt_bias, g_norm):
    B, L, _ = qkv.shape
    f32 = jnp.float32
    qkv = jax.nn.silu(centred_dwconv(qkv, w_conv)).astype(f32)
    q, k, v = jnp.split(qkv, [DN_HEADS * DN_DK, 2 * DN_HEADS * DN_DK], axis=-1)
    q = l2norm(q.reshape(B, L, DN_HEADS, DN_DK))
    k = l2norm(k.reshape(B, L, DN_HEADS, DN_DK))
    v = v.reshape(B, L, DN_HEADS, DN_DV)
    ab = ab.astype(f32).reshape(B, L, 2, 2, DN_HEADS)
    g = -jnp.exp(a_log.astype(f32)) * jax.nn.softplus(ab[:, :, :, 0] + dt_bias.astype(f32))
    beta = jax.nn.sigmoid(ab[:, :, :, 1])
    o_fwd = gated_delta_chunked(q, k, v, g[:, :, 0], beta[:, :, 0])
    flip = lambda t: jnp.flip(t, axis=1)
    o_bwd = flip(gated_delta_chunked(flip(q), flip(k), flip(v), flip(g[:, :, 1]), flip(beta[:, :, 1])))
    o = rms_norm(o_fwd + o_bwd, g_norm) * jax.nn.silu(gate.astype(f32).reshape(B, L, DN_HEADS, DN_DV))
    return o.reshape(B, L, DN_HEADS * DN_DV).astype(gate.dtype)


def mixer_block(h, w_in, na_rpb, hy_short, hy_skip, hy_w1, hy_b1, hy_w2, hy_b2, hy_w3,
                mla_g_q, mla_g_kv, mla_w_uq, mla_w_ukv, dn_conv, dn_a_log, dn_dt_bias, dn_g_norm,
                w_branch, w_out):
    B, L, _ = h.shape
    z = h @ w_in
    cuts = [int(c) for c in np.cumsum(SPLIT_SIZES)]
    na_qkv, hy_u, c_q, c_kv, k_r, dn_qkv, dn_gate, dn_ab, gate_logits = jnp.split(z, cuts, axis=-1)
    na_qkv = na_qkv.reshape(B, L, 3, NA_HEADS, NA_HEAD_DIM)
    br_a = neighbourhood_attention(na_qkv[:, :, 0], na_qkv[:, :, 1], na_qkv[:, :, 2], na_rpb)
    br_b = hyena_mixer(hy_u, hy_short, hy_skip, hy_w1, hy_b1, hy_w2, hy_b2, hy_w3)
    br_c = mla_mixer(c_q, c_kv, k_r, mla_g_q, mla_g_kv, mla_w_uq, mla_w_ukv)
    br_d = deltanet_mixer(dn_qkv, dn_gate, dn_ab, dn_conv, dn_a_log, dn_dt_bias, dn_g_norm)
    gates = jax.nn.sigmoid(gate_logits.reshape(B, L, N_BRANCH, D_MODEL))
    branches = (br_a, br_b, br_c, br_d)
    merged = sum(gates[:, :, n] * (branches[n] @ w_branch[n]) for n in range(N_BRANCH))
    return merged @ w_out


def setup_inputs(seed: int = 0) -> dict:
    key = jax.random.key(seed)
    ks = iter(jax.random.split(key, 40))
    f32 = jnp.float32

    def nrm(shape, scale):
        return jax.random.normal(next(ks), shape, f32) * scale

    def gain(shape):
        return 1.0 + nrm(shape, 0.01)

    dt = jnp.exp(jax.random.uniform(next(ks), (DEPTH, 2, DN_HEADS), f32)
                 * (math.log(0.1) - math.log(1e-3)) + math.log(1e-3))
    return {
        'x_prompt': nrm((BATCH, SEQ, D_MODEL), 1.0),
        'x_sample': nrm((DEC_BATCH, DEC_SEQ, D_MODEL), 1.0),
        'norm_mix': gain((DEPTH, D_MODEL)),
        'w_in': nrm((DEPTH, D_MODEL, IN_COLS), D_MODEL ** -0.5),
        'na_rpb': nrm((DEPTH, NA_HEADS, 2 * NA_WIN_R - 1, 2 * NA_WIN_C - 1), 0.1),
        'hy_short': nrm((DEPTH, HY_SHORT, HY_IN_COLS), HY_SHORT ** -0.5),
        'hy_skip': nrm((DEPTH, HY_WIDTH), 1.0),
        'hy_w1': nrm((DEPTH, HY_POS_DIM, HY_FILT_HIDDEN), 1.0),
        'hy_b1': nrm((DEPTH, HY_FILT_HIDDEN), 0.1),
        'hy_w2': nrm((DEPTH, HY_FILT_HIDDEN, HY_FILT_HIDDEN), HY_FILT_HIDDEN ** -0.5),
        'hy_b2': nrm((DEPTH, HY_FILT_HIDDEN), 0.1),
        'hy_w3': nrm((DEPTH, HY_FILT_HIDDEN, 2 * HY_WIDTH), HY_FILT_HIDDEN ** -0.5),
        'mla_g_q': gain((DEPTH, MLA_Q_RANK)),
        'mla_g_kv': gain((DEPTH, MLA_KV_RANK)),
        'mla_w_uq': nrm((DEPTH, MLA_Q_RANK, MLA_HEADS * (MLA_NOPE + MLA_ROPE)), MLA_Q_RANK ** -0.5),
        'mla_w_ukv': nrm((DEPTH, MLA_KV_RANK, MLA_HEADS * (MLA_NOPE + MLA_V)), MLA_KV_RANK ** -0.5),
        'dn_conv': nrm((DEPTH, DN_CONV, DN_QKV_COLS), DN_CONV ** -0.5),
        'dn_a_log': jnp.log(jax.random.uniform(next(ks), (DEPTH, 2, DN_HEADS), f32, minval=1.0, maxval=16.0)),
        'dn_dt_bias': dt + jnp.log(-jnp.expm1(-dt)),
        'dn_g_norm': gain((DEPTH, DN_DV)),
        'w_branch': nrm((DEPTH, N_BRANCH, BRANCH_W, D_MODEL), BRANCH_W ** -0.5),
        'w_out': nrm((DEPTH, D_MODEL, D_MODEL), D_MODEL ** -0.5),
        'norm_mlp': gain((DEPTH, D_MODEL)),
        'w_up': nrm((DEPTH, D_MODEL, D_FF), D_MODEL ** -0.5),
        'w_down': nrm((DEPTH, D_FF, D_MODEL), D_FF ** -0.5),
        'norm_final': gain((D_MODEL,)),
    }


def reference(x_prompt, x_sample, norm_mix, w_in, na_rpb, hy_short, hy_skip, hy_w1, hy_b1, hy_w2, hy_b2, hy_w3,
              mla_g_q, mla_g_kv, mla_w_uq, mla_w_ukv, dn_conv, dn_a_log, dn_dt_bias, dn_g_norm,
              w_branch, w_out, norm_mlp, w_up, w_down, norm_final):
    def trunk(x):
        for l in range(DEPTH):
            h = rms_norm(x, norm_mix[l])
            x = x + mixer_block(h, w_in[l], na_rpb[l], hy_short[l], hy_skip[l], hy_w1[l], hy_b1[l], hy_w2[l],
                                hy_b2[l], hy_w3[l], mla_g_q[l], mla_g_kv[l], mla_w_uq[l], mla_w_ukv[l],
                                dn_conv[l], dn_a_log[l], dn_dt_bias[l], dn_g_norm[l], w_branch[l], w_out[l])
            h = rms_norm(x, norm_mlp[l])
            x = x + jnp.square(jax.nn.relu(h @ w_up[l])) @ w_down[l]
        return rms_norm(x, norm_final)

    y_prompt = trunk(x_prompt)
    y_sample = trunk(x_sample)
    return (y_prompt, y_sample)
```

```python
import functools
import math

import jax
import jax.numpy as jnp
import numpy as np
from jax import lax
from jax.experimental import pallas as pl
from jax.experimental.pallas import tpu as pltpu

F32 = jnp.float32
BF16 = jnp.bfloat16
HIGHEST = lax.Precision.HIGHEST

VMEM_LIMIT_BYTES = 56 * 1024 * 1024
LANES = 128

D_MODEL = 2048
RMS_EPS = 1e-6
GRID_W = 64
N_BRANCH = 4
BRANCH_W = 512
NA_HEADS = 8
NA_HEAD_DIM = 64
NA_WIN_R = 8
NA_WIN_C = 16
HY_WIDTH = 512
HY_POS_BANDS = 16
HY_FILT_HIDDEN = 64
HY_FAST_DECAY = 0.3
HY_SLOW_DECAY = 1.5
HY_DECAY_TARGET = 1e-2
MLA_HEADS = 4
MLA_Q_RANK = 512
MLA_KV_RANK = 256
MLA_NOPE = 128
MLA_ROPE = 64
MLA_V = 128
ROPE_THETA = 10000.0
DN_HEADS = 4
DN_DK = 128
DN_DV = 128
DN_CHUNK = 64
D_FF = 4 * D_MODEL

Z_NA = 0
Z_HY = 1536
Z_DNQKV = 3072
Z_DNGATE = 4608
Z_CQ = 5120
Z_CKV = 5632
Z_KR = 5888
Z_GATE = 6144
Z_COLS = Z_GATE + N_BRANCH * D_MODEL

FFT_N2 = 128


def _params(*sem):
    return pltpu.CompilerParams(dimension_semantics=sem, vmem_limit_bytes=VMEM_LIMIT_BYTES)


def _rms_bf16(x, g):
    xf = x.astype(F32)
    y = xf * lax.rsqrt(jnp.mean(xf * xf, axis=-1, keepdims=True) + RMS_EPS)
    return (y * g).astype(BF16)


def _norm_mm_kernel(x_ref, g_ref, w_ref, o_ref, h_ref):
    @pl.when(pl.program_id(1) == 0)
    def _():
        h_ref[...] = _rms_bf16(x_ref[...], g_ref[...])

    o_ref[...] = jnp.dot(h_ref[...], w_ref[...], preferred_element_type=F32).astype(o_ref.dtype)


def norm_matmul(x, g, w, out_dtype, tm, tn):
    m, k = x.shape
    n = w.shape[1]
    return pl.pallas_call(
        _norm_mm_kernel,
        grid=(m // tm, n // tn),
        in_specs=[pl.BlockSpec((tm, k), lambda i, j: (i, 0)),
                  pl.BlockSpec((1, k), lambda i, j: (0, 0)),
                  pl.BlockSpec((k, tn), lambda i, j: (0, j))],
        out_specs=pl.BlockSpec((tm, tn), lambda i, j: (i, j)),
        out_shape=jax.ShapeDtypeStruct((m, n), out_dtype),
        scratch_shapes=[pltpu.VMEM((tm, k), BF16)],
        compiler_params=_params("parallel", "arbitrary"),
        name="norm_matmul",
    )(x, g.reshape(1, k), w)


def _mm_res_kernel(a_ref, w_ref, r_ref, o_ref):
    o_ref[...] = r_ref[...] + jnp.dot(a_ref[...], w_ref[...], preferred_element_type=F32)


def matmul_residual(a, w, r, tm, tn):
    m, k = a.shape
    n = w.shape[1]
    return pl.pallas_call(
        _mm_res_kernel,
        grid=(m // tm, n // tn),
        in_specs=[pl.BlockSpec((tm, k), lambda i, j: (i, 0)),
                  pl.BlockSpec((k, tn), lambda i, j: (0, j)),
                  pl.BlockSpec((tm, tn), lambda i, j: (i, j))],
        out_specs=pl.BlockSpec((tm, tn), lambda i, j: (i, j)),
        out_shape=jax.ShapeDtypeStruct((m, n), F32),
        compiler_params=_params("parallel", "arbitrary"),
        name="matmul_residual",
    )(a, w, r)


def _mlp_kernel(x_ref, g_ref, wu_ref, wd_ref, gf_ref, o_ref, h_ref, *, final_norm):
    j = pl.program_id(1)

    @pl.when(j == 0)
    def _():
        x = x_ref[...]
        h_ref[...] = _rms_bf16(x, g_ref[...])
        o_ref[...] = x

    u = jnp.dot(h_ref[...], wu_ref[...], preferred_element_type=F32)
    a = jnp.square(jnp.maximum(u, 0.0)).astype(BF16)
    o_ref[...] += jnp.dot(a, wd_ref[...], preferred_element_type=F32)

    if final_norm:
        @pl.when(j == pl.num_programs(1) - 1)
        def _():
            y = o_ref[...]
            o_ref[...] = y * lax.rsqrt(jnp.mean(y * y, axis=-1, keepdims=True) + RMS_EPS) * gf_ref[...]


def mlp_block(x, g, w_up, w_down, g_final, final_norm, tm, tf):
    m, d = x.shape
    f = w_up.shape[1]
    return pl.pallas_call(
        functools.partial(_mlp_kernel, final_norm=final_norm),
        grid=(m // tm, f // tf),
        in_specs=[pl.BlockSpec((tm, d), lambda i, j: (i, 0)),
                  pl.BlockSpec((1, d), lambda i, j: (0, 0)),
                  pl.BlockSpec((d, tf), lambda i, j: (0, j)),
                  pl.BlockSpec((tf, d), lambda i, j: (j, 0)),
                  pl.BlockSpec((1, d), lambda i, j: (0, 0))],
        out_specs=pl.BlockSpec((tm, d), lambda i, j: (i, 0)),
        out_shape=jax.ShapeDtypeStruct((m, d), F32),
        scratch_shapes=[pltpu.VMEM((tm, d), BF16)],
        compiler_params=_params("parallel", "arbitrary"),
        name="mlp_block",
    )(x, g.reshape(1, d), w_up, w_down, g_final.reshape(1, d))


def _merge_kernel(gate_ref, ba_ref, bb_ref, bc_ref, bd_ref, wb_ref, o_ref, acc_ref):
    n = pl.program_id(1)

    def contribution(b_ref):
        y = jnp.dot(b_ref[...], wb_ref[0], preferred_element_type=F32)
        return jax.nn.sigmoid(gate_ref[...].astype(F32)) * y

    @pl.when(n == 0)
    def _():
        acc_ref[...] = contribution(ba_ref)

    @pl.when(n == 1)
    def _():
        acc_ref[...] += contribution(bb_ref)

    @pl.when(n == 2)
    def _():
        acc_ref[...] += contribution(bc_ref)

    @pl.when(n == 3)
    def _():
        o_ref[...] = (acc_ref[...] + contribution(bd_ref)).astype(o_ref.dtype)


def gated_merge(z, branches, w_branch, tm):
    m = z.shape[0]
    gate_blk0 = Z_GATE // D_MODEL
    br_spec = pl.BlockSpec((tm, BRANCH_W), lambda i, n: (i, 0))
    return pl.pallas_call(
        _merge_kernel,
        grid=(m // tm, N_BRANCH),
        in_specs=[pl.BlockSpec((tm, D_MODEL), lambda i, n: (i, gate_blk0 + n)),
                  br_spec, br_spec, br_spec, br_spec,
                  pl.BlockSpec((1, BRANCH_W, D_MODEL), lambda i, n: (n, 0, 0))],
        out_specs=pl.BlockSpec((tm, D_MODEL), lambda i, n: (i, 0)),
        out_shape=jax.ShapeDtypeStruct((m, D_MODEL), BF16),
        scratch_shapes=[pltpu.VMEM((tm, D_MODEL), F32)],
        compiler_params=_params("parallel", "arbitrary"),
        name="gated_merge",
    )(z, *branches, w_branch)


NA_MASK = -1e30


def na_bias_table(rpb):
    q = np.arange(GRID_W)
    kc = np.arange(GRID_W)
    cs = np.clip(q - NA_WIN_C // 2, 0, GRID_W - NA_WIN_C)
    ok = (kc[None, :] >= cs[:, None]) & (kc[None, :] < cs[:, None] + NA_WIN_C)
    dcol = np.clip(kc[None, :] - q[:, None] + NA_WIN_C - 1, 0, 2 * NA_WIN_C - 2)
    drow = np.arange(NA_WIN_R)[None, :] - np.arange(NA_WIN_R)[:, None] + NA_WIN_R - 1
    t = rpb.astype(F32)[:, drow]
    t = t[:, :, :, dcol]
    t = jnp.transpose(t, (1, 0, 3, 2, 4))
    t = jnp.where(jnp.asarray(ok)[None, None, :, None, :], t, NA_MASK)
    return t.reshape(NA_WIN_R, NA_HEADS, GRID_W, NA_WIN_R * GRID_W)


def _na_kernel(q_ref, k_ref, v_ref, bias_ref, o_ref, *, rows_per_step, rows):
    rblk = pl.program_id(1)
    win = NA_WIN_R * GRID_W
    scale = NA_HEAD_DIM ** -0.5

    def row_body(rl, carry):
        r = rblk * rows_per_step + rl
        rs = jnp.clip(r - NA_WIN_R // 2, 0, rows - NA_WIN_R)
        off = r - rs
        qrow = q_ref[0, pl.ds(pl.multiple_of(rl * GRID_W, GRID_W), GRID_W), :]
        kstart = pl.multiple_of(rs * GRID_W, GRID_W)
        kb = k_ref[0, pl.ds(kstart, win), :]
        vb = v_ref[0, pl.ds(kstart, win), :]
        outs = []
        for h in range(NA_HEADS):
            sl = slice(h * NA_HEAD_DIM, (h + 1) * NA_HEAD_DIM)
            s = lax.dot_general(qrow[:, sl], kb[:, sl], (((1,), (1,)), ((), ())), preferred_element_type=F32)
            s = s * scale + bias_ref[off, h]
            p = jnp.exp(s - jnp.max(s, axis=-1, keepdims=True))
            l = jnp.sum(p, axis=-1, keepdims=True)
            outs.append(jnp.dot(p.astype(BF16), vb[:, sl], preferred_element_type=F32) / l)
        o_ref[0, pl.ds(pl.multiple_of(rl * GRID_W, GRID_W), GRID_W), :] = (
            jnp.concatenate(outs, axis=1).astype(o_ref.dtype))
        return carry

    lax.fori_loop(0, rows_per_step, row_body, 0)


def neighbourhood_attention(z3, bias, rows_per_step):
    b, l, _ = z3.shape
    rows = l // GRID_W
    w = NA_HEADS * NA_HEAD_DIM
    blk = Z_NA // w
    return pl.pallas_call(
        functools.partial(_na_kernel, rows_per_step=rows_per_step, rows=rows),
        grid=(b, rows // rows_per_step),
        in_specs=[pl.BlockSpec((1, rows_per_step * GRID_W, w), lambda i, r: (i, r, blk)),
                  pl.BlockSpec((1, l, w), lambda i, r: (i, 0, blk + 1)),
                  pl.BlockSpec((1, l, w), lambda i, r: (i, 0, blk + 2)),
                  pl.BlockSpec(bias.shape, lambda i, r: (0, 0, 0, 0))],
        out_specs=pl.BlockSpec((1, rows_per_step * GRID_W, w), lambda i, r: (i, r, 0)),
        out_shape=jax.ShapeDtypeStruct((b, l, w), BF16),
        compiler_params=_params("parallel", "arbitrary"),
        name="neighbourhood_attention",
    )(z3, z3, z3, bias)


def _fft_tables(l):
    n = 2 * l
    n2 = FFT_N2
    n1 = n // n2
    a = np.arange(n1)
    b = np.arange(n2)
    k1 = np.arange(n1)
    t = n2 * a[None, :] + b[:, None]
    th = 2.0 * np.pi * (k1[None, :, None] * t[:, None, :] % n) / n
    fa = np.concatenate([np.cos(th), -np.sin(th)], axis=1)
    th_t = np.transpose(th, (0, 2, 1))
    ga = np.concatenate([np.cos(th_t), -np.sin(th_t)], axis=2)
    ph = 2.0 * np.pi * (np.outer(b, b) % n2) / n2
    cr, ci = np.cos(ph), -np.sin(ph)
    fb = np.block([[cr, -ci], [ci, cr]])
    fbi = np.block([[cr, ci], [-ci, cr]])
    return fa, ga, fb, fbi


def _filter_fft_kernel(k_ref, fa_ref, fb_ref, o_ref, y_ref, *, n1):
    n2 = FFT_N2

    def step_a(b, carry):
        xb = k_ref[pl.ds(b, n1, stride=n2), :]
        r = jnp.dot(fa_ref[b], xb, precision=HIGHEST, preferred_element_type=F32)
        y_ref[pl.ds(b, n1, stride=2 * n2), :] = r[:n1]
        y_ref[pl.ds(b + n2, n1, stride=2 * n2), :] = r[n1:]
        return carry

    lax.fori_loop(0, n2, step_a, 0)

    def step_b(k1, carry):
        r0 = pl.multiple_of(k1 * 2 * n2, 2 * n2)
        x = jnp.dot(fb_ref[...], y_ref[pl.ds(r0, 2 * n2), :], precision=HIGHEST, preferred_element_type=F32)
        o_ref[pl.ds(r0, 2 * n2), :] = x * (1.0 / (n1 * n2))
        return carry

    lax.fori_loop(0, n1, step_b, 0)


def filter_spectrum(k_two, fa, fb):
    n, c = k_two.shape
    n1 = n // FFT_N2
    return pl.pallas_call(
        functools.partial(_filter_fft_kernel, n1=n1),
        grid=(c // LANES,),
        in_specs=[pl.BlockSpec((n, LANES), lambda i: (0, i)),
                  pl.BlockSpec(fa.shape, lambda i: (0, 0, 0)),
                  pl.BlockSpec(fb.shape, lambda i: (0, 0))],
        out_specs=pl.BlockSpec((2 * n, LANES), lambda i: (0, i)),
        out_shape=jax.ShapeDtypeStruct((2 * n, c), F32),
        scratch_shapes=[pltpu.VMEM((2 * n, LANES), F32)],
        compiler_params=_params("parallel"),
        name="hyena_filter_spectrum",
    )(k_two, fa, fb)


def _hyena_filter_kernel(z_ref, w1_ref, b1_ref, w2_ref, b2_ref, w3_ref, dec_ref, o_ref):
    hid = jnp.sin(jnp.dot(z_ref[...], w1_ref[...], precision=HIGHEST, preferred_element_type=F32) + b1_ref[...])
    hid = jnp.sin(jnp.dot(hid, w2_ref[...], precision=HIGHEST, preferred_element_type=F32) + b2_ref[...])
    h = jnp.dot(hid, w3_ref[...], precision=HIGHEST, preferred_element_type=F32)
    dec = dec_ref[...]
    hf = h[:, :HY_WIDTH] * dec
    hb = h[:, HY_WIDTH:] * dec
    norm = (jnp.sum(jnp.abs(hf), axis=0, keepdims=True) + jnp.sum(jnp.abs(hb), axis=0, keepdims=True)) + RMS_EPS
    o_ref[:, :HY_WIDTH] = hf / norm
    o_ref[:, HY_WIDTH:] = hb / norm


def hyena_filters(l, w1, b1, w2, b2, w3):
    t = jnp.linspace(0.0, 1.0, l, dtype=F32)[:, None]
    w = 2.0 * math.pi * jnp.arange(l, dtype=F32)[:, None] / l
    bands = jnp.linspace(1e-4, HY_POS_BANDS - 1, HY_POS_BANDS, dtype=F32)[None, :]
    z = jnp.concatenate([t, jnp.cos(bands * w), -jnp.sin(bands * w)], axis=-1)
    pad = LANES - z.shape[1]
    z = jnp.pad(z, ((0, 0), (0, pad)))
    w1p = jnp.pad(w1.astype(F32), ((0, pad), (0, 0)))
    max_decay = math.log(HY_DECAY_TARGET) / HY_FAST_DECAY
    min_decay = math.log(HY_DECAY_TARGET) / HY_SLOW_DECAY
    deltas = jnp.abs(jnp.linspace(min_decay, max_decay, HY_WIDTH, dtype=F32))
    dec = jnp.exp(-t * deltas[None, :])
    h = pl.pallas_call(
        _hyena_filter_kernel,
        out_shape=jax.ShapeDtypeStruct((l, 2 * HY_WIDTH), F32),
        compiler_params=pltpu.CompilerParams(vmem_limit_bytes=VMEM_LIMIT_BYTES),
        name="hyena_filter_ffn",
    )(z, w1p, b1.reshape(1, -1).astype(F32), w2.astype(F32), b2.reshape(1, -1).astype(F32), w3.astype(F32), dec)
    h_f, h_b = h[:, :HY_WIDTH], h[:, HY_WIDTH:]
    return jnp.concatenate([h_f[:1] + h_b[:1], h_f[1:], jnp.zeros_like(h_f[:1]), h_b[:0:-1]], axis=0)


def _shift_rows(x, delta):
    n = x.shape[0]
    row = lax.broadcasted_iota(jnp.int32, x.shape, 0)
    if delta == 1:
        return jnp.where(row == 0, 0.0, pltpu.roll(x, 1, 0))
    return jnp.where(row == n - 1, 0.0, pltpu.roll(x, n - 1, 0))


def _dwconv3(x, w_ref):
    w = w_ref[...].astype(F32)
    return _shift_rows(x, 1) * w[0:1] + x * w[1:2] + _shift_rows(x, -1) * w[2:3]


def _hyena_kernel(x1_ref, x2_ref, v_ref, w1_ref, w2_ref, wv_ref, skip_ref, kf_ref, fa_ref, ga_ref, fb_ref, fbi_ref,
                  o_ref, xs_ref, y_ref, *, n1):
    n2 = FFT_N2
    na = n1 // 2
    vg = _dwconv3(v_ref[0].astype(F32), wv_ref) * _dwconv3(x1_ref[0].astype(F32), w1_ref)
    xs_ref[...] = vg

    def fwd_a(b, carry):
        xb = xs_ref[pl.ds(b, na, stride=n2), :].astype(BF16)
        r = jnp.dot(fa_ref[b], xb, preferred_element_type=F32)
        y_ref[pl.ds(b, n1, stride=2 * n2), :] = r[:n1]
        y_ref[pl.ds(b + n2, n1, stride=2 * n2), :] = r[n1:]
        return carry

    lax.fori_loop(0, n2, fwd_a, 0)

    def mid(k1, carry):
        r0 = pl.multiple_of(k1 * 2 * n2, 2 * n2)
        x = jnp.dot(fb_ref[...], y_ref[pl.ds(r0, 2 * n2), :].astype(BF16), preferred_element_type=F32)
        xr, xi = x[:n2], x[n2:]
        kr = kf_ref[pl.ds(r0, n2), :]
        ki = kf_ref[pl.ds(r0 + n2, n2), :]
        p = jnp.concatenate([xr * kr - xi * ki, xr * ki + xi * kr], axis=0).astype(BF16)
        y_ref[pl.ds(r0, 2 * n2), :] = jnp.dot(fbi_ref[...], p, preferred_element_type=F32)
        return carry

    lax.fori_loop(0, n1, mid, 0)

    def inv_a(b, carry):
        qb = jnp.concatenate([y_ref[pl.ds(b, n1, stride=2 * n2), :],
                              y_ref[pl.ds(b + n2, n1, stride=2 * n2), :]], axis=0).astype(BF16)
        xs_ref[pl.ds(b, na, stride=n2), :] = jnp.dot(ga_ref[b], qb, preferred_element_type=F32)
        return carry

    lax.fori_loop(0, n2, inv_a, 0)

    y = xs_ref[...] + vg * skip_ref[...].astype(F32)
    o_ref[0] = (y * _dwconv3(x2_ref[0].astype(F32), w2_ref)).astype(o_ref.dtype)


def hyena_mixer(z3, w_short, skip, kf, tables):
    b, l, _ = z3.shape
    n1 = 2 * l // FFT_N2
    fa, ga, fb, fbi = tables
    nch = HY_WIDTH // LANES
    blk = Z_HY // LANES

    def zspec(seg):
        return pl.BlockSpec((1, l, LANES), lambda c, i: (i, 0, blk + seg * nch + c))

    def wspec(seg):
        return pl.BlockSpec((3, LANES), lambda c, i: (0, seg * nch + c))

    return pl.pallas_call(
        functools.partial(_hyena_kernel, n1=n1),
        grid=(nch, b),
        in_specs=[zspec(0), zspec(1), zspec(2), wspec(0), wspec(1), wspec(2),
                  pl.BlockSpec((1, LANES), lambda c, i: (0, c)),
                  pl.BlockSpec((4 * l, LANES), lambda c, i: (0, c)),
                  pl.BlockSpec(fa.shape, lambda c, i: (0, 0, 0)),
                  pl.BlockSpec(ga.shape, lambda c, i: (0, 0, 0)),
                  pl.BlockSpec(fb.shape, lambda c, i: (0, 0)),
                  pl.BlockSpec(fbi.shape, lambda c, i: (0, 0))],
        out_specs=pl.BlockSpec((1, l, LANES), lambda c, i: (i, 0, c)),
        out_shape=jax.ShapeDtypeStruct((b, l, HY_WIDTH), BF16),
        scratch_shapes=[pltpu.VMEM((l, LANES), F32), pltpu.VMEM((4 * l, LANES), F32)],
        compiler_params=_params("parallel", "arbitrary"),
        name="hyena_mixer",
    )(z3, z3, z3, w_short, w_short, w_short, skip.reshape(1, -1), kf, fa, ga, fb, fbi)


MLA_HW = 2 * LANES


def _rope_group(g, cs):
    prod = g * cs
    s = prod + pltpu.roll(prod, MLA_ROPE, 1)
    lane = lax.broadcasted_iota(jnp.int32, s.shape, 1)
    return jnp.where(lane < MLA_ROPE, s, 0.0)


def _mla_prep_kernel(cq_ref, ckv_ref, kr_ref, gq_ref, gkv_ref, wq_ref, wkv_ref, cs_ref, q_ref, k_ref, v_ref):
    cs = cs_ref[...]
    hq = _rms_bf16(cq_ref[...], gq_ref[...])
    q = jnp.dot(hq, wq_ref[...], preferred_element_type=F32)
    hkv = _rms_bf16(ckv_ref[...], gkv_ref[...])
    kv = jnp.dot(hkv, wkv_ref[...], preferred_element_type=F32)
    k_rope = _rope_group(kr_ref[...].astype(F32), cs).astype(BF16)
    for h in range(MLA_HEADS):
        o = h * MLA_HW
        q_ref[:, o:o + LANES] = q[:, o:o + LANES].astype(BF16)
        q_ref[:, o + LANES:o + MLA_HW] = _rope_group(q[:, o + LANES:o + MLA_HW], cs).astype(BF16)
        k_ref[:, o:o + LANES] = kv[:, o:o + LANES].astype(BF16)
        k_ref[:, o + LANES:o + MLA_HW] = k_rope
        v_ref[:, h * MLA_V:(h + 1) * MLA_V] = kv[:, o + LANES:o + MLA_HW].astype(BF16)


def _mla_attn_kernel(q_ref, k_ref, v_ref, o_ref):
    scale = (MLA_NOPE + MLA_ROPE) ** -0.5
    s = lax.dot_general(q_ref[0], k_ref[0], (((1,), (1,)), ((), ())), preferred_element_type=F32)
    p = jnp.exp((s - jnp.max(s, axis=-1, keepdims=True)) * scale)
    l = jnp.sum(p, axis=-1, keepdims=True)
    o = jnp.dot(p.astype(BF16), v_ref[0], preferred_element_type=F32) / l
    o_ref[0] = o.astype(o_ref.dtype)


def mla_mixer(z, b, l, g_q, g_kv, wq_p, w_ukv, cs_tab, tm, tq):
    m = z.shape[0]
    lt = l // tm
    qp, kp, vp = pl.pallas_call(
        _mla_prep_kernel,
        grid=(m // tm,),
        in_specs=[pl.BlockSpec((tm, MLA_Q_RANK), lambda i: (i, Z_CQ // MLA_Q_RANK)),
                  pl.BlockSpec((tm, MLA_KV_RANK), lambda i: (i, Z_CKV // MLA_KV_RANK)),
                  pl.BlockSpec((tm, LANES), lambda i: (i, Z_KR // LANES)),
                  pl.BlockSpec((1, MLA_Q_RANK), lambda i: (0, 0)),
                  pl.BlockSpec((1, MLA_KV_RANK), lambda i: (0, 0)),
                  pl.BlockSpec(wq_p.shape, lambda i: (0, 0)),
                  pl.BlockSpec(w_ukv.shape, lambda i: (0, 0)),
                  pl.BlockSpec((tm, LANES), lambda i: (i % lt, 0))],
        out_specs=[pl.BlockSpec((tm, MLA_HEADS * MLA_HW), lambda i: (i, 0)),
                   pl.BlockSpec((tm, MLA_HEADS * MLA_HW), lambda i: (i, 0)),
                   pl.BlockSpec((tm, MLA_HEADS * MLA_V), lambda i: (i, 0))],
        out_shape=[jax.ShapeDtypeStruct((m, MLA_HEADS * MLA_HW), BF16),
                   jax.ShapeDtypeStruct((m, MLA_HEADS * MLA_HW), BF16),
                   jax.ShapeDtypeStruct((m, MLA_HEADS * MLA_V), BF16)],
        compiler_params=_params("parallel"),
        name="mla_prep",
    )(z, z, z, g_q.reshape(1, -1), g_kv.reshape(1, -1), wq_p, w_ukv, cs_tab)
    qp = qp.reshape(b, l, -1)
    kp = kp.reshape(b, l, -1)
    vp = vp.reshape(b, l, -1)
    return pl.pallas_call(
        _mla_attn_kernel,
        grid=(b, MLA_HEADS, l // tq),
        in_specs=[pl.BlockSpec((1, tq, MLA_HW), lambda i, h, t: (i, t, h)),
                  pl.BlockSpec((1, l, MLA_HW), lambda i, h, t: (i, 0, h)),
                  pl.BlockSpec((1, l, MLA_V), lambda i, h, t: (i, 0, h))],
        out_specs=pl.BlockSpec((1, tq, MLA_V), lambda i, h, t: (i, t, h)),
        out_shape=jax.ShapeDtypeStruct((b, l, MLA_HEADS * MLA_V), BF16),
        compiler_params=_params("parallel", "parallel", "arbitrary"),
        name="mla_attention",
    )(qp, kp, vp)


def _tri_unit_inverse(a):
    c = a.shape[0]
    eye = (lax.broadcasted_iota(jnp.int32, (c, c), 0) == lax.broadcasted_iota(jnp.int32, (c, c), 1)).astype(F32)
    p = eye - a
    m = a
    for _ in range(int(math.log2(c)) - 1):
        m = jnp.dot(m, m, precision=HIGHEST, preferred_element_type=F32)
        p = jnp.dot(p, eye + m, precision=HIGHEST, preferred_element_type=F32)
    return p


def _bdot(a, b):
    return jnp.dot(a.astype(BF16), b.astype(BF16), preferred_element_type=F32)


def _bdot_nt(a, b):
    return lax.dot_general(a.astype(BF16), b.astype(BF16), (((1,), (1,)), ((), ())), preferred_element_type=F32)


def _dn_chunk(c, s, refs, reverse):
    q_s, k_s, v_s, g_s, b_s = refs
    cc = DN_CHUNK
    r0 = pl.multiple_of(c * cc, cc)
    q = q_s[pl.ds(r0, cc), :]
    k = k_s[pl.ds(r0, cc), :]
    v = v_s[pl.ds(r0, cc), :]
    lane = 2 if reverse else 0
    ri = lax.broadcasted_iota(jnp.int32, (cc, cc), 0)
    ci = lax.broadcasted_iota(jnp.int32, (cc, cc), 1)
    incl = (ri <= ci) if reverse else (ri >= ci)
    strict = (ri < ci) if reverse else (ri > ci)
    gc = jnp.dot(incl.astype(F32), g_s[pl.ds(r0, cc), :], precision=HIGHEST, preferred_element_type=F32)
    gcol = gc[:, lane:lane + 1]
    grow = jnp.transpose(gc)[lane:lane + 1, :]
    beta = b_s[pl.ds(r0, cc), :][:, lane + 1:lane + 2]
    last = 0 if reverse else cc - 1
    g_last = gcol[last:last + 1, :]
    decay = jnp.where(incl, jnp.exp(jnp.where(incl, gcol - grow, 0.0)), 0.0)
    e_g = jnp.exp(gcol)
    kb = k * beta
    a_kk = jnp.where(strict, _bdot_nt(kb, k) * decay, 0.0)
    t_inv = _tri_unit_inverse(a_kk)
    u = _bdot(t_inv, v * beta)
    w = _bdot(t_inv, kb * e_g)
    a_qk = jnp.where(incl, _bdot_nt(q, k) * decay, 0.0)
    v_new = u - _bdot(w, s)
    o = _bdot(q * e_g, s) + _bdot(a_qk, v_new)
    k_dec = k * jnp.exp(g_last - gcol)
    s = s * jnp.exp(g_last) + _bdot(jnp.transpose(k_dec), v_new)
    return s, o


def _dn_kernel(zq_ref, zk_ref, zv_ref, zg_ref, ab_ref, wq_ref, wk_ref, wv_ref, alog_ref, dtb_ref, gn_ref, o_ref,
               q_s, k_s, v_s, g_s, b_s, of_s, ob_s, *, nchunks):
    def act(z_ref, w_ref):
        c = _dwconv3(z_ref[0].astype(F32), w_ref)
        return c * jax.nn.sigmoid(c)

    def l2n(x):
        return x * lax.rsqrt(jnp.sum(x * x, axis=-1, keepdims=True) + RMS_EPS)

    q_s[...] = l2n(act(zq_ref, wq_ref)) * (DN_DK ** -0.5)
    k_s[...] = l2n(act(zk_ref, wk_ref))
    v_s[...] = act(zv_ref, wv_ref)
    ab = ab_ref[0]
    x = ab + dtb_ref[0]
    softplus = jnp.maximum(x, 0.0) + jnp.log1p(jnp.exp(-jnp.abs(x)))
    g_s[...] = -jnp.exp(alog_ref[0]) * softplus
    b_s[...] = jax.nn.sigmoid(ab)
    refs = (q_s, k_s, v_s, g_s, b_s)

    def body(i, carry):
        s_f, s_b = carry
        s_f, o_f = _dn_chunk(i, s_f, refs, False)
        of_s[pl.ds(pl.multiple_of(i * DN_CHUNK, DN_CHUNK), DN_CHUNK), :] = o_f
        j = nchunks - 1 - i
        s_b, o_b = _dn_chunk(j, s_b, refs, True)
        ob_s[pl.ds(pl.multiple_of(j * DN_CHUNK, DN_CHUNK), DN_CHUNK), :] = o_b
        return s_f, s_b

    zero = jnp.zeros((DN_DK, DN_DV), F32)
    lax.fori_loop(0, nchunks, body, (zero, zero))

    o = of_s[...] + ob_s[...]
    o = o * lax.rsqrt(jnp.mean(o * o, axis=-1, keepdims=True) + RMS_EPS) * gn_ref[...]
    gate = zg_ref[0].astype(F32)
    o_ref[0] = (o * (gate * jax.nn.sigmoid(gate))).astype(o_ref.dtype)


def deltanet_mixer(z3, ab3, w_conv, alog_p, dtb_p, g_norm):
    b, l, _ = z3.shape
    blk = Z_DNQKV // LANES
    gblk = Z_DNGATE // LANES
    hh = DN_HEADS

    def zspec(off):
        return pl.BlockSpec((1, l, LANES), lambda i, h: (i, 0, off + h))

    def wspec(seg):
        return pl.BlockSpec((3, LANES), lambda i, h: (0, seg * hh + h))

    vec = pl.BlockSpec((1, 1, LANES), lambda i, h: (h, 0, 0))
    seq = pltpu.VMEM((l, LANES), F32)
    return pl.pallas_call(
        functools.partial(_dn_kernel, nchunks=l // DN_CHUNK),
        grid=(b, hh),
        in_specs=[zspec(blk), zspec(blk + hh), zspec(blk + 2 * hh), zspec(gblk),
                  pl.BlockSpec((1, l, LANES), lambda i, h: (i, 0, h)),
                  wspec(0), wspec(1), wspec(2), vec, vec,
                  pl.BlockSpec((1, LANES), lambda i, h: (0, 0))],
        out_specs=pl.BlockSpec((1, l, LANES), lambda i, h: (i, 0, h)),
        out_shape=jax.ShapeDtypeStruct((b, l, hh * DN_DV), BF16),
        scratch_shapes=[seq] * 7,
        compiler_params=_params("parallel", "arbitrary"),
        name="deltanet_mixer",
    )(z3, z3, z3, z3, ab3, w_conv, w_conv, w_conv, alog_p, dtb_p, g_norm.reshape(1, -1))


def _rotate_half_cols(w):
    half = w.shape[-1] // 2
    return jnp.concatenate([-w[..., half:], w[..., :half]], axis=-1)


def _prep_w_in(w_in):
    cuts = np.cumsum([0, 1536, 1536, MLA_Q_RANK, MLA_KV_RANK, MLA_ROPE, 1536, 512, 16])
    na, hy, cq, ckv, kr, dnqkv, dngate, dnab = (w_in[:, cuts[i]:cuts[i + 1]] for i in range(8))
    gates = w_in[:, cuts[8]:]
    pad = jnp.zeros((w_in.shape[0], Z_GATE - Z_KR - 2 * MLA_ROPE), w_in.dtype)
    main = jnp.concatenate([na, hy, dnqkv, dngate, cq, ckv, kr, _rotate_half_cols(kr), pad, gates], axis=1)
    ab = dnab.reshape(-1, 2, 2, DN_HEADS)
    ab = jnp.transpose(ab, (0, 3, 1, 2)).reshape(-1, DN_HEADS, 4)
    ab = jnp.pad(ab, ((0, 0), (0, 0), (0, LANES - 4))).reshape(-1, DN_HEADS * LANES)
    return main.astype(BF16), ab.astype(BF16)


def _prep_w_uq(w_uq):
    k = w_uq.shape[0]
    w = w_uq.reshape(k, MLA_HEADS, MLA_NOPE + MLA_ROPE)
    rope_w = w[:, :, MLA_NOPE:]
    return jnp.concatenate([w, _rotate_half_cols(rope_w)], axis=-1).reshape(k, MLA_HEADS * MLA_HW).astype(BF16)


def _head_lane_vec(p):
    v = jnp.zeros((DN_HEADS, 1, LANES), F32)
    v = v.at[:, 0, 0].set(p[0].astype(F32))
    return v.at[:, 0, 2].set(p[1].astype(F32))


def _rope_table(l):
    half = MLA_ROPE // 2
    inv = ROPE_THETA ** (-jnp.arange(half, dtype=F32) / half)
    ang = jnp.arange(l, dtype=F32)[:, None] * inv[None, :]
    cos, sin = jnp.cos(ang), jnp.sin(ang)
    return jnp.concatenate([cos, cos, sin, sin], axis=-1)


def _pick(n, pref):
    for t in pref:
        if n % t == 0:
            return t
    return n


def trunk(x3, norm_mix, w_in, na_rpb, hy_short, hy_skip, hy_w1, hy_b1, hy_w2, hy_b2, hy_w3,
          mla_g_q, mla_g_kv, mla_w_uq, mla_w_ukv, dn_conv, dn_a_log, dn_dt_bias, dn_g_norm,
          w_branch, w_out, norm_mlp, w_up, w_down, norm_final):
    b, l, d = x3.shape
    m = b * l
    depth = w_in.shape[0]
    x = x3.reshape(m, d)
    tm_big = _pick(m, (1024, 512, 256, 128))
    tm_mid = _pick(m, (512, 256, 128))
    tm_small = _pick(m, (256, 128))
    fa, ga, fb, fbi = _fft_tables(l)
    n1 = 2 * l // FFT_N2
    tables = (jnp.asarray(fa[:, :, :n1 // 2], BF16), jnp.asarray(ga[:, :n1 // 2, :], BF16),
              jnp.asarray(fb, BF16), jnp.asarray(fbi, BF16))
    fa32, fb32 = jnp.asarray(fa, F32), jnp.asarray(fb, F32)
    cs_tab = _rope_table(l)
    for layer in range(depth):
        w_main, w_ab = _prep_w_in(w_in[layer])
        z = norm_matmul(x, norm_mix[layer], w_main, BF16, tm_big, 1024)
        ab = norm_matmul(x, norm_mix[layer], w_ab, F32, tm_big, DN_HEADS * LANES)
        z3 = z.reshape(b, l, Z_COLS)
        br_a = neighbourhood_attention(z3, na_bias_table(na_rpb[layer]), _pick(l // GRID_W, (8, 4, 2, 1)))
        k_two = hyena_filters(l, hy_w1[layer], hy_b1[layer], hy_w2[layer], hy_b2[layer], hy_w3[layer])
        kf = filter_spectrum(k_two, fa32, fb32)
        br_b = hyena_mixer(z3, hy_short[layer], hy_skip[layer], kf, tables)
        br_c = mla_mixer(z, b, l, mla_g_q[layer], mla_g_kv[layer], _prep_w_uq(mla_w_uq[layer]),
                         mla_w_ukv[layer].astype(BF16), cs_tab, tm_mid, _pick(l, (512, 256, 128)))
        br_d = deltanet_mixer(z3, ab.reshape(b, l, -1), dn_conv[layer], _head_lane_vec(dn_a_log[layer]),
                              _head_lane_vec(dn_dt_bias[layer]), dn_g_norm[layer])
        branches = [t.reshape(m, BRANCH_W) for t in (br_a, br_b, br_c, br_d)]
        merged = gated_merge(z, branches, w_branch[layer].astype(BF16), tm_small)
        x = matmul_residual(merged, w_out[layer].astype(BF16), x, tm_mid, 1024)
        x = mlp_block(x, norm_mlp[layer], w_up[layer].astype(BF16), w_down[layer].astype(BF16), norm_final,
                      layer == depth - 1, tm_mid, 1024)
    return x.reshape(b, l, d)


def kernel(x_prompt, x_sample, norm_mix, w_in, na_rpb, hy_short, hy_skip, hy_w1, hy_b1, hy_w2, hy_b2, hy_w3,
           mla_g_q, mla_g_kv, mla_w_uq, mla_w_ukv, dn_conv, dn_a_log, dn_dt_bias, dn_g_norm,
           w_branch, w_out, norm_mlp, w_up, w_down, norm_final):
    assert x_prompt.shape[1:] == x_sample.shape[1:]
    nb = x_prompt.shape[0]
    y = trunk(jnp.concatenate([x_prompt, x_sample], axis=0), norm_mix, w_in, na_rpb, hy_short, hy_skip,
              hy_w1, hy_b1, hy_w2, hy_b2, hy_w3, mla_g_q, mla_g_kv, mla_w_uq, mla_w_ukv, dn_conv, dn_a_log,
              dn_dt_bias, dn_g_norm, w_branch, w_out, norm_mlp, w_up, w_down, norm_final)
    return (y[:nb], y[nb:])
```

```python
import functools
import math

import jax
import jax.numpy as jnp
import numpy as np
from jax import lax
from jax.experimental import pallas as pl
from jax.experimental.pallas import tpu as pltpu

F32 = jnp.float32
BF16 = jnp.bfloat16
HIGHEST = lax.Precision.HIGHEST

VMEM_LIMIT_BYTES = 56 * 1024 * 1024
LANES = 128

D_MODEL = 2048
RMS_EPS = 1e-6
GRID_W = 64
N_BRANCH = 4
BRANCH_W = 512
NA_HEADS = 8
NA_HEAD_DIM = 64
NA_WIN_R = 8
NA_WIN_C = 16
HY_WIDTH = 512
HY_POS_BANDS = 16
HY_FILT_HIDDEN = 64
HY_FAST_DECAY = 0.3
HY_SLOW_DECAY = 1.5
HY_DECAY_TARGET = 1e-2
MLA_HEADS = 4
MLA_Q_RANK = 512
MLA_KV_RANK = 256
MLA_NOPE = 128
MLA_ROPE = 64
MLA_V = 128
ROPE_THETA = 10000.0
DN_HEADS = 4
DN_DK = 128
DN_DV = 128
DN_CHUNK = 64
DN_GROUP = 4
D_FF = 4 * D_MODEL

Z_NA = 0
Z_HY = 1536
Z_DNQKV = 3072
Z_DNGATE = 4608
Z_CQ = 5120
Z_CKV = 5632
Z_KR = 5888
Z_GATE = 6144
Z_COLS = Z_GATE + N_BRANCH * D_MODEL

FFT_N2 = 128
FFT_UNROLL = 8
FFT_GROUP = 4


def _params(*sem):
    return pltpu.CompilerParams(dimension_semantics=sem, vmem_limit_bytes=VMEM_LIMIT_BYTES)


def _rms_bf16(x, g):
    xf = x.astype(F32)
    y = xf * lax.rsqrt(jnp.mean(xf * xf, axis=-1, keepdims=True) + RMS_EPS)
    return (y * g).astype(BF16)


def _norm_mm_kernel(x_ref, g_ref, w_ref, o_ref, h_ref):
    @pl.when(pl.program_id(1) == 0)
    def _():
        h_ref[...] = _rms_bf16(x_ref[...], g_ref[...])

    o_ref[...] = jnp.dot(h_ref[...], w_ref[...], preferred_element_type=F32).astype(o_ref.dtype)


def norm_matmul(x, g, w, out_dtype, tm, tn):
    m, k = x.shape
    n = w.shape[1]
    return pl.pallas_call(
        _norm_mm_kernel,
        grid=(m // tm, n // tn),
        in_specs=[pl.BlockSpec((tm, k), lambda i, j: (i, 0)),
                  pl.BlockSpec((1, k), lambda i, j: (0, 0)),
                  pl.BlockSpec((k, tn), lambda i, j: (0, j))],
        out_specs=pl.BlockSpec((tm, tn), lambda i, j: (i, j)),
        out_shape=jax.ShapeDtypeStruct((m, n), out_dtype),
        scratch_shapes=[pltpu.VMEM((tm, k), BF16)],
        compiler_params=_params("parallel", "arbitrary"),
        name="norm_matmul",
    )(x, g.reshape(1, k), w)


def _mm_res_kernel(a_ref, w_ref, r_ref, o_ref):
    o_ref[...] = r_ref[...] + jnp.dot(a_ref[...], w_ref[...], preferred_element_type=F32)


def matmul_residual(a, w, r, tm, tn):
    m, k = a.shape
    n = w.shape[1]
    return pl.pallas_call(
        _mm_res_kernel,
        grid=(m // tm, n // tn),
        in_specs=[pl.BlockSpec((tm, k), lambda i, j: (i, 0)),
                  pl.BlockSpec((k, tn), lambda i, j: (0, j)),
                  pl.BlockSpec((tm, tn), lambda i, j: (i, j))],
        out_specs=pl.BlockSpec((tm, tn), lambda i, j: (i, j)),
        out_shape=jax.ShapeDtypeStruct((m, n), F32),
        compiler_params=_params("parallel", "arbitrary"),
        name="matmul_residual",
    )(a, w, r)


def _mlp_kernel(x_ref, g_ref, wu_ref, wd_ref, gf_ref, o_ref, h_ref, *, final_norm):
    j = pl.program_id(1)

    @pl.when(j == 0)
    def _():
        x = x_ref[...]
        h_ref[...] = _rms_bf16(x, g_ref[...])
        o_ref[...] = x

    u = jnp.dot(h_ref[...], wu_ref[...], preferred_element_type=F32)
    a = jnp.square(jnp.maximum(u, 0.0)).astype(BF16)
    o_ref[...] += jnp.dot(a, wd_ref[...], preferred_element_type=F32)

    if final_norm:
        @pl.when(j == pl.num_programs(1) - 1)
        def _():
            y = o_ref[...]
            o_ref[...] = y * lax.rsqrt(jnp.mean(y * y, axis=-1, keepdims=True) + RMS_EPS) * gf_ref[...]


def mlp_block(x, g, w_up, w_down, g_final, final_norm, tm, tf):
    m, d = x.shape
    f = w_up.shape[1]
    return pl.pallas_call(
        functools.partial(_mlp_kernel, final_norm=final_norm),
        grid=(m // tm, f // tf),
        in_specs=[pl.BlockSpec((tm, d), lambda i, j: (i, 0)),
                  pl.BlockSpec((1, d), lambda i, j: (0, 0)),
                  pl.BlockSpec((d, tf), lambda i, j: (0, j)),
                  pl.BlockSpec((tf, d), lambda i, j: (j, 0)),
                  pl.BlockSpec((1, d), lambda i, j: (0, 0))],
        out_specs=pl.BlockSpec((tm, d), lambda i, j: (i, 0)),
        out_shape=jax.ShapeDtypeStruct((m, d), F32),
        scratch_shapes=[pltpu.VMEM((tm, d), BF16)],
        compiler_params=_params("parallel", "arbitrary"),
        name="mlp_block",
    )(x, g.reshape(1, d), w_up, w_down, g_final.reshape(1, d))


def _merge_kernel(gate_ref, ba_ref, bb_ref, bc_ref, bd_ref, wb_ref, o_ref, acc_ref):
    n = pl.program_id(1)

    def contribution(b_ref):
        y = jnp.dot(b_ref[...], wb_ref[0], preferred_element_type=F32)
        return jax.nn.sigmoid(gate_ref[...].astype(F32)) * y

    @pl.when(n == 0)
    def _():
        acc_ref[...] = contribution(ba_ref)

    @pl.when(n == 1)
    def _():
        acc_ref[...] += contribution(bb_ref)

    @pl.when(n == 2)
    def _():
        acc_ref[...] += contribution(bc_ref)

    @pl.when(n == 3)
    def _():
        o_ref[...] = (acc_ref[...] + contribution(bd_ref)).astype(o_ref.dtype)


def gated_merge(z, branches, w_branch, tm):
    m = z.shape[0]
    gate_blk0 = Z_GATE // D_MODEL
    br_spec = pl.BlockSpec((tm, BRANCH_W), lambda i, n: (i, 0))
    return pl.pallas_call(
        _merge_kernel,
        grid=(m // tm, N_BRANCH),
        in_specs=[pl.BlockSpec((tm, D_MODEL), lambda i, n: (i, gate_blk0 + n)),
                  br_spec, br_spec, br_spec, br_spec,
                  pl.BlockSpec((1, BRANCH_W, D_MODEL), lambda i, n: (n, 0, 0))],
        out_specs=pl.BlockSpec((tm, D_MODEL), lambda i, n: (i, 0)),
        out_shape=jax.ShapeDtypeStruct((m, D_MODEL), BF16),
        scratch_shapes=[pltpu.VMEM((tm, D_MODEL), F32)],
        compiler_params=_params("parallel", "arbitrary"),
        name="gated_merge",
    )(z, *branches, w_branch)


NA_MASK = -1e30


def na_bias_table(rpb):
    q = np.arange(GRID_W)
    kc = np.arange(GRID_W)
    cs = np.clip(q - NA_WIN_C // 2, 0, GRID_W - NA_WIN_C)
    ok = (kc[None, :] >= cs[:, None]) & (kc[None, :] < cs[:, None] + NA_WIN_C)
    dcol = np.clip(kc[None, :] - q[:, None] + NA_WIN_C - 1, 0, 2 * NA_WIN_C - 2)
    drow = np.arange(NA_WIN_R)[None, :] - np.arange(NA_WIN_R)[:, None] + NA_WIN_R - 1
    t = rpb.astype(F32)[:, drow]
    t = t[:, :, :, dcol]
    t = jnp.transpose(t, (1, 0, 3, 2, 4))
    t = jnp.where(jnp.asarray(ok)[None, None, :, None, :], t, NA_MASK)
    return t.reshape(NA_WIN_R, NA_HEADS, GRID_W, NA_WIN_R * GRID_W)


def _na_kernel(q_ref, k_ref, v_ref, bias_ref, o_ref, *, rows_per_step, rows):
    rblk = pl.program_id(1)
    win = NA_WIN_R * GRID_W
    scale = NA_HEAD_DIM ** -0.5

    def row_body(rl, carry):
        r = rblk * rows_per_step + rl
        rs = jnp.clip(r - NA_WIN_R // 2, 0, rows - NA_WIN_R)
        off = r - rs
        qrow = q_ref[0, pl.ds(pl.multiple_of(rl * GRID_W, GRID_W), GRID_W), :]
        kstart = pl.multiple_of(rs * GRID_W, GRID_W)
        kb = k_ref[0, pl.ds(kstart, win), :]
        vb = v_ref[0, pl.ds(kstart, win), :]
        heads = range(NA_HEADS)
        sl = [slice(h * NA_HEAD_DIM, (h + 1) * NA_HEAD_DIM) for h in heads]
        s = [lax.dot_general(qrow[:, sl[h]], kb[:, sl[h]], (((1,), (1,)), ((), ())), preferred_element_type=F32)
             for h in heads]
        s = [s[h] * scale + bias_ref[off, h] for h in heads]
        p = [jnp.exp(s[h] - jnp.max(s[h], axis=-1, keepdims=True)) for h in heads]
        l = [jnp.sum(p[h], axis=-1, keepdims=True) for h in heads]
        outs = [jnp.dot(p[h].astype(BF16), vb[:, sl[h]], preferred_element_type=F32) for h in heads]
        outs = [outs[h] / l[h] for h in heads]
        o_ref[0, pl.ds(pl.multiple_of(rl * GRID_W, GRID_W), GRID_W), :] = (
            jnp.concatenate(outs, axis=1).astype(o_ref.dtype))
        return carry

    lax.fori_loop(0, rows_per_step, row_body, 0)


def neighbourhood_attention(z3, bias, rows_per_step):
    b, l, _ = z3.shape
    rows = l // GRID_W
    w = NA_HEADS * NA_HEAD_DIM
    blk = Z_NA // w
    return pl.pallas_call(
        functools.partial(_na_kernel, rows_per_step=rows_per_step, rows=rows),
        grid=(b, rows // rows_per_step),
        in_specs=[pl.BlockSpec((1, rows_per_step * GRID_W, w), lambda i, r: (i, r, blk)),
                  pl.BlockSpec((1, l, w), lambda i, r: (i, 0, blk + 1)),
                  pl.BlockSpec((1, l, w), lambda i, r: (i, 0, blk + 2)),
                  pl.BlockSpec(bias.shape, lambda i, r: (0, 0, 0, 0))],
        out_specs=pl.BlockSpec((1, rows_per_step * GRID_W, w), lambda i, r: (i, r, 0)),
        out_shape=jax.ShapeDtypeStruct((b, l, w), BF16),
        compiler_params=_params("parallel", "arbitrary"),
        name="neighbourhood_attention",
    )(z3, z3, z3, bias)


def _fft_tables(l):
    n = 2 * l
    n2 = FFT_N2
    n1 = n // n2
    a = np.arange(n1)
    b = np.arange(n2)
    k1 = np.arange(n1)
    t = n2 * a[None, :] + b[:, None]
    th = 2.0 * np.pi * (k1[None, :, None] * t[:, None, :] % n) / n
    fa = np.concatenate([np.cos(th), -np.sin(th)], axis=1)
    th_t = np.transpose(th, (0, 2, 1))
    ga = np.concatenate([np.cos(th_t), -np.sin(th_t)], axis=2)
    ph = 2.0 * np.pi * (np.outer(b, b) % n2) / n2
    cr, ci = np.cos(ph), -np.sin(ph)
    fb = np.block([[cr, -ci], [ci, cr]])
    fbi = np.block([[cr, ci], [-ci, cr]])
    return fa, ga, fb, fbi


def _filter_fft_kernel(k_ref, fa_ref, fb_ref, o_ref, y_ref, *, n1):
    n2 = FFT_N2

    def step_a(b, carry):
        xb = k_ref[pl.ds(b, n1, stride=n2), :]
        r = jnp.dot(fa_ref[b], xb, precision=HIGHEST, preferred_element_type=F32)
        y_ref[pl.ds(b, n1, stride=2 * n2), :] = r[:n1]
        y_ref[pl.ds(b + n2, n1, stride=2 * n2), :] = r[n1:]
        return carry

    lax.fori_loop(0, n2, step_a, 0, unroll=2)

    def step_b(k1, carry):
        r0 = pl.multiple_of(k1 * 2 * n2, 2 * n2)
        x = jnp.dot(fb_ref[...], y_ref[pl.ds(r0, 2 * n2), :], precision=HIGHEST, preferred_element_type=F32)
        o_ref[pl.ds(r0, 2 * n2), :] = x * (1.0 / (n1 * n2))
        return carry

    lax.fori_loop(0, n1, step_b, 0, unroll=2)


def filter_spectrum(k_two, fa, fb):
    n, c = k_two.shape
    n1 = n // FFT_N2
    return pl.pallas_call(
        functools.partial(_filter_fft_kernel, n1=n1),
        grid=(c // LANES,),
        in_specs=[pl.BlockSpec((n, LANES), lambda i: (0, i)),
                  pl.BlockSpec(fa.shape, lambda i: (0, 0, 0)),
                  pl.BlockSpec(fb.shape, lambda i: (0, 0))],
        out_specs=pl.BlockSpec((2 * n, LANES), lambda i: (0, i)),
        out_shape=jax.ShapeDtypeStruct((2 * n, c), F32),
        scratch_shapes=[pltpu.VMEM((2 * n, LANES), F32)],
        compiler_params=_params("parallel"),
        name="hyena_filter_spectrum",
    )(k_two, fa, fb)


def _hyena_filter_kernel(z_ref, w1_ref, b1_ref, w2_ref, b2_ref, w3_ref, dec_ref, o_ref):
    hid = jnp.sin(jnp.dot(z_ref[...], w1_ref[...], precision=HIGHEST, preferred_element_type=F32) + b1_ref[...])
    hid = jnp.sin(jnp.dot(hid, w2_ref[...], precision=HIGHEST, preferred_element_type=F32) + b2_ref[...])
    h = jnp.dot(hid, w3_ref[...], precision=HIGHEST, preferred_element_type=F32)
    dec = dec_ref[...]
    hf = h[:, :HY_WIDTH] * dec
    hb = h[:, HY_WIDTH:] * dec
    norm = (jnp.sum(jnp.abs(hf), axis=0, keepdims=True) + jnp.sum(jnp.abs(hb), axis=0, keepdims=True)) + RMS_EPS
    o_ref[:, :HY_WIDTH] = hf / norm
    o_ref[:, HY_WIDTH:] = hb / norm


def hyena_filters(l, w1, b1, w2, b2, w3):
    t = jnp.linspace(0.0, 1.0, l, dtype=F32)[:, None]
    w = 2.0 * math.pi * jnp.arange(l, dtype=F32)[:, None] / l
    bands = jnp.linspace(1e-4, HY_POS_BANDS - 1, HY_POS_BANDS, dtype=F32)[None, :]
    z = jnp.concatenate([t, jnp.cos(bands * w), -jnp.sin(bands * w)], axis=-1)
    pad = LANES - z.shape[1]
    z = jnp.pad(z, ((0, 0), (0, pad)))
    w1p = jnp.pad(w1.astype(F32), ((0, pad), (0, 0)))
    max_decay = math.log(HY_DECAY_TARGET) / HY_FAST_DECAY
    min_decay = math.log(HY_DECAY_TARGET) / HY_SLOW_DECAY
    deltas = jnp.abs(jnp.linspace(min_decay, max_decay, HY_WIDTH, dtype=F32))
    dec = jnp.exp(-t * deltas[None, :])
    h = pl.pallas_call(
        _hyena_filter_kernel,
        out_shape=jax.ShapeDtypeStruct((l, 2 * HY_WIDTH), F32),
        compiler_params=pltpu.CompilerParams(vmem_limit_bytes=VMEM_LIMIT_BYTES),
        name="hyena_filter_ffn",
    )(z, w1p, b1.reshape(1, -1).astype(F32), w2.astype(F32), b2.reshape(1, -1).astype(F32), w3.astype(F32), dec)
    h_f, h_b = h[:, :HY_WIDTH], h[:, HY_WIDTH:]
    return jnp.concatenate([h_f[:1] + h_b[:1], h_f[1:], jnp.zeros_like(h_f[:1]), h_b[:0:-1]], axis=0)


def _shift_rows(x, delta):
    n = x.shape[0]
    row = lax.broadcasted_iota(jnp.int32, x.shape, 0)
    if delta == 1:
        return jnp.where(row == 0, 0.0, pltpu.roll(x, 1, 0))
    return jnp.where(row == n - 1, 0.0, pltpu.roll(x, n - 1, 0))


def _dwconv3(x, w_ref):
    w = w_ref[...].astype(F32)
    return _shift_rows(x, 1) * w[0:1] + x * w[1:2] + _shift_rows(x, -1) * w[2:3]


def _hyena_kernel(x1_ref, x2_ref, v_ref, w1_ref, w2_ref, wv_ref, skip_ref, kf_ref, fa_ref, ga_ref, fb_ref, fbi_ref,
                  o_ref, xs_ref, y_ref, *, n1):
    n2 = FFT_N2
    na = n1 // 2
    vg = _dwconv3(v_ref[0].astype(F32), wv_ref) * _dwconv3(x1_ref[0].astype(F32), w1_ref)
    xs_ref[...] = vg

    def fwd_a(b, carry):
        xb = xs_ref[pl.ds(b, na, stride=n2), :].astype(BF16)
        r = jnp.dot(fa_ref[b], xb, preferred_element_type=F32)
        y_ref[pl.ds(b, n1, stride=2 * n2), :] = r[:n1]
        y_ref[pl.ds(b + n2, n1, stride=2 * n2), :] = r[n1:]
        return carry

    lax.fori_loop(0, n2, fwd_a, 0, unroll=FFT_UNROLL)

    def mid(kg, carry):
        grp = range(FFT_GROUP)
        r0 = [pl.multiple_of((kg * FFT_GROUP + t) * 2 * n2, 2 * n2) for t in grp]
        fb = fb_ref[...]
        x = [jnp.dot(fb, y_ref[pl.ds(r0[t], 2 * n2), :].astype(BF16), preferred_element_type=F32) for t in grp]
        p = []
        for t in grp:
            xr, xi = x[t][:n2], x[t][n2:]
            kr = kf_ref[pl.ds(r0[t], n2), :]
            ki = kf_ref[pl.ds(r0[t] + n2, n2), :]
            p.append(jnp.concatenate([xr * kr - xi * ki, xr * ki + xi * kr], axis=0).astype(BF16))
        fbi = fbi_ref[...]
        q = [jnp.dot(fbi, p[t], preferred_element_type=F32) for t in grp]
        for t in grp:
            y_ref[pl.ds(r0[t], 2 * n2), :] = q[t]
        return carry

    lax.fori_loop(0, n1 // FFT_GROUP, mid, 0)

    def inv_a(b, carry):
        qb = jnp.concatenate([y_ref[pl.ds(b, n1, stride=2 * n2), :],
                              y_ref[pl.ds(b + n2, n1, stride=2 * n2), :]], axis=0).astype(BF16)
        xs_ref[pl.ds(b, na, stride=n2), :] = jnp.dot(ga_ref[b], qb, preferred_element_type=F32)
        return carry

    lax.fori_loop(0, n2, inv_a, 0, unroll=FFT_UNROLL)

    y = xs_ref[...] + vg * skip_ref[...].astype(F32)
    o_ref[0] = (y * _dwconv3(x2_ref[0].astype(F32), w2_ref)).astype(o_ref.dtype)


def hyena_mixer(z3, w_short, skip, kf, tables):
    b, l, _ = z3.shape
    n1 = 2 * l // FFT_N2
    fa, ga, fb, fbi = tables
    nch = HY_WIDTH // LANES
    blk = Z_HY // LANES

    def zspec(seg):
        return pl.BlockSpec((1, l, LANES), lambda c, i: (i, 0, blk + seg * nch + c))

    def wspec(seg):
        return pl.BlockSpec((3, LANES), lambda c, i: (0, seg * nch + c))

    return pl.pallas_call(
        functools.partial(_hyena_kernel, n1=n1),
        grid=(nch, b),
        in_specs=[zspec(0), zspec(1), zspec(2), wspec(0), wspec(1), wspec(2),
                  pl.BlockSpec((1, LANES), lambda c, i: (0, c)),
                  pl.BlockSpec((4 * l, LANES), lambda c, i: (0, c)),
                  pl.BlockSpec(fa.shape, lambda c, i: (0, 0, 0)),
                  pl.BlockSpec(ga.shape, lambda c, i: (0, 0, 0)),
                  pl.BlockSpec(fb.shape, lambda c, i: (0, 0)),
                  pl.BlockSpec(fbi.shape, lambda c, i: (0, 0))],
        out_specs=pl.BlockSpec((1, l, LANES), lambda c, i: (i, 0, c)),
        out_shape=jax.ShapeDtypeStruct((b, l, HY_WIDTH), BF16),
        scratch_shapes=[pltpu.VMEM((l, LANES), F32), pltpu.VMEM((4 * l, LANES), F32)],
        compiler_params=_params("parallel", "arbitrary"),
        name="hyena_mixer",
    )(z3, z3, z3, w_short, w_short, w_short, skip.reshape(1, -1), kf, fa, ga, fb, fbi)


MLA_HW = 2 * LANES
MLA_QSPLIT = 4


def _rope_group(g, cs):
    prod = g * cs
    s = prod + pltpu.roll(prod, MLA_ROPE, 1)
    lane = lax.broadcasted_iota(jnp.int32, s.shape, 1)
    return jnp.where(lane < MLA_ROPE, s, 0.0)


def _mla_prep_kernel(cq_ref, ckv_ref, kr_ref, gq_ref, gkv_ref, wq_ref, wkv_ref, cs_ref, q_ref, k_ref, v_ref):
    cs = cs_ref[...]
    hq = _rms_bf16(cq_ref[...], gq_ref[...])
    q = jnp.dot(hq, wq_ref[...], preferred_element_type=F32)
    hkv = _rms_bf16(ckv_ref[...], gkv_ref[...])
    kv = jnp.dot(hkv, wkv_ref[...], preferred_element_type=F32)
    k_rope = _rope_group(kr_ref[...].astype(F32), cs).astype(BF16)
    for h in range(MLA_HEADS):
        o = h * MLA_HW
        q_ref[:, o:o + LANES] = q[:, o:o + LANES].astype(BF16)
        q_ref[:, o + LANES:o + MLA_HW] = _rope_group(q[:, o + LANES:o + MLA_HW], cs).astype(BF16)
        k_ref[:, o:o + LANES] = kv[:, o:o + LANES].astype(BF16)
        k_ref[:, o + LANES:o + MLA_HW] = k_rope
        v_ref[:, h * MLA_V:(h + 1) * MLA_V] = kv[:, o + LANES:o + MLA_HW].astype(BF16)


def _mla_attn_kernel(q_ref, k_ref, v_ref, o_ref):
    scale = (MLA_NOPE + MLA_ROPE) ** -0.5
    tq = q_ref.shape[1]
    rows = tq // MLA_QSPLIT
    k = k_ref[0]
    v = v_ref[0]
    parts = range(MLA_QSPLIT)
    s = [lax.dot_general(q_ref[0, t * rows:(t + 1) * rows, :], k, (((1,), (1,)), ((), ())),
                         preferred_element_type=F32) for t in parts]
    for t in parts:
        p = jnp.exp((s[t] - jnp.max(s[t], axis=-1, keepdims=True)) * scale)
        l = jnp.sum(p, axis=-1, keepdims=True)
        o = jnp.dot(p.astype(BF16), v, preferred_element_type=F32) / l
        o_ref[0, t * rows:(t + 1) * rows, :] = o.astype(o_ref.dtype)


def mla_mixer(z, b, l, g_q, g_kv, wq_p, w_ukv, cs_tab, tm, tq):
    m = z.shape[0]
    lt = l // tm
    qp, kp, vp = pl.pallas_call(
        _mla_prep_kernel,
        grid=(m // tm,),
        in_specs=[pl.BlockSpec((tm, MLA_Q_RANK), lambda i: (i, Z_CQ // MLA_Q_RANK)),
                  pl.BlockSpec((tm, MLA_KV_RANK), lambda i: (i, Z_CKV // MLA_KV_RANK)),
                  pl.BlockSpec((tm, LANES), lambda i: (i, Z_KR // LANES)),
                  pl.BlockSpec((1, MLA_Q_RANK), lambda i: (0, 0)),
                  pl.BlockSpec((1, MLA_KV_RANK), lambda i: (0, 0)),
                  pl.BlockSpec(wq_p.shape, lambda i: (0, 0)),
                  pl.BlockSpec(w_ukv.shape, lambda i: (0, 0)),
                  pl.BlockSpec((tm, LANES), lambda i: (i % lt, 0))],
        out_specs=[pl.BlockSpec((tm, MLA_HEADS * MLA_HW), lambda i: (i, 0)),
                   pl.BlockSpec((tm, MLA_HEADS * MLA_HW), lambda i: (i, 0)),
                   pl.BlockSpec((tm, MLA_HEADS * MLA_V), lambda i: (i, 0))],
        out_shape=[jax.ShapeDtypeStruct((m, MLA_HEADS * MLA_HW), BF16),
                   jax.ShapeDtypeStruct((m, MLA_HEADS * MLA_HW), BF16),
                   jax.ShapeDtypeStruct((m, MLA_HEADS * MLA_V), BF16)],
        compiler_params=_params("parallel"),
        name="mla_prep",
    )(z, z, z, g_q.reshape(1, -1), g_kv.reshape(1, -1), wq_p, w_ukv, cs_tab)
    qp = qp.reshape(b, l, -1)
    kp = kp.reshape(b, l, -1)
    vp = vp.reshape(b, l, -1)
    return pl.pallas_call(
        _mla_attn_kernel,
        grid=(b, MLA_HEADS, l // tq),
        in_specs=[pl.BlockSpec((1, tq, MLA_HW), lambda i, h, t: (i, t, h)),
                  pl.BlockSpec((1, l, MLA_HW), lambda i, h, t: (i, 0, h)),
                  pl.BlockSpec((1, l, MLA_V), lambda i, h, t: (i, 0, h))],
        out_specs=pl.BlockSpec((1, tq, MLA_V), lambda i, h, t: (i, t, h)),
        out_shape=jax.ShapeDtypeStruct((b, l, MLA_HEADS * MLA_V), BF16),
        compiler_params=_params("parallel", "parallel", "arbitrary"),
        name="mla_attention",
    )(qp, kp, vp)


def _split_bf16(x):
    hi = x.astype(BF16)
    return hi, (x - hi.astype(F32)).astype(BF16)


def _dot_split(a, b):
    (ah, al), (bh, bl) = a, b
    return (jnp.dot(ah, bh, preferred_element_type=F32)
            + (jnp.dot(ah, bl, preferred_element_type=F32) + jnp.dot(al, bh, preferred_element_type=F32)))


def _tri_unit_inverse(mats):
    c = mats[0].shape[0]
    eye = (lax.broadcasted_iota(jnp.int32, (c, c), 0) == lax.broadcasted_iota(jnp.int32, (c, c), 1)).astype(F32)
    ps = [eye - a for a in mats]
    ms = list(mats)
    for _ in range(int(math.log2(c)) - 1):
        splits = [_split_bf16(m) for m in ms]
        ms = [_dot_split(sp, sp) for sp in splits]
        ps = [_dot_split(_split_bf16(p), _split_bf16(eye + m)) for p, m in zip(ps, ms)]
    return ps


def _bdot(a, b):
    return jnp.dot(a.astype(BF16), b.astype(BF16), preferred_element_type=F32)


def _bdot_nt(a, b):
    return lax.dot_general(a.astype(BF16), b.astype(BF16), (((1,), (1,)), ((), ())), preferred_element_type=F32)


def _dn_group_setup(chunks, refs):
    q_s, k_s, v_s, g_s, b_s = refs
    cc = DN_CHUNK
    n = range(len(chunks))
    rev = [r for _, r in chunks]
    r0 = [pl.multiple_of(c * cc, cc) for c, _ in chunks]
    q = [q_s[pl.ds(r, cc), :] for r in r0]
    k = [k_s[pl.ds(r, cc), :] for r in r0]
    v = [v_s[pl.ds(r, cc), :] for r in r0]
    lane = [2 if r else 0 for r in rev]
    ri = lax.broadcasted_iota(jnp.int32, (cc, cc), 0)
    ci = lax.broadcasted_iota(jnp.int32, (cc, cc), 1)
    incl = [(ri <= ci) if r else (ri >= ci) for r in rev]
    strict = [(ri < ci) if r else (ri > ci) for r in rev]
    g = [g_s[pl.ds(r, cc), :] for r in r0]
    g1 = [x.astype(BF16) for x in g]
    r1 = [x - h.astype(F32) for x, h in zip(g, g1)]
    g2 = [x.astype(BF16) for x in r1]
    g3 = [(x - h.astype(F32)).astype(BF16) for x, h in zip(r1, g2)]
    ones = [m.astype(BF16) for m in incl]
    gc = [jnp.dot(ones[i], g1[i], preferred_element_type=F32) for i in n]
    gc = [gc[i] + jnp.dot(ones[i], g2[i], preferred_element_type=F32) for i in n]
    gc = [gc[i] + jnp.dot(ones[i], g3[i], preferred_element_type=F32) for i in n]
    gcol = [gc[i][:, lane[i]:lane[i] + 1] for i in n]
    grow = [jnp.transpose(gc[i])[lane[i]:lane[i] + 1, :] for i in n]
    beta = [b_s[pl.ds(r0[i], cc), :][:, lane[i] + 1:lane[i] + 2] for i in n]
    g_last = [gcol[i][0:1, :] if rev[i] else gcol[i][cc - 1:cc, :] for i in n]
    decay = [jnp.where(incl[i], jnp.exp(jnp.where(incl[i], gcol[i] - grow[i], 0.0)), 0.0) for i in n]
    e_g = [jnp.exp(x) for x in gcol]
    kb = [k[i] * beta[i] for i in n]
    a_both = [_bdot_nt(jnp.concatenate([kb[i], q[i]], axis=0), k[i]) for i in n]
    a_kk = [jnp.where(strict[i], a_both[i][:cc] * decay[i], 0.0) for i in n]
    a_qk = [jnp.where(incl[i], a_both[i][cc:] * decay[i], 0.0) for i in n]
    t_inv = _tri_unit_inverse(a_kk)
    uw = [_bdot(t_inv[i], jnp.concatenate([v[i] * beta[i], kb[i] * e_g[i]], axis=1)) for i in n]
    k_dec = [k[i] * jnp.exp(g_last[i] - gcol[i]) for i in n]
    wq = [jnp.concatenate([uw[i][:, DN_DV:], q[i] * e_g[i]], axis=0).astype(BF16) for i in n]
    ak = [jnp.concatenate([a_qk[i], jnp.transpose(k_dec[i])], axis=0).astype(BF16) for i in n]
    return [(wq[i], ak[i], uw[i][:, :DN_DV], jnp.exp(g_last[i])) for i in n]


def _dn_chunk_step(states, setups):
    cc = DN_CHUNK
    n = range(len(states))
    r = [jnp.dot(setups[i][0], states[i].astype(BF16), preferred_element_type=F32) for i in n]
    v_new = [setups[i][2] - r[i][:cc] for i in n]
    r2 = [jnp.dot(setups[i][1], v_new[i].astype(BF16), preferred_element_type=F32) for i in n]
    return ([states[i] * setups[i][3] + r2[i][cc:] for i in n], [r[i][cc:] + r2[i][:cc] for i in n])


def _dn_kernel(zq_ref, zk_ref, zv_ref, zg_ref, ab_ref, wq_ref, wk_ref, wv_ref, alog_ref, dtb_ref, gn_ref, o_ref,
               q_s, k_s, v_s, g_s, b_s, of_s, ob_s, *, nchunks):
    def act(z_ref, w_ref):
        c = _dwconv3(z_ref[0].astype(F32), w_ref)
        return c * jax.nn.sigmoid(c)

    def l2n(x):
        return x * lax.rsqrt(jnp.sum(x * x, axis=-1, keepdims=True) + RMS_EPS)

    q_s[...] = l2n(act(zq_ref, wq_ref)) * (DN_DK ** -0.5)
    k_s[...] = l2n(act(zk_ref, wk_ref))
    v_s[...] = act(zv_ref, wv_ref)
    ab = ab_ref[0]
    x = ab + dtb_ref[0]
    softplus = jnp.maximum(x, 0.0) + jnp.log1p(jnp.exp(-jnp.abs(x)))
    g_s[...] = -jnp.exp(alog_ref[0]) * softplus
    b_s[...] = jax.nn.sigmoid(ab)
    refs = (q_s, k_s, v_s, g_s, b_s)

    def body(i, carry):
        s_f, s_b = carry
        fwd = [i * DN_GROUP + t for t in range(DN_GROUP)]
        bwd = [nchunks - 1 - c for c in fwd]
        setups = _dn_group_setup([(c, False) for c in fwd] + [(c, True) for c in bwd], refs)
        setups_f, setups_b = setups[:DN_GROUP], setups[DN_GROUP:]
        for t in range(DN_GROUP):
            (s_f, s_b), (o_f, o_b) = _dn_chunk_step([s_f, s_b], [setups_f[t], setups_b[t]])
            of_s[pl.ds(pl.multiple_of(fwd[t] * DN_CHUNK, DN_CHUNK), DN_CHUNK), :] = o_f
            ob_s[pl.ds(pl.multiple_of(bwd[t] * DN_CHUNK, DN_CHUNK), DN_CHUNK), :] = o_b
        return s_f, s_b

    zero = jnp.zeros((DN_DK, DN_DV), F32)
    lax.fori_loop(0, nchunks // DN_GROUP, body, (zero, zero))

    o = of_s[...] + ob_s[...]
    o = o * lax.rsqrt(jnp.mean(o * o, axis=-1, keepdims=True) + RMS_EPS) * gn_ref[...]
    gate = zg_ref[0].astype(F32)
    o_ref[0] = (o * (gate * jax.nn.sigmoid(gate))).astype(o_ref.dtype)


def deltanet_mixer(z3, ab3, w_conv, alog_p, dtb_p, g_norm):
    b, l, _ = z3.shape
    blk = Z_DNQKV // LANES
    gblk = Z_DNGATE // LANES
    hh = DN_HEADS

    def zspec(off):
        return pl.BlockSpec((1, l, LANES), lambda i, h: (i, 0, off + h))

    def wspec(seg):
        return pl.BlockSpec((3, LANES), lambda i, h: (0, seg * hh + h))

    vec = pl.BlockSpec((1, 1, LANES), lambda i, h: (h, 0, 0))
    seq = pltpu.VMEM((l, LANES), F32)
    return pl.pallas_call(
        functools.partial(_dn_kernel, nchunks=l // DN_CHUNK),
        grid=(b, hh),
        in_specs=[zspec(blk), zspec(blk + hh), zspec(blk + 2 * hh), zspec(gblk),
                  pl.BlockSpec((1, l, LANES), lambda i, h: (i, 0, h)),
                  wspec(0), wspec(1), wspec(2), vec, vec,
                  pl.BlockSpec((1, LANES), lambda i, h: (0, 0))],
        out_specs=pl.BlockSpec((1, l, LANES), lambda i, h: (i, 0, h)),
        out_shape=jax.ShapeDtypeStruct((b, l, hh * DN_DV), BF16),
        scratch_shapes=[seq] * 7,
        compiler_params=_params("parallel", "arbitrary"),
        name="deltanet_mixer",
    )(z3, z3, z3, z3, ab3, w_conv, w_conv, w_conv, alog_p, dtb_p, g_norm.reshape(1, -1))


def _rotate_half_cols(w):
    half = w.shape[-1] // 2
    return jnp.concatenate([-w[..., half:], w[..., :half]], axis=-1)


def _prep_w_in(w_in):
    cuts = np.cumsum([0, 1536, 1536, MLA_Q_RANK, MLA_KV_RANK, MLA_ROPE, 1536, 512, 16])
    na, hy, cq, ckv, kr, dnqkv, dngate, dnab = (w_in[:, cuts[i]:cuts[i + 1]] for i in range(8))
    gates = w_in[:, cuts[8]:]
    pad = jnp.zeros((w_in.shape[0], Z_GATE - Z_KR - 2 * MLA_ROPE), w_in.dtype)
    main = jnp.concatenate([na, hy, dnqkv, dngate, cq, ckv, kr, _rotate_half_cols(kr), pad, gates], axis=1)
    ab = dnab.reshape(-1, 2, 2, DN_HEADS)
    ab = jnp.transpose(ab, (0, 3, 1, 2)).reshape(-1, DN_HEADS, 4)
    ab = jnp.pad(ab, ((0, 0), (0, 0), (0, LANES - 4))).reshape(-1, DN_HEADS * LANES)
    return main.astype(BF16), ab.astype(BF16)


def _prep_w_uq(w_uq):
    k = w_uq.shape[0]
    w = w_uq.reshape(k, MLA_HEADS, MLA_NOPE + MLA_ROPE)
    rope_w = w[:, :, MLA_NOPE:]
    return jnp.concatenate([w, _rotate_half_cols(rope_w)], axis=-1).reshape(k, MLA_HEADS * MLA_HW).astype(BF16)


def _head_lane_vec(p):
    v = jnp.zeros((DN_HEADS, 1, LANES), F32)
    v = v.at[:, 0, 0].set(p[0].astype(F32))
    return v.at[:, 0, 2].set(p[1].astype(F32))


def _rope_table(l):
    half = MLA_ROPE // 2
    inv = ROPE_THETA ** (-jnp.arange(half, dtype=F32) / half)
    ang = jnp.arange(l, dtype=F32)[:, None] * inv[None, :]
    cos, sin = jnp.cos(ang), jnp.sin(ang)
    return jnp.concatenate([cos, cos, sin, sin], axis=-1)


def _pick(n, pref):
    for t in pref:
        if n % t == 0:
            return t
    return n


def trunk(x3, norm_mix, w_in, na_rpb, hy_short, hy_skip, hy_w1, hy_b1, hy_w2, hy_b2, hy_w3,
          mla_g_q, mla_g_kv, mla_w_uq, mla_w_ukv, dn_conv, dn_a_log, dn_dt_bias, dn_g_norm,
          w_branch, w_out, norm_mlp, w_up, w_down, norm_final):
    b, l, d = x3.shape
    m = b * l
    depth = w_in.shape[0]
    x = x3.reshape(m, d)
    tm_big = _pick(m, (1024, 512, 256, 128))
    tm_mid = _pick(m, (512, 256, 128))
    tm_small = _pick(m, (256, 128))
    fa, ga, fb, fbi = _fft_tables(l)
    n1 = 2 * l // FFT_N2
    tables = (jnp.asarray(fa[:, :, :n1 // 2], BF16), jnp.asarray(ga[:, :n1 // 2, :], BF16),
              jnp.asarray(fb, BF16), jnp.asarray(fbi, BF16))
    fa32, fb32 = jnp.asarray(fa, F32), jnp.asarray(fb, F32)
    cs_tab = _rope_table(l)
    for layer in range(depth):
        w_main, w_ab = _prep_w_in(w_in[layer])
        z = norm_matmul(x, norm_mix[layer], w_main, BF16, tm_big, 1024)
        ab = norm_matmul(x, norm_mix[layer], w_ab, F32, tm_big, DN_HEADS * LANES)
        z3 = z.reshape(b, l, Z_COLS)
        br_a = neighbourhood_attention(z3, na_bias_table(na_rpb[layer]), _pick(l // GRID_W, (8, 4, 2, 1)))
        k_two = hyena_filters(l, hy_w1[layer], hy_b1[layer], hy_w2[layer], hy_b2[layer], hy_w3[layer])
        kf = filter_spectrum(k_two, fa32, fb32)
        br_b = hyena_mixer(z3, hy_short[layer], hy_skip[layer], kf, tables)
        br_c = mla_mixer(z, b, l, mla_g_q[layer], mla_g_kv[layer], _prep_w_uq(mla_w_uq[layer]),
                         mla_w_ukv[layer].astype(BF16), cs_tab, tm_mid, _pick(l, (512, 256, 128)))
        br_d = deltanet_mixer(z3, ab.reshape(b, l, -1), dn_conv[layer], _head_lane_vec(dn_a_log[layer]),
                              _head_lane_vec(dn_dt_bias[layer]), dn_g_norm[layer])
        branches = [t.reshape(m, BRANCH_W) for t in (br_a, br_b, br_c, br_d)]
        merged = gated_merge(z, branches, w_branch[layer].astype(BF16), tm_small)
        x = matmul_residual(merged, w_out[layer].astype(BF16), x, tm_mid, 1024)
        x = mlp_block(x, norm_mlp[layer], w_up[layer].astype(BF16), w_down[layer].astype(BF16), norm_final,
                      layer == depth - 1, tm_mid, 1024)
    return x.reshape(b, l, d)


def kernel(x_prompt, x_sample, norm_mix, w_in, na_rpb, hy_short, hy_skip, hy_w1, hy_b1, hy_w2, hy_b2, hy_w3,
           mla_g_q, mla_g_kv, mla_w_uq, mla_w_ukv, dn_conv, dn_a_log, dn_dt_bias, dn_g_norm,
           w_branch, w_out, norm_mlp, w_up, w_down, norm_final):
    assert x_prompt.shape[1:] == x_sample.shape[1:]
    nb = x_prompt.shape[0]
    y = trunk(jnp.concatenate([x_prompt, x_sample], axis=0), norm_mix, w_in, na_rpb, hy_short, hy_skip,
              hy_w1, hy_b1, hy_w2, hy_b2, hy_w3, mla_g_q, mla_g_kv, mla_w_uq, mla_w_ukv, dn_conv, dn_a_log,
              dn_dt_bias, dn_g_norm, w_branch, w_out, norm_mlp, w_up, w_down, norm_final)
    return (y[:nb], y[nb:])
```

```python
import functools
import math

import jax
import jax.numpy as jnp
import numpy as np
from jax import lax
from jax.experimental import pallas as pl
from jax.experimental.pallas import tpu as pltpu

F32 = jnp.float32
BF16 = jnp.bfloat16
HIGHEST = lax.Precision.HIGHEST

VMEM_LIMIT_BYTES = 56 * 1024 * 1024
LANES = 128

D_MODEL = 2048
RMS_EPS = 1e-6
GRID_W = 64
N_BRANCH = 4
BRANCH_W = 512
NA_HEADS = 8
NA_HEAD_DIM = 64
NA_WIN_R = 8
NA_WIN_C = 16
HY_WIDTH = 512
HY_POS_BANDS = 16
HY_FILT_HIDDEN = 64
HY_FAST_DECAY = 0.3
HY_SLOW_DECAY = 1.5
HY_DECAY_TARGET = 1e-2
MLA_HEADS = 4
MLA_Q_RANK = 512
MLA_KV_RANK = 256
MLA_NOPE = 128
MLA_ROPE = 64
MLA_V = 128
ROPE_THETA = 10000.0
DN_HEADS = 4
DN_DK = 128
DN_DV = 128
DN_CHUNK = 64
DN_GROUP = 4
D_FF = 4 * D_MODEL

Z_NA = 0
Z_HY = 1536
Z_DNQKV = 3072
Z_DNGATE = 4608
Z_CQ = 5120
Z_CKV = 5632
Z_KR = 5888
Z_GATE = 6144
Z_COLS = Z_GATE + N_BRANCH * D_MODEL

FFT_N2 = 128
FFT_UNROLL = 8
FFT_GROUP = 4
FFT_PITCH_X = FFT_N2 + 8
FFT_PITCH_Y = 2 * FFT_N2 + 8


def _params(*sem):
    return pltpu.CompilerParams(dimension_semantics=sem, vmem_limit_bytes=VMEM_LIMIT_BYTES)


def _rms_bf16(x, g):
    xf = x.astype(F32)
    y = xf * lax.rsqrt(jnp.mean(xf * xf, axis=-1, keepdims=True) + RMS_EPS)
    return (y * g).astype(BF16)


def _norm_mm_kernel(x_ref, g_ref, w_ref, o_ref, h_ref):
    @pl.when(pl.program_id(1) == 0)
    def _():
        h_ref[...] = _rms_bf16(x_ref[...], g_ref[...])

    o_ref[...] = jnp.dot(h_ref[...], w_ref[...], preferred_element_type=F32).astype(o_ref.dtype)


def norm_matmul(x, g, w, out_dtype, tm, tn):
    m, k = x.shape
    n = w.shape[1]
    return pl.pallas_call(
        _norm_mm_kernel,
        grid=(m // tm, n // tn),
        in_specs=[pl.BlockSpec((tm, k), lambda i, j: (i, 0)),
                  pl.BlockSpec((1, k), lambda i, j: (0, 0)),
                  pl.BlockSpec((k, tn), lambda i, j: (0, j))],
        out_specs=pl.BlockSpec((tm, tn), lambda i, j: (i, j)),
        out_shape=jax.ShapeDtypeStruct((m, n), out_dtype),
        scratch_shapes=[pltpu.VMEM((tm, k), BF16)],
        compiler_params=_params("parallel", "arbitrary"),
        name="norm_matmul",
    )(x, g.reshape(1, k), w)


def _mm_res_kernel(a_ref, w_ref, r_ref, o_ref):
    o_ref[...] = r_ref[...] + jnp.dot(a_ref[...], w_ref[...], preferred_element_type=F32)


def matmul_residual(a, w, r, tm, tn):
    m, k = a.shape
    n = w.shape[1]
    return pl.pallas_call(
        _mm_res_kernel,
        grid=(m // tm, n // tn),
        in_specs=[pl.BlockSpec((tm, k), lambda i, j: (i, 0)),
                  pl.BlockSpec((k, tn), lambda i, j: (0, j)),
                  pl.BlockSpec((tm, tn), lambda i, j: (i, j))],
        out_specs=pl.BlockSpec((tm, tn), lambda i, j: (i, j)),
        out_shape=jax.ShapeDtypeStruct((m, n), F32),
        compiler_params=_params("parallel", "arbitrary"),
        name="matmul_residual",
    )(a, w, r)


def _mlp_kernel(x_ref, g_ref, wu_ref, wd_ref, gf_ref, o_ref, h_ref, *, final_norm):
    j = pl.program_id(1)

    @pl.when(j == 0)
    def _():
        x = x_ref[...]
        h_ref[...] = _rms_bf16(x, g_ref[...])
        o_ref[...] = x

    u = jnp.dot(h_ref[...], wu_ref[...], preferred_element_type=F32)
    a = jnp.square(jnp.maximum(u, 0.0)).astype(BF16)
    o_ref[...] += jnp.dot(a, wd_ref[...], preferred_element_type=F32)

    if final_norm:
        @pl.when(j == pl.num_programs(1) - 1)
        def _():
            y = o_ref[...]
            o_ref[...] = y * lax.rsqrt(jnp.mean(y * y, axis=-1, keepdims=True) + RMS_EPS) * gf_ref[...]


def mlp_block(x, g, w_up, w_down, g_final, final_norm, tm, tf):
    m, d = x.shape
    f = w_up.shape[1]
    return pl.pallas_call(
        functools.partial(_mlp_kernel, final_norm=final_norm),
        grid=(m // tm, f // tf),
        in_specs=[pl.BlockSpec((tm, d), lambda i, j: (i, 0)),
                  pl.BlockSpec((1, d), lambda i, j: (0, 0)),
                  pl.BlockSpec((d, tf), lambda i, j: (0, j)),
                  pl.BlockSpec((tf, d), lambda i, j: (j, 0)),
                  pl.BlockSpec((1, d), lambda i, j: (0, 0))],
        out_specs=pl.BlockSpec((tm, d), lambda i, j: (i, 0)),
        out_shape=jax.ShapeDtypeStruct((m, d), F32),
        scratch_shapes=[pltpu.VMEM((tm, d), BF16)],
        compiler_params=_params("parallel", "arbitrary"),
        name="mlp_block",
    )(x, g.reshape(1, d), w_up, w_down, g_final.reshape(1, d))


def _merge_kernel(gate_ref, ba_ref, bb_ref, bc_ref, bd_ref, wb_ref, o_ref, acc_ref):
    n = pl.program_id(1)

    def contribution(b_ref):
        y = jnp.dot(b_ref[...], wb_ref[0], preferred_element_type=F32)
        return (0.5 * jnp.tanh(0.5 * gate_ref[...].astype(F32)) + 0.5) * y

    @pl.when(n == 0)
    def _():
        acc_ref[...] = contribution(ba_ref)

    @pl.when(n == 1)
    def _():
        acc_ref[...] += contribution(bb_ref)

    @pl.when(n == 2)
    def _():
        acc_ref[...] += contribution(bc_ref)

    @pl.when(n == 3)
    def _():
        o_ref[...] = (acc_ref[...] + contribution(bd_ref)).astype(o_ref.dtype)


def gated_merge(z, branches, w_branch, tm):
    m = z.shape[0]
    gate_blk0 = Z_GATE // D_MODEL
    br_spec = pl.BlockSpec((tm, BRANCH_W), lambda i, n: (i, 0))
    return pl.pallas_call(
        _merge_kernel,
        grid=(m // tm, N_BRANCH),
        in_specs=[pl.BlockSpec((tm, D_MODEL), lambda i, n: (i, gate_blk0 + n)),
                  br_spec, br_spec, br_spec, br_spec,
                  pl.BlockSpec((1, BRANCH_W, D_MODEL), lambda i, n: (n, 0, 0))],
        out_specs=pl.BlockSpec((tm, D_MODEL), lambda i, n: (i, 0)),
        out_shape=jax.ShapeDtypeStruct((m, D_MODEL), BF16),
        scratch_shapes=[pltpu.VMEM((tm, D_MODEL), F32)],
        compiler_params=_params("parallel", "arbitrary"),
        name="gated_merge",
    )(z, *branches, w_branch)


NA_MASK = -1e30


def na_bias_table(rpb):
    q = np.arange(GRID_W)
    kc = np.arange(GRID_W)
    cs = np.clip(q - NA_WIN_C // 2, 0, GRID_W - NA_WIN_C)
    ok = (kc[None, :] >= cs[:, None]) & (kc[None, :] < cs[:, None] + NA_WIN_C)
    dcol = np.clip(kc[None, :] - q[:, None] + NA_WIN_C - 1, 0, 2 * NA_WIN_C - 2)
    drow = np.arange(NA_WIN_R)[None, :] - np.arange(NA_WIN_R)[:, None] + NA_WIN_R - 1
    t = rpb.astype(F32)[:, drow]
    t = t[:, :, :, dcol]
    t = jnp.transpose(t, (1, 0, 3, 2, 4))
    t = jnp.where(jnp.asarray(ok)[None, None, :, None, :], t, NA_MASK)
    return t.reshape(NA_WIN_R, NA_HEADS, GRID_W, NA_WIN_R * GRID_W)


def _na_kernel(q_ref, k_ref, v_ref, bias_ref, o_ref, *, rows_per_step, rows):
    rblk = pl.program_id(1)
    win = NA_WIN_R * GRID_W
    scale = NA_HEAD_DIM ** -0.5

    def row_body(rl, carry):
        r = rblk * rows_per_step + rl
        rs = jnp.clip(r - NA_WIN_R // 2, 0, rows - NA_WIN_R)
        off = r - rs
        qrow = q_ref[0, pl.ds(pl.multiple_of(rl * GRID_W, GRID_W), GRID_W), :]
        kstart = pl.multiple_of(rs * GRID_W, GRID_W)
        kb = k_ref[0, pl.ds(kstart, win), :]
        vb = v_ref[0, pl.ds(kstart, win), :]
        heads = range(NA_HEADS)
        sl = [slice(h * NA_HEAD_DIM, (h + 1) * NA_HEAD_DIM) for h in heads]
        s = [lax.dot_general(qrow[:, sl[h]], kb[:, sl[h]], (((1,), (1,)), ((), ())), preferred_element_type=F32)
             for h in heads]
        s = [s[h] * scale + bias_ref[off, h] for h in heads]
        p = [jnp.exp(s[h] - jnp.max(s[h], axis=-1, keepdims=True)) for h in heads]
        l = [jnp.sum(p[h], axis=-1, keepdims=True) for h in heads]
        outs = [jnp.dot(p[h].astype(BF16), vb[:, sl[h]], preferred_element_type=F32) for h in heads]
        outs = [outs[h] / l[h] for h in heads]
        o_ref[0, pl.ds(pl.multiple_of(rl * GRID_W, GRID_W), GRID_W), :] = (
            jnp.concatenate(outs, axis=1).astype(o_ref.dtype))
        return carry

    lax.fori_loop(0, rows_per_step, row_body, 0)


def neighbourhood_attention(z3, bias, rows_per_step):
    b, l, _ = z3.shape
    rows = l // GRID_W
    w = NA_HEADS * NA_HEAD_DIM
    blk = Z_NA // w
    return pl.pallas_call(
        functools.partial(_na_kernel, rows_per_step=rows_per_step, rows=rows),
        grid=(b, rows // rows_per_step),
        in_specs=[pl.BlockSpec((1, rows_per_step * GRID_W, w), lambda i, r: (i, r, blk)),
                  pl.BlockSpec((1, l, w), lambda i, r: (i, 0, blk + 1)),
                  pl.BlockSpec((1, l, w), lambda i, r: (i, 0, blk + 2)),
                  pl.BlockSpec(bias.shape, lambda i, r: (0, 0, 0, 0))],
        out_specs=pl.BlockSpec((1, rows_per_step * GRID_W, w), lambda i, r: (i, r, 0)),
        out_shape=jax.ShapeDtypeStruct((b, l, w), BF16),
        compiler_params=_params("parallel", "arbitrary"),
        name="neighbourhood_attention",
    )(z3, z3, z3, bias)


def _fft_tables(l):
    n = 2 * l
    n2 = FFT_N2
    n1 = n // n2
    a = np.arange(n1)
    b = np.arange(n2)
    k1 = np.arange(n1)
    t = n2 * a[None, :] + b[:, None]
    th = 2.0 * np.pi * (k1[None, :, None] * t[:, None, :] % n) / n
    fa = np.concatenate([np.cos(th), -np.sin(th)], axis=1)
    th_t = np.transpose(th, (0, 2, 1))
    ga = np.concatenate([np.cos(th_t), -np.sin(th_t)], axis=2)
    ph = 2.0 * np.pi * (np.outer(b, b) % n2) / n2
    cr, ci = np.cos(ph), -np.sin(ph)
    fb = np.block([[cr, -ci], [ci, cr]])
    fbi = np.block([[cr, ci], [-ci, cr]])
    return fa, ga, fb, fbi


def _filter_fft_kernel(k_ref, fa_ref, fb_ref, o_ref, y_ref, *, n1):
    n2 = FFT_N2

    def step_a(b, carry):
        xb = k_ref[pl.ds(b, n1, stride=n2), :]
        r = jnp.dot(fa_ref[b], xb, precision=HIGHEST, preferred_element_type=F32)
        y_ref[pl.ds(b, n1, stride=2 * n2), :] = r[:n1]
        y_ref[pl.ds(b + n2, n1, stride=2 * n2), :] = r[n1:]
        return carry

    lax.fori_loop(0, n2, step_a, 0, unroll=2)

    def step_b(k1, carry):
        r0 = pl.multiple_of(k1 * 2 * n2, 2 * n2)
        x = jnp.dot(fb_ref[...], y_ref[pl.ds(r0, 2 * n2), :], precision=HIGHEST, preferred_element_type=F32)
        o_ref[pl.ds(r0, 2 * n2), :] = x * (1.0 / (n1 * n2))
        return carry

    lax.fori_loop(0, n1, step_b, 0, unroll=2)


def filter_spectrum(k_two, fa, fb):
    n, c = k_two.shape
    n1 = n // FFT_N2
    return pl.pallas_call(
        functools.partial(_filter_fft_kernel, n1=n1),
        grid=(c // LANES,),
        in_specs=[pl.BlockSpec((n, LANES), lambda i: (0, i)),
                  pl.BlockSpec(fa.shape, lambda i: (0, 0, 0)),
                  pl.BlockSpec(fb.shape, lambda i: (0, 0))],
        out_specs=pl.BlockSpec((2 * n, LANES), lambda i: (0, i)),
        out_shape=jax.ShapeDtypeStruct((2 * n, c), F32),
        scratch_shapes=[pltpu.VMEM((2 * n, LANES), F32)],
        compiler_params=_params("parallel"),
        name="hyena_filter_spectrum",
    )(k_two, fa, fb)


def _hyena_filter_kernel(z_ref, w1_ref, b1_ref, w2_ref, b2_ref, w3_ref, dec_ref, o_ref):
    hid = jnp.sin(jnp.dot(z_ref[...], w1_ref[...], precision=HIGHEST, preferred_element_type=F32) + b1_ref[...])
    hid = jnp.sin(jnp.dot(hid, w2_ref[...], precision=HIGHEST, preferred_element_type=F32) + b2_ref[...])
    h = jnp.dot(hid, w3_ref[...], precision=HIGHEST, preferred_element_type=F32)
    dec = dec_ref[...]
    hf = h[:, :HY_WIDTH] * dec
    hb = h[:, HY_WIDTH:] * dec
    norm = (jnp.sum(jnp.abs(hf), axis=0, keepdims=True) + jnp.sum(jnp.abs(hb), axis=0, keepdims=True)) + RMS_EPS
    o_ref[:, :HY_WIDTH] = hf / norm
    o_ref[:, HY_WIDTH:] = hb / norm


def hyena_filters(l, w1, b1, w2, b2, w3):
    t = jnp.linspace(0.0, 1.0, l, dtype=F32)[:, None]
    w = 2.0 * math.pi * jnp.arange(l, dtype=F32)[:, None] / l
    bands = jnp.linspace(1e-4, HY_POS_BANDS - 1, HY_POS_BANDS, dtype=F32)[None, :]
    z = jnp.concatenate([t, jnp.cos(bands * w), -jnp.sin(bands * w)], axis=-1)
    pad = LANES - z.shape[1]
    z = jnp.pad(z, ((0, 0), (0, pad)))
    w1p = jnp.pad(w1.astype(F32), ((0, pad), (0, 0)))
    max_decay = math.log(HY_DECAY_TARGET) / HY_FAST_DECAY
    min_decay = math.log(HY_DECAY_TARGET) / HY_SLOW_DECAY
    deltas = jnp.abs(jnp.linspace(min_decay, max_decay, HY_WIDTH, dtype=F32))
    dec = jnp.exp(-t * deltas[None, :])
    h = pl.pallas_call(
        _hyena_filter_kernel,
        out_shape=jax.ShapeDtypeStruct((l, 2 * HY_WIDTH), F32),
        compiler_params=pltpu.CompilerParams(vmem_limit_bytes=VMEM_LIMIT_BYTES),
        name="hyena_filter_ffn",
    )(z, w1p, b1.reshape(1, -1).astype(F32), w2.astype(F32), b2.reshape(1, -1).astype(F32), w3.astype(F32), dec)
    h_f, h_b = h[:, :HY_WIDTH], h[:, HY_WIDTH:]
    return jnp.concatenate([h_f[:1] + h_b[:1], h_f[1:], jnp.zeros_like(h_f[:1]), h_b[:0:-1]], axis=0)


def _shift_rows(x, delta):
    n = x.shape[0]
    row = lax.broadcasted_iota(jnp.int32, x.shape, 0)
    if delta == 1:
        return jnp.where(row == 0, 0.0, pltpu.roll(x, 1, 0))
    return jnp.where(row == n - 1, 0.0, pltpu.roll(x, n - 1, 0))


def _dwconv3(x, w_ref):
    w = w_ref[...].astype(F32)
    return _shift_rows(x, 1) * w[0:1] + x * w[1:2] + _shift_rows(x, -1) * w[2:3]


def _hyena_kernel(x1_ref, x2_ref, v_ref, w1_ref, w2_ref, wv_ref, skip_ref, kf_ref, fa_ref, ga_ref, fb_ref, fbi_ref,
                  o_ref, xs_ref, y_ref, *, n1):
    n2 = FFT_N2
    na = n1 // 2
    pa, py = FFT_PITCH_X, FFT_PITCH_Y
    vg = _dwconv3(v_ref[0].astype(F32), wv_ref) * _dwconv3(x1_ref[0].astype(F32), w1_ref)
    for a in range(na):
        xs_ref[a * pa:a * pa + n2, :] = vg[a * n2:(a + 1) * n2]

    def fwd_a(b, carry):
        xb = xs_ref[pl.ds(b, na, stride=pa), :].astype(BF16)
        r = jnp.dot(fa_ref[b], xb, preferred_element_type=F32)
        y_ref[pl.ds(b, n1, stride=py), :] = r[:n1]
        y_ref[pl.ds(b + n2, n1, stride=py), :] = r[n1:]
        return carry

    lax.fori_loop(0, n2, fwd_a, 0, unroll=FFT_UNROLL)

    def mid(kg, carry):
        grp = range(FFT_GROUP)
        k1 = [kg * FFT_GROUP + t for t in grp]
        r0 = [pl.multiple_of(k * py, 8) for k in k1]
        f0 = [pl.multiple_of(k * 2 * n2, 2 * n2) for k in k1]
        fb = fb_ref[...]
        x = [jnp.dot(fb, y_ref[pl.ds(r0[t], 2 * n2), :].astype(BF16), preferred_element_type=F32) for t in grp]
        p = []
        for t in grp:
            xr, xi = x[t][:n2], x[t][n2:]
            kr = kf_ref[pl.ds(f0[t], n2), :]
            ki = kf_ref[pl.ds(f0[t] + n2, n2), :]
            p.append(jnp.concatenate([xr * kr - xi * ki, xr * ki + xi * kr], axis=0).astype(BF16))
        fbi = fbi_ref[...]
        q = [jnp.dot(fbi, p[t], preferred_element_type=F32) for t in grp]
        for t in grp:
            y_ref[pl.ds(r0[t], 2 * n2), :] = q[t]
        return carry

    lax.fori_loop(0, n1 // FFT_GROUP, mid, 0)

    def inv_a(b, carry):
        qb = jnp.concatenate([y_ref[pl.ds(b, n1, stride=py), :],
                              y_ref[pl.ds(b + n2, n1, stride=py), :]], axis=0).astype(BF16)
        xs_ref[pl.ds(b, na, stride=pa), :] = jnp.dot(ga_ref[b], qb, preferred_element_type=F32)
        return carry

    lax.fori_loop(0, n2, inv_a, 0, unroll=FFT_UNROLL)

    gate = _dwconv3(x2_ref[0].astype(F32), w2_ref)
    skip = skip_ref[...].astype(F32)
    for a in range(na):
        rows = slice(a * n2, (a + 1) * n2)
        o_ref[0, rows, :] = ((xs_ref[a * pa:a * pa + n2, :] + vg[rows] * skip) * gate[rows]).astype(o_ref.dtype)


def hyena_mixer(z3, w_short, skip, kf, tables):
    b, l, _ = z3.shape
    n1 = 2 * l // FFT_N2
    fa, ga, fb, fbi = tables
    nch = HY_WIDTH // LANES
    blk = Z_HY // LANES

    def zspec(seg):
        return pl.BlockSpec((1, l, LANES), lambda c, i: (i, 0, blk + seg * nch + c))

    def wspec(seg):
        return pl.BlockSpec((3, LANES), lambda c, i: (0, seg * nch + c))

    return pl.pallas_call(
        functools.partial(_hyena_kernel, n1=n1),
        grid=(nch, b),
        in_specs=[zspec(0), zspec(1), zspec(2), wspec(0), wspec(1), wspec(2),
                  pl.BlockSpec((1, LANES), lambda c, i: (0, c)),
                  pl.BlockSpec((4 * l, LANES), lambda c, i: (0, c)),
                  pl.BlockSpec(fa.shape, lambda c, i: (0, 0, 0)),
                  pl.BlockSpec(ga.shape, lambda c, i: (0, 0, 0)),
                  pl.BlockSpec(fb.shape, lambda c, i: (0, 0)),
                  pl.BlockSpec(fbi.shape, lambda c, i: (0, 0))],
        out_specs=pl.BlockSpec((1, l, LANES), lambda c, i: (i, 0, c)),
        out_shape=jax.ShapeDtypeStruct((b, l, HY_WIDTH), BF16),
        scratch_shapes=[pltpu.VMEM((n1 // 2 * FFT_PITCH_X, LANES), F32), pltpu.VMEM((n1 * FFT_PITCH_Y, LANES), F32)],
        compiler_params=_params("parallel", "arbitrary"),
        name="hyena_mixer",
    )(z3, z3, z3, w_short, w_short, w_short, skip.reshape(1, -1), kf, fa, ga, fb, fbi)


MLA_HW = 2 * LANES
MLA_HEADS_PER_STEP = 2


def _rope_group(g, cs):
    prod = g * cs
    s = prod + pltpu.roll(prod, MLA_ROPE, 1)
    lane = lax.broadcasted_iota(jnp.int32, s.shape, 1)
    return jnp.where(lane < MLA_ROPE, s, 0.0)


def _mla_prep_kernel(cq_ref, ckv_ref, kr_ref, gq_ref, gkv_ref, wq_ref, wkv_ref, cs_ref, q_ref, k_ref, v_ref):
    cs = cs_ref[...]
    hq = _rms_bf16(cq_ref[...], gq_ref[...])
    q = jnp.dot(hq, wq_ref[...], preferred_element_type=F32)
    hkv = _rms_bf16(ckv_ref[...], gkv_ref[...])
    kv = jnp.dot(hkv, wkv_ref[...], preferred_element_type=F32)
    k_rope = _rope_group(kr_ref[...].astype(F32), cs).astype(BF16)
    lane = lax.broadcasted_iota(jnp.int32, (cs.shape[0], LANES), 1)
    ones_lane = jnp.where(lane == 0, 1.0, 0.0).astype(BF16)
    for h in range(MLA_HEADS):
        o = h * MLA_HW
        q_ref[:, o:o + LANES] = q[:, o:o + LANES].astype(BF16)
        q_ref[:, o + LANES:o + MLA_HW] = _rope_group(q[:, o + LANES:o + MLA_HW], cs).astype(BF16)
        k_ref[:, o:o + LANES] = kv[:, o:o + LANES].astype(BF16)
        k_ref[:, o + LANES:o + MLA_HW] = k_rope
        v_ref[:, o:o + LANES] = kv[:, o + LANES:o + MLA_HW].astype(BF16)
        v_ref[:, o + LANES:o + MLA_HW] = ones_lane


def _mla_attn_kernel(q_ref, k_ref, v_ref, o_ref):
    c = (MLA_NOPE + MLA_ROPE) ** -0.5 * math.log2(math.e)
    heads = range(MLA_HEADS_PER_STEP)
    s = [lax.dot_general(q_ref[0, :, h * MLA_HW:(h + 1) * MLA_HW], k_ref[0, :, h * MLA_HW:(h + 1) * MLA_HW],
                         (((1,), (1,)), ((), ())), preferred_element_type=F32) for h in heads]
    for h in heads:
        p = jnp.exp2((s[h] - jnp.max(s[h], axis=-1, keepdims=True)) * c)
        o = jnp.dot(p.astype(BF16), v_ref[0, :, h * MLA_HW:(h + 1) * MLA_HW], preferred_element_type=F32)
        o_ref[0, :, h * MLA_V:(h + 1) * MLA_V] = (o[:, :MLA_V] / o[:, MLA_V:MLA_V + 1]).astype(o_ref.dtype)


def mla_mixer(z, b, l, g_q, g_kv, wq_p, w_ukv, cs_tab, tm, tq):
    m = z.shape[0]
    lt = l // tm
    qp, kp, vp = pl.pallas_call(
        _mla_prep_kernel,
        grid=(m // tm,),
        in_specs=[pl.BlockSpec((tm, MLA_Q_RANK), lambda i: (i, Z_CQ // MLA_Q_RANK)),
                  pl.BlockSpec((tm, MLA_KV_RANK), lambda i: (i, Z_CKV // MLA_KV_RANK)),
                  pl.BlockSpec((tm, LANES), lambda i: (i, Z_KR // LANES)),
                  pl.BlockSpec((1, MLA_Q_RANK), lambda i: (0, 0)),
                  pl.BlockSpec((1, MLA_KV_RANK), lambda i: (0, 0)),
                  pl.BlockSpec(wq_p.shape, lambda i: (0, 0)),
                  pl.BlockSpec(w_ukv.shape, lambda i: (0, 0)),
                  pl.BlockSpec((tm, LANES), lambda i: (i % lt, 0))],
        out_specs=[pl.BlockSpec((tm, MLA_HEADS * MLA_HW), lambda i: (i, 0)),
                   pl.BlockSpec((tm, MLA_HEADS * MLA_HW), lambda i: (i, 0)),
                   pl.BlockSpec((tm, MLA_HEADS * MLA_HW), lambda i: (i, 0))],
        out_shape=[jax.ShapeDtypeStruct((m, MLA_HEADS * MLA_HW), BF16),
                   jax.ShapeDtypeStruct((m, MLA_HEADS * MLA_HW), BF16),
                   jax.ShapeDtypeStruct((m, MLA_HEADS * MLA_HW), BF16)],
        compiler_params=_params("parallel"),
        name="mla_prep",
    )(z, z, z, g_q.reshape(1, -1), g_kv.reshape(1, -1), wq_p, w_ukv, cs_tab)
    qp = qp.reshape(b, l, -1)
    kp = kp.reshape(b, l, -1)
    vp = vp.reshape(b, l, -1)
    hs = MLA_HEADS_PER_STEP
    return pl.pallas_call(
        _mla_attn_kernel,
        grid=(b, MLA_HEADS // hs, l // tq),
        in_specs=[pl.BlockSpec((1, tq, hs * MLA_HW), lambda i, h, t: (i, t, h)),
                  pl.BlockSpec((1, l, hs * MLA_HW), lambda i, h, t: (i, 0, h)),
                  pl.BlockSpec((1, l, hs * MLA_HW), lambda i, h, t: (i, 0, h))],
        out_specs=pl.BlockSpec((1, tq, hs * MLA_V), lambda i, h, t: (i, t, h)),
        out_shape=jax.ShapeDtypeStruct((b, l, MLA_HEADS * MLA_V), BF16),
        compiler_params=_params("parallel", "parallel", "arbitrary"),
        name="mla_attention",
    )(qp, kp, vp)


def _split_bf16(x):
    hi = x.astype(BF16)
    return hi, (x - hi.astype(F32)).astype(BF16)


def _dot_split(a, b):
    (ah, al), (bh, bl) = a, b
    return (jnp.dot(ah, bh, preferred_element_type=F32)
            + (jnp.dot(ah, bl, preferred_element_type=F32) + jnp.dot(al, bh, preferred_element_type=F32)))


def _tri_unit_inverse(mats):
    c = mats[0].shape[0]
    eye = (lax.broadcasted_iota(jnp.int32, (c, c), 0) == lax.broadcasted_iota(jnp.int32, (c, c), 1)).astype(F32)
    ps = [eye - a for a in mats]
    ms = list(mats)
    for _ in range(int(math.log2(c)) - 1):
        splits = [_split_bf16(m) for m in ms]
        ms = [_dot_split(sp, sp) for sp in splits]
        yield
        ps = [_dot_split(_split_bf16(p), _split_bf16(eye + m)) for p, m in zip(ps, ms)]
        yield
    return ps


def _interleave(main, side=None):
    gens = [g for g in (main, side) if g is not None]
    results = [None] * len(gens)
    live = [True] * len(gens)
    while any(live):
        for idx, gen in enumerate(gens):
            if live[idx]:
                try:
                    next(gen)
                except StopIteration as stop:
                    results[idx] = stop.value
                    live[idx] = False
    return results


def _bdot(a, b):
    return jnp.dot(a.astype(BF16), b.astype(BF16), preferred_element_type=F32)


def _bdot_nt(a, b):
    return lax.dot_general(a.astype(BF16), b.astype(BF16), (((1,), (1,)), ((), ())), preferred_element_type=F32)


def _dn_group_setup(chunks, refs):
    q_s, k_s, v_s, g_s, b_s = refs
    cc = DN_CHUNK
    n = range(len(chunks))
    rev = [r for _, r in chunks]
    r0 = [pl.multiple_of(c * cc, cc) for c, _ in chunks]
    q = [q_s[pl.ds(r, cc), :] for r in r0]
    k = [k_s[pl.ds(r, cc), :] for r in r0]
    v = [v_s[pl.ds(r, cc), :] for r in r0]
    lane = [2 if r else 0 for r in rev]
    ri = lax.broadcasted_iota(jnp.int32, (cc, cc), 0)
    ci = lax.broadcasted_iota(jnp.int32, (cc, cc), 1)
    incl = [(ri <= ci) if r else (ri >= ci) for r in rev]
    strict = [(ri < ci) if r else (ri > ci) for r in rev]
    g = [g_s[pl.ds(r, cc), :] for r in r0]
    g1 = [x.astype(BF16) for x in g]
    r1 = [x - h.astype(F32) for x, h in zip(g, g1)]
    g2 = [x.astype(BF16) for x in r1]
    g3 = [(x - h.astype(F32)).astype(BF16) for x, h in zip(r1, g2)]
    ones = [m.astype(BF16) for m in incl]
    gc = [jnp.dot(ones[i], g1[i], preferred_element_type=F32) for i in n]
    gc = [gc[i] + jnp.dot(ones[i], g2[i], preferred_element_type=F32) for i in n]
    gc = [gc[i] + jnp.dot(ones[i], g3[i], preferred_element_type=F32) for i in n]
    yield
    gcol = [gc[i][:, lane[i]:lane[i] + 1] for i in n]
    grow = [jnp.transpose(gc[i])[lane[i]:lane[i] + 1, :] for i in n]
    beta = [b_s[pl.ds(r0[i], cc), :][:, lane[i] + 1:lane[i] + 2] for i in n]
    g_last = [gcol[i][0:1, :] if rev[i] else gcol[i][cc - 1:cc, :] for i in n]
    decay = [jnp.where(incl[i], jnp.exp(jnp.where(incl[i], gcol[i] - grow[i], 0.0)), 0.0) for i in n]
    e_g = [jnp.exp(x) for x in gcol]
    kb = [k[i] * beta[i] for i in n]
    a_both = [_bdot_nt(jnp.concatenate([kb[i], q[i]], axis=0), k[i]) for i in n]
    yield
    a_kk = [jnp.where(strict[i], a_both[i][:cc] * decay[i], 0.0) for i in n]
    a_qk = [jnp.where(incl[i], a_both[i][cc:] * decay[i], 0.0) for i in n]
    t_inv = yield from _tri_unit_inverse(a_kk)
    uw = [_bdot(t_inv[i], jnp.concatenate([v[i] * beta[i], kb[i] * e_g[i]], axis=1)) for i in n]
    yield
    k_dec = [k[i] * jnp.exp(g_last[i] - gcol[i]) for i in n]
    wq = [jnp.concatenate([uw[i][:, DN_DV:], q[i] * e_g[i]], axis=0).astype(BF16) for i in n]
    ak = [jnp.concatenate([a_qk[i], jnp.transpose(k_dec[i])], axis=0).astype(BF16) for i in n]
    return [(wq[i], ak[i], uw[i][:, :DN_DV], jnp.exp(g_last[i])) for i in n]


def _dn_group_steps(states, setups, out_refs, chunks):
    cc = DN_CHUNK
    dirs = range(len(states))
    for t in range(len(setups[0])):
        st = [setups[d][t] for d in dirs]
        r = [jnp.dot(st[d][0], states[d].astype(BF16), preferred_element_type=F32) for d in dirs]
        yield
        v_new = [st[d][2] - r[d][:cc] for d in dirs]
        r2 = [jnp.dot(st[d][1], v_new[d].astype(BF16), preferred_element_type=F32) for d in dirs]
        yield
        states = [states[d] * st[d][3] + r2[d][cc:] for d in dirs]
        for d in dirs:
            out_refs[d][pl.ds(pl.multiple_of(chunks[d][t] * cc, cc), cc), :] = r[d][cc:] + r2[d][:cc]
    return states


def _dn_kernel(zq_ref, zk_ref, zv_ref, zg_ref, ab_ref, wq_ref, wk_ref, wv_ref, alog_ref, dtb_ref, gn_ref, o_ref,
               q_s, k_s, v_s, g_s, b_s, of_s, ob_s, *, nchunks):
    def act(z_ref, w_ref):
        c = _dwconv3(z_ref[0].astype(F32), w_ref)
        return c * jax.nn.sigmoid(c)

    def l2n(x):
        return x * lax.rsqrt(jnp.sum(x * x, axis=-1, keepdims=True) + RMS_EPS)

    q_s[...] = l2n(act(zq_ref, wq_ref)) * (DN_DK ** -0.5)
    k_s[...] = l2n(act(zk_ref, wk_ref))
    v_s[...] = act(zv_ref, wv_ref)
    ab = ab_ref[0]
    x = ab + dtb_ref[0]
    softplus = jnp.maximum(x, 0.0) + jnp.log1p(jnp.exp(-jnp.abs(x)))
    g_s[...] = -jnp.exp(alog_ref[0]) * softplus
    b_s[...] = jax.nn.sigmoid(ab)
    refs = (q_s, k_s, v_s, g_s, b_s)

    ngroups = nchunks // DN_GROUP

    def group_chunks(gi):
        fwd = [gi * DN_GROUP + t for t in range(DN_GROUP)]
        return fwd, [nchunks - 1 - c for c in fwd]

    def setup_gen(gi):
        fwd, bwd = group_chunks(gi)
        return _dn_group_setup([(c, False) for c in fwd] + [(c, True) for c in bwd], refs)

    def steps_gen(gi, states, flat):
        setups = [tuple(flat[4 * i:4 * i + 4]) for i in range(2 * DN_GROUP)]
        return _dn_group_steps(states, [setups[:DN_GROUP], setups[DN_GROUP:]], (of_s, ob_s), group_chunks(gi))

    def flatten(setups):
        return [x for st in setups for x in st]

    def body(gi, carry):
        nxt, states = _interleave(setup_gen(gi + 1), steps_gen(gi, list(carry[:2]), carry[2:]))
        return (*states, *flatten(nxt))

    zero = jnp.zeros((DN_DK, DN_DV), F32)
    (first,) = _interleave(setup_gen(0))
    carry = lax.fori_loop(0, ngroups - 1, body, (zero, zero, *flatten(first)))
    _interleave(steps_gen(ngroups - 1, list(carry[:2]), carry[2:]))

    o = of_s[...] + ob_s[...]
    o = o * lax.rsqrt(jnp.mean(o * o, axis=-1, keepdims=True) + RMS_EPS) * gn_ref[...]
    gate = zg_ref[0].astype(F32)
    o_ref[0] = (o * (gate * jax.nn.sigmoid(gate))).astype(o_ref.dtype)


def deltanet_mixer(z3, ab3, w_conv, alog_p, dtb_p, g_norm):
    b, l, _ = z3.shape
    blk = Z_DNQKV // LANES
    gblk = Z_DNGATE // LANES
    hh = DN_HEADS

    def zspec(off):
        return pl.BlockSpec((1, l, LANES), lambda i, h: (i, 0, off + h))

    def wspec(seg):
        return pl.BlockSpec((3, LANES), lambda i, h: (0, seg * hh + h))

    vec = pl.BlockSpec((1, 1, LANES), lambda i, h: (h, 0, 0))
    seq = pltpu.VMEM((l, LANES), F32)
    return pl.pallas_call(
        functools.partial(_dn_kernel, nchunks=l // DN_CHUNK),
        grid=(b, hh),
        in_specs=[zspec(blk), zspec(blk + hh), zspec(blk + 2 * hh), zspec(gblk),
                  pl.BlockSpec((1, l, LANES), lambda i, h: (i, 0, h)),
                  wspec(0), wspec(1), wspec(2), vec, vec,
                  pl.BlockSpec((1, LANES), lambda i, h: (0, 0))],
        out_specs=pl.BlockSpec((1, l, LANES), lambda i, h: (i, 0, h)),
        out_shape=jax.ShapeDtypeStruct((b, l, hh * DN_DV), BF16),
        scratch_shapes=[seq] * 7,
        compiler_params=_params("parallel", "arbitrary"),
        name="deltanet_mixer",
    )(z3, z3, z3, z3, ab3, w_conv, w_conv, w_conv, alog_p, dtb_p, g_norm.reshape(1, -1))


def _rotate_half_cols(w):
    half = w.shape[-1] // 2
    return jnp.concatenate([-w[..., half:], w[..., :half]], axis=-1)


def _prep_w_in(w_in):
    cuts = np.cumsum([0, 1536, 1536, MLA_Q_RANK, MLA_KV_RANK, MLA_ROPE, 1536, 512, 16])
    na, hy, cq, ckv, kr, dnqkv, dngate, dnab = (w_in[:, cuts[i]:cuts[i + 1]] for i in range(8))
    gates = w_in[:, cuts[8]:]
    pad = jnp.zeros((w_in.shape[0], Z_GATE - Z_KR - 2 * MLA_ROPE), w_in.dtype)
    main = jnp.concatenate([na, hy, dnqkv, dngate, cq, ckv, kr, _rotate_half_cols(kr), pad, gates], axis=1)
    ab = dnab.reshape(-1, 2, 2, DN_HEADS)
    ab = jnp.transpose(ab, (0, 3, 1, 2)).reshape(-1, DN_HEADS, 4)
    ab = jnp.pad(ab, ((0, 0), (0, 0), (0, LANES - 4))).reshape(-1, DN_HEADS * LANES)
    return main.astype(BF16), ab.astype(BF16)


def _prep_w_uq(w_uq):
    k = w_uq.shape[0]
    w = w_uq.reshape(k, MLA_HEADS, MLA_NOPE + MLA_ROPE)
    rope_w = w[:, :, MLA_NOPE:]
    return jnp.concatenate([w, _rotate_half_cols(rope_w)], axis=-1).reshape(k, MLA_HEADS * MLA_HW).astype(BF16)


def _head_lane_vec(p):
    v = jnp.zeros((DN_HEADS, 1, LANES), F32)
    v = v.at[:, 0, 0].set(p[0].astype(F32))
    return v.at[:, 0, 2].set(p[1].astype(F32))


def _rope_table(l):
    half = MLA_ROPE // 2
    inv = ROPE_THETA ** (-jnp.arange(half, dtype=F32) / half)
    ang = jnp.arange(l, dtype=F32)[:, None] * inv[None, :]
    cos, sin = jnp.cos(ang), jnp.sin(ang)
    return jnp.concatenate([cos, cos, sin, sin], axis=-1)


def _pick(n, pref):
    for t in pref:
        if n % t == 0:
            return t
    return n


def trunk(x3, norm_mix, w_in, na_rpb, hy_short, hy_skip, hy_w1, hy_b1, hy_w2, hy_b2, hy_w3,
          mla_g_q, mla_g_kv, mla_w_uq, mla_w_ukv, dn_conv, dn_a_log, dn_dt_bias, dn_g_norm,
          w_branch, w_out, norm_mlp, w_up, w_down, norm_final):
    b, l, d = x3.shape
    m = b * l
    depth = w_in.shape[0]
    x = x3.reshape(m, d)
    tm_big = _pick(m, (1024, 512, 256, 128))
    tm_mid = _pick(m, (512, 256, 128))
    fa, ga, fb, fbi = _fft_tables(l)
    n1 = 2 * l // FFT_N2
    tables = (jnp.asarray(fa[:, :, :n1 // 2], BF16), jnp.asarray(ga[:, :n1 // 2, :], BF16),
              jnp.asarray(fb, BF16), jnp.asarray(fbi, BF16))
    fa32, fb32 = jnp.asarray(fa, F32), jnp.asarray(fb, F32)
    cs_tab = _rope_table(l)
    for layer in range(depth):
        w_main, w_ab = _prep_w_in(w_in[layer])
        z = norm_matmul(x, norm_mix[layer], w_main, BF16, tm_big, 1024)
        ab = norm_matmul(x, norm_mix[layer], w_ab, F32, tm_big, DN_HEADS * LANES)
        z3 = z.reshape(b, l, Z_COLS)
        br_a = neighbourhood_attention(z3, na_bias_table(na_rpb[layer]), _pick(l // GRID_W, (8, 4, 2, 1)))
        k_two = hyena_filters(l, hy_w1[layer], hy_b1[layer], hy_w2[layer], hy_b2[layer], hy_w3[layer])
        kf = filter_spectrum(k_two, fa32, fb32)
        br_b = hyena_mixer(z3, hy_short[layer], hy_skip[layer], kf, tables)
        br_c = mla_mixer(z, b, l, mla_g_q[layer], mla_g_kv[layer], _prep_w_uq(mla_w_uq[layer]),
                         mla_w_ukv[layer].astype(BF16), cs_tab, tm_mid, _pick(l, (512, 256, 128)))
        br_d = deltanet_mixer(z3, ab.reshape(b, l, -1), dn_conv[layer], _head_lane_vec(dn_a_log[layer]),
                              _head_lane_vec(dn_dt_bias[layer]), dn_g_norm[layer])
        branches = [t.reshape(m, BRANCH_W) for t in (br_a, br_b, br_c, br_d)]
        merged = gated_merge(z, branches, w_branch[layer].astype(BF16), tm_mid)
        x = matmul_residual(merged, w_out[layer].astype(BF16), x, tm_mid, 1024)
        x = mlp_block(x, norm_mlp[layer], w_up[layer].astype(BF16), w_down[layer].astype(BF16), norm_final,
                      layer == depth - 1, tm_mid, 1024)
    return x.reshape(b, l, d)


def kernel(x_prompt, x_sample, norm_mix, w_in, na_rpb, hy_short, hy_skip, hy_w1, hy_b1, hy_w2, hy_b2, hy_w3,
           mla_g_q, mla_g_kv, mla_w_uq, mla_w_ukv, dn_conv, dn_a_log, dn_dt_bias, dn_g_norm,
           w_branch, w_out, norm_mlp, w_up, w_down, norm_final):
    assert x_prompt.shape[1:] == x_sample.shape[1:]
    nb = x_prompt.shape[0]
    y = trunk(jnp.concatenate([x_prompt, x_sample], axis=0), norm_mix, w_in, na_rpb, hy_short, hy_skip,
              hy_w1, hy_b1, hy_w2, hy_b2, hy_w3, mla_g_q, mla_g_kv, mla_w_uq, mla_w_ukv, dn_conv, dn_a_log,
              dn_dt_bias, dn_g_norm, w_branch, w_out, norm_mlp, w_up, w_down, norm_final)
    return (y[:nb], y[nb:])
```

```python
import functools
import math

import jax
import jax.numpy as jnp
import numpy as np
from jax import lax
from jax.experimental import pallas as pl
from jax.experimental.pallas import tpu as pltpu

F32 = jnp.float32
BF16 = jnp.bfloat16
HIGHEST = lax.Precision.HIGHEST

VMEM_LIMIT_BYTES = 56 * 1024 * 1024
LANES = 128

D_MODEL = 2048
RMS_EPS = 1e-6
GRID_W = 64
N_BRANCH = 4
BRANCH_W = 512
NA_HEADS = 8
NA_HEAD_DIM = 64
NA_WIN_R = 8
NA_WIN_C = 16
HY_WIDTH = 512
HY_POS_BANDS = 16
HY_FILT_HIDDEN = 64
HY_FAST_DECAY = 0.3
HY_SLOW_DECAY = 1.5
HY_DECAY_TARGET = 1e-2
MLA_HEADS = 4
MLA_Q_RANK = 512
MLA_KV_RANK = 256
MLA_NOPE = 128
MLA_ROPE = 64
MLA_V = 128
ROPE_THETA = 10000.0
DN_HEADS = 4
DN_DK = 128
DN_DV = 128
DN_CHUNK = 64
DN_GROUP = 4
D_FF = 4 * D_MODEL

Z_NA = 0
Z_HY = 1536
Z_DNQKV = 3072
Z_DNGATE = 4608
Z_CQ = 5120
Z_CKV = 5632
Z_KR = 5888
Z_GATE = 6144
Z_COLS = Z_GATE + N_BRANCH * D_MODEL

FFT_N2 = 128
FFT_UNROLL = 8
FFT_GROUP = 4
FFT_PITCH_X = FFT_N2 + 8
FFT_PITCH_Y = 2 * FFT_N2 + 8


def _params(*sem):
    return pltpu.CompilerParams(dimension_semantics=sem, vmem_limit_bytes=VMEM_LIMIT_BYTES)


def _silu(x):
    return 0.5 * x * (jnp.tanh(0.5 * x) + 1.0)


def _rms_bf16(x, g):
    xf = x.astype(F32)
    y = xf * lax.rsqrt(jnp.mean(xf * xf, axis=-1, keepdims=True) + RMS_EPS)
    return (y * g).astype(BF16)


def _norm_mm_kernel(x_ref, g_ref, w_ref, o_ref, h_ref):
    @pl.when(pl.program_id(1) == 0)
    def _():
        h_ref[...] = _rms_bf16(x_ref[...], g_ref[...])

    o_ref[...] = jnp.dot(h_ref[...], w_ref[...], preferred_element_type=F32).astype(o_ref.dtype)


def norm_matmul(x, g, w, out_dtype, tm, tn):
    m, k = x.shape
    n = w.shape[1]
    return pl.pallas_call(
        _norm_mm_kernel,
        grid=(m // tm, n // tn),
        in_specs=[pl.BlockSpec((tm, k), lambda i, j: (i, 0)),
                  pl.BlockSpec((1, k), lambda i, j: (0, 0)),
                  pl.BlockSpec((k, tn), lambda i, j: (0, j))],
        out_specs=pl.BlockSpec((tm, tn), lambda i, j: (i, j)),
        out_shape=jax.ShapeDtypeStruct((m, n), out_dtype),
        scratch_shapes=[pltpu.VMEM((tm, k), BF16)],
        compiler_params=_params("parallel", "arbitrary"),
        name="norm_matmul",
    )(x, g.reshape(1, k), w)


def _mm_res_kernel(a_ref, w_ref, r_ref, o_ref):
    o_ref[...] = r_ref[...] + jnp.dot(a_ref[...], w_ref[...], preferred_element_type=F32)


def matmul_residual(a, w, r, tm, tn):
    m, k = a.shape
    n = w.shape[1]
    return pl.pallas_call(
        _mm_res_kernel,
        grid=(m // tm, n // tn),
        in_specs=[pl.BlockSpec((tm, k), lambda i, j: (i, 0)),
                  pl.BlockSpec((k, tn), lambda i, j: (0, j)),
                  pl.BlockSpec((tm, tn), lambda i, j: (i, j))],
        out_specs=pl.BlockSpec((tm, tn), lambda i, j: (i, j)),
        out_shape=jax.ShapeDtypeStruct((m, n), F32),
        compiler_params=_params("parallel", "arbitrary"),
        name="matmul_residual",
    )(a, w, r)


def _mlp_kernel(x_ref, g_ref, wu_ref, wd_ref, gf_ref, o_ref, h_ref, *, final_norm):
    j = pl.program_id(1)

    @pl.when(j == 0)
    def _():
        x = x_ref[...]
        h_ref[...] = _rms_bf16(x, g_ref[...])
        o_ref[...] = x

    u = jnp.dot(h_ref[...], wu_ref[...], preferred_element_type=F32)
    a = jnp.square(jnp.maximum(u, 0.0)).astype(BF16)
    o_ref[...] += jnp.dot(a, wd_ref[...], preferred_element_type=F32)

    if final_norm:
        @pl.when(j == pl.num_programs(1) - 1)
        def _():
            y = o_ref[...]
            o_ref[...] = y * lax.rsqrt(jnp.mean(y * y, axis=-1, keepdims=True) + RMS_EPS) * gf_ref[...]


def mlp_block(x, g, w_up, w_down, g_final, final_norm, tm, tf):
    m, d = x.shape
    f = w_up.shape[1]
    return pl.pallas_call(
        functools.partial(_mlp_kernel, final_norm=final_norm),
        grid=(m // tm, f // tf),
        in_specs=[pl.BlockSpec((tm, d), lambda i, j: (i, 0)),
                  pl.BlockSpec((1, d), lambda i, j: (0, 0)),
                  pl.BlockSpec((d, tf), lambda i, j: (0, j)),
                  pl.BlockSpec((tf, d), lambda i, j: (j, 0)),
                  pl.BlockSpec((1, d), lambda i, j: (0, 0))],
        out_specs=pl.BlockSpec((tm, d), lambda i, j: (i, 0)),
        out_shape=jax.ShapeDtypeStruct((m, d), F32),
        scratch_shapes=[pltpu.VMEM((tm, d), BF16)],
        compiler_params=_params("parallel", "arbitrary"),
        name="mlp_block",
    )(x, g.reshape(1, d), w_up, w_down, g_final.reshape(1, d))


def _merge_kernel(gate_ref, ba_ref, bb_ref, bc_ref, bd_ref, wb_ref, o_ref, acc_ref):
    n = pl.program_id(1)

    def contribution(b_ref):
        y = jnp.dot(b_ref[...], wb_ref[0], preferred_element_type=F32)
        return (jnp.tanh(gate_ref[...].astype(F32)) + 1.0) * y

    @pl.when(n == 0)
    def _():
        acc_ref[...] = contribution(ba_ref)

    @pl.when(n == 1)
    def _():
        acc_ref[...] += contribution(bb_ref)

    @pl.when(n == 2)
    def _():
        acc_ref[...] += contribution(bc_ref)

    @pl.when(n == 3)
    def _():
        o_ref[...] = (acc_ref[...] + contribution(bd_ref)).astype(o_ref.dtype)


def gated_merge(z, branches, w_branch, tm):
    m = z.shape[0]
    gate_blk0 = Z_GATE // D_MODEL
    br_spec = pl.BlockSpec((tm, BRANCH_W), lambda i, n: (i, 0))
    return pl.pallas_call(
        _merge_kernel,
        grid=(m // tm, N_BRANCH),
        in_specs=[pl.BlockSpec((tm, D_MODEL), lambda i, n: (i, gate_blk0 + n)),
                  br_spec, br_spec, br_spec, br_spec,
                  pl.BlockSpec((1, BRANCH_W, D_MODEL), lambda i, n: (n, 0, 0))],
        out_specs=pl.BlockSpec((tm, D_MODEL), lambda i, n: (i, 0)),
        out_shape=jax.ShapeDtypeStruct((m, D_MODEL), BF16),
        scratch_shapes=[pltpu.VMEM((tm, D_MODEL), F32)],
        compiler_params=_params("parallel", "arbitrary"),
        name="gated_merge",
    )(z, *branches, w_branch)


NA_ROWS_LOCKSTEP = 2
NA_MASK = -1e30


def na_bias_table(rpb):
    q = np.arange(GRID_W)
    kc = np.arange(GRID_W)
    cs = np.clip(q - NA_WIN_C // 2, 0, GRID_W - NA_WIN_C)
    ok = (kc[None, :] >= cs[:, None]) & (kc[None, :] < cs[:, None] + NA_WIN_C)
    dcol = np.clip(kc[None, :] - q[:, None] + NA_WIN_C - 1, 0, 2 * NA_WIN_C - 2)
    drow = np.arange(NA_WIN_R)[None, :] - np.arange(NA_WIN_R)[:, None] + NA_WIN_R - 1
    t = rpb.astype(F32)[:, drow]
    t = t[:, :, :, dcol]
    t = jnp.transpose(t, (1, 0, 3, 2, 4))
    t = jnp.where(jnp.asarray(ok)[None, None, :, None, :], t, NA_MASK)
    return t.reshape(NA_WIN_R, NA_HEADS, GRID_W, NA_WIN_R * GRID_W)


def _na_kernel(q_ref, k_ref, v_ref, bias_ref, o_ref, *, rows_per_step, rows):
    rblk = pl.program_id(1)
    win = NA_WIN_R * GRID_W
    scale = NA_HEAD_DIM ** -0.5

    pair_w = 2 * NA_HEAD_DIM
    low = lax.broadcasted_iota(jnp.int32, (GRID_W, pair_w), 1) < NA_HEAD_DIM

    def rows_body(it, carry):
        inst = []
        q, kb, vb, off, qstart = [], [], [], [], []
        for t in range(NA_ROWS_LOCKSTEP):
            rl = it * NA_ROWS_LOCKSTEP + t
            r = rblk * rows_per_step + rl
            rs = jnp.clip(r - NA_WIN_R // 2, 0, rows - NA_WIN_R)
            off.append(r - rs)
            qstart.append(pl.multiple_of(rl * GRID_W, GRID_W))
            kstart = pl.multiple_of(rs * GRID_W, GRID_W)
            q.append(q_ref[0, pl.ds(qstart[t], GRID_W), :])
            kb.append(k_ref[0, pl.ds(kstart, win), :])
            vb.append(v_ref[0, pl.ds(kstart, win), :])
            inst += [(t, h) for h in range(NA_HEADS)]

        def pair(x, h):
            return x[:, (h // 2) * pair_w:(h // 2 + 1) * pair_w]

        qm = [jnp.where(low if h % 2 == 0 else ~low, pair(q[t], h), jnp.zeros((), BF16)) for t, h in inst]
        s = [lax.dot_general(qm[i], pair(kb[t], h), (((1,), (1,)), ((), ())), preferred_element_type=F32)
             for i, (t, h) in enumerate(inst)]
        s = [s[i] * scale + bias_ref[off[t], h] for i, (t, h) in enumerate(inst)]
        p = [jnp.exp(x - jnp.max(x, axis=-1, keepdims=True)) for x in s]
        l = [jnp.sum(x, axis=-1, keepdims=True) for x in p]
        o = [jnp.dot(p[i].astype(BF16), pair(vb[t], h), preferred_element_type=F32) / l[i]
             for i, (t, h) in enumerate(inst)]
        for t in range(NA_ROWS_LOCKSTEP):
            base = t * NA_HEADS
            outs = [jnp.where(low, o[base + h], o[base + h + 1]) for h in range(0, NA_HEADS, 2)]
            o_ref[0, pl.ds(qstart[t], GRID_W), :] = jnp.concatenate(outs, axis=1).astype(o_ref.dtype)
        return carry

    lax.fori_loop(0, rows_per_step // NA_ROWS_LOCKSTEP, rows_body, 0)


def neighbourhood_attention(z3, bias, rows_per_step):
    b, l, _ = z3.shape
    rows = l // GRID_W
    w = NA_HEADS * NA_HEAD_DIM
    blk = Z_NA // w
    return pl.pallas_call(
        functools.partial(_na_kernel, rows_per_step=rows_per_step, rows=rows),
        grid=(b, rows // rows_per_step),
        in_specs=[pl.BlockSpec((1, rows_per_step * GRID_W, w), lambda i, r: (i, r, blk)),
                  pl.BlockSpec((1, l, w), lambda i, r: (i, 0, blk + 1)),
                  pl.BlockSpec((1, l, w), lambda i, r: (i, 0, blk + 2)),
                  pl.BlockSpec(bias.shape, lambda i, r: (0, 0, 0, 0))],
        out_specs=pl.BlockSpec((1, rows_per_step * GRID_W, w), lambda i, r: (i, r, 0)),
        out_shape=jax.ShapeDtypeStruct((b, l, w), BF16),
        compiler_params=_params("parallel", "arbitrary"),
        name="neighbourhood_attention",
    )(z3, z3, z3, bias)


def _fft_tables(l):
    n = 2 * l
    n2 = FFT_N2
    n1 = n // n2
    a = np.arange(n1)
    b = np.arange(n2)
    k1 = np.arange(n1)
    t = n2 * a[None, :] + b[:, None]
    th = 2.0 * np.pi * (k1[None, :, None] * t[:, None, :] % n) / n
    fa = np.concatenate([np.cos(th), -np.sin(th)], axis=1)
    th_t = np.transpose(th, (0, 2, 1))
    ga = np.concatenate([np.cos(th_t), -np.sin(th_t)], axis=2)
    ph = 2.0 * np.pi * (np.outer(b, b) % n2) / n2
    cr, ci = np.cos(ph), -np.sin(ph)
    fb = np.block([[cr, -ci], [ci, cr]])
    fbi = np.block([[cr, ci], [-ci, cr]])
    return fa, ga, fb, fbi


def _filter_fft_kernel(k_ref, fa_ref, fb_ref, o_ref, y_ref, *, n1):
    n2 = FFT_N2

    def step_a(b, carry):
        xb = k_ref[pl.ds(b, n1, stride=n2), :]
        r = jnp.dot(fa_ref[b], xb, precision=HIGHEST, preferred_element_type=F32)
        y_ref[pl.ds(b, n1, stride=2 * n2), :] = r[:n1]
        y_ref[pl.ds(b + n2, n1, stride=2 * n2), :] = r[n1:]
        return carry

    lax.fori_loop(0, n2, step_a, 0, unroll=2)

    def step_b(k1, carry):
        r0 = pl.multiple_of(k1 * 2 * n2, 2 * n2)
        x = jnp.dot(fb_ref[...], y_ref[pl.ds(r0, 2 * n2), :], precision=HIGHEST, preferred_element_type=F32)
        o_ref[pl.ds(r0, 2 * n2), :] = x * (1.0 / (n1 * n2))
        return carry

    lax.fori_loop(0, n1, step_b, 0, unroll=2)


def filter_spectrum(k_two, fa, fb):
    n, c = k_two.shape
    n1 = n // FFT_N2
    return pl.pallas_call(
        functools.partial(_filter_fft_kernel, n1=n1),
        grid=(c // LANES,),
        in_specs=[pl.BlockSpec((n, LANES), lambda i: (0, i)),
                  pl.BlockSpec(fa.shape, lambda i: (0, 0, 0)),
                  pl.BlockSpec(fb.shape, lambda i: (0, 0))],
        out_specs=pl.BlockSpec((2 * n, LANES), lambda i: (0, i)),
        out_shape=jax.ShapeDtypeStruct((2 * n, c), F32),
        scratch_shapes=[pltpu.VMEM((2 * n, LANES), F32)],
        compiler_params=_params("parallel"),
        name="hyena_filter_spectrum",
    )(k_two, fa, fb)


def _hyena_filter_kernel(z_ref, w1_ref, b1_ref, w2_ref, b2_ref, w3_ref, dec_ref, o_ref):
    hid = jnp.sin(jnp.dot(z_ref[...], w1_ref[...], precision=HIGHEST, preferred_element_type=F32) + b1_ref[...])
    hid = jnp.sin(jnp.dot(hid, w2_ref[...], precision=HIGHEST, preferred_element_type=F32) + b2_ref[...])
    h = jnp.dot(hid, w3_ref[...], precision=HIGHEST, preferred_element_type=F32)
    dec = dec_ref[...]
    hf = h[:, :HY_WIDTH] * dec
    hb = h[:, HY_WIDTH:] * dec
    norm = (jnp.sum(jnp.abs(hf), axis=0, keepdims=True) + jnp.sum(jnp.abs(hb), axis=0, keepdims=True)) + RMS_EPS
    o_ref[:, :HY_WIDTH] = hf / norm
    o_ref[:, HY_WIDTH:] = hb / norm


def hyena_filters(l, w1, b1, w2, b2, w3):
    t = jnp.linspace(0.0, 1.0, l, dtype=F32)[:, None]
    w = 2.0 * math.pi * jnp.arange(l, dtype=F32)[:, None] / l
    bands = jnp.linspace(1e-4, HY_POS_BANDS - 1, HY_POS_BANDS, dtype=F32)[None, :]
    z = jnp.concatenate([t, jnp.cos(bands * w), -jnp.sin(bands * w)], axis=-1)
    pad = LANES - z.shape[1]
    z = jnp.pad(z, ((0, 0), (0, pad)))
    w1p = jnp.pad(w1.astype(F32), ((0, pad), (0, 0)))
    max_decay = math.log(HY_DECAY_TARGET) / HY_FAST_DECAY
    min_decay = math.log(HY_DECAY_TARGET) / HY_SLOW_DECAY
    deltas = jnp.abs(jnp.linspace(min_decay, max_decay, HY_WIDTH, dtype=F32))
    dec = jnp.exp(-t * deltas[None, :])
    h = pl.pallas_call(
        _hyena_filter_kernel,
        out_shape=jax.ShapeDtypeStruct((l, 2 * HY_WIDTH), F32),
        compiler_params=pltpu.CompilerParams(vmem_limit_bytes=VMEM_LIMIT_BYTES),
        name="hyena_filter_ffn",
    )(z, w1p, b1.reshape(1, -1).astype(F32), w2.astype(F32), b2.reshape(1, -1).astype(F32), w3.astype(F32), dec)
    h_f, h_b = h[:, :HY_WIDTH], h[:, HY_WIDTH:]
    return jnp.concatenate([h_f[:1] + h_b[:1], h_f[1:], jnp.zeros_like(h_f[:1]), h_b[:0:-1]], axis=0)


def _shift_rows(x, delta):
    n = x.shape[0]
    row = lax.broadcasted_iota(jnp.int32, x.shape, 0)
    if delta == 1:
        return jnp.where(row == 0, 0.0, pltpu.roll(x, 1, 0))
    return jnp.where(row == n - 1, 0.0, pltpu.roll(x, n - 1, 0))


def _dwconv3(x, w_ref):
    w = w_ref[...].astype(F32)
    return _shift_rows(x, 1) * w[0:1] + x * w[1:2] + _shift_rows(x, -1) * w[2:3]


def _hyena_kernel(x1_ref, x2_ref, v_ref, w1_ref, w2_ref, wv_ref, skip_ref, kf_ref, fa_ref, ga_ref, fb_ref, fbi_ref,
                  o_ref, xs_ref, y_ref, *, n1):
    n2 = FFT_N2
    na = n1 // 2
    pa, py = FFT_PITCH_X, FFT_PITCH_Y
    vg = _dwconv3(v_ref[0].astype(F32), wv_ref) * _dwconv3(x1_ref[0].astype(F32), w1_ref)
    for a in range(na):
        xs_ref[a * pa:a * pa + n2, :] = vg[a * n2:(a + 1) * n2]

    def fwd_a(b, carry):
        xb = xs_ref[pl.ds(b, na, stride=pa), :].astype(BF16)
        r = jnp.dot(fa_ref[b], xb, preferred_element_type=F32)
        y_ref[pl.ds(b, n1, stride=py), :] = r[:n1]
        y_ref[pl.ds(b + n2, n1, stride=py), :] = r[n1:]
        return carry

    lax.fori_loop(0, n2, fwd_a, 0, unroll=FFT_UNROLL)

    def mid(kg, carry):
        grp = range(FFT_GROUP)
        k1 = [kg * FFT_GROUP + t for t in grp]
        r0 = [pl.multiple_of(k * py, 8) for k in k1]
        f0 = [pl.multiple_of(k * 2 * n2, 2 * n2) for k in k1]
        fb = fb_ref[...]
        x = [jnp.dot(fb, y_ref[pl.ds(r0[t], 2 * n2), :].astype(BF16), preferred_element_type=F32) for t in grp]
        p = []
        for t in grp:
            xr, xi = x[t][:n2], x[t][n2:]
            kr = kf_ref[pl.ds(f0[t], n2), :]
            ki = kf_ref[pl.ds(f0[t] + n2, n2), :]
            p.append(jnp.concatenate([xr * kr - xi * ki, xr * ki + xi * kr], axis=0).astype(BF16))
        fbi = fbi_ref[...]
        q = [jnp.dot(fbi, p[t], preferred_element_type=F32) for t in grp]
        for t in grp:
            y_ref[pl.ds(r0[t], 2 * n2), :] = q[t]
        return carry

    lax.fori_loop(0, n1 // FFT_GROUP, mid, 0)

    def inv_a(b, carry):
        qb = jnp.concatenate([y_ref[pl.ds(b, n1, stride=py), :],
                              y_ref[pl.ds(b + n2, n1, stride=py), :]], axis=0).astype(BF16)
        xs_ref[pl.ds(b, na, stride=pa), :] = jnp.dot(ga_ref[b], qb, preferred_element_type=F32)
        return carry

    lax.fori_loop(0, n2, inv_a, 0, unroll=FFT_UNROLL)

    gate = _dwconv3(x2_ref[0].astype(F32), w2_ref)
    skip = skip_ref[...].astype(F32)
    for a in range(na):
        rows = slice(a * n2, (a + 1) * n2)
        o_ref[0, rows, :] = ((xs_ref[a * pa:a * pa + n2, :] + vg[rows] * skip) * gate[rows]).astype(o_ref.dtype)


def hyena_mixer(z3, w_short, skip, kf, tables):
    b, l, _ = z3.shape
    n1 = 2 * l // FFT_N2
    fa, ga, fb, fbi = tables
    nch = HY_WIDTH // LANES
    blk = Z_HY // LANES

    def zspec(seg):
        return pl.BlockSpec((1, l, LANES), lambda c, i: (i, 0, blk + seg * nch + c))

    def wspec(seg):
        return pl.BlockSpec((3, LANES), lambda c, i: (0, seg * nch + c))

    return pl.pallas_call(
        functools.partial(_hyena_kernel, n1=n1),
        grid=(nch, b),
        in_specs=[zspec(0), zspec(1), zspec(2), wspec(0), wspec(1), wspec(2),
                  pl.BlockSpec((1, LANES), lambda c, i: (0, c)),
                  pl.BlockSpec((4 * l, LANES), lambda c, i: (0, c)),
                  pl.BlockSpec(fa.shape, lambda c, i: (0, 0, 0)),
                  pl.BlockSpec(ga.shape, lambda c, i: (0, 0, 0)),
                  pl.BlockSpec(fb.shape, lambda c, i: (0, 0)),
                  pl.BlockSpec(fbi.shape, lambda c, i: (0, 0))],
        out_specs=pl.BlockSpec((1, l, LANES), lambda c, i: (i, 0, c)),
        out_shape=jax.ShapeDtypeStruct((b, l, HY_WIDTH), BF16),
        scratch_shapes=[pltpu.VMEM((n1 // 2 * FFT_PITCH_X, LANES), F32), pltpu.VMEM((n1 * FFT_PITCH_Y, LANES), F32)],
        compiler_params=_params("parallel", "arbitrary"),
        name="hyena_mixer",
    )(z3, z3, z3, w_short, w_short, w_short, skip.reshape(1, -1), kf, fa, ga, fb, fbi)


MLA_HW = 2 * LANES
MLA_HEADS_PER_STEP = 2


def _rope_group(g, cs):
    prod = g * cs
    s = prod + pltpu.roll(prod, MLA_ROPE, 1)
    lane = lax.broadcasted_iota(jnp.int32, s.shape, 1)
    return jnp.where(lane < MLA_ROPE, s, 0.0)


def _mla_prep_kernel(cq_ref, ckv_ref, kr_ref, gq_ref, gkv_ref, wq_ref, wkv_ref, cs_ref, q_ref, k_ref, v_ref):
    cs = cs_ref[...]
    hq = _rms_bf16(cq_ref[...], gq_ref[...])
    q = jnp.dot(hq, wq_ref[...], preferred_element_type=F32)
    hkv = _rms_bf16(ckv_ref[...], gkv_ref[...])
    kv = jnp.dot(hkv, wkv_ref[...], preferred_element_type=F32)
    k_rope = _rope_group(kr_ref[...].astype(F32), cs).astype(BF16)
    lane = lax.broadcasted_iota(jnp.int32, (cs.shape[0], LANES), 1)
    ones_lane = jnp.where(lane == 0, 1.0, 0.0).astype(BF16)
    for h in range(MLA_HEADS):
        o = h * MLA_HW
        q_ref[:, o:o + LANES] = q[:, o:o + LANES].astype(BF16)
        q_ref[:, o + LANES:o + MLA_HW] = _rope_group(q[:, o + LANES:o + MLA_HW], cs).astype(BF16)
        k_ref[:, o:o + LANES] = kv[:, o:o + LANES].astype(BF16)
        k_ref[:, o + LANES:o + MLA_HW] = k_rope
        v_ref[:, o:o + LANES] = kv[:, o + LANES:o + MLA_HW].astype(BF16)
        v_ref[:, o + LANES:o + MLA_HW] = ones_lane


def _mla_attn_kernel(q_ref, k_ref, v_ref, o_ref):
    c = (MLA_NOPE + MLA_ROPE) ** -0.5 * math.log2(math.e)
    heads = range(MLA_HEADS_PER_STEP)
    s = [lax.dot_general(q_ref[0, :, h * MLA_HW:(h + 1) * MLA_HW], k_ref[0, :, h * MLA_HW:(h + 1) * MLA_HW],
                         (((1,), (1,)), ((), ())), preferred_element_type=F32) for h in heads]
    for h in heads:
        p = jnp.exp2((s[h] - jnp.max(s[h], axis=-1, keepdims=True)) * c)
        o = jnp.dot(p.astype(BF16), v_ref[0, :, h * MLA_HW:(h + 1) * MLA_HW], preferred_element_type=F32)
        o_ref[0, :, h * MLA_V:(h + 1) * MLA_V] = (o[:, :MLA_V] / o[:, MLA_V:MLA_V + 1]).astype(o_ref.dtype)


def mla_mixer(z, b, l, g_q, g_kv, wq_p, w_ukv, cs_tab, tm, tq):
    m = z.shape[0]
    lt = l // tm
    qp, kp, vp = pl.pallas_call(
        _mla_prep_kernel,
        grid=(m // tm,),
        in_specs=[pl.BlockSpec((tm, MLA_Q_RANK), lambda i: (i, Z_CQ // MLA_Q_RANK)),
                  pl.BlockSpec((tm, MLA_KV_RANK), lambda i: (i, Z_CKV // MLA_KV_RANK)),
                  pl.BlockSpec((tm, LANES), lambda i: (i, Z_KR // LANES)),
                  pl.BlockSpec((1, MLA_Q_RANK), lambda i: (0, 0)),
                  pl.BlockSpec((1, MLA_KV_RANK), lambda i: (0, 0)),
                  pl.BlockSpec(wq_p.shape, lambda i: (0, 0)),
                  pl.BlockSpec(w_ukv.shape, lambda i: (0, 0)),
                  pl.BlockSpec((tm, LANES), lambda i: (i % lt, 0))],
        out_specs=[pl.BlockSpec((tm, MLA_HEADS * MLA_HW), lambda i: (i, 0)),
                   pl.BlockSpec((tm, MLA_HEADS * MLA_HW), lambda i: (i, 0)),
                   pl.BlockSpec((tm, MLA_HEADS * MLA_HW), lambda i: (i, 0))],
        out_shape=[jax.ShapeDtypeStruct((m, MLA_HEADS * MLA_HW), BF16),
                   jax.ShapeDtypeStruct((m, MLA_HEADS * MLA_HW), BF16),
                   jax.ShapeDtypeStruct((m, MLA_HEADS * MLA_HW), BF16)],
        compiler_params=_params("parallel"),
        name="mla_prep",
    )(z, z, z, g_q.reshape(1, -1), g_kv.reshape(1, -1), wq_p, w_ukv, cs_tab)
    qp = qp.reshape(b, l, -1)
    kp = kp.reshape(b, l, -1)
    vp = vp.reshape(b, l, -1)
    hs = MLA_HEADS_PER_STEP
    return pl.pallas_call(
        _mla_attn_kernel,
        grid=(b, MLA_HEADS // hs, l // tq),
        in_specs=[pl.BlockSpec((1, tq, hs * MLA_HW), lambda i, h, t: (i, t, h)),
                  pl.BlockSpec((1, l, hs * MLA_HW), lambda i, h, t: (i, 0, h)),
                  pl.BlockSpec((1, l, hs * MLA_HW), lambda i, h, t: (i, 0, h))],
        out_specs=pl.BlockSpec((1, tq, hs * MLA_V), lambda i, h, t: (i, t, h)),
        out_shape=jax.ShapeDtypeStruct((b, l, MLA_HEADS * MLA_V), BF16),
        compiler_params=_params("parallel", "parallel", "arbitrary"),
        name="mla_attention",
    )(qp, kp, vp)


def _split_bf16(x):
    hi = x.astype(BF16)
    return hi, (x - hi.astype(F32)).astype(BF16)


def _dot_split(a, b):
    (ah, al), (bh, bl) = a, b
    return (jnp.dot(ah, bh, preferred_element_type=F32)
            + (jnp.dot(ah, bl, preferred_element_type=F32) + jnp.dot(al, bh, preferred_element_type=F32)))


def _tri_unit_inverse(mats):
    c = mats[0].shape[0]
    eye = (lax.broadcasted_iota(jnp.int32, (c, c), 0) == lax.broadcasted_iota(jnp.int32, (c, c), 1)).astype(F32)
    ps = [eye - a for a in mats]
    ms = list(mats)
    for _ in range(int(math.log2(c)) - 1):
        splits = [_split_bf16(m) for m in ms]
        ms = [_dot_split(sp, sp) for sp in splits]
        yield
        ps = [_dot_split(_split_bf16(p), _split_bf16(eye + m)) for p, m in zip(ps, ms)]
        yield
    return ps


def _interleave(main, side=None):
    gens = [g for g in (main, side) if g is not None]
    results = [None] * len(gens)
    live = [True] * len(gens)
    while any(live):
        for idx, gen in enumerate(gens):
            if live[idx]:
                try:
                    next(gen)
                except StopIteration as stop:
                    results[idx] = stop.value
                    live[idx] = False
    return results


def _bdot(a, b):
    return jnp.dot(a.astype(BF16), b.astype(BF16), preferred_element_type=F32)


def _bdot_nt(a, b):
    return lax.dot_general(a.astype(BF16), b.astype(BF16), (((1,), (1,)), ((), ())), preferred_element_type=F32)


def _dn_group_setup(chunks, refs):
    q_s, k_s, v_s, g_s, b_s = refs
    cc = DN_CHUNK
    n = range(len(chunks))
    rev = [r for _, r in chunks]
    r0 = [pl.multiple_of(c * cc, cc) for c, _ in chunks]
    q = [q_s[pl.ds(r, cc), :] for r in r0]
    k = [k_s[pl.ds(r, cc), :] for r in r0]
    v = [v_s[pl.ds(r, cc), :] for r in r0]
    lane = [2 if r else 0 for r in rev]
    ri = lax.broadcasted_iota(jnp.int32, (cc, cc), 0)
    ci = lax.broadcasted_iota(jnp.int32, (cc, cc), 1)
    incl = [(ri <= ci) if r else (ri >= ci) for r in rev]
    strict = [(ri < ci) if r else (ri > ci) for r in rev]
    g = [g_s[pl.ds(r, cc), :] for r in r0]
    g1 = [x.astype(BF16) for x in g]
    r1 = [x - h.astype(F32) for x, h in zip(g, g1)]
    g2 = [x.astype(BF16) for x in r1]
    g3 = [(x - h.astype(F32)).astype(BF16) for x, h in zip(r1, g2)]
    ones = [m.astype(BF16) for m in incl]
    gc = [jnp.dot(ones[i], g1[i], preferred_element_type=F32) for i in n]
    gc = [gc[i] + jnp.dot(ones[i], g2[i], preferred_element_type=F32) for i in n]
    gc = [gc[i] + jnp.dot(ones[i], g3[i], preferred_element_type=F32) for i in n]
    yield
    gcol = [gc[i][:, lane[i]:lane[i] + 1] for i in n]
    grow = [jnp.transpose(gc[i])[lane[i]:lane[i] + 1, :] for i in n]
    beta = [b_s[pl.ds(r0[i], cc), :][:, lane[i] + 1:lane[i] + 2] for i in n]
    g_last = [gcol[i][0:1, :] if rev[i] else gcol[i][cc - 1:cc, :] for i in n]
    decay = [jnp.where(incl[i], jnp.exp(jnp.where(incl[i], gcol[i] - grow[i], 0.0)), 0.0) for i in n]
    e_g = [jnp.exp(x) for x in gcol]
    kb = [k[i] * beta[i] for i in n]
    a_both = [_bdot_nt(jnp.concatenate([kb[i], q[i]], axis=0), k[i]) for i in n]
    yield
    a_kk = [jnp.where(strict[i], a_both[i][:cc] * decay[i], 0.0) for i in n]
    a_qk = [jnp.where(incl[i], a_both[i][cc:] * decay[i], 0.0) for i in n]
    t_inv = yield from _tri_unit_inverse(a_kk)
    uw = [_bdot(t_inv[i], jnp.concatenate([v[i] * beta[i], kb[i] * e_g[i]], axis=1)) for i in n]
    yield
    k_dec = [k[i] * jnp.exp(g_last[i] - gcol[i]) for i in n]
    wq = [jnp.concatenate([uw[i][:, DN_DV:], q[i] * e_g[i]], axis=0).astype(BF16) for i in n]
    ak = [jnp.concatenate([a_qk[i], jnp.transpose(k_dec[i])], axis=0).astype(BF16) for i in n]
    return [(wq[i], ak[i], uw[i][:, :DN_DV], jnp.exp(g_last[i])) for i in n]


def _dn_group_steps(states, setups, out_refs, chunks):
    cc = DN_CHUNK
    dirs = range(len(states))
    for t in range(len(setups[0])):
        st = [setups[d][t] for d in dirs]
        r = [jnp.dot(st[d][0], states[d].astype(BF16), preferred_element_type=F32) for d in dirs]
        yield
        v_new = [st[d][2] - r[d][:cc] for d in dirs]
        r2 = [jnp.dot(st[d][1], v_new[d].astype(BF16), preferred_element_type=F32) for d in dirs]
        yield
        states = [states[d] * st[d][3] + r2[d][cc:] for d in dirs]
        for d in dirs:
            out_refs[d][pl.ds(pl.multiple_of(chunks[d][t] * cc, cc), cc), :] = r[d][cc:] + r2[d][:cc]
    return states


def _dn_kernel(zq_ref, zk_ref, zv_ref, zg_ref, ab_ref, wq_ref, wk_ref, wv_ref, alog_ref, dtb_ref, gn_ref, o_ref,
               q_s, k_s, v_s, g_s, b_s, of_s, ob_s, *, nchunks):
    def act(z_ref, w_ref):
        c = _dwconv3(z_ref[0].astype(F32), w_ref)
        return _silu(c)

    def l2n(x):
        return x * lax.rsqrt(jnp.sum(x * x, axis=-1, keepdims=True) + RMS_EPS)

    q_s[...] = l2n(act(zq_ref, wq_ref)) * (DN_DK ** -0.5)
    k_s[...] = l2n(act(zk_ref, wk_ref))
    v_s[...] = act(zv_ref, wv_ref)
    ab = ab_ref[0]
    x = ab + dtb_ref[0]
    softplus = jnp.maximum(x, 0.0) + jnp.log1p(jnp.exp(-jnp.abs(x)))
    g_s[...] = -jnp.exp(alog_ref[0]) * softplus
    b_s[...] = jax.nn.sigmoid(ab)
    refs = (q_s, k_s, v_s, g_s, b_s)

    ngroups = nchunks // DN_GROUP

    def group_chunks(gi):
        fwd = [gi * DN_GROUP + t for t in range(DN_GROUP)]
        return fwd, [nchunks - 1 - c for c in fwd]

    def setup_gen(gi):
        fwd, bwd = group_chunks(gi)
        return _dn_group_setup([(c, False) for c in fwd] + [(c, True) for c in bwd], refs)

    def steps_gen(gi, states, flat):
        setups = [tuple(flat[4 * i:4 * i + 4]) for i in range(2 * DN_GROUP)]
        return _dn_group_steps(states, [setups[:DN_GROUP], setups[DN_GROUP:]], (of_s, ob_s), group_chunks(gi))

    def flatten(setups):
        return [x for st in setups for x in st]

    def body(gi, carry):
        nxt, states = _interleave(setup_gen(gi + 1), steps_gen(gi, list(carry[:2]), carry[2:]))
        return (*states, *flatten(nxt))

    zero = jnp.zeros((DN_DK, DN_DV), F32)
    (first,) = _interleave(setup_gen(0))
    carry = lax.fori_loop(0, ngroups - 1, body, (zero, zero, *flatten(first)))
    _interleave(steps_gen(ngroups - 1, list(carry[:2]), carry[2:]))

    o = of_s[...] + ob_s[...]
    o = o * lax.rsqrt(jnp.mean(o * o, axis=-1, keepdims=True) + RMS_EPS) * gn_ref[...]
    gate = zg_ref[0].astype(F32)
    o_ref[0] = (o * _silu(gate)).astype(o_ref.dtype)


def deltanet_mixer(z3, ab3, w_conv, alog_p, dtb_p, g_norm):
    b, l, _ = z3.shape
    blk = Z_DNQKV // LANES
    gblk = Z_DNGATE // LANES
    hh = DN_HEADS

    def zspec(off):
        return pl.BlockSpec((1, l, LANES), lambda i, h: (i, 0, off + h))

    def wspec(seg):
        return pl.BlockSpec((3, LANES), lambda i, h: (0, seg * hh + h))

    vec = pl.BlockSpec((1, 1, LANES), lambda i, h: (h, 0, 0))
    seq = pltpu.VMEM((l, LANES), F32)
    return pl.pallas_call(
        functools.partial(_dn_kernel, nchunks=l // DN_CHUNK),
        grid=(b, hh),
        in_specs=[zspec(blk), zspec(blk + hh), zspec(blk + 2 * hh), zspec(gblk),
                  pl.BlockSpec((1, l, LANES), lambda i, h: (i, 0, h)),
                  wspec(0), wspec(1), wspec(2), vec, vec,
                  pl.BlockSpec((1, LANES), lambda i, h: (0, 0))],
        out_specs=pl.BlockSpec((1, l, LANES), lambda i, h: (i, 0, h)),
        out_shape=jax.ShapeDtypeStruct((b, l, hh * DN_DV), BF16),
        scratch_shapes=[seq] * 7,
        compiler_params=_params("parallel", "arbitrary"),
        name="deltanet_mixer",
    )(z3, z3, z3, z3, ab3, w_conv, w_conv, w_conv, alog_p, dtb_p, g_norm.reshape(1, -1))


def _rotate_half_cols(w):
    half = w.shape[-1] // 2
    return jnp.concatenate([-w[..., half:], w[..., :half]], axis=-1)


def _prep_w_in(w_in):
    cuts = np.cumsum([0, 1536, 1536, MLA_Q_RANK, MLA_KV_RANK, MLA_ROPE, 1536, 512, 16])
    na, hy, cq, ckv, kr, dnqkv, dngate, dnab = (w_in[:, cuts[i]:cuts[i + 1]] for i in range(8))
    gates = w_in[:, cuts[8]:]
    pad = jnp.zeros((w_in.shape[0], Z_GATE - Z_KR - 2 * MLA_ROPE), w_in.dtype)
    main = jnp.concatenate([na, hy, dnqkv, dngate, cq, ckv, kr, _rotate_half_cols(kr), pad, 0.5 * gates], axis=1)
    ab = dnab.reshape(-1, 2, 2, DN_HEADS)
    ab = jnp.transpose(ab, (0, 3, 1, 2)).reshape(-1, DN_HEADS, 4)
    ab = jnp.pad(ab, ((0, 0), (0, 0), (0, LANES - 4))).reshape(-1, DN_HEADS * LANES)
    return main.astype(BF16), ab.astype(BF16)


def _prep_w_uq(w_uq):
    k = w_uq.shape[0]
    w = w_uq.reshape(k, MLA_HEADS, MLA_NOPE + MLA_ROPE)
    rope_w = w[:, :, MLA_NOPE:]
    return jnp.concatenate([w, _rotate_half_cols(rope_w)], axis=-1).reshape(k, MLA_HEADS * MLA_HW).astype(BF16)


def _head_lane_vec(p):
    v = jnp.zeros((DN_HEADS, 1, LANES), F32)
    v = v.at[:, 0, 0].set(p[0].astype(F32))
    return v.at[:, 0, 2].set(p[1].astype(F32))


def _rope_table(l):
    half = MLA_ROPE // 2
    inv = ROPE_THETA ** (-jnp.arange(half, dtype=F32) / half)
    ang = jnp.arange(l, dtype=F32)[:, None] * inv[None, :]
    cos, sin = jnp.cos(ang), jnp.sin(ang)
    return jnp.concatenate([cos, cos, sin, sin], axis=-1)


def _pick(n, pref):
    for t in pref:
        if n % t == 0:
            return t
    return n


def trunk(x3, norm_mix, w_in, na_rpb, hy_short, hy_skip, hy_w1, hy_b1, hy_w2, hy_b2, hy_w3,
          mla_g_q, mla_g_kv, mla_w_uq, mla_w_ukv, dn_conv, dn_a_log, dn_dt_bias, dn_g_norm,
          w_branch, w_out, norm_mlp, w_up, w_down, norm_final):
    b, l, d = x3.shape
    m = b * l
    depth = w_in.shape[0]
    x = x3.reshape(m, d)
    tm_big = _pick(m, (1024, 512, 256, 128))
    tm_mid = _pick(m, (512, 256, 128))
    fa, ga, fb, fbi = _fft_tables(l)
    n1 = 2 * l // FFT_N2
    tables = (jnp.asarray(fa[:, :, :n1 // 2], BF16), jnp.asarray(ga[:, :n1 // 2, :], BF16),
              jnp.asarray(fb, BF16), jnp.asarray(fbi, BF16))
    fa32, fb32 = jnp.asarray(fa, F32), jnp.asarray(fb, F32)
    cs_tab = _rope_table(l)
    for layer in range(depth):
        w_main, w_ab = _prep_w_in(w_in[layer])
        z = norm_matmul(x, norm_mix[layer], w_main, BF16, tm_big, 1024)
        ab = norm_matmul(x, norm_mix[layer], w_ab, F32, tm_big, DN_HEADS * LANES)
        z3 = z.reshape(b, l, Z_COLS)
        br_a = neighbourhood_attention(z3, na_bias_table(na_rpb[layer]), _pick(l // GRID_W, (8, 4, 2, 1)))
        k_two = hyena_filters(l, hy_w1[layer], hy_b1[layer], hy_w2[layer], hy_b2[layer], hy_w3[layer])
        kf = filter_spectrum(k_two, fa32, fb32)
        br_b = hyena_mixer(z3, hy_short[layer], hy_skip[layer], kf, tables)
        br_c = mla_mixer(z, b, l, mla_g_q[layer], mla_g_kv[layer], _prep_w_uq(mla_w_uq[layer]),
                         mla_w_ukv[layer].astype(BF16), cs_tab, tm_mid, _pick(l, (512, 256, 128)))
        br_d = deltanet_mixer(z3, ab.reshape(b, l, -1), dn_conv[layer], _head_lane_vec(dn_a_log[layer]),
                              _head_lane_vec(dn_dt_bias[layer]), dn_g_norm[layer])
        branches = [t.reshape(m, BRANCH_W) for t in (br_a, br_b, br_c, br_d)]
        merged = gated_merge(z, branches, (0.5 * w_branch[layer]).astype(BF16), tm_mid)
        x = matmul_residual(merged, w_out[layer].astype(BF16), x, tm_mid, d)
        x = mlp_block(x, norm_mlp[layer], w_up[layer].astype(BF16), w_down[layer].astype(BF16), norm_final,
                      layer == depth - 1, tm_mid, 1024)
    return x.reshape(b, l, d)


def kernel(x_prompt, x_sample, norm_mix, w_in, na_rpb, hy_short, hy_skip, hy_w1, hy_b1, hy_w2, hy_b2, hy_w3,
           mla_g_q, mla_g_kv, mla_w_uq, mla_w_ukv, dn_conv, dn_a_log, dn_dt_bias, dn_g_norm,
           w_branch, w_out, norm_mlp, w_up, w_down, norm_final):
    assert x_prompt.shape[1:] == x_sample.shape[1:]
    nb = x_prompt.shape[0]
    y = trunk(jnp.concatenate([x_prompt, x_sample], axis=0), norm_mix, w_in, na_rpb, hy_short, hy_skip,
              hy_w1, hy_b1, hy_w2, hy_b2, hy_w3, mla_g_q, mla_g_kv, mla_w_uq, mla_w_ukv, dn_conv, dn_a_log,
              dn_dt_bias, dn_g_norm, w_branch, w_out, norm_mlp, w_up, w_down, norm_final)
    return (y[:nb], y[nb:])
```

```python
import functools
import math

import jax
import jax.numpy as jnp
import numpy as np
from jax import lax
from jax.experimental import pallas as pl
from jax.experimental.pallas import tpu as pltpu

F32 = jnp.float32
BF16 = jnp.bfloat16
HIGHEST = lax.Precision.HIGHEST

VMEM_LIMIT_BYTES = 56 * 1024 * 1024
LANES = 128

D_MODEL = 2048
RMS_EPS = 1e-6
GRID_W = 64
N_BRANCH = 4
BRANCH_W = 512
NA_HEADS = 8
NA_HEAD_DIM = 64
NA_WIN_R = 8
NA_WIN_C = 16
HY_WIDTH = 512
HY_POS_BANDS = 16
HY_FILT_HIDDEN = 64
HY_FAST_DECAY = 0.3
HY_SLOW_DECAY = 1.5
HY_DECAY_TARGET = 1e-2
MLA_HEADS = 4
MLA_Q_RANK = 512
MLA_KV_RANK = 256
MLA_NOPE = 128
MLA_ROPE = 64
MLA_V = 128
ROPE_THETA = 10000.0
DN_HEADS = 4
DN_DK = 128
DN_DV = 128
DN_CHUNK = 64
DN_GROUP = 4
D_FF = 4 * D_MODEL

Z_NA = 0
Z_HY = 1536
Z_DNQKV = 3072
Z_DNGATE = 4608
Z_CQ = 5120
Z_CKV = 5632
Z_KR = 5888
Z_GATE = 6144
Z_COLS = Z_GATE + N_BRANCH * D_MODEL

FFT_N2 = 128
FFT_UNROLL = 8
FFT_GROUP = 4
FFT_PITCH_X = FFT_N2 + 8
FFT_PITCH_Y = 2 * FFT_N2 + 8


def _params(*sem):
    return pltpu.CompilerParams(dimension_semantics=sem, vmem_limit_bytes=VMEM_LIMIT_BYTES)


def _silu(x):
    return 0.5 * x * (jnp.tanh(0.5 * x) + 1.0)


def _rms_bf16(x, g):
    xf = x.astype(F32)
    y = xf * lax.rsqrt(jnp.mean(xf * xf, axis=-1, keepdims=True) + RMS_EPS)
    return (y * g).astype(BF16)


def _part_tiles(parts, tm):
    assert all(p.shape[0] % tm == 0 for p in parts)
    return tuple(p.shape[0] // tm for p in parts)


def _part_specs(tiles, block):
    specs, start = [], 0
    for count in tiles:
        specs.append(pl.BlockSpec(block, functools.partial(
            lambda i, j, start, count: (jnp.clip(i - start, 0, count - 1), 0), start=start, count=count)))
        start += count
    return specs


def _for_owning_part(tiles, refs, fn):
    if len(refs) == 1:
        fn(refs[0])
        return
    i = pl.program_id(0)
    start = 0
    for ref, count in zip(refs, tiles):
        pl.when((i >= start) & (i < start + count))(functools.partial(fn, ref))
        start += count


def _norm_mm_kernel(*refs, tiles):
    x_refs = refs[:len(tiles)]
    g_ref, w_ref, o_ref, h_ref = refs[len(tiles):]

    @pl.when(pl.program_id(1) == 0)
    def _():
        def norm(x_ref):
            h_ref[...] = _rms_bf16(x_ref[...], g_ref[...])

        _for_owning_part(tiles, x_refs, norm)

    o_ref[...] = jnp.dot(h_ref[...], w_ref[...], preferred_element_type=F32).astype(o_ref.dtype)


def norm_matmul(x_parts, g, w, out_dtype, tm, tn):
    k, n = w.shape
    tiles = _part_tiles(x_parts, tm)
    return pl.pallas_call(
        functools.partial(_norm_mm_kernel, tiles=tiles),
        grid=(sum(tiles), n // tn),
        in_specs=_part_specs(tiles, (tm, k)) + [pl.BlockSpec((1, k), lambda i, j: (0, 0)),
                                                 pl.BlockSpec((k, tn), lambda i, j: (0, j))],
        out_specs=pl.BlockSpec((tm, tn), lambda i, j: (i, j)),
        out_shape=jax.ShapeDtypeStruct((sum(tiles) * tm, n), out_dtype),
        scratch_shapes=[pltpu.VMEM((tm, k), BF16)],
        compiler_params=_params("parallel", "arbitrary"),
        name="norm_matmul",
    )(*x_parts, g.reshape(1, k), w)


def _mm_res_kernel(*refs, tiles):
    a_ref, w_ref = refs[:2]
    r_refs = refs[2:2 + len(tiles)]
    o_ref = refs[-1]
    y = jnp.dot(a_ref[...], w_ref[...], preferred_element_type=F32)

    def add(r_ref):
        o_ref[...] = r_ref[...] + y

    _for_owning_part(tiles, r_refs, add)


def matmul_residual(a, w, r_parts, tm):
    m, k = a.shape
    n = w.shape[1]
    tiles = _part_tiles(r_parts, tm)
    return pl.pallas_call(
        functools.partial(_mm_res_kernel, tiles=tiles),
        grid=(m // tm, 1),
        in_specs=[pl.BlockSpec((tm, k), lambda i, j: (i, 0)),
                  pl.BlockSpec((k, n), lambda i, j: (0, 0))] + _part_specs(tiles, (tm, n)),
        out_specs=pl.BlockSpec((tm, n), lambda i, j: (i, 0)),
        out_shape=jax.ShapeDtypeStruct((m, n), F32),
        compiler_params=_params("parallel", "arbitrary"),
        name="matmul_residual",
    )(a, w, *r_parts)


def _mlp_kernel(x_ref, g_ref, wu_ref, wd_ref, gf_ref, *refs, final_norm, tiles):
    o_refs, h_ref = refs[:-1], refs[-1]
    j = pl.program_id(1)

    def run(o_ref):
        @pl.when(j == 0)
        def _():
            x = x_ref[...]
            h_ref[...] = _rms_bf16(x, g_ref[...])
            o_ref[...] = x

        u = jnp.dot(h_ref[...], wu_ref[...], preferred_element_type=F32)
        a = jnp.square(jnp.maximum(u, 0.0)).astype(BF16)
        o_ref[...] += jnp.dot(a, wd_ref[...], preferred_element_type=F32)

        if final_norm:
            @pl.when(j == pl.num_programs(1) - 1)
            def _():
                y = o_ref[...]
                o_ref[...] = y * lax.rsqrt(jnp.mean(y * y, axis=-1, keepdims=True) + RMS_EPS) * gf_ref[...]

    _for_owning_part(tiles, o_refs, run)


def mlp_block(x, g, w_up, w_down, g_final, final_norm, tm, tf, out_rows):
    m, d = x.shape
    f = w_up.shape[1]
    assert sum(out_rows) == m and all(r % tm == 0 for r in out_rows)
    tiles = tuple(r // tm for r in out_rows)
    return pl.pallas_call(
        functools.partial(_mlp_kernel, final_norm=final_norm, tiles=tiles),
        grid=(m // tm, f // tf),
        in_specs=[pl.BlockSpec((tm, d), lambda i, j: (i, 0)),
                  pl.BlockSpec((1, d), lambda i, j: (0, 0)),
                  pl.BlockSpec((d, tf), lambda i, j: (0, j)),
                  pl.BlockSpec((tf, d), lambda i, j: (j, 0)),
                  pl.BlockSpec((1, d), lambda i, j: (0, 0))],
        out_specs=_part_specs(tiles, (tm, d)),
        out_shape=[jax.ShapeDtypeStruct((r, d), F32) for r in out_rows],
        scratch_shapes=[pltpu.VMEM((tm, d), BF16)],
        compiler_params=_params("parallel", "arbitrary"),
        name="mlp_block",
    )(x, g.reshape(1, d), w_up, w_down, g_final.reshape(1, d))


def _merge_kernel(gate_ref, ba_ref, bb_ref, bc_ref, bd_ref, wb_ref, o_ref, acc_ref):
    n = pl.program_id(1)

    def contribution(b_ref):
        y = jnp.dot(b_ref[...], wb_ref[0], preferred_element_type=F32)
        return (jnp.tanh(gate_ref[...].astype(F32)) + 1.0) * y

    @pl.when(n == 0)
    def _():
        acc_ref[...] = contribution(ba_ref)

    @pl.when(n == 1)
    def _():
        acc_ref[...] += contribution(bb_ref)

    @pl.when(n == 2)
    def _():
        acc_ref[...] += contribution(bc_ref)

    @pl.when(n == 3)
    def _():
        o_ref[...] = (acc_ref[...] + contribution(bd_ref)).astype(o_ref.dtype)


def gated_merge(z, branches, w_branch, tm):
    m = z.shape[0]
    gate_blk0 = Z_GATE // D_MODEL
    br_spec = pl.BlockSpec((tm, BRANCH_W), lambda i, n: (i, 0))
    return pl.pallas_call(
        _merge_kernel,
        grid=(m // tm, N_BRANCH),
        in_specs=[pl.BlockSpec((tm, D_MODEL), lambda i, n: (i, gate_blk0 + n)),
                  br_spec, br_spec, br_spec, br_spec,
                  pl.BlockSpec((1, BRANCH_W, D_MODEL), lambda i, n: (n, 0, 0))],
        out_specs=pl.BlockSpec((tm, D_MODEL), lambda i, n: (i, 0)),
        out_shape=jax.ShapeDtypeStruct((m, D_MODEL), BF16),
        scratch_shapes=[pltpu.VMEM((tm, D_MODEL), F32)],
        compiler_params=_params("parallel", "arbitrary"),
        name="gated_merge",
    )(z, *branches, w_branch)


NA_ROWS_LOCKSTEP = 2
NA_MASK = -1e30


def na_bias_table(rpb):
    q = np.arange(GRID_W)
    kc = np.arange(GRID_W)
    cs = np.clip(q - NA_WIN_C // 2, 0, GRID_W - NA_WIN_C)
    ok = (kc[None, :] >= cs[:, None]) & (kc[None, :] < cs[:, None] + NA_WIN_C)
    dcol = np.clip(kc[None, :] - q[:, None] + NA_WIN_C - 1, 0, 2 * NA_WIN_C - 2)
    drow = np.arange(NA_WIN_R)[None, :] - np.arange(NA_WIN_R)[:, None] + NA_WIN_R - 1
    t = rpb.astype(F32)[:, drow]
    t = t[:, :, :, dcol]
    t = jnp.transpose(t, (1, 0, 3, 2, 4))
    t = jnp.where(jnp.asarray(ok)[None, None, :, None, :], t, NA_MASK)
    return t.reshape(NA_WIN_R, NA_HEADS, GRID_W, NA_WIN_R * GRID_W)


def _na_kernel(q_ref, k_ref, v_ref, bias_ref, o_ref, *, rows_per_step, rows):
    rblk = pl.program_id(1)
    win = NA_WIN_R * GRID_W
    scale = NA_HEAD_DIM ** -0.5

    pair_w = 2 * NA_HEAD_DIM
    low = lax.broadcasted_iota(jnp.int32, (GRID_W, pair_w), 1) < NA_HEAD_DIM

    def rows_body(it, carry):
        inst = []
        q, kb, vb, off, qstart = [], [], [], [], []
        for t in range(NA_ROWS_LOCKSTEP):
            rl = it * NA_ROWS_LOCKSTEP + t
            r = rblk * rows_per_step + rl
            rs = jnp.clip(r - NA_WIN_R // 2, 0, rows - NA_WIN_R)
            off.append(r - rs)
            qstart.append(pl.multiple_of(rl * GRID_W, GRID_W))
            kstart = pl.multiple_of(rs * GRID_W, GRID_W)
            q.append(q_ref[0, pl.ds(qstart[t], GRID_W), :])
            kb.append(k_ref[0, pl.ds(kstart, win), :])
            vb.append(v_ref[0, pl.ds(kstart, win), :])
            inst += [(t, h) for h in range(NA_HEADS)]

        def pair(x, h):
            return x[:, (h // 2) * pair_w:(h // 2 + 1) * pair_w]

        qm = [jnp.where(low if h % 2 == 0 else ~low, pair(q[t], h), jnp.zeros((), BF16)) for t, h in inst]
        s = [lax.dot_general(qm[i], pair(kb[t], h), (((1,), (1,)), ((), ())), preferred_element_type=F32)
             for i, (t, h) in enumerate(inst)]
        s = [s[i] * scale + bias_ref[off[t], h] for i, (t, h) in enumerate(inst)]
        p = [jnp.exp(x - jnp.max(x, axis=-1, keepdims=True)) for x in s]
        l = [jnp.sum(x, axis=-1, keepdims=True) for x in p]
        o = [jnp.dot(p[i].astype(BF16), pair(vb[t], h), preferred_element_type=F32) / l[i]
             for i, (t, h) in enumerate(inst)]
        for t in range(NA_ROWS_LOCKSTEP):
            base = t * NA_HEADS
            outs = [jnp.where(low, o[base + h], o[base + h + 1]) for h in range(0, NA_HEADS, 2)]
            o_ref[0, pl.ds(qstart[t], GRID_W), :] = jnp.concatenate(outs, axis=1).astype(o_ref.dtype)
        return carry

    lax.fori_loop(0, rows_per_step // NA_ROWS_LOCKSTEP, rows_body, 0)


def neighbourhood_attention(z3, bias, rows_per_step):
    b, l, _ = z3.shape
    rows = l // GRID_W
    w = NA_HEADS * NA_HEAD_DIM
    blk = Z_NA // w
    return pl.pallas_call(
        functools.partial(_na_kernel, rows_per_step=rows_per_step, rows=rows),
        grid=(b, rows // rows_per_step),
        in_specs=[pl.BlockSpec((1, rows_per_step * GRID_W, w), lambda i, r: (i, r, blk)),
                  pl.BlockSpec((1, l, w), lambda i, r: (i, 0, blk + 1)),
                  pl.BlockSpec((1, l, w), lambda i, r: (i, 0, blk + 2)),
                  pl.BlockSpec(bias.shape, lambda i, r: (0, 0, 0, 0))],
        out_specs=pl.BlockSpec((1, rows_per_step * GRID_W, w), lambda i, r: (i, r, 0)),
        out_shape=jax.ShapeDtypeStruct((b, l, w), BF16),
        compiler_params=_params("parallel", "arbitrary"),
        name="neighbourhood_attention",
    )(z3, z3, z3, bias)


def _fft_tables(l):
    n = 2 * l
    n2 = FFT_N2
    n1 = n // n2
    a = np.arange(n1)
    b = np.arange(n2)
    k1 = np.arange(n1)
    t = n2 * a[None, :] + b[:, None]
    th = 2.0 * np.pi * (k1[None, :, None] * t[:, None, :] % n) / n
    fa = np.concatenate([np.cos(th), -np.sin(th)], axis=1)
    th_t = np.transpose(th, (0, 2, 1))
    ga = np.concatenate([np.cos(th_t), -np.sin(th_t)], axis=2)
    ph = 2.0 * np.pi * (np.outer(b, b) % n2) / n2
    cr, ci = np.cos(ph), -np.sin(ph)
    fb = np.block([[cr, -ci], [ci, cr]])
    fbi = np.block([[cr, ci], [-ci, cr]])
    return fa, ga, fb, fbi


def _filter_fft_kernel(hf_ref, hb_ref, fa_hi_ref, fa_lo_ref, fb_hi_ref, fb_lo_ref, o_ref, y_ref, *, n1):
    n2 = FFT_N2
    na = n1 // 2
    inv_n = 1.0 / (n1 * n2)

    def transform(src_ref, combine):
        def step_a(b, carry):
            xb = _split_bf16(src_ref[pl.ds(b, na, stride=n2), :])
            r = _dot_split((fa_hi_ref[b], fa_lo_ref[b]), xb)
            y_ref[pl.ds(b, n1, stride=2 * n2), :] = r[:n1]
            y_ref[pl.ds(b + n2, n1, stride=2 * n2), :] = r[n1:]
            return carry

        lax.fori_loop(0, n2, step_a, 0, unroll=2)

        def step_b(k1, carry):
            r0 = pl.multiple_of(k1 * 2 * n2, 2 * n2)
            x = _dot_split((fb_hi_ref[...], fb_lo_ref[...]), _split_bf16(y_ref[pl.ds(r0, 2 * n2), :]))
            combine(r0, x * inv_n)
            return carry

        lax.fori_loop(0, n1, step_b, 0, unroll=2)

    def store(r0, x):
        o_ref[pl.ds(r0, 2 * n2), :] = x

    def add_conjugate(r0, x):
        o_ref[pl.ds(r0, n2), :] += x[:n2]
        o_ref[pl.ds(r0 + n2, n2), :] -= x[n2:]

    transform(hf_ref, store)
    transform(hb_ref, add_conjugate)


def filter_spectrum(h, fa, fb):
    l = h.shape[0]
    c = h.shape[1] // 2
    n = 2 * l
    n1 = n // FFT_N2
    nch = c // LANES
    return pl.pallas_call(
        functools.partial(_filter_fft_kernel, n1=n1),
        grid=(nch,),
        in_specs=[pl.BlockSpec((l, LANES), lambda i: (0, i)),
                  pl.BlockSpec((l, LANES), lambda i: (0, nch + i)),
                  pl.BlockSpec(fa[0].shape, lambda i: (0, 0, 0)),
                  pl.BlockSpec(fa[1].shape, lambda i: (0, 0, 0)),
                  pl.BlockSpec(fb[0].shape, lambda i: (0, 0)),
                  pl.BlockSpec(fb[1].shape, lambda i: (0, 0))],
        out_specs=pl.BlockSpec((2 * n, LANES), lambda i: (0, i)),
        out_shape=jax.ShapeDtypeStruct((2 * n, c), F32),
        scratch_shapes=[pltpu.VMEM((2 * n, LANES), F32)],
        compiler_params=_params("parallel"),
        name="hyena_filter_spectrum",
    )(h, h, *fa, *fb)


def _hyena_filter_kernel(z_ref, w1_ref, b1_ref, w2_ref, b2_ref, w3_ref, dec_ref, o_ref):
    hid = jnp.sin(jnp.dot(z_ref[...], w1_ref[...], precision=HIGHEST, preferred_element_type=F32) + b1_ref[...])
    hid = jnp.sin(jnp.dot(hid, w2_ref[...], precision=HIGHEST, preferred_element_type=F32) + b2_ref[...])
    h = jnp.dot(hid, w3_ref[...], precision=HIGHEST, preferred_element_type=F32)
    dec = dec_ref[...]
    hf = h[:, :HY_WIDTH] * dec
    hb = h[:, HY_WIDTH:] * dec
    norm = (jnp.sum(jnp.abs(hf), axis=0, keepdims=True) + jnp.sum(jnp.abs(hb), axis=0, keepdims=True)) + RMS_EPS
    o_ref[:, :HY_WIDTH] = hf / norm
    o_ref[:, HY_WIDTH:] = hb / norm


def hyena_filters(l, w1, b1, w2, b2, w3):
    t = jnp.linspace(0.0, 1.0, l, dtype=F32)[:, None]
    w = 2.0 * math.pi * jnp.arange(l, dtype=F32)[:, None] / l
    bands = jnp.linspace(1e-4, HY_POS_BANDS - 1, HY_POS_BANDS, dtype=F32)[None, :]
    z = jnp.concatenate([t, jnp.cos(bands * w), -jnp.sin(bands * w)], axis=-1)
    pad = LANES - z.shape[1]
    z = jnp.pad(z, ((0, 0), (0, pad)))
    w1p = jnp.pad(w1.astype(F32), ((0, pad), (0, 0)))
    max_decay = math.log(HY_DECAY_TARGET) / HY_FAST_DECAY
    min_decay = math.log(HY_DECAY_TARGET) / HY_SLOW_DECAY
    deltas = jnp.abs(jnp.linspace(min_decay, max_decay, HY_WIDTH, dtype=F32))
    dec = jnp.exp(-t * deltas[None, :])
    return pl.pallas_call(
        _hyena_filter_kernel,
        out_shape=jax.ShapeDtypeStruct((l, 2 * HY_WIDTH), F32),
        compiler_params=pltpu.CompilerParams(vmem_limit_bytes=VMEM_LIMIT_BYTES),
        name="hyena_filter_ffn",
    )(z, w1p, b1.reshape(1, -1).astype(F32), w2.astype(F32), b2.reshape(1, -1).astype(F32), w3.astype(F32), dec)


CONV_PAD = 8


def _dwconv3(x, w_ref, pad_ref):
    n = x.shape[0]
    zeros = jnp.zeros((CONV_PAD, x.shape[1]), F32)
    pad_ref[0:CONV_PAD, :] = zeros
    pad_ref[CONV_PAD + n:2 * CONV_PAD + n, :] = zeros
    pad_ref[CONV_PAD:CONV_PAD + n, :] = x
    w = w_ref[...].astype(F32)
    return (pad_ref[CONV_PAD - 1:CONV_PAD - 1 + n, :] * w[0:1] + x * w[1:2]
            + pad_ref[CONV_PAD + 1:CONV_PAD + 1 + n, :] * w[2:3])


def _hyena_kernel(x1_ref, x2_ref, v_ref, w1_ref, w2_ref, wv_ref, skip_ref, kf_ref, fa_ref, ga_ref, fb_ref, fbi_ref,
                  o_ref, xs_ref, y_ref, pad_ref, *, n1):
    n2 = FFT_N2
    na = n1 // 2
    pa, py = FFT_PITCH_X, FFT_PITCH_Y
    vg = _dwconv3(v_ref[0].astype(F32), wv_ref, pad_ref) * _dwconv3(x1_ref[0].astype(F32), w1_ref, pad_ref)
    for a in range(na):
        xs_ref[a * pa:a * pa + n2, :] = vg[a * n2:(a + 1) * n2]

    def fwd_a(b, carry):
        xb = xs_ref[pl.ds(b, na, stride=pa), :].astype(BF16)
        r = jnp.dot(fa_ref[b], xb, preferred_element_type=F32)
        y_ref[pl.ds(b, n1, stride=py), :] = r[:n1]
        y_ref[pl.ds(b + n2, n1, stride=py), :] = r[n1:]
        return carry

    lax.fori_loop(0, n2, fwd_a, 0, unroll=FFT_UNROLL)

    def mid(kg, carry):
        grp = range(FFT_GROUP)
        k1 = [kg * FFT_GROUP + t for t in grp]
        r0 = [pl.multiple_of(k * py, 8) for k in k1]
        f0 = [pl.multiple_of(k * 2 * n2, 2 * n2) for k in k1]
        fb = fb_ref[...]
        x = [jnp.dot(fb, y_ref[pl.ds(r0[t], 2 * n2), :].astype(BF16), preferred_element_type=F32) for t in grp]
        p = []
        for t in grp:
            xr, xi = x[t][:n2], x[t][n2:]
            kr = kf_ref[pl.ds(f0[t], n2), :]
            ki = kf_ref[pl.ds(f0[t] + n2, n2), :]
            p.append(jnp.concatenate([xr * kr - xi * ki, xr * ki + xi * kr], axis=0).astype(BF16))
        fbi = fbi_ref[...]
        q = [jnp.dot(fbi, p[t], preferred_element_type=F32) for t in grp]
        for t in grp:
            y_ref[pl.ds(r0[t], 2 * n2), :] = q[t]
        return carry

    lax.fori_loop(0, n1 // FFT_GROUP, mid, 0)

    def inv_a(b, carry):
        qb = jnp.concatenate([y_ref[pl.ds(b, n1, stride=py), :],
                              y_ref[pl.ds(b + n2, n1, stride=py), :]], axis=0).astype(BF16)
        xs_ref[pl.ds(b, na, stride=pa), :] = jnp.dot(ga_ref[b], qb, preferred_element_type=F32)
        return carry

    lax.fori_loop(0, n2, inv_a, 0, unroll=FFT_UNROLL)

    gate = _dwconv3(x2_ref[0].astype(F32), w2_ref, pad_ref)
    skip = skip_ref[...].astype(F32)
    for a in range(na):
        rows = slice(a * n2, (a + 1) * n2)
        o_ref[0, rows, :] = ((xs_ref[a * pa:a * pa + n2, :] + vg[rows] * skip) * gate[rows]).astype(o_ref.dtype)


def hyena_mixer(z3, w_short, skip, kf, tables):
    b, l, _ = z3.shape
    n1 = 2 * l // FFT_N2
    fa, ga, fb, fbi = tables
    nch = HY_WIDTH // LANES
    blk = Z_HY // LANES

    def zspec(seg):
        return pl.BlockSpec((1, l, LANES), lambda c, i: (i, 0, blk + seg * nch + c))

    def wspec(seg):
        return pl.BlockSpec((3, LANES), lambda c, i: (0, seg * nch + c))

    return pl.pallas_call(
        functools.partial(_hyena_kernel, n1=n1),
        grid=(nch, b),
        in_specs=[zspec(0), zspec(1), zspec(2), wspec(0), wspec(1), wspec(2),
                  pl.BlockSpec((1, LANES), lambda c, i: (0, c)),
                  pl.BlockSpec((4 * l, LANES), lambda c, i: (0, c)),
                  pl.BlockSpec(fa.shape, lambda c, i: (0, 0, 0)),
                  pl.BlockSpec(ga.shape, lambda c, i: (0, 0, 0)),
                  pl.BlockSpec(fb.shape, lambda c, i: (0, 0)),
                  pl.BlockSpec(fbi.shape, lambda c, i: (0, 0))],
        out_specs=pl.BlockSpec((1, l, LANES), lambda c, i: (i, 0, c)),
        out_shape=jax.ShapeDtypeStruct((b, l, HY_WIDTH), BF16),
        scratch_shapes=[pltpu.VMEM((n1 // 2 * FFT_PITCH_X, LANES), F32), pltpu.VMEM((n1 * FFT_PITCH_Y, LANES), F32),
                        pltpu.VMEM((l + 2 * CONV_PAD, LANES), F32)],
        compiler_params=_params("parallel", "arbitrary"),
        name="hyena_mixer",
    )(z3, z3, z3, w_short, w_short, w_short, skip.reshape(1, -1), kf, fa, ga, fb, fbi)


MLA_HW = 2 * LANES
MLA_HEADS_PER_STEP = 2


def _rope_group(g, cs):
    prod = g * cs
    s = prod + pltpu.roll(prod, MLA_ROPE, 1)
    lane = lax.broadcasted_iota(jnp.int32, s.shape, 1)
    return jnp.where(lane < MLA_ROPE, s, 0.0)


def _mla_prep_kernel(cq_ref, ckv_ref, kr_ref, gq_ref, gkv_ref, wq_ref, wkv_ref, cs_ref, q_ref, k_ref, v_ref):
    cs = cs_ref[...]
    hq = _rms_bf16(cq_ref[...], gq_ref[...])
    q = jnp.dot(hq, wq_ref[...], preferred_element_type=F32)
    hkv = _rms_bf16(ckv_ref[...], gkv_ref[...])
    kv = jnp.dot(hkv, wkv_ref[...], preferred_element_type=F32)
    k_rope = _rope_group(kr_ref[...].astype(F32), cs).astype(BF16)
    lane = lax.broadcasted_iota(jnp.int32, (cs.shape[0], LANES), 1)
    ones_lane = jnp.where(lane == 0, 1.0, 0.0).astype(BF16)
    for h in range(MLA_HEADS):
        o = h * MLA_HW
        q_ref[:, o:o + LANES] = q[:, o:o + LANES].astype(BF16)
        q_ref[:, o + LANES:o + MLA_HW] = _rope_group(q[:, o + LANES:o + MLA_HW], cs).astype(BF16)
        k_ref[:, o:o + LANES] = kv[:, o:o + LANES].astype(BF16)
        k_ref[:, o + LANES:o + MLA_HW] = k_rope
        v_ref[:, o:o + LANES] = kv[:, o + LANES:o + MLA_HW].astype(BF16)
        v_ref[:, o + LANES:o + MLA_HW] = ones_lane


def _mla_attn_kernel(q_ref, k_ref, v_ref, o_ref):
    c = (MLA_NOPE + MLA_ROPE) ** -0.5 * math.log2(math.e)
    heads = range(MLA_HEADS_PER_STEP)
    s = [lax.dot_general(q_ref[0, :, h * MLA_HW:(h + 1) * MLA_HW], k_ref[0, :, h * MLA_HW:(h + 1) * MLA_HW],
                         (((1,), (1,)), ((), ())), preferred_element_type=F32) for h in heads]
    for h in heads:
        p = jnp.exp2((s[h] - jnp.max(s[h], axis=-1, keepdims=True)) * c)
        o = jnp.dot(p.astype(BF16), v_ref[0, :, h * MLA_HW:(h + 1) * MLA_HW], preferred_element_type=F32)
        o_ref[0, :, h * MLA_V:(h + 1) * MLA_V] = (o[:, :MLA_V] / o[:, MLA_V:MLA_V + 1]).astype(o_ref.dtype)


def mla_mixer(z, b, l, g_q, g_kv, wq_p, w_ukv, cs_tab, tm, tq):
    m = z.shape[0]
    lt = l // tm
    qp, kp, vp = pl.pallas_call(
        _mla_prep_kernel,
        grid=(m // tm,),
        in_specs=[pl.BlockSpec((tm, MLA_Q_RANK), lambda i: (i, Z_CQ // MLA_Q_RANK)),
                  pl.BlockSpec((tm, MLA_KV_RANK), lambda i: (i, Z_CKV // MLA_KV_RANK)),
                  pl.BlockSpec((tm, LANES), lambda i: (i, Z_KR // LANES)),
                  pl.BlockSpec((1, MLA_Q_RANK), lambda i: (0, 0)),
                  pl.BlockSpec((1, MLA_KV_RANK), lambda i: (0, 0)),
                  pl.BlockSpec(wq_p.shape, lambda i: (0, 0)),
                  pl.BlockSpec(w_ukv.shape, lambda i: (0, 0)),
                  pl.BlockSpec((tm, LANES), lambda i: (i % lt, 0))],
        out_specs=[pl.BlockSpec((tm, MLA_HEADS * MLA_HW), lambda i: (i, 0)),
                   pl.BlockSpec((tm, MLA_HEADS * MLA_HW), lambda i: (i, 0)),
                   pl.BlockSpec((tm, MLA_HEADS * MLA_HW), lambda i: (i, 0))],
        out_shape=[jax.ShapeDtypeStruct((m, MLA_HEADS * MLA_HW), BF16),
                   jax.ShapeDtypeStruct((m, MLA_HEADS * MLA_HW), BF16),
                   jax.ShapeDtypeStruct((m, MLA_HEADS * MLA_HW), BF16)],
        compiler_params=_params("parallel"),
        name="mla_prep",
    )(z, z, z, g_q.reshape(1, -1), g_kv.reshape(1, -1), wq_p, w_ukv, cs_tab)
    qp = qp.reshape(b, l, -1)
    kp = kp.reshape(b, l, -1)
    vp = vp.reshape(b, l, -1)
    hs = MLA_HEADS_PER_STEP
    return pl.pallas_call(
        _mla_attn_kernel,
        grid=(b, MLA_HEADS // hs, l // tq),
        in_specs=[pl.BlockSpec((1, tq, hs * MLA_HW), lambda i, h, t: (i, t, h)),
                  pl.BlockSpec((1, l, hs * MLA_HW), lambda i, h, t: (i, 0, h)),
                  pl.BlockSpec((1, l, hs * MLA_HW), lambda i, h, t: (i, 0, h))],
        out_specs=pl.BlockSpec((1, tq, hs * MLA_V), lambda i, h, t: (i, t, h)),
        out_shape=jax.ShapeDtypeStruct((b, l, MLA_HEADS * MLA_V), BF16),
        compiler_params=_params("parallel", "parallel", "arbitrary"),
        name="mla_attention",
    )(qp, kp, vp)


def _split_bf16(x):
    hi = x.astype(BF16)
    return hi, (x - hi.astype(F32)).astype(BF16)


def _dot_split(a, b):
    (ah, al), (bh, bl) = a, b
    return (jnp.dot(ah, bh, preferred_element_type=F32)
            + (jnp.dot(ah, bl, preferred_element_type=F32) + jnp.dot(al, bh, preferred_element_type=F32)))


def _tri_unit_inverse(mats):
    c = mats[0].shape[0]
    eye = (lax.broadcasted_iota(jnp.int32, (c, c), 0) == lax.broadcasted_iota(jnp.int32, (c, c), 1)).astype(F32)
    ps = [eye - a for a in mats]
    ms = list(mats)
    for _ in range(int(math.log2(c)) - 1):
        splits = [_split_bf16(m) for m in ms]
        ms = [_dot_split(sp, sp) for sp in splits]
        yield
        ps = [_dot_split(_split_bf16(p), _split_bf16(eye + m)) for p, m in zip(ps, ms)]
        yield
    return ps


def _interleave(main, side=None):
    gens = [g for g in (main, side) if g is not None]
    results = [None] * len(gens)
    live = [True] * len(gens)
    while any(live):
        for idx, gen in enumerate(gens):
            if live[idx]:
                try:
                    next(gen)
                except StopIteration as stop:
                    results[idx] = stop.value
                    live[idx] = False
    return results


def _bdot(a, b):
    return jnp.dot(a.astype(BF16), b.astype(BF16), preferred_element_type=F32)


def _bdot_nt(a, b):
    return lax.dot_general(a.astype(BF16), b.astype(BF16), (((1,), (1,)), ((), ())), preferred_element_type=F32)


def _dn_group_setup(chunks, refs):
    q_s, k_s, v_s, g_s, b_s = refs
    cc = DN_CHUNK
    n = range(len(chunks))
    rev = [r for _, r in chunks]
    r0 = [pl.multiple_of(c * cc, cc) for c, _ in chunks]
    q = [q_s[pl.ds(r, cc), :] for r in r0]
    k = [k_s[pl.ds(r, cc), :] for r in r0]
    v = [v_s[pl.ds(r, cc), :] for r in r0]
    lane = [2 if r else 0 for r in rev]
    ri = lax.broadcasted_iota(jnp.int32, (cc, cc), 0)
    ci = lax.broadcasted_iota(jnp.int32, (cc, cc), 1)
    incl = [(ri <= ci) if r else (ri >= ci) for r in rev]
    strict = [(ri < ci) if r else (ri > ci) for r in rev]
    g = [g_s[pl.ds(r, cc), :] for r in r0]
    g1 = [x.astype(BF16) for x in g]
    r1 = [x - h.astype(F32) for x, h in zip(g, g1)]
    g2 = [x.astype(BF16) for x in r1]
    g3 = [(x - h.astype(F32)).astype(BF16) for x, h in zip(r1, g2)]
    ones = [m.astype(BF16) for m in incl]
    gc = [jnp.dot(ones[i], g1[i], preferred_element_type=F32) for i in n]
    gc = [gc[i] + jnp.dot(ones[i], g2[i], preferred_element_type=F32) for i in n]
    gc = [gc[i] + jnp.dot(ones[i], g3[i], preferred_element_type=F32) for i in n]
    yield
    gcol = [gc[i][:, lane[i]:lane[i] + 1] for i in n]
    grow = [jnp.transpose(gc[i])[lane[i]:lane[i] + 1, :] for i in n]
    beta = [b_s[pl.ds(r0[i], cc), :][:, lane[i] + 1:lane[i] + 2] for i in n]
    g_last = [gcol[i][0:1, :] if rev[i] else gcol[i][cc - 1:cc, :] for i in n]
    decay = [jnp.where(incl[i], jnp.exp(jnp.where(incl[i], gcol[i] - grow[i], 0.0)), 0.0) for i in n]
    e_g = [jnp.exp(x) for x in gcol]
    kb = [k[i] * beta[i] for i in n]
    a_both = [_bdot_nt(jnp.concatenate([kb[i], q[i]], axis=0), k[i]) for i in n]
    yield
    a_kk = [jnp.where(strict[i], a_both[i][:cc] * decay[i], 0.0) for i in n]
    a_qk = [jnp.where(incl[i], a_both[i][cc:] * decay[i], 0.0) for i in n]
    t_inv = yield from _tri_unit_inverse(a_kk)
    uw = [_bdot(t_inv[i], jnp.concatenate([v[i] * beta[i], kb[i] * e_g[i]], axis=1)) for i in n]
    yield
    k_dec = [k[i] * jnp.exp(g_last[i] - gcol[i]) for i in n]
    wq = [jnp.concatenate([uw[i][:, DN_DV:], q[i] * e_g[i]], axis=0).astype(BF16) for i in n]
    ak = [jnp.concatenate([a_qk[i], jnp.transpose(k_dec[i])], axis=0).astype(BF16) for i in n]
    return [(wq[i], ak[i], uw[i][:, :DN_DV], jnp.exp(g_last[i])) for i in n]


def _dn_group_steps(states, setups, out_refs, chunks):
    cc = DN_CHUNK
    dirs = range(len(states))
    for t in range(len(setups[0])):
        st = [setups[d][t] for d in dirs]
        r = [jnp.dot(st[d][0], states[d].astype(BF16), preferred_element_type=F32) for d in dirs]
        yield
        v_new = [st[d][2] - r[d][:cc] for d in dirs]
        r2 = [jnp.dot(st[d][1], v_new[d].astype(BF16), preferred_element_type=F32) for d in dirs]
        yield
        states = [states[d] * st[d][3] + r2[d][cc:] for d in dirs]
        for d in dirs:
            out_refs[d][pl.ds(pl.multiple_of(chunks[d][t] * cc, cc), cc), :] = r[d][cc:] + r2[d][:cc]
    return states


def _dn_kernel(zq_ref, zk_ref, zv_ref, zg_ref, ab_ref, wq_ref, wk_ref, wv_ref, alog_ref, dtb_ref, gn_ref, o_ref,
               q_s, k_s, v_s, g_s, b_s, of_s, ob_s, pad_ref, *, nchunks):
    def act(z_ref, w_ref):
        c = _dwconv3(z_ref[0].astype(F32), w_ref, pad_ref)
        return _silu(c)

    def l2n(x):
        return x * lax.rsqrt(jnp.sum(x * x, axis=-1, keepdims=True) + RMS_EPS)

    q_s[...] = l2n(act(zq_ref, wq_ref)) * (DN_DK ** -0.5)
    k_s[...] = l2n(act(zk_ref, wk_ref))
    v_s[...] = act(zv_ref, wv_ref)
    ab = ab_ref[0]
    x = ab + dtb_ref[0]
    softplus = jnp.maximum(x, 0.0) + jnp.log1p(jnp.exp(-jnp.abs(x)))
    g_s[...] = -jnp.exp(alog_ref[0]) * softplus
    b_s[...] = jax.nn.sigmoid(ab)
    refs = (q_s, k_s, v_s, g_s, b_s)

    ngroups = nchunks // DN_GROUP

    def group_chunks(gi):
        fwd = [gi * DN_GROUP + t for t in range(DN_GROUP)]
        return fwd, [nchunks - 1 - c for c in fwd]

    def setup_gen(gi):
        fwd, bwd = group_chunks(gi)
        return _dn_group_setup([(c, False) for c in fwd] + [(c, True) for c in bwd], refs)

    def steps_gen(gi, states, flat):
        setups = [tuple(flat[4 * i:4 * i + 4]) for i in range(2 * DN_GROUP)]
        return _dn_group_steps(states, [setups[:DN_GROUP], setups[DN_GROUP:]], (of_s, ob_s), group_chunks(gi))

    def flatten(setups):
        return [x for st in setups for x in st]

    def body(gi, carry):
        nxt, states = _interleave(setup_gen(gi + 1), steps_gen(gi, list(carry[:2]), carry[2:]))
        return (*states, *flatten(nxt))

    zero = jnp.zeros((DN_DK, DN_DV), F32)
    (first,) = _interleave(setup_gen(0))
    carry = lax.fori_loop(0, ngroups - 1, body, (zero, zero, *flatten(first)))
    _interleave(steps_gen(ngroups - 1, list(carry[:2]), carry[2:]))

    o = of_s[...] + ob_s[...]
    o = o * lax.rsqrt(jnp.mean(o * o, axis=-1, keepdims=True) + RMS_EPS) * gn_ref[...]
    gate = zg_ref[0].astype(F32)
    o_ref[0] = (o * _silu(gate)).astype(o_ref.dtype)


def deltanet_mixer(z3, ab3, w_conv, alog_p, dtb_p, g_norm):
    b, l, _ = z3.shape
    blk = Z_DNQKV // LANES
    gblk = Z_DNGATE // LANES
    hh = DN_HEADS

    def zspec(off):
        return pl.BlockSpec((1, l, LANES), lambda i, h: (i, 0, off + h))

    def wspec(seg):
        return pl.BlockSpec((3, LANES), lambda i, h: (0, seg * hh + h))

    vec = pl.BlockSpec((1, 1, LANES), lambda i, h: (h, 0, 0))
    seq = pltpu.VMEM((l, LANES), F32)
    return pl.pallas_call(
        functools.partial(_dn_kernel, nchunks=l // DN_CHUNK),
        grid=(b, hh),
        in_specs=[zspec(blk), zspec(blk + hh), zspec(blk + 2 * hh), zspec(gblk),
                  pl.BlockSpec((1, l, LANES), lambda i, h: (i, 0, h)),
                  wspec(0), wspec(1), wspec(2), vec, vec,
                  pl.BlockSpec((1, LANES), lambda i, h: (0, 0))],
        out_specs=pl.BlockSpec((1, l, LANES), lambda i, h: (i, 0, h)),
        out_shape=jax.ShapeDtypeStruct((b, l, hh * DN_DV), BF16),
        scratch_shapes=[seq] * 7 + [pltpu.VMEM((l + 2 * CONV_PAD, LANES), F32)],
        compiler_params=_params("parallel", "arbitrary"),
        name="deltanet_mixer",
    )(z3, z3, z3, z3, ab3, w_conv, w_conv, w_conv, alog_p, dtb_p, g_norm.reshape(1, -1))


def _rotate_half_cols(w):
    half = w.shape[-1] // 2
    return jnp.concatenate([-w[..., half:], w[..., :half]], axis=-1)


def _prep_w_in(w_in):
    cuts = np.cumsum([0, 1536, 1536, MLA_Q_RANK, MLA_KV_RANK, MLA_ROPE, 1536, 512, 16])
    dnab = w_in[:, cuts[7]:cuts[8]]
    w16 = w_in.astype(BF16)
    na, hy, cq, ckv, kr, dnqkv, dngate = (w16[:, cuts[i]:cuts[i + 1]] for i in range(7))
    gates = w16[:, cuts[8]:]
    pad = jnp.zeros((w_in.shape[0], Z_GATE - Z_KR - 2 * MLA_ROPE), BF16)
    main = jnp.concatenate([na, hy, dnqkv, dngate, cq, ckv, kr, _rotate_half_cols(kr), pad, 0.5 * gates], axis=1)
    ab = dnab.reshape(-1, 2, 2, DN_HEADS)
    ab = jnp.transpose(ab, (0, 3, 1, 2)).reshape(-1, DN_HEADS, 4)
    ab = jnp.pad(ab, ((0, 0), (0, 0), (0, LANES - 4))).reshape(-1, DN_HEADS * LANES)
    return main, ab.astype(BF16)


def _prep_w_uq(w_uq):
    k = w_uq.shape[0]
    w = w_uq.reshape(k, MLA_HEADS, MLA_NOPE + MLA_ROPE)
    rope_w = w[:, :, MLA_NOPE:]
    return jnp.concatenate([w, _rotate_half_cols(rope_w)], axis=-1).reshape(k, MLA_HEADS * MLA_HW).astype(BF16)


def _head_lane_vec(p):
    v = jnp.zeros((DN_HEADS, 1, LANES), F32)
    v = v.at[:, 0, 0].set(p[0].astype(F32))
    return v.at[:, 0, 2].set(p[1].astype(F32))


def _rope_table(l):
    half = MLA_ROPE // 2
    inv = ROPE_THETA ** (-jnp.arange(half, dtype=F32) / half)
    ang = jnp.arange(l, dtype=F32)[:, None] * inv[None, :]
    cos, sin = jnp.cos(ang), jnp.sin(ang)
    return jnp.concatenate([cos, cos, sin, sin], axis=-1)


def _split_const(x):
    hi = x.astype(BF16)
    lo = (x - hi.astype(np.float64)).astype(BF16)
    return jnp.asarray(hi), jnp.asarray(lo)


def _pick(n, pref):
    for t in pref:
        if n % t == 0:
            return t
    return n


def trunk(x_parts, l, norm_mix, w_in, na_rpb, hy_short, hy_skip, hy_w1, hy_b1, hy_w2, hy_b2, hy_w3,
          mla_g_q, mla_g_kv, mla_w_uq, mla_w_ukv, dn_conv, dn_a_log, dn_dt_bias, dn_g_norm,
          w_branch, w_out, norm_mlp, w_up, w_down, norm_final):
    d = x_parts[0].shape[1]
    part_rows = tuple(p.shape[0] for p in x_parts)
    m = sum(part_rows)
    b = m // l
    depth = w_in.shape[0]
    common = math.gcd(*part_rows)
    tm_big = _pick(common, (1024, 512, 256, 128))
    tm_mid = _pick(common, (512, 256, 128))
    fa, ga, fb, fbi = _fft_tables(l)
    n1 = 2 * l // FFT_N2
    tables = (jnp.asarray(fa[:, :, :n1 // 2], BF16), jnp.asarray(ga[:, :n1 // 2, :], BF16),
              jnp.asarray(fb, BF16), jnp.asarray(fbi, BF16))
    fa_split, fb_split = _split_const(fa[:, :, :n1 // 2]), _split_const(fb)
    cs_tab = _rope_table(l)
    x = tuple(x_parts)
    for layer in range(depth):
        last = layer == depth - 1
        w_main, w_ab = _prep_w_in(w_in[layer])
        z = norm_matmul(x, norm_mix[layer], w_main, BF16, tm_big, 1024)
        ab = norm_matmul(x, norm_mix[layer], w_ab, F32, tm_big, DN_HEADS * LANES)
        z3 = z.reshape(b, l, Z_COLS)
        br_a = neighbourhood_attention(z3, na_bias_table(na_rpb[layer]), _pick(l // GRID_W, (8, 4, 2, 1)))
        taps = hyena_filters(l, hy_w1[layer], hy_b1[layer], hy_w2[layer], hy_b2[layer], hy_w3[layer])
        kf = filter_spectrum(taps, fa_split, fb_split)
        br_b = hyena_mixer(z3, hy_short[layer], hy_skip[layer], kf, tables)
        br_c = mla_mixer(z, b, l, mla_g_q[layer], mla_g_kv[layer], _prep_w_uq(mla_w_uq[layer]),
                         mla_w_ukv[layer].astype(BF16), cs_tab, tm_mid, _pick(l, (512, 256, 128)))
        br_d = deltanet_mixer(z3, ab.reshape(b, l, -1), dn_conv[layer], _head_lane_vec(dn_a_log[layer]),
                              _head_lane_vec(dn_dt_bias[layer]), dn_g_norm[layer])
        branches = [t.reshape(m, BRANCH_W) for t in (br_a, br_b, br_c, br_d)]
        merged = gated_merge(z, branches, (0.5 * w_branch[layer]).astype(BF16), tm_mid)
        x_mid = matmul_residual(merged, w_out[layer].astype(BF16), x, tm_mid)
        x = tuple(mlp_block(x_mid, norm_mlp[layer], w_up[layer].astype(BF16), w_down[layer].astype(BF16), norm_final,
                            last, tm_mid, 1024, part_rows if last else (m,)))
    return x


def kernel(x_prompt, x_sample, norm_mix, w_in, na_rpb, hy_short, hy_skip, hy_w1, hy_b1, hy_w2, hy_b2, hy_w3,
           mla_g_q, mla_g_kv, mla_w_uq, mla_w_ukv, dn_conv, dn_a_log, dn_dt_bias, dn_g_norm,
           w_branch, w_out, norm_mlp, w_up, w_down, norm_final):
    assert x_prompt.shape[1:] == x_sample.shape[1:]
    l, d = x_prompt.shape[1:]
    y_prompt, y_sample = trunk((x_prompt.reshape(-1, d), x_sample.reshape(-1, d)), l, norm_mix, w_in, na_rpb,
                               hy_short, hy_skip, hy_w1, hy_b1, hy_w2, hy_b2, hy_w3, mla_g_q, mla_g_kv, mla_w_uq,
                               mla_w_ukv, dn_conv, dn_a_log, dn_dt_bias, dn_g_norm, w_branch, w_out, norm_mlp,
                               w_up, w_down, norm_final)
    return (y_prompt.reshape(x_prompt.shape), y_sample.reshape(x_sample.shape))
```

```python
import functools
import math

import jax
import jax.numpy as jnp
import numpy as np
from jax import lax
from jax.experimental import pallas as pl
from jax.experimental.pallas import tpu as pltpu

F32 = jnp.float32
BF16 = jnp.bfloat16
HIGHEST = lax.Precision.HIGHEST

VMEM_LIMIT_BYTES = 56 * 1024 * 1024
LANES = 128

D_MODEL = 2048
RMS_EPS = 1e-6
GRID_W = 64
N_BRANCH = 4
BRANCH_W = 512
NA_HEADS = 8
NA_HEAD_DIM = 64
NA_WIN_R = 8
NA_WIN_C = 16
HY_WIDTH = 512
HY_POS_BANDS = 16
HY_FILT_HIDDEN = 64
HY_FAST_DECAY = 0.3
HY_SLOW_DECAY = 1.5
HY_DECAY_TARGET = 1e-2
MLA_HEADS = 4
MLA_Q_RANK = 512
MLA_KV_RANK = 256
MLA_NOPE = 128
MLA_ROPE = 64
MLA_V = 128
ROPE_THETA = 10000.0
DN_HEADS = 4
DN_DK = 128
DN_DV = 128
DN_CHUNK = 64
DN_GROUP = 4
D_FF = 4 * D_MODEL

Z_NA = 0
Z_HY = 1536
Z_DNQKV = 3072
Z_DNGATE = 4608
Z_CQ = 5120
Z_CKV = 5632
Z_KR = 5888
Z_GATE = 6144
Z_COLS = Z_GATE + N_BRANCH * D_MODEL

FFT_N2 = 128
FFT_UNROLL = 8
FFT_GROUP = 4
FFT_PITCH_X = FFT_N2 + 8
FFT_PITCH_Y = 2 * FFT_N2 + 8


def _params(*sem):
    return pltpu.CompilerParams(dimension_semantics=sem, vmem_limit_bytes=VMEM_LIMIT_BYTES)


def _silu(x):
    return 0.5 * x * (jnp.tanh(0.5 * x) + 1.0)


def _rms_bf16(x, g):
    xf = x.astype(F32)
    y = xf * lax.rsqrt(jnp.mean(xf * xf, axis=-1, keepdims=True) + RMS_EPS)
    return (y * g).astype(BF16)


def _part_tiles(parts, tm):
    assert all(p.shape[0] % tm == 0 for p in parts)
    return tuple(p.shape[0] // tm for p in parts)


def _part_specs(tiles, block):
    specs, start = [], 0
    for count in tiles:
        specs.append(pl.BlockSpec(block, functools.partial(
            lambda i, j, start, count: (jnp.clip(i - start, 0, count - 1), 0), start=start, count=count)))
        start += count
    return specs


def _for_owning_part(tiles, refs, fn):
    if len(refs) == 1:
        fn(refs[0])
        return
    i = pl.program_id(0)
    start = 0
    for ref, count in zip(refs, tiles):
        pl.when((i >= start) & (i < start + count))(functools.partial(fn, ref))
        start += count


def _norm_mm_kernel(*refs, tiles):
    x_refs = refs[:len(tiles)]
    g_ref, w_ref, ws_ref, o_ref, os_ref, h_ref = refs[len(tiles):]

    @pl.when(pl.program_id(1) == 0)
    def _():
        def norm(x_ref):
            h_ref[...] = _rms_bf16(x_ref[...], g_ref[...])

        _for_owning_part(tiles, x_refs, norm)
        os_ref[...] = jnp.dot(h_ref[...], ws_ref[...], preferred_element_type=F32)

    o_ref[...] = jnp.dot(h_ref[...], w_ref[...], preferred_element_type=F32).astype(o_ref.dtype)


def norm_matmul(x_parts, g, w, w_side, tm, tn):
    k, n = w.shape
    ns = w_side.shape[1]
    tiles = _part_tiles(x_parts, tm)
    rows = sum(tiles) * tm
    return pl.pallas_call(
        functools.partial(_norm_mm_kernel, tiles=tiles),
        grid=(sum(tiles), n // tn),
        in_specs=_part_specs(tiles, (tm, k)) + [pl.BlockSpec((1, k), lambda i, j: (0, 0)),
                                                 pl.BlockSpec((k, tn), lambda i, j: (0, j)),
                                                 pl.BlockSpec((k, ns), lambda i, j: (0, 0))],
        out_specs=[pl.BlockSpec((tm, tn), lambda i, j: (i, j)),
                   pl.BlockSpec((tm, ns), lambda i, j: (i, 0))],
        out_shape=[jax.ShapeDtypeStruct((rows, n), BF16), jax.ShapeDtypeStruct((rows, ns), F32)],
        scratch_shapes=[pltpu.VMEM((tm, k), BF16)],
        compiler_params=_params("parallel", "arbitrary"),
        name="norm_matmul",
    )(*x_parts, g.reshape(1, k), w, w_side)


def _mm_res_kernel(*refs, tiles):
    a_ref, w_ref = refs[:2]
    r_refs = refs[2:2 + len(tiles)]
    o_ref = refs[-1]
    y = jnp.dot(a_ref[...], w_ref[...], preferred_element_type=F32)

    def add(r_ref):
        o_ref[...] = r_ref[...] + y

    _for_owning_part(tiles, r_refs, add)


def matmul_residual(a, w, r_parts, tm):
    m, k = a.shape
    n = w.shape[1]
    tiles = _part_tiles(r_parts, tm)
    return pl.pallas_call(
        functools.partial(_mm_res_kernel, tiles=tiles),
        grid=(m // tm, 1),
        in_specs=[pl.BlockSpec((tm, k), lambda i, j: (i, 0)),
                  pl.BlockSpec((k, n), lambda i, j: (0, 0))] + _part_specs(tiles, (tm, n)),
        out_specs=pl.BlockSpec((tm, n), lambda i, j: (i, 0)),
        out_shape=jax.ShapeDtypeStruct((m, n), F32),
        compiler_params=_params("parallel", "arbitrary"),
        name="matmul_residual",
    )(a, w, *r_parts)


def _mlp_kernel(x_ref, g_ref, wu_ref, wd_ref, gf_ref, *refs, final_norm, tiles):
    o_refs, h_ref = refs[:-1], refs[-1]
    j = pl.program_id(1)

    def run(o_ref):
        @pl.when(j == 0)
        def _():
            x = x_ref[...]
            h_ref[...] = _rms_bf16(x, g_ref[...])
            o_ref[...] = x

        u = jnp.dot(h_ref[...], wu_ref[...], preferred_element_type=F32)
        a = jnp.square(jnp.maximum(u, 0.0)).astype(BF16)
        o_ref[...] += jnp.dot(a, wd_ref[...], preferred_element_type=F32)

        if final_norm:
            @pl.when(j == pl.num_programs(1) - 1)
            def _():
                y = o_ref[...]
                o_ref[...] = y * lax.rsqrt(jnp.mean(y * y, axis=-1, keepdims=True) + RMS_EPS) * gf_ref[...]

    _for_owning_part(tiles, o_refs, run)


def mlp_block(x, g, w_up, w_down, g_final, final_norm, tm, tf, out_rows):
    m, d = x.shape
    f = w_up.shape[1]
    assert sum(out_rows) == m and all(r % tm == 0 for r in out_rows)
    tiles = tuple(r // tm for r in out_rows)
    return pl.pallas_call(
        functools.partial(_mlp_kernel, final_norm=final_norm, tiles=tiles),
        grid=(m // tm, f // tf),
        in_specs=[pl.BlockSpec((tm, d), lambda i, j: (i, 0)),
                  pl.BlockSpec((1, d), lambda i, j: (0, 0)),
                  pl.BlockSpec((d, tf), lambda i, j: (0, j)),
                  pl.BlockSpec((tf, d), lambda i, j: (j, 0)),
                  pl.BlockSpec((1, d), lambda i, j: (0, 0))],
        out_specs=_part_specs(tiles, (tm, d)),
        out_shape=[jax.ShapeDtypeStruct((r, d), F32) for r in out_rows],
        scratch_shapes=[pltpu.VMEM((tm, d), BF16)],
        compiler_params=_params("parallel", "arbitrary"),
        name="mlp_block",
    )(x, g.reshape(1, d), w_up, w_down, g_final.reshape(1, d))


def _merge_kernel(gate_ref, ba_ref, bb_ref, bc_ref, bd_ref, wb_ref, o_ref, acc_ref):
    n = pl.program_id(1)

    def contribution(b_ref):
        y = jnp.dot(b_ref[...], wb_ref[0], preferred_element_type=F32)
        return (jnp.tanh(gate_ref[...].astype(F32)) + 1.0) * y

    @pl.when(n == 0)
    def _():
        acc_ref[...] = contribution(ba_ref)

    @pl.when(n == 1)
    def _():
        acc_ref[...] += contribution(bb_ref)

    @pl.when(n == 2)
    def _():
        acc_ref[...] += contribution(bc_ref)

    @pl.when(n == 3)
    def _():
        o_ref[...] = (acc_ref[...] + contribution(bd_ref)).astype(o_ref.dtype)


def gated_merge(z, branches, w_branch, tm):
    m = z.shape[0]
    gate_blk0 = Z_GATE // D_MODEL
    br_spec = pl.BlockSpec((tm, BRANCH_W), lambda i, n: (i, 0))
    return pl.pallas_call(
        _merge_kernel,
        grid=(m // tm, N_BRANCH),
        in_specs=[pl.BlockSpec((tm, D_MODEL), lambda i, n: (i, gate_blk0 + n)),
                  br_spec, br_spec, br_spec, br_spec,
                  pl.BlockSpec((1, BRANCH_W, D_MODEL), lambda i, n: (n, 0, 0))],
        out_specs=pl.BlockSpec((tm, D_MODEL), lambda i, n: (i, 0)),
        out_shape=jax.ShapeDtypeStruct((m, D_MODEL), BF16),
        scratch_shapes=[pltpu.VMEM((tm, D_MODEL), F32)],
        compiler_params=_params("parallel", "arbitrary"),
        name="gated_merge",
    )(z, *branches, w_branch)


NA_ROWS_LOCKSTEP = 2
NA_MASK = -1e30


def na_bias_table(rpb):
    q = np.arange(GRID_W)
    kc = np.arange(GRID_W)
    cs = np.clip(q - NA_WIN_C // 2, 0, GRID_W - NA_WIN_C)
    ok = (kc[None, :] >= cs[:, None]) & (kc[None, :] < cs[:, None] + NA_WIN_C)
    dcol = np.clip(kc[None, :] - q[:, None] + NA_WIN_C - 1, 0, 2 * NA_WIN_C - 2)
    drow = np.arange(NA_WIN_R)[None, :] - np.arange(NA_WIN_R)[:, None] + NA_WIN_R - 1
    t = rpb.astype(F32)[:, drow]
    t = t[:, :, :, dcol]
    t = jnp.transpose(t, (1, 0, 3, 2, 4))
    t = jnp.where(jnp.asarray(ok)[None, None, :, None, :], t, NA_MASK)
    return t.reshape(NA_WIN_R, NA_HEADS, GRID_W, NA_WIN_R * GRID_W)


def _na_kernel(q_ref, k_ref, v_ref, bias_ref, o_ref, *, rows_per_step, rows):
    rblk = pl.program_id(1)
    win = NA_WIN_R * GRID_W
    scale = NA_HEAD_DIM ** -0.5

    pair_w = 2 * NA_HEAD_DIM
    low = lax.broadcasted_iota(jnp.int32, (GRID_W, pair_w), 1) < NA_HEAD_DIM

    def rows_body(it, carry):
        inst = []
        q, kb, vb, off, qstart = [], [], [], [], []
        for t in range(NA_ROWS_LOCKSTEP):
            rl = it * NA_ROWS_LOCKSTEP + t
            r = rblk * rows_per_step + rl
            rs = jnp.clip(r - NA_WIN_R // 2, 0, rows - NA_WIN_R)
            off.append(r - rs)
            qstart.append(pl.multiple_of(rl * GRID_W, GRID_W))
            kstart = pl.multiple_of(rs * GRID_W, GRID_W)
            q.append(q_ref[0, pl.ds(qstart[t], GRID_W), :])
            kb.append(k_ref[0, pl.ds(kstart, win), :])
            vb.append(v_ref[0, pl.ds(kstart, win), :])
            inst += [(t, h) for h in range(NA_HEADS)]

        def pair(x, h):
            return x[:, (h // 2) * pair_w:(h // 2 + 1) * pair_w]

        qm = [jnp.where(low if h % 2 == 0 else ~low, pair(q[t], h), jnp.zeros((), BF16)) for t, h in inst]
        s = [lax.dot_general(qm[i], pair(kb[t], h), (((1,), (1,)), ((), ())), preferred_element_type=F32)
             for i, (t, h) in enumerate(inst)]
        s = [s[i] * scale + bias_ref[off[t], h] for i, (t, h) in enumerate(inst)]
        p = [jnp.exp(x - jnp.max(x, axis=-1, keepdims=True)) for x in s]
        l = [jnp.sum(x, axis=-1, keepdims=True) for x in p]
        o = [jnp.dot(p[i].astype(BF16), pair(vb[t], h), preferred_element_type=F32) / l[i]
             for i, (t, h) in enumerate(inst)]
        for t in range(NA_ROWS_LOCKSTEP):
            base = t * NA_HEADS
            outs = [jnp.where(low, o[base + h], o[base + h + 1]) for h in range(0, NA_HEADS, 2)]
            o_ref[0, pl.ds(qstart[t], GRID_W), :] = jnp.concatenate(outs, axis=1).astype(o_ref.dtype)
        return carry

    lax.fori_loop(0, rows_per_step // NA_ROWS_LOCKSTEP, rows_body, 0)


def neighbourhood_attention(z3, bias, rows_per_step):
    b, l, _ = z3.shape
    rows = l // GRID_W
    w = NA_HEADS * NA_HEAD_DIM
    blk = Z_NA // w
    return pl.pallas_call(
        functools.partial(_na_kernel, rows_per_step=rows_per_step, rows=rows),
        grid=(b, rows // rows_per_step),
        in_specs=[pl.BlockSpec((1, rows_per_step * GRID_W, w), lambda i, r: (i, r, blk)),
                  pl.BlockSpec((1, l, w), lambda i, r: (i, 0, blk + 1)),
                  pl.BlockSpec((1, l, w), lambda i, r: (i, 0, blk + 2)),
                  pl.BlockSpec(bias.shape, lambda i, r: (0, 0, 0, 0))],
        out_specs=pl.BlockSpec((1, rows_per_step * GRID_W, w), lambda i, r: (i, r, 0)),
        out_shape=jax.ShapeDtypeStruct((b, l, w), BF16),
        compiler_params=_params("parallel", "arbitrary"),
        name="neighbourhood_attention",
    )(z3, z3, z3, bias)


def _fft_tables(l):
    n = 2 * l
    n2 = FFT_N2
    n1 = n // n2
    a = np.arange(n1)
    b = np.arange(n2)
    k1 = np.arange(n1)
    t = n2 * a[None, :] + b[:, None]
    th = 2.0 * np.pi * (k1[None, :, None] * t[:, None, :] % n) / n
    fa = np.concatenate([np.cos(th), -np.sin(th)], axis=1)
    th_t = np.transpose(th, (0, 2, 1))
    ga = np.concatenate([np.cos(th_t), -np.sin(th_t)], axis=2)
    ph = 2.0 * np.pi * (np.outer(b, b) % n2) / n2
    cr, ci = np.cos(ph), -np.sin(ph)
    fb = np.block([[cr, -ci], [ci, cr]])
    fbi = np.block([[cr, ci], [-ci, cr]])
    return fa, ga, fb, fbi


def _filter_fft_kernel(hf_ref, hb_ref, fa_hi_ref, fa_lo_ref, fb_hi_ref, fb_lo_ref, o_ref, y_ref, *, n1):
    n2 = FFT_N2
    na = n1 // 2
    inv_n = 1.0 / (n1 * n2)

    def transform(src_ref, combine):
        def step_a(b, carry):
            xb = _split_bf16(src_ref[pl.ds(b, na, stride=n2), :])
            r = _dot_split((fa_hi_ref[b], fa_lo_ref[b]), xb)
            y_ref[pl.ds(b, n1, stride=2 * n2), :] = r[:n1]
            y_ref[pl.ds(b + n2, n1, stride=2 * n2), :] = r[n1:]
            return carry

        lax.fori_loop(0, n2, step_a, 0, unroll=2)

        def step_b(k1, carry):
            r0 = pl.multiple_of(k1 * 2 * n2, 2 * n2)
            x = _dot_split((fb_hi_ref[...], fb_lo_ref[...]), _split_bf16(y_ref[pl.ds(r0, 2 * n2), :]))
            combine(r0, x * inv_n)
            return carry

        lax.fori_loop(0, n1, step_b, 0, unroll=2)

    def store(r0, x):
        o_ref[pl.ds(r0, 2 * n2), :] = x

    def add_conjugate(r0, x):
        o_ref[pl.ds(r0, n2), :] += x[:n2]
        o_ref[pl.ds(r0 + n2, n2), :] -= x[n2:]

    transform(hf_ref, store)
    transform(hb_ref, add_conjugate)


def filter_spectrum(h, fa, fb):
    l = h.shape[0]
    c = h.shape[1] // 2
    n = 2 * l
    n1 = n // FFT_N2
    nch = c // LANES
    return pl.pallas_call(
        functools.partial(_filter_fft_kernel, n1=n1),
        grid=(nch,),
        in_specs=[pl.BlockSpec((l, LANES), lambda i: (0, i)),
                  pl.BlockSpec((l, LANES), lambda i: (0, nch + i)),
                  pl.BlockSpec(fa[0].shape, lambda i: (0, 0, 0)),
                  pl.BlockSpec(fa[1].shape, lambda i: (0, 0, 0)),
                  pl.BlockSpec(fb[0].shape, lambda i: (0, 0)),
                  pl.BlockSpec(fb[1].shape, lambda i: (0, 0))],
        out_specs=pl.BlockSpec((2 * n, LANES), lambda i: (0, i)),
        out_shape=jax.ShapeDtypeStruct((2 * n, c), F32),
        scratch_shapes=[pltpu.VMEM((2 * n, LANES), F32)],
        compiler_params=_params("parallel"),
        name="hyena_filter_spectrum",
    )(h, h, *fa, *fb)


def _hyena_filter_kernel(z_ref, w1_ref, b1_ref, w2_ref, b2_ref, w3_ref, dec_ref, o_ref):
    hid = jnp.sin(jnp.dot(z_ref[...], w1_ref[...], precision=HIGHEST, preferred_element_type=F32) + b1_ref[...])
    hid = jnp.sin(jnp.dot(hid, w2_ref[...], precision=HIGHEST, preferred_element_type=F32) + b2_ref[...])
    h = jnp.dot(hid, w3_ref[...], precision=HIGHEST, preferred_element_type=F32)
    dec = dec_ref[...]
    hf = h[:, :HY_WIDTH] * dec
    hb = h[:, HY_WIDTH:] * dec
    norm = (jnp.sum(jnp.abs(hf), axis=0, keepdims=True) + jnp.sum(jnp.abs(hb), axis=0, keepdims=True)) + RMS_EPS
    o_ref[:, :HY_WIDTH] = hf / norm
    o_ref[:, HY_WIDTH:] = hb / norm


def hyena_filters(l, w1, b1, w2, b2, w3):
    t = jnp.linspace(0.0, 1.0, l, dtype=F32)[:, None]
    w = 2.0 * math.pi * jnp.arange(l, dtype=F32)[:, None] / l
    bands = jnp.linspace(1e-4, HY_POS_BANDS - 1, HY_POS_BANDS, dtype=F32)[None, :]
    z = jnp.concatenate([t, jnp.cos(bands * w), -jnp.sin(bands * w)], axis=-1)
    pad = LANES - z.shape[1]
    z = jnp.pad(z, ((0, 0), (0, pad)))
    w1p = jnp.pad(w1.astype(F32), ((0, pad), (0, 0)))
    max_decay = math.log(HY_DECAY_TARGET) / HY_FAST_DECAY
    min_decay = math.log(HY_DECAY_TARGET) / HY_SLOW_DECAY
    deltas = jnp.abs(jnp.linspace(min_decay, max_decay, HY_WIDTH, dtype=F32))
    dec = jnp.exp(-t * deltas[None, :])
    return pl.pallas_call(
        _hyena_filter_kernel,
        out_shape=jax.ShapeDtypeStruct((l, 2 * HY_WIDTH), F32),
        compiler_params=pltpu.CompilerParams(vmem_limit_bytes=VMEM_LIMIT_BYTES),
        name="hyena_filter_ffn",
    )(z, w1p, b1.reshape(1, -1).astype(F32), w2.astype(F32), b2.reshape(1, -1).astype(F32), w3.astype(F32), dec)


CONV_PAD = 8


def _dwconv3(x, w_ref, pad_ref):
    n = x.shape[0]
    zeros = jnp.zeros((CONV_PAD, x.shape[1]), F32)
    pad_ref[0:CONV_PAD, :] = zeros
    pad_ref[CONV_PAD + n:2 * CONV_PAD + n, :] = zeros
    pad_ref[CONV_PAD:CONV_PAD + n, :] = x
    w = w_ref[...].astype(F32)
    return (pad_ref[CONV_PAD - 1:CONV_PAD - 1 + n, :] * w[0:1] + x * w[1:2]
            + pad_ref[CONV_PAD + 1:CONV_PAD + 1 + n, :] * w[2:3])


def _hyena_kernel(x1_ref, x2_ref, v_ref, w1_ref, w2_ref, wv_ref, skip_ref, kf_ref, fa_ref, ga_ref, fb_ref, fbi_ref,
                  o_ref, xs_ref, y_ref, pad_ref, *, n1):
    n2 = FFT_N2
    na = n1 // 2
    pa, py = FFT_PITCH_X, FFT_PITCH_Y
    vg = _dwconv3(v_ref[0].astype(F32), wv_ref, pad_ref) * _dwconv3(x1_ref[0].astype(F32), w1_ref, pad_ref)
    for a in range(na):
        xs_ref[a * pa:a * pa + n2, :] = vg[a * n2:(a + 1) * n2]

    def fwd_a(b, carry):
        xb = xs_ref[pl.ds(b, na, stride=pa), :].astype(BF16)
        r = jnp.dot(fa_ref[b], xb, preferred_element_type=F32)
        y_ref[pl.ds(b, n1, stride=py), :] = r[:n1]
        y_ref[pl.ds(b + n2, n1, stride=py), :] = r[n1:]
        return carry

    lax.fori_loop(0, n2, fwd_a, 0, unroll=FFT_UNROLL)

    def mid(kg, carry):
        grp = range(FFT_GROUP)
        k1 = [kg * FFT_GROUP + t for t in grp]
        r0 = [pl.multiple_of(k * py, 8) for k in k1]
        f0 = [pl.multiple_of(k * 2 * n2, 2 * n2) for k in k1]
        fb = fb_ref[...]
        x = [jnp.dot(fb, y_ref[pl.ds(r0[t], 2 * n2), :].astype(BF16), preferred_element_type=F32) for t in grp]
        p = []
        for t in grp:
            xr, xi = x[t][:n2], x[t][n2:]
            kr = kf_ref[pl.ds(f0[t], n2), :]
            ki = kf_ref[pl.ds(f0[t] + n2, n2), :]
            p.append(jnp.concatenate([xr * kr - xi * ki, xr * ki + xi * kr], axis=0).astype(BF16))
        fbi = fbi_ref[...]
        q = [jnp.dot(fbi, p[t], preferred_element_type=F32) for t in grp]
        for t in grp:
            y_ref[pl.ds(r0[t], 2 * n2), :] = q[t]
        return carry

    lax.fori_loop(0, n1 // FFT_GROUP, mid, 0)

    def inv_a(b, carry):
        qb = jnp.concatenate([y_ref[pl.ds(b, n1, stride=py), :],
                              y_ref[pl.ds(b + n2, n1, stride=py), :]], axis=0).astype(BF16)
        xs_ref[pl.ds(b, na, stride=pa), :] = jnp.dot(ga_ref[b], qb, preferred_element_type=F32)
        return carry

    lax.fori_loop(0, n2, inv_a, 0, unroll=FFT_UNROLL)

    gate = _dwconv3(x2_ref[0].astype(F32), w2_ref, pad_ref)
    skip = skip_ref[...].astype(F32)
    for a in range(na):
        rows = slice(a * n2, (a + 1) * n2)
        o_ref[0, rows, :] = ((xs_ref[a * pa:a * pa + n2, :] + vg[rows] * skip) * gate[rows]).astype(o_ref.dtype)


def hyena_mixer(z3, w_short, skip, kf, tables):
    b, l, _ = z3.shape
    n1 = 2 * l // FFT_N2
    fa, ga, fb, fbi = tables
    nch = HY_WIDTH // LANES
    blk = Z_HY // LANES

    def zspec(seg):
        return pl.BlockSpec((1, l, LANES), lambda c, i: (i, 0, blk + seg * nch + c))

    def wspec(seg):
        return pl.BlockSpec((3, LANES), lambda c, i: (0, seg * nch + c))

    return pl.pallas_call(
        functools.partial(_hyena_kernel, n1=n1),
        grid=(nch, b),
        in_specs=[zspec(0), zspec(1), zspec(2), wspec(0), wspec(1), wspec(2),
                  pl.BlockSpec((1, LANES), lambda c, i: (0, c)),
                  pl.BlockSpec((4 * l, LANES), lambda c, i: (0, c)),
                  pl.BlockSpec(fa.shape, lambda c, i: (0, 0, 0)),
                  pl.BlockSpec(ga.shape, lambda c, i: (0, 0, 0)),
                  pl.BlockSpec(fb.shape, lambda c, i: (0, 0)),
                  pl.BlockSpec(fbi.shape, lambda c, i: (0, 0))],
        out_specs=pl.BlockSpec((1, l, LANES), lambda c, i: (i, 0, c)),
        out_shape=jax.ShapeDtypeStruct((b, l, HY_WIDTH), BF16),
        scratch_shapes=[pltpu.VMEM((n1 // 2 * FFT_PITCH_X, LANES), F32), pltpu.VMEM((n1 * FFT_PITCH_Y, LANES), F32),
                        pltpu.VMEM((l + 2 * CONV_PAD, LANES), F32)],
        compiler_params=_params("parallel", "arbitrary"),
        name="hyena_mixer",
    )(z3, z3, z3, w_short, w_short, w_short, skip.reshape(1, -1), kf, fa, ga, fb, fbi)


MLA_HW = 2 * LANES
MLA_HEADS_PER_STEP = 2


def _rope_group(g, cs):
    prod = g * cs
    s = prod + pltpu.roll(prod, MLA_ROPE, 1)
    lane = lax.broadcasted_iota(jnp.int32, s.shape, 1)
    return jnp.where(lane < MLA_ROPE, s, 0.0)


def _mla_prep_kernel(cq_ref, ckv_ref, kr_ref, gq_ref, gkv_ref, wq_ref, wkv_ref, cs_ref, q_ref, k_ref, v_ref):
    cs = cs_ref[...]
    hq = _rms_bf16(cq_ref[...], gq_ref[...])
    q = jnp.dot(hq, wq_ref[...], preferred_element_type=F32)
    hkv = _rms_bf16(ckv_ref[...], gkv_ref[...])
    kv = jnp.dot(hkv, wkv_ref[...], preferred_element_type=F32)
    k_rope = _rope_group(kr_ref[...].astype(F32), cs).astype(BF16)
    lane = lax.broadcasted_iota(jnp.int32, (cs.shape[0], LANES), 1)
    ones_lane = jnp.where(lane == 0, 1.0, 0.0).astype(BF16)
    for h in range(MLA_HEADS):
        o = h * MLA_HW
        q_ref[:, o:o + LANES] = q[:, o:o + LANES].astype(BF16)
        q_ref[:, o + LANES:o + MLA_HW] = _rope_group(q[:, o + LANES:o + MLA_HW], cs).astype(BF16)
        k_ref[:, o:o + LANES] = kv[:, o:o + LANES].astype(BF16)
        k_ref[:, o + LANES:o + MLA_HW] = k_rope
        v_ref[:, o:o + LANES] = kv[:, o + LANES:o + MLA_HW].astype(BF16)
        v_ref[:, o + LANES:o + MLA_HW] = ones_lane


def _mla_attn_kernel(q_ref, k_ref, v_ref, o_ref):
    c = (MLA_NOPE + MLA_ROPE) ** -0.5 * math.log2(math.e)
    heads = range(MLA_HEADS_PER_STEP)
    s = [lax.dot_general(q_ref[0, :, h * MLA_HW:(h + 1) * MLA_HW], k_ref[0, :, h * MLA_HW:(h + 1) * MLA_HW],
                         (((1,), (1,)), ((), ())), preferred_element_type=F32) for h in heads]
    for h in heads:
        p = jnp.exp2((s[h] - jnp.max(s[h], axis=-1, keepdims=True)) * c)
        o = jnp.dot(p.astype(BF16), v_ref[0, :, h * MLA_HW:(h + 1) * MLA_HW], preferred_element_type=F32)
        o_ref[0, :, h * MLA_V:(h + 1) * MLA_V] = (o[:, :MLA_V] / o[:, MLA_V:MLA_V + 1]).astype(o_ref.dtype)


def mla_mixer(z, b, l, g_q, g_kv, wq_p, w_ukv, cs_tab, tm, tq):
    m = z.shape[0]
    lt = l // tm
    qp, kp, vp = pl.pallas_call(
        _mla_prep_kernel,
        grid=(m // tm,),
        in_specs=[pl.BlockSpec((tm, MLA_Q_RANK), lambda i: (i, Z_CQ // MLA_Q_RANK)),
                  pl.BlockSpec((tm, MLA_KV_RANK), lambda i: (i, Z_CKV // MLA_KV_RANK)),
                  pl.BlockSpec((tm, LANES), lambda i: (i, Z_KR // LANES)),
                  pl.BlockSpec((1, MLA_Q_RANK), lambda i: (0, 0)),
                  pl.BlockSpec((1, MLA_KV_RANK), lambda i: (0, 0)),
                  pl.BlockSpec(wq_p.shape, lambda i: (0, 0)),
                  pl.BlockSpec(w_ukv.shape, lambda i: (0, 0)),
                  pl.BlockSpec((tm, LANES), lambda i: (i % lt, 0))],
        out_specs=[pl.BlockSpec((tm, MLA_HEADS * MLA_HW), lambda i: (i, 0)),
                   pl.BlockSpec((tm, MLA_HEADS * MLA_HW), lambda i: (i, 0)),
                   pl.BlockSpec((tm, MLA_HEADS * MLA_HW), lambda i: (i, 0))],
        out_shape=[jax.ShapeDtypeStruct((m, MLA_HEADS * MLA_HW), BF16),
                   jax.ShapeDtypeStruct((m, MLA_HEADS * MLA_HW), BF16),
                   jax.ShapeDtypeStruct((m, MLA_HEADS * MLA_HW), BF16)],
        compiler_params=_params("parallel"),
        name="mla_prep",
    )(z, z, z, g_q.reshape(1, -1), g_kv.reshape(1, -1), wq_p, w_ukv, cs_tab)
    qp = qp.reshape(b, l, -1)
    kp = kp.reshape(b, l, -1)
    vp = vp.reshape(b, l, -1)
    hs = MLA_HEADS_PER_STEP
    return pl.pallas_call(
        _mla_attn_kernel,
        grid=(b, MLA_HEADS // hs, l // tq),
        in_specs=[pl.BlockSpec((1, tq, hs * MLA_HW), lambda i, h, t: (i, t, h)),
                  pl.BlockSpec((1, l, hs * MLA_HW), lambda i, h, t: (i, 0, h)),
                  pl.BlockSpec((1, l, hs * MLA_HW), lambda i, h, t: (i, 0, h))],
        out_specs=pl.BlockSpec((1, tq, hs * MLA_V), lambda i, h, t: (i, t, h)),
        out_shape=jax.ShapeDtypeStruct((b, l, MLA_HEADS * MLA_V), BF16),
        compiler_params=_params("parallel", "parallel", "arbitrary"),
        name="mla_attention",
    )(qp, kp, vp)


def _split_bf16(x):
    hi = x.astype(BF16)
    return hi, (x - hi.astype(F32)).astype(BF16)


def _dot_split(a, b):
    (ah, al), (bh, bl) = a, b
    return (jnp.dot(ah, bh, preferred_element_type=F32)
            + (jnp.dot(ah, bl, preferred_element_type=F32) + jnp.dot(al, bh, preferred_element_type=F32)))


def _tri_unit_inverse(mats):
    c = mats[0].shape[0]
    eye = (lax.broadcasted_iota(jnp.int32, (c, c), 0) == lax.broadcasted_iota(jnp.int32, (c, c), 1)).astype(F32)
    ps = [eye - a for a in mats]
    ms = list(mats)
    for _ in range(int(math.log2(c)) - 1):
        splits = [_split_bf16(m) for m in ms]
        ms = [_dot_split(sp, sp) for sp in splits]
        yield
        ps = [_dot_split(_split_bf16(p), _split_bf16(eye + m)) for p, m in zip(ps, ms)]
        yield
    return ps


def _interleave(main, side=None):
    gens = [g for g in (main, side) if g is not None]
    results = [None] * len(gens)
    live = [True] * len(gens)
    while any(live):
        for idx, gen in enumerate(gens):
            if live[idx]:
                try:
                    next(gen)
                except StopIteration as stop:
                    results[idx] = stop.value
                    live[idx] = False
    return results


def _bdot(a, b):
    return jnp.dot(a.astype(BF16), b.astype(BF16), preferred_element_type=F32)


def _bdot_nt(a, b):
    return lax.dot_general(a.astype(BF16), b.astype(BF16), (((1,), (1,)), ((), ())), preferred_element_type=F32)


def _dn_group_setup(chunks, refs):
    q_s, k_s, v_s, g_s, b_s = refs
    cc = DN_CHUNK
    n = range(len(chunks))
    rev = [r for _, r in chunks]
    r0 = [pl.multiple_of(c * cc, cc) for c, _ in chunks]
    q = [q_s[pl.ds(r, cc), :] for r in r0]
    k = [k_s[pl.ds(r, cc), :] for r in r0]
    v = [v_s[pl.ds(r, cc), :] for r in r0]
    lane = [2 if r else 0 for r in rev]
    ri = lax.broadcasted_iota(jnp.int32, (cc, cc), 0)
    ci = lax.broadcasted_iota(jnp.int32, (cc, cc), 1)
    incl = [(ri <= ci) if r else (ri >= ci) for r in rev]
    strict = [(ri < ci) if r else (ri > ci) for r in rev]
    g = [g_s[pl.ds(r, cc), :] for r in r0]
    g1 = [x.astype(BF16) for x in g]
    r1 = [x - h.astype(F32) for x, h in zip(g, g1)]
    g2 = [x.astype(BF16) for x in r1]
    g3 = [(x - h.astype(F32)).astype(BF16) for x, h in zip(r1, g2)]
    ones = [m.astype(BF16) for m in incl]
    gc = [jnp.dot(ones[i], g1[i], preferred_element_type=F32) for i in n]
    gc = [gc[i] + jnp.dot(ones[i], g2[i], preferred_element_type=F32) for i in n]
    gc = [gc[i] + jnp.dot(ones[i], g3[i], preferred_element_type=F32) for i in n]
    yield
    gcol = [gc[i][:, lane[i]:lane[i] + 1] for i in n]
    grow = [jnp.transpose(gc[i])[lane[i]:lane[i] + 1, :] for i in n]
    beta = [b_s[pl.ds(r0[i], cc), :][:, lane[i] + 1:lane[i] + 2] for i in n]
    g_last = [gcol[i][0:1, :] if rev[i] else gcol[i][cc - 1:cc, :] for i in n]
    decay = [jnp.where(incl[i], jnp.exp(jnp.where(incl[i], gcol[i] - grow[i], 0.0)), 0.0) for i in n]
    e_g = [jnp.exp(x) for x in gcol]
    kb = [k[i] * beta[i] for i in n]
    a_both = [_bdot_nt(jnp.concatenate([kb[i], q[i]], axis=0), k[i]) for i in n]
    yield
    a_kk = [jnp.where(strict[i], a_both[i][:cc] * decay[i], 0.0) for i in n]
    a_qk = [jnp.where(incl[i], a_both[i][cc:] * decay[i], 0.0) for i in n]
    t_inv = yield from _tri_unit_inverse(a_kk)
    uw = [_bdot(t_inv[i], jnp.concatenate([v[i] * beta[i], kb[i] * e_g[i]], axis=1)) for i in n]
    yield
    k_dec = [k[i] * jnp.exp(g_last[i] - gcol[i]) for i in n]
    wq = [jnp.concatenate([uw[i][:, DN_DV:], q[i] * e_g[i]], axis=0).astype(BF16) for i in n]
    ak = [jnp.concatenate([a_qk[i], jnp.transpose(k_dec[i])], axis=0).astype(BF16) for i in n]
    return [(wq[i], ak[i], uw[i][:, :DN_DV], jnp.exp(g_last[i])) for i in n]


def _dn_group_steps(states, setups, out_refs, chunks):
    cc = DN_CHUNK
    dirs = range(len(states))
    for t in range(len(setups[0])):
        st = [setups[d][t] for d in dirs]
        r = [jnp.dot(st[d][0], states[d].astype(BF16), preferred_element_type=F32) for d in dirs]
        yield
        v_new = [st[d][2] - r[d][:cc] for d in dirs]
        r2 = [jnp.dot(st[d][1], v_new[d].astype(BF16), preferred_element_type=F32) for d in dirs]
        yield
        states = [states[d] * st[d][3] + r2[d][cc:] for d in dirs]
        for d in dirs:
            out_refs[d][pl.ds(pl.multiple_of(chunks[d][t] * cc, cc), cc), :] = r[d][cc:] + r2[d][:cc]
    return states


def _dn_kernel(zq_ref, zk_ref, zv_ref, zg_ref, ab_ref, wq_ref, wk_ref, wv_ref, alog_ref, dtb_ref, gn_ref, o_ref,
               q_s, k_s, v_s, g_s, b_s, of_s, ob_s, pad_ref, *, nchunks):
    def act(z_ref, w_ref):
        c = _dwconv3(z_ref[0].astype(F32), w_ref, pad_ref)
        return _silu(c)

    def l2n(x):
        return x * lax.rsqrt(jnp.sum(x * x, axis=-1, keepdims=True) + RMS_EPS)

    q_s[...] = l2n(act(zq_ref, wq_ref)) * (DN_DK ** -0.5)
    k_s[...] = l2n(act(zk_ref, wk_ref))
    v_s[...] = act(zv_ref, wv_ref)
    ab = ab_ref[0]
    x = ab + dtb_ref[0]
    softplus = jnp.maximum(x, 0.0) + jnp.log1p(jnp.exp(-jnp.abs(x)))
    g_s[...] = -jnp.exp(alog_ref[0]) * softplus
    b_s[...] = jax.nn.sigmoid(ab)
    refs = (q_s, k_s, v_s, g_s, b_s)

    ngroups = nchunks // DN_GROUP

    def group_chunks(gi):
        fwd = [gi * DN_GROUP + t for t in range(DN_GROUP)]
        return fwd, [nchunks - 1 - c for c in fwd]

    def setup_gen(gi):
        fwd, bwd = group_chunks(gi)
        return _dn_group_setup([(c, False) for c in fwd] + [(c, True) for c in bwd], refs)

    def steps_gen(gi, states, flat):
        setups = [tuple(flat[4 * i:4 * i + 4]) for i in range(2 * DN_GROUP)]
        return _dn_group_steps(states, [setups[:DN_GROUP], setups[DN_GROUP:]], (of_s, ob_s), group_chunks(gi))

    def flatten(setups):
        return [x for st in setups for x in st]

    def body(gi, carry):
        nxt, states = _interleave(setup_gen(gi + 1), steps_gen(gi, list(carry[:2]), carry[2:]))
        return (*states, *flatten(nxt))

    zero = jnp.zeros((DN_DK, DN_DV), F32)
    (first,) = _interleave(setup_gen(0))
    carry = lax.fori_loop(0, ngroups - 1, body, (zero, zero, *flatten(first)))
    _interleave(steps_gen(ngroups - 1, list(carry[:2]), carry[2:]))

    o = of_s[...] + ob_s[...]
    o = o * lax.rsqrt(jnp.mean(o * o, axis=-1, keepdims=True) + RMS_EPS) * gn_ref[...]
    gate = zg_ref[0].astype(F32)
    o_ref[0] = (o * _silu(gate)).astype(o_ref.dtype)


def deltanet_mixer(z3, ab3, w_conv, alog_p, dtb_p, g_norm):
    b, l, _ = z3.shape
    blk = Z_DNQKV // LANES
    gblk = Z_DNGATE // LANES
    hh = DN_HEADS

    def zspec(off):
        return pl.BlockSpec((1, l, LANES), lambda i, h: (i, 0, off + h))

    def wspec(seg):
        return pl.BlockSpec((3, LANES), lambda i, h: (0, seg * hh + h))

    vec = pl.BlockSpec((1, 1, LANES), lambda i, h: (h, 0, 0))
    seq = pltpu.VMEM((l, LANES), F32)
    return pl.pallas_call(
        functools.partial(_dn_kernel, nchunks=l // DN_CHUNK),
        grid=(b, hh),
        in_specs=[zspec(blk), zspec(blk + hh), zspec(blk + 2 * hh), zspec(gblk),
                  pl.BlockSpec((1, l, LANES), lambda i, h: (i, 0, h)),
                  wspec(0), wspec(1), wspec(2), vec, vec,
                  pl.BlockSpec((1, LANES), lambda i, h: (0, 0))],
        out_specs=pl.BlockSpec((1, l, LANES), lambda i, h: (i, 0, h)),
        out_shape=jax.ShapeDtypeStruct((b, l, hh * DN_DV), BF16),
        scratch_shapes=[seq] * 7 + [pltpu.VMEM((l + 2 * CONV_PAD, LANES), F32)],
        compiler_params=_params("parallel", "arbitrary"),
        name="deltanet_mixer",
    )(z3, z3, z3, z3, ab3, w_conv, w_conv, w_conv, alog_p, dtb_p, g_norm.reshape(1, -1))


def _rotate_half_cols(w):
    half = w.shape[-1] // 2
    return jnp.concatenate([-w[..., half:], w[..., :half]], axis=-1)


def _prep_w_in(w_in):
    cuts = np.cumsum([0, 1536, 1536, MLA_Q_RANK, MLA_KV_RANK, MLA_ROPE, 1536, 512, 16])
    dnab = w_in[:, cuts[7]:cuts[8]]
    w16 = w_in.astype(BF16)
    na, hy, cq, ckv, kr, dnqkv, dngate = (w16[:, cuts[i]:cuts[i + 1]] for i in range(7))
    gates = w16[:, cuts[8]:]
    pad = jnp.zeros((w_in.shape[0], Z_GATE - Z_KR - 2 * MLA_ROPE), BF16)
    main = jnp.concatenate([na, hy, dnqkv, dngate, cq, ckv, kr, _rotate_half_cols(kr), pad, 0.5 * gates], axis=1)
    ab = dnab.reshape(-1, 2, 2, DN_HEADS)
    ab = jnp.transpose(ab, (0, 3, 1, 2)).reshape(-1, DN_HEADS, 4)
    ab = jnp.pad(ab, ((0, 0), (0, 0), (0, LANES - 4))).reshape(-1, DN_HEADS * LANES)
    return main, ab.astype(BF16)


def _prep_w_uq(w_uq):
    k = w_uq.shape[0]
    w = w_uq.reshape(k, MLA_HEADS, MLA_NOPE + MLA_ROPE)
    rope_w = w[:, :, MLA_NOPE:]
    return jnp.concatenate([w, _rotate_half_cols(rope_w)], axis=-1).reshape(k, MLA_HEADS * MLA_HW).astype(BF16)


def _head_lane_vec(p):
    v = jnp.zeros((DN_HEADS, 1, LANES), F32)
    v = v.at[:, 0, 0].set(p[0].astype(F32))
    return v.at[:, 0, 2].set(p[1].astype(F32))


def _rope_table(l):
    half = MLA_ROPE // 2
    inv = ROPE_THETA ** (-jnp.arange(half, dtype=F32) / half)
    ang = jnp.arange(l, dtype=F32)[:, None] * inv[None, :]
    cos, sin = jnp.cos(ang), jnp.sin(ang)
    return jnp.concatenate([cos, cos, sin, sin], axis=-1)


def _split_const(x):
    hi = x.astype(BF16)
    lo = (x - hi.astype(np.float64)).astype(BF16)
    return jnp.asarray(hi), jnp.asarray(lo)


def _pick(n, pref):
    for t in pref:
        if n % t == 0:
            return t
    return n


def trunk(x_parts, l, norm_mix, w_in, na_rpb, hy_short, hy_skip, hy_w1, hy_b1, hy_w2, hy_b2, hy_w3,
          mla_g_q, mla_g_kv, mla_w_uq, mla_w_ukv, dn_conv, dn_a_log, dn_dt_bias, dn_g_norm,
          w_branch, w_out, norm_mlp, w_up, w_down, norm_final):
    d = x_parts[0].shape[1]
    part_rows = tuple(p.shape[0] for p in x_parts)
    m = sum(part_rows)
    b = m // l
    depth = w_in.shape[0]
    common = math.gcd(*part_rows)
    tm_big = _pick(common, (1024, 512, 256, 128))
    tm_mid = _pick(common, (512, 256, 128))
    fa, ga, fb, fbi = _fft_tables(l)
    n1 = 2 * l // FFT_N2
    tables = (jnp.asarray(fa[:, :, :n1 // 2], BF16), jnp.asarray(ga[:, :n1 // 2, :], BF16),
              jnp.asarray(fb, BF16), jnp.asarray(fbi, BF16))
    fa_split, fb_split = _split_const(fa[:, :, :n1 // 2]), _split_const(fb)
    cs_tab = _rope_table(l)
    x = tuple(x_parts)
    for layer in range(depth):
        last = layer == depth - 1
        w_main, w_ab = _prep_w_in(w_in[layer])
        z, ab = norm_matmul(x, norm_mix[layer], w_main, w_ab, tm_big if len(x) == 1 else tm_mid, 1024)
        z3 = z.reshape(b, l, Z_COLS)
        br_a = neighbourhood_attention(z3, na_bias_table(na_rpb[layer]), _pick(l // GRID_W, (8, 4, 2, 1)))
        taps = hyena_filters(l, hy_w1[layer], hy_b1[layer], hy_w2[layer], hy_b2[layer], hy_w3[layer])
        kf = filter_spectrum(taps, fa_split, fb_split)
        br_b = hyena_mixer(z3, hy_short[layer], hy_skip[layer], kf, tables)
        br_c = mla_mixer(z, b, l, mla_g_q[layer], mla_g_kv[layer], _prep_w_uq(mla_w_uq[layer]),
                         mla_w_ukv[layer].astype(BF16), cs_tab, tm_mid, _pick(l, (512, 256, 128)))
        br_d = deltanet_mixer(z3, ab.reshape(b, l, -1), dn_conv[layer], _head_lane_vec(dn_a_log[layer]),
                              _head_lane_vec(dn_dt_bias[layer]), dn_g_norm[layer])
        branches = [t.reshape(m, BRANCH_W) for t in (br_a, br_b, br_c, br_d)]
        merged = gated_merge(z, branches, (0.5 * w_branch[layer]).astype(BF16), tm_mid)
        x_mid = matmul_residual(merged, w_out[layer].astype(BF16), x, tm_mid)
        x = tuple(mlp_block(x_mid, norm_mlp[layer], w_up[layer].astype(BF16), w_down[layer].astype(BF16), norm_final,
                            last, tm_mid, 1024, part_rows if last else (m,)))
    return x


def kernel(x_prompt, x_sample, norm_mix, w_in, na_rpb, hy_short, hy_skip, hy_w1, hy_b1, hy_w2, hy_b2, hy_w3,
           mla_g_q, mla_g_kv, mla_w_uq, mla_w_ukv, dn_conv, dn_a_log, dn_dt_bias, dn_g_norm,
           w_branch, w_out, norm_mlp, w_up, w_down, norm_final):
    assert x_prompt.shape[1:] == x_sample.shape[1:]
    l, d = x_prompt.shape[1:]
    y_prompt, y_sample = trunk((x_prompt.reshape(-1, d), x_sample.reshape(-1, d)), l, norm_mix, w_in, na_rpb,
                               hy_short, hy_skip, hy_w1, hy_b1, hy_w2, hy_b2, hy_w3, mla_g_q, mla_g_kv, mla_w_uq,
                               mla_w_ukv, dn_conv, dn_a_log, dn_dt_bias, dn_g_norm, w_branch, w_out, norm_mlp,
                               w_up, w_down, norm_final)
    return (y_prompt.reshape(x_prompt.shape), y_sample.reshape(x_sample.shape))
```

```python
import functools
import math

import jax
import jax.numpy as jnp
import numpy as np
from jax import lax
from jax.experimental import pallas as pl
from jax.experimental.pallas import tpu as pltpu

F32 = jnp.float32
BF16 = jnp.bfloat16
HIGHEST = lax.Precision.HIGHEST

VMEM_LIMIT_BYTES = 56 * 1024 * 1024
LANES = 128

D_MODEL = 2048
RMS_EPS = 1e-6
GRID_W = 64
N_BRANCH = 4
BRANCH_W = 512
NA_HEADS = 8
NA_HEAD_DIM = 64
NA_WIN_R = 8
NA_WIN_C = 16
HY_WIDTH = 512
HY_POS_BANDS = 16
HY_FILT_HIDDEN = 64
HY_FAST_DECAY = 0.3
HY_SLOW_DECAY = 1.5
HY_DECAY_TARGET = 1e-2
MLA_HEADS = 4
MLA_Q_RANK = 512
MLA_KV_RANK = 256
MLA_NOPE = 128
MLA_ROPE = 64
MLA_V = 128
ROPE_THETA = 10000.0
DN_HEADS = 4
DN_DK = 128
DN_DV = 128
DN_CHUNK = 64
DN_GROUP = 4
D_FF = 4 * D_MODEL

Z_NA = 0
Z_HY = 1536
Z_DNQKV = 3072
Z_DNGATE = 4608
Z_CQ = 5120
Z_CKV = 5632
Z_KR = 5888
Z_GATE = 6144
Z_COLS = Z_GATE + N_BRANCH * D_MODEL

FFT_N2 = 128
FFT_UNROLL = 8
FFT_GROUP = 4
FFT_PITCH_X = FFT_N2 + 8
FFT_PITCH_Y = 2 * FFT_N2 + 8


def _params(*sem):
    return pltpu.CompilerParams(dimension_semantics=sem, vmem_limit_bytes=VMEM_LIMIT_BYTES)


def _silu(x):
    return 0.5 * x * (jnp.tanh(0.5 * x) + 1.0)


def _rms_bf16(x, g):
    xf = x.astype(F32)
    y = xf * lax.rsqrt(jnp.mean(xf * xf, axis=-1, keepdims=True) + RMS_EPS)
    return (y * g).astype(BF16)


def _part_tiles(parts, tm):
    assert all(p.shape[0] % tm == 0 for p in parts)
    return tuple(p.shape[0] // tm for p in parts)


def _part_specs(tiles, block):
    specs, start = [], 0
    for count in tiles:
        specs.append(pl.BlockSpec(block, functools.partial(
            lambda i, j, start, count: (jnp.clip(i - start, 0, count - 1), 0), start=start, count=count)))
        start += count
    return specs


def _for_owning_part(tiles, refs, fn):
    if len(refs) == 1:
        fn(refs[0])
        return
    i = pl.program_id(0)
    start = 0
    for ref, count in zip(refs, tiles):
        pl.when((i >= start) & (i < start + count))(functools.partial(fn, ref))
        start += count


def _norm_mm_kernel(*refs, tiles):
    x_refs = refs[:len(tiles)]
    g_ref, w_ref, o_ref, h_ref = refs[len(tiles):]

    @pl.when(pl.program_id(1) == 0)
    def _():
        def norm(x_ref):
            h_ref[...] = _rms_bf16(x_ref[...], g_ref[...])

        _for_owning_part(tiles, x_refs, norm)

    o_ref[...] = jnp.dot(h_ref[...], w_ref[...], preferred_element_type=F32).astype(o_ref.dtype)


def norm_matmul(x_parts, g, w, out_dtype, tm, tn):
    k, n = w.shape
    tiles = _part_tiles(x_parts, tm)
    return pl.pallas_call(
        functools.partial(_norm_mm_kernel, tiles=tiles),
        grid=(sum(tiles), n // tn),
        in_specs=_part_specs(tiles, (tm, k)) + [pl.BlockSpec((1, k), lambda i, j: (0, 0)),
                                                 pl.BlockSpec((k, tn), lambda i, j: (0, j))],
        out_specs=pl.BlockSpec((tm, tn), lambda i, j: (i, j)),
        out_shape=jax.ShapeDtypeStruct((sum(tiles) * tm, n), out_dtype),
        scratch_shapes=[pltpu.VMEM((tm, k), BF16)],
        compiler_params=_params("parallel", "arbitrary"),
        name="norm_matmul",
    )(*x_parts, g.reshape(1, k), w)


def _mm_res_kernel(*refs, tiles):
    a_ref, w_ref = refs[:2]
    r_refs = refs[2:2 + len(tiles)]
    o_ref = refs[-1]
    y = jnp.dot(a_ref[...], w_ref[...], preferred_element_type=F32)

    def add(r_ref):
        o_ref[...] = r_ref[...] + y

    _for_owning_part(tiles, r_refs, add)


def matmul_residual(a, w, r_parts, tm):
    m, k = a.shape
    n = w.shape[1]
    tiles = _part_tiles(r_parts, tm)
    return pl.pallas_call(
        functools.partial(_mm_res_kernel, tiles=tiles),
        grid=(m // tm, 1),
        in_specs=[pl.BlockSpec((tm, k), lambda i, j: (i, 0)),
                  pl.BlockSpec((k, n), lambda i, j: (0, 0))] + _part_specs(tiles, (tm, n)),
        out_specs=pl.BlockSpec((tm, n), lambda i, j: (i, 0)),
        out_shape=jax.ShapeDtypeStruct((m, n), F32),
        compiler_params=_params("parallel", "arbitrary"),
        name="matmul_residual",
    )(a, w, *r_parts)


def _mlp_kernel(x_ref, g_ref, wu_ref, wd_ref, gf_ref, *refs, final_norm, tiles):
    o_refs, h_ref = refs[:-1], refs[-1]
    j = pl.program_id(1)

    def run(o_ref):
        @pl.when(j == 0)
        def _():
            x = x_ref[...]
            h_ref[...] = _rms_bf16(x, g_ref[...])
            o_ref[...] = x

        u = jnp.dot(h_ref[...], wu_ref[...], preferred_element_type=F32)
        a = jnp.square(jnp.maximum(u, 0.0)).astype(BF16)
        o_ref[...] += jnp.dot(a, wd_ref[...], preferred_element_type=F32)

        if final_norm:
            @pl.when(j == pl.num_programs(1) - 1)
            def _():
                y = o_ref[...]
                o_ref[...] = y * lax.rsqrt(jnp.mean(y * y, axis=-1, keepdims=True) + RMS_EPS) * gf_ref[...]

    _for_owning_part(tiles, o_refs, run)


def mlp_block(x, g, w_up, w_down, g_final, final_norm, tm, tf, out_rows):
    m, d = x.shape
    f = w_up.shape[1]
    assert sum(out_rows) == m and all(r % tm == 0 for r in out_rows)
    tiles = tuple(r // tm for r in out_rows)
    return pl.pallas_call(
        functools.partial(_mlp_kernel, final_norm=final_norm, tiles=tiles),
        grid=(m // tm, f // tf),
        in_specs=[pl.BlockSpec((tm, d), lambda i, j: (i, 0)),
                  pl.BlockSpec((1, d), lambda i, j: (0, 0)),
                  pl.BlockSpec((d, tf), lambda i, j: (0, j)),
                  pl.BlockSpec((tf, d), lambda i, j: (j, 0)),
                  pl.BlockSpec((1, d), lambda i, j: (0, 0))],
        out_specs=_part_specs(tiles, (tm, d)),
        out_shape=[jax.ShapeDtypeStruct((r, d), F32) for r in out_rows],
        scratch_shapes=[pltpu.VMEM((tm, d), BF16)],
        compiler_params=_params("parallel" if len(tiles) == 1 else "arbitrary", "arbitrary"),
        name="mlp_block",
    )(x, g.reshape(1, d), w_up, w_down, g_final.reshape(1, d))


def _merge_kernel(gate_ref, ba_ref, bb_ref, bc_ref, bd_ref, wb_ref, o_ref, acc_ref):
    n = pl.program_id(1)

    def contribution(b_ref):
        y = jnp.dot(b_ref[...], wb_ref[0], preferred_element_type=F32)
        return (jnp.tanh(gate_ref[...].astype(F32)) + 1.0) * y

    @pl.when(n == 0)
    def _():
        acc_ref[...] = contribution(ba_ref)

    @pl.when(n == 1)
    def _():
        acc_ref[...] += contribution(bb_ref)

    @pl.when(n == 2)
    def _():
        acc_ref[...] += contribution(bc_ref)

    @pl.when(n == 3)
    def _():
        o_ref[...] = (acc_ref[...] + contribution(bd_ref)).astype(o_ref.dtype)


def gated_merge(z, branches, w_branch, tm):
    m = z.shape[0]
    gate_blk0 = Z_GATE // D_MODEL
    br_spec = pl.BlockSpec((tm, BRANCH_W), lambda i, n: (i, 0))
    return pl.pallas_call(
        _merge_kernel,
        grid=(m // tm, N_BRANCH),
        in_specs=[pl.BlockSpec((tm, D_MODEL), lambda i, n: (i, gate_blk0 + n)),
                  br_spec, br_spec, br_spec, br_spec,
                  pl.BlockSpec((1, BRANCH_W, D_MODEL), lambda i, n: (n, 0, 0))],
        out_specs=pl.BlockSpec((tm, D_MODEL), lambda i, n: (i, 0)),
        out_shape=jax.ShapeDtypeStruct((m, D_MODEL), BF16),
        scratch_shapes=[pltpu.VMEM((tm, D_MODEL), F32)],
        compiler_params=_params("parallel", "arbitrary"),
        name="gated_merge",
    )(z, *branches, w_branch)


NA_ROWS_LOCKSTEP = 2
NA_MASK = -1e30


def na_bias_table(rpb):
    q = np.arange(GRID_W)
    kc = np.arange(GRID_W)
    cs = np.clip(q - NA_WIN_C // 2, 0, GRID_W - NA_WIN_C)
    ok = (kc[None, :] >= cs[:, None]) & (kc[None, :] < cs[:, None] + NA_WIN_C)
    dcol = np.clip(kc[None, :] - q[:, None] + NA_WIN_C - 1, 0, 2 * NA_WIN_C - 2)
    drow = np.arange(NA_WIN_R)[None, :] - np.arange(NA_WIN_R)[:, None] + NA_WIN_R - 1
    t = rpb.astype(F32)[:, drow]
    t = t[:, :, :, dcol]
    t = jnp.transpose(t, (1, 0, 3, 2, 4))
    t = jnp.where(jnp.asarray(ok)[None, None, :, None, :], t, NA_MASK)
    return t.reshape(NA_WIN_R, NA_HEADS, GRID_W, NA_WIN_R * GRID_W)


def _na_kernel(q_ref, k_ref, v_ref, bias_ref, o_ref, *, rows_per_step, rows):
    rblk = pl.program_id(1)
    win = NA_WIN_R * GRID_W
    scale = NA_HEAD_DIM ** -0.5

    pair_w = 2 * NA_HEAD_DIM
    low = lax.broadcasted_iota(jnp.int32, (GRID_W, pair_w), 1) < NA_HEAD_DIM

    def rows_body(it, carry):
        inst = []
        q, kb, vb, off, qstart = [], [], [], [], []
        for t in range(NA_ROWS_LOCKSTEP):
            rl = it * NA_ROWS_LOCKSTEP + t
            r = rblk * rows_per_step + rl
            rs = jnp.clip(r - NA_WIN_R // 2, 0, rows - NA_WIN_R)
            off.append(r - rs)
            qstart.append(pl.multiple_of(rl * GRID_W, GRID_W))
            kstart = pl.multiple_of(rs * GRID_W, GRID_W)
            q.append(q_ref[0, pl.ds(qstart[t], GRID_W), :])
            kb.append(k_ref[0, pl.ds(kstart, win), :])
            vb.append(v_ref[0, pl.ds(kstart, win), :])
            inst += [(t, h) for h in range(NA_HEADS)]

        def pair(x, h):
            return x[:, (h // 2) * pair_w:(h // 2 + 1) * pair_w]

        qm = [jnp.where(low if h % 2 == 0 else ~low, pair(q[t], h), jnp.zeros((), BF16)) for t, h in inst]
        s = [lax.dot_general(qm[i], pair(kb[t], h), (((1,), (1,)), ((), ())), preferred_element_type=F32)
             for i, (t, h) in enumerate(inst)]
        s = [s[i] * scale + bias_ref[off[t], h] for i, (t, h) in enumerate(inst)]
        p = [jnp.exp(x - jnp.max(x, axis=-1, keepdims=True)) for x in s]
        l = [jnp.sum(x, axis=-1, keepdims=True) for x in p]
        o = [jnp.dot(p[i].astype(BF16), pair(vb[t], h), preferred_element_type=F32) / l[i]
             for i, (t, h) in enumerate(inst)]
        for t in range(NA_ROWS_LOCKSTEP):
            base = t * NA_HEADS
            outs = [jnp.where(low, o[base + h], o[base + h + 1]) for h in range(0, NA_HEADS, 2)]
            o_ref[0, pl.ds(qstart[t], GRID_W), :] = jnp.concatenate(outs, axis=1).astype(o_ref.dtype)
        return carry

    lax.fori_loop(0, rows_per_step // NA_ROWS_LOCKSTEP, rows_body, 0)


def neighbourhood_attention(z3, bias, rows_per_step):
    b, l, _ = z3.shape
    rows = l // GRID_W
    w = NA_HEADS * NA_HEAD_DIM
    blk = Z_NA // w
    return pl.pallas_call(
        functools.partial(_na_kernel, rows_per_step=rows_per_step, rows=rows),
        grid=(b, rows // rows_per_step),
        in_specs=[pl.BlockSpec((1, rows_per_step * GRID_W, w), lambda i, r: (i, r, blk)),
                  pl.BlockSpec((1, l, w), lambda i, r: (i, 0, blk + 1)),
                  pl.BlockSpec((1, l, w), lambda i, r: (i, 0, blk + 2)),
                  pl.BlockSpec(bias.shape, lambda i, r: (0, 0, 0, 0))],
        out_specs=pl.BlockSpec((1, rows_per_step * GRID_W, w), lambda i, r: (i, r, 0)),
        out_shape=jax.ShapeDtypeStruct((b, l, w), BF16),
        compiler_params=_params("parallel", "arbitrary"),
        name="neighbourhood_attention",
    )(z3, z3, z3, bias)


def _fft_tables(l):
    n = 2 * l
    n2 = FFT_N2
    n1 = n // n2
    a = np.arange(n1)
    b = np.arange(n2)
    k1 = np.arange(n1)
    t = n2 * a[None, :] + b[:, None]
    th = 2.0 * np.pi * (k1[None, :, None] * t[:, None, :] % n) / n
    fa = np.concatenate([np.cos(th), -np.sin(th)], axis=1)
    th_t = np.transpose(th, (0, 2, 1))
    ga = np.concatenate([np.cos(th_t), -np.sin(th_t)], axis=2)
    ph = 2.0 * np.pi * (np.outer(b, b) % n2) / n2
    cr, ci = np.cos(ph), -np.sin(ph)
    fb = np.block([[cr, -ci], [ci, cr]])
    fbi = np.block([[cr, ci], [-ci, cr]])
    return fa, ga, fb, fbi


def _filter_fft_kernel(hf_ref, hb_ref, fa_hi_ref, fa_lo_ref, fb_hi_ref, fb_lo_ref, o_ref, y_ref, *, n1):
    n2 = FFT_N2
    na = n1 // 2
    inv_n = 1.0 / (n1 * n2)

    def transform(src_ref, combine):
        def step_a(b, carry):
            xb = _split_bf16(src_ref[pl.ds(b, na, stride=n2), :])
            r = _dot_split((fa_hi_ref[b], fa_lo_ref[b]), xb)
            y_ref[pl.ds(b, n1, stride=2 * n2), :] = r[:n1]
            y_ref[pl.ds(b + n2, n1, stride=2 * n2), :] = r[n1:]
            return carry

        lax.fori_loop(0, n2, step_a, 0, unroll=2)

        def step_b(k1, carry):
            r0 = pl.multiple_of(k1 * 2 * n2, 2 * n2)
            x = _dot_split((fb_hi_ref[...], fb_lo_ref[...]), _split_bf16(y_ref[pl.ds(r0, 2 * n2), :]))
            combine(r0, x * inv_n)
            return carry

        lax.fori_loop(0, n1, step_b, 0, unroll=2)

    def store(r0, x):
        o_ref[pl.ds(r0, 2 * n2), :] = x

    def add_conjugate(r0, x):
        o_ref[pl.ds(r0, n2), :] += x[:n2]
        o_ref[pl.ds(r0 + n2, n2), :] -= x[n2:]

    transform(hf_ref, store)
    transform(hb_ref, add_conjugate)


def filter_spectrum(h, fa, fb):
    l = h.shape[0]
    c = h.shape[1] // 2
    n = 2 * l
    n1 = n // FFT_N2
    nch = c // LANES
    return pl.pallas_call(
        functools.partial(_filter_fft_kernel, n1=n1),
        grid=(nch,),
        in_specs=[pl.BlockSpec((l, LANES), lambda i: (0, i)),
                  pl.BlockSpec((l, LANES), lambda i: (0, nch + i)),
                  pl.BlockSpec(fa[0].shape, lambda i: (0, 0, 0)),
                  pl.BlockSpec(fa[1].shape, lambda i: (0, 0, 0)),
                  pl.BlockSpec(fb[0].shape, lambda i: (0, 0)),
                  pl.BlockSpec(fb[1].shape, lambda i: (0, 0))],
        out_specs=pl.BlockSpec((2 * n, LANES), lambda i: (0, i)),
        out_shape=jax.ShapeDtypeStruct((2 * n, c), F32),
        scratch_shapes=[pltpu.VMEM((2 * n, LANES), F32)],
        compiler_params=_params("parallel"),
        name="hyena_filter_spectrum",
    )(h, h, *fa, *fb)


def _hyena_filter_kernel(z_ref, w1_ref, b1_ref, w2_ref, b2_ref, w3_ref, dec_ref, o_ref):
    hid = jnp.sin(jnp.dot(z_ref[...], w1_ref[...], precision=HIGHEST, preferred_element_type=F32) + b1_ref[...])
    hid = jnp.sin(jnp.dot(hid, w2_ref[...], precision=HIGHEST, preferred_element_type=F32) + b2_ref[...])
    h = jnp.dot(hid, w3_ref[...], precision=HIGHEST, preferred_element_type=F32)
    dec = dec_ref[...]
    hf = h[:, :HY_WIDTH] * dec
    hb = h[:, HY_WIDTH:] * dec
    norm = (jnp.sum(jnp.abs(hf), axis=0, keepdims=True) + jnp.sum(jnp.abs(hb), axis=0, keepdims=True)) + RMS_EPS
    o_ref[:, :HY_WIDTH] = hf / norm
    o_ref[:, HY_WIDTH:] = hb / norm


def hyena_filters(l, w1, b1, w2, b2, w3):
    t = jnp.linspace(0.0, 1.0, l, dtype=F32)[:, None]
    w = 2.0 * math.pi * jnp.arange(l, dtype=F32)[:, None] / l
    bands = jnp.linspace(1e-4, HY_POS_BANDS - 1, HY_POS_BANDS, dtype=F32)[None, :]
    z = jnp.concatenate([t, jnp.cos(bands * w), -jnp.sin(bands * w)], axis=-1)
    pad = LANES - z.shape[1]
    z = jnp.pad(z, ((0, 0), (0, pad)))
    w1p = jnp.pad(w1.astype(F32), ((0, pad), (0, 0)))
    max_decay = math.log(HY_DECAY_TARGET) / HY_FAST_DECAY
    min_decay = math.log(HY_DECAY_TARGET) / HY_SLOW_DECAY
    deltas = jnp.abs(jnp.linspace(min_decay, max_decay, HY_WIDTH, dtype=F32))
    dec = jnp.exp(-t * deltas[None, :])
    return pl.pallas_call(
        _hyena_filter_kernel,
        out_shape=jax.ShapeDtypeStruct((l, 2 * HY_WIDTH), F32),
        compiler_params=pltpu.CompilerParams(vmem_limit_bytes=VMEM_LIMIT_BYTES),
        name="hyena_filter_ffn",
    )(z, w1p, b1.reshape(1, -1).astype(F32), w2.astype(F32), b2.reshape(1, -1).astype(F32), w3.astype(F32), dec)


CONV_PAD = 8


def _dwconv3(x, w_ref, pad_ref):
    n = x.shape[0]
    zeros = jnp.zeros((CONV_PAD, x.shape[1]), F32)
    pad_ref[0:CONV_PAD, :] = zeros
    pad_ref[CONV_PAD + n:2 * CONV_PAD + n, :] = zeros
    pad_ref[CONV_PAD:CONV_PAD + n, :] = x
    w = w_ref[...].astype(F32)
    return (pad_ref[CONV_PAD - 1:CONV_PAD - 1 + n, :] * w[0:1] + x * w[1:2]
            + pad_ref[CONV_PAD + 1:CONV_PAD + 1 + n, :] * w[2:3])


def _hyena_kernel(x1_ref, x2_ref, v_ref, w1_ref, w2_ref, wv_ref, skip_ref, kf_ref, fa_ref, ga_ref, fb_ref, fbi_ref,
                  o_ref, xs_ref, y_ref, pad_ref, *, n1):
    n2 = FFT_N2
    na = n1 // 2
    pa, py = FFT_PITCH_X, FFT_PITCH_Y
    vg = _dwconv3(v_ref[0].astype(F32), wv_ref, pad_ref) * _dwconv3(x1_ref[0].astype(F32), w1_ref, pad_ref)
    for a in range(na):
        xs_ref[a * pa:a * pa + n2, :] = vg[a * n2:(a + 1) * n2]

    def fwd_a(b, carry):
        xb = xs_ref[pl.ds(b, na, stride=pa), :].astype(BF16)
        r = jnp.dot(fa_ref[b], xb, preferred_element_type=F32)
        y_ref[pl.ds(b, n1, stride=py), :] = r[:n1]
        y_ref[pl.ds(b + n2, n1, stride=py), :] = r[n1:]
        return carry

    lax.fori_loop(0, n2, fwd_a, 0, unroll=FFT_UNROLL)

    def mid(kg, carry):
        grp = range(FFT_GROUP)
        k1 = [kg * FFT_GROUP + t for t in grp]
        r0 = [pl.multiple_of(k * py, 8) for k in k1]
        f0 = [pl.multiple_of(k * 2 * n2, 2 * n2) for k in k1]
        fb = fb_ref[...]
        x = [jnp.dot(fb, y_ref[pl.ds(r0[t], 2 * n2), :].astype(BF16), preferred_element_type=F32) for t in grp]
        p = []
        for t in grp:
            xr, xi = x[t][:n2], x[t][n2:]
            kr = kf_ref[pl.ds(f0[t], n2), :]
            ki = kf_ref[pl.ds(f0[t] + n2, n2), :]
            p.append(jnp.concatenate([xr * kr - xi * ki, xr * ki + xi * kr], axis=0).astype(BF16))
        fbi = fbi_ref[...]
        q = [jnp.dot(fbi, p[t], preferred_element_type=F32) for t in grp]
        for t in grp:
            y_ref[pl.ds(r0[t], 2 * n2), :] = q[t]
        return carry

    lax.fori_loop(0, n1 // FFT_GROUP, mid, 0)

    def inv_a(b, carry):
        qb = jnp.concatenate([y_ref[pl.ds(b, n1, stride=py), :],
                              y_ref[pl.ds(b + n2, n1, stride=py), :]], axis=0).astype(BF16)
        xs_ref[pl.ds(b, na, stride=pa), :] = jnp.dot(ga_ref[b], qb, preferred_element_type=F32)
        return carry

    lax.fori_loop(0, n2, inv_a, 0, unroll=FFT_UNROLL)

    gate = _dwconv3(x2_ref[0].astype(F32), w2_ref, pad_ref)
    skip = skip_ref[...].astype(F32)
    for a in range(na):
        rows = slice(a * n2, (a + 1) * n2)
        o_ref[0, rows, :] = ((xs_ref[a * pa:a * pa + n2, :] + vg[rows] * skip) * gate[rows]).astype(o_ref.dtype)


def hyena_mixer(z3, w_short, skip, kf, tables):
    b, l, _ = z3.shape
    n1 = 2 * l // FFT_N2
    fa, ga, fb, fbi = tables
    nch = HY_WIDTH // LANES
    blk = Z_HY // LANES

    def zspec(seg):
        return pl.BlockSpec((1, l, LANES), lambda c, i: (i, 0, blk + seg * nch + c))

    def wspec(seg):
        return pl.BlockSpec((3, LANES), lambda c, i: (0, seg * nch + c))

    return pl.pallas_call(
        functools.partial(_hyena_kernel, n1=n1),
        grid=(nch, b),
        in_specs=[zspec(0), zspec(1), zspec(2), wspec(0), wspec(1), wspec(2),
                  pl.BlockSpec((1, LANES), lambda c, i: (0, c)),
                  pl.BlockSpec((4 * l, LANES), lambda c, i: (0, c)),
                  pl.BlockSpec(fa.shape, lambda c, i: (0, 0, 0)),
                  pl.BlockSpec(ga.shape, lambda c, i: (0, 0, 0)),
                  pl.BlockSpec(fb.shape, lambda c, i: (0, 0)),
                  pl.BlockSpec(fbi.shape, lambda c, i: (0, 0))],
        out_specs=pl.BlockSpec((1, l, LANES), lambda c, i: (i, 0, c)),
        out_shape=jax.ShapeDtypeStruct((b, l, HY_WIDTH), BF16),
        scratch_shapes=[pltpu.VMEM((n1 // 2 * FFT_PITCH_X, LANES), F32), pltpu.VMEM((n1 * FFT_PITCH_Y, LANES), F32),
                        pltpu.VMEM((l + 2 * CONV_PAD, LANES), F32)],
        compiler_params=_params("parallel", "arbitrary"),
        name="hyena_mixer",
    )(z3, z3, z3, w_short, w_short, w_short, skip.reshape(1, -1), kf, fa, ga, fb, fbi)


MLA_HW = 2 * LANES
MLA_HEADS_PER_STEP = 2


def _rope_group(g, cs):
    prod = g * cs
    s = prod + pltpu.roll(prod, MLA_ROPE, 1)
    lane = lax.broadcasted_iota(jnp.int32, s.shape, 1)
    return jnp.where(lane < MLA_ROPE, s, 0.0)


def _mla_prep_kernel(cq_ref, ckv_ref, kr_ref, gq_ref, gkv_ref, wq_ref, wkv_ref, cs_ref, q_ref, k_ref, v_ref):
    cs = cs_ref[...]
    hq = _rms_bf16(cq_ref[...], gq_ref[...])
    q = jnp.dot(hq, wq_ref[...], preferred_element_type=F32)
    hkv = _rms_bf16(ckv_ref[...], gkv_ref[...])
    kv = jnp.dot(hkv, wkv_ref[...], preferred_element_type=F32)
    k_rope = _rope_group(kr_ref[...].astype(F32), cs).astype(BF16)
    lane = lax.broadcasted_iota(jnp.int32, (cs.shape[0], LANES), 1)
    ones_lane = jnp.where(lane == 0, 1.0, 0.0).astype(BF16)
    for h in range(MLA_HEADS):
        o = h * MLA_HW
        q_ref[:, o:o + LANES] = q[:, o:o + LANES].astype(BF16)
        q_ref[:, o + LANES:o + MLA_HW] = _rope_group(q[:, o + LANES:o + MLA_HW], cs).astype(BF16)
        k_ref[:, o:o + LANES] = kv[:, o:o + LANES].astype(BF16)
        k_ref[:, o + LANES:o + MLA_HW] = k_rope
        v_ref[:, o:o + LANES] = kv[:, o + LANES:o + MLA_HW].astype(BF16)
        v_ref[:, o + LANES:o + MLA_HW] = ones_lane


def _mla_attn_kernel(q_ref, k_ref, v_ref, o_ref):
    c = (MLA_NOPE + MLA_ROPE) ** -0.5 * math.log2(math.e)
    heads = range(MLA_HEADS_PER_STEP)
    s = [lax.dot_general(q_ref[0, :, h * MLA_HW:(h + 1) * MLA_HW], k_ref[0, :, h * MLA_HW:(h + 1) * MLA_HW],
                         (((1,), (1,)), ((), ())), preferred_element_type=F32) for h in heads]
    for h in heads:
        p = jnp.exp2((s[h] - jnp.max(s[h], axis=-1, keepdims=True)) * c)
        o = jnp.dot(p.astype(BF16), v_ref[0, :, h * MLA_HW:(h + 1) * MLA_HW], preferred_element_type=F32)
        o_ref[0, :, h * MLA_V:(h + 1) * MLA_V] = (o[:, :MLA_V] / o[:, MLA_V:MLA_V + 1]).astype(o_ref.dtype)


def mla_mixer(z, b, l, g_q, g_kv, wq_p, w_ukv, cs_tab, tm, tq):
    m = z.shape[0]
    lt = l // tm
    qp, kp, vp = pl.pallas_call(
        _mla_prep_kernel,
        grid=(m // tm,),
        in_specs=[pl.BlockSpec((tm, MLA_Q_RANK), lambda i: (i, Z_CQ // MLA_Q_RANK)),
                  pl.BlockSpec((tm, MLA_KV_RANK), lambda i: (i, Z_CKV // MLA_KV_RANK)),
                  pl.BlockSpec((tm, LANES), lambda i: (i, Z_KR // LANES)),
                  pl.BlockSpec((1, MLA_Q_RANK), lambda i: (0, 0)),
                  pl.BlockSpec((1, MLA_KV_RANK), lambda i: (0, 0)),
                  pl.BlockSpec(wq_p.shape, lambda i: (0, 0)),
                  pl.BlockSpec(w_ukv.shape, lambda i: (0, 0)),
                  pl.BlockSpec((tm, LANES), lambda i: (i % lt, 0))],
        out_specs=[pl.BlockSpec((tm, MLA_HEADS * MLA_HW), lambda i: (i, 0)),
                   pl.BlockSpec((tm, MLA_HEADS * MLA_HW), lambda i: (i, 0)),
                   pl.BlockSpec((tm, MLA_HEADS * MLA_HW), lambda i: (i, 0))],
        out_shape=[jax.ShapeDtypeStruct((m, MLA_HEADS * MLA_HW), BF16),
                   jax.ShapeDtypeStruct((m, MLA_HEADS * MLA_HW), BF16),
                   jax.ShapeDtypeStruct((m, MLA_HEADS * MLA_HW), BF16)],
        compiler_params=_params("parallel"),
        name="mla_prep",
    )(z, z, z, g_q.reshape(1, -1), g_kv.reshape(1, -1), wq_p, w_ukv, cs_tab)
    qp = qp.reshape(b, l, -1)
    kp = kp.reshape(b, l, -1)
    vp = vp.reshape(b, l, -1)
    hs = MLA_HEADS_PER_STEP
    return pl.pallas_call(
        _mla_attn_kernel,
        grid=(b, MLA_HEADS // hs, l // tq),
        in_specs=[pl.BlockSpec((1, tq, hs * MLA_HW), lambda i, h, t: (i, t, h)),
                  pl.BlockSpec((1, l, hs * MLA_HW), lambda i, h, t: (i, 0, h)),
                  pl.BlockSpec((1, l, hs * MLA_HW), lambda i, h, t: (i, 0, h))],
        out_specs=pl.BlockSpec((1, tq, hs * MLA_V), lambda i, h, t: (i, t, h)),
        out_shape=jax.ShapeDtypeStruct((b, l, MLA_HEADS * MLA_V), BF16),
        compiler_params=_params("parallel", "parallel", "arbitrary"),
        name="mla_attention",
    )(qp, kp, vp)


def _split_bf16(x):
    hi = x.astype(BF16)
    return hi, (x - hi.astype(F32)).astype(BF16)


def _dot_split(a, b):
    (ah, al), (bh, bl) = a, b
    return (jnp.dot(ah, bh, preferred_element_type=F32)
            + (jnp.dot(ah, bl, preferred_element_type=F32) + jnp.dot(al, bh, preferred_element_type=F32)))


def _tri_unit_inverse(mats):
    c = mats[0].shape[0]
    eye = (lax.broadcasted_iota(jnp.int32, (c, c), 0) == lax.broadcasted_iota(jnp.int32, (c, c), 1)).astype(F32)
    ps = [eye - a for a in mats]
    ms = list(mats)
    for level in range(int(math.log2(c)) - 1):
        if level == 0:
            splits = [_split_bf16(m) for m in ms]
            ms = [_dot_split(sp, sp) for sp in splits]
            yield
            ps = [_dot_split(_split_bf16(p), _split_bf16(eye + m)) for p, m in zip(ps, ms)]
        else:
            hs = [m.astype(BF16) for m in ms]
            ms = [jnp.dot(h, h, preferred_element_type=F32) for h in hs]
            yield
            fs = [(eye + m).astype(BF16) for m in ms]
            ps = [jnp.dot(ph, f, preferred_element_type=F32) + jnp.dot(pl_, f, preferred_element_type=F32)
                  for (ph, pl_), f in zip([_split_bf16(p) for p in ps], fs)]
        yield
    return ps


def _interleave(main, side=None):
    gens = [g for g in (main, side) if g is not None]
    results = [None] * len(gens)
    live = [True] * len(gens)
    while any(live):
        for idx, gen in enumerate(gens):
            if live[idx]:
                try:
                    next(gen)
                except StopIteration as stop:
                    results[idx] = stop.value
                    live[idx] = False
    return results


def _bdot(a, b):
    return jnp.dot(a.astype(BF16), b.astype(BF16), preferred_element_type=F32)


def _bdot_nt(a, b):
    return lax.dot_general(a.astype(BF16), b.astype(BF16), (((1,), (1,)), ((), ())), preferred_element_type=F32)


def _dn_group_setup(chunks, refs):
    q_s, k_s, v_s, g_s, b_s = refs
    cc = DN_CHUNK
    n = range(len(chunks))
    rev = [r for _, r in chunks]
    r0 = [pl.multiple_of(c * cc, cc) for c, _ in chunks]
    q = [q_s[pl.ds(r, cc), :] for r in r0]
    k = [k_s[pl.ds(r, cc), :] for r in r0]
    v = [v_s[pl.ds(r, cc), :] for r in r0]
    lane = [2 if r else 0 for r in rev]
    ri = lax.broadcasted_iota(jnp.int32, (cc, cc), 0)
    ci = lax.broadcasted_iota(jnp.int32, (cc, cc), 1)
    incl = [(ri <= ci) if r else (ri >= ci) for r in rev]
    strict = [(ri < ci) if r else (ri > ci) for r in rev]
    g = [g_s[pl.ds(r, cc), :] for r in r0]
    g1 = [x.astype(BF16) for x in g]
    r1 = [x - h.astype(F32) for x, h in zip(g, g1)]
    g2 = [x.astype(BF16) for x in r1]
    g3 = [(x - h.astype(F32)).astype(BF16) for x, h in zip(r1, g2)]
    ones = [m.astype(BF16) for m in incl]
    gc = [jnp.dot(ones[i], g1[i], preferred_element_type=F32) for i in n]
    gc = [gc[i] + jnp.dot(ones[i], g2[i], preferred_element_type=F32) for i in n]
    gc = [gc[i] + jnp.dot(ones[i], g3[i], preferred_element_type=F32) for i in n]
    yield
    gcol = [gc[i][:, lane[i]:lane[i] + 1] for i in n]
    grow = [jnp.transpose(gc[i])[lane[i]:lane[i] + 1, :] for i in n]
    beta = [b_s[pl.ds(r0[i], cc), :][:, lane[i] + 1:lane[i] + 2] for i in n]
    g_last = [gcol[i][0:1, :] if rev[i] else gcol[i][cc - 1:cc, :] for i in n]
    decay = [jnp.where(incl[i], jnp.exp(jnp.where(incl[i], gcol[i] - grow[i], 0.0)), 0.0) for i in n]
    e_g = [jnp.exp(x) for x in gcol]
    kb = [k[i] * beta[i] for i in n]
    a_both = [_bdot_nt(jnp.concatenate([kb[i], q[i]], axis=0), k[i]) for i in n]
    yield
    a_kk = [jnp.where(strict[i], a_both[i][:cc] * decay[i], 0.0) for i in n]
    a_qk = [jnp.where(incl[i], a_both[i][cc:] * decay[i], 0.0) for i in n]
    t_inv = yield from _tri_unit_inverse(a_kk)
    uw = [_bdot(t_inv[i], jnp.concatenate([v[i] * beta[i], kb[i] * e_g[i]], axis=1)) for i in n]
    yield
    k_dec = [k[i] * jnp.exp(g_last[i] - gcol[i]) for i in n]
    wq = [jnp.concatenate([uw[i][:, DN_DV:], q[i] * e_g[i]], axis=0).astype(BF16) for i in n]
    ak = [jnp.concatenate([a_qk[i], jnp.transpose(k_dec[i])], axis=0).astype(BF16) for i in n]
    return [(wq[i], ak[i], uw[i][:, :DN_DV], jnp.exp(g_last[i])) for i in n]


def _dn_group_steps(states, setups, out_refs, chunks):
    cc = DN_CHUNK
    dirs = range(len(states))
    for t in range(len(setups[0])):
        st = [setups[d][t] for d in dirs]
        r = [jnp.dot(st[d][0], states[d].astype(BF16), preferred_element_type=F32) for d in dirs]
        yield
        v_new = [st[d][2] - r[d][:cc] for d in dirs]
        r2 = [jnp.dot(st[d][1], v_new[d].astype(BF16), preferred_element_type=F32) for d in dirs]
        yield
        states = [states[d] * st[d][3] + r2[d][cc:] for d in dirs]
        for d in dirs:
            out_refs[d][pl.ds(pl.multiple_of(chunks[d][t] * cc, cc), cc), :] = r[d][cc:] + r2[d][:cc]
    return states


def _dn_kernel(zq_ref, zk_ref, zv_ref, zg_ref, ab_ref, wq_ref, wk_ref, wv_ref, alog_ref, dtb_ref, gn_ref, o_ref,
               q_s, k_s, v_s, g_s, b_s, of_s, ob_s, pad_ref, *, nchunks):
    def act(z_ref, w_ref):
        c = _dwconv3(z_ref[0].astype(F32), w_ref, pad_ref)
        return _silu(c)

    def l2n(x):
        return x * lax.rsqrt(jnp.sum(x * x, axis=-1, keepdims=True) + RMS_EPS)

    q_s[...] = l2n(act(zq_ref, wq_ref)) * (DN_DK ** -0.5)
    k_s[...] = l2n(act(zk_ref, wk_ref))
    v_s[...] = act(zv_ref, wv_ref)
    ab = ab_ref[0]
    x = ab + dtb_ref[0]
    softplus = jnp.maximum(x, 0.0) + jnp.log1p(jnp.exp(-jnp.abs(x)))
    g_s[...] = -jnp.exp(alog_ref[0]) * softplus
    b_s[...] = jax.nn.sigmoid(ab)
    refs = (q_s, k_s, v_s, g_s, b_s)

    ngroups = nchunks // DN_GROUP

    def group_chunks(gi):
        fwd = [gi * DN_GROUP + t for t in range(DN_GROUP)]
        return fwd, [nchunks - 1 - c for c in fwd]

    def setup_gen(gi):
        fwd, bwd = group_chunks(gi)
        return _dn_group_setup([(c, False) for c in fwd] + [(c, True) for c in bwd], refs)

    def steps_gen(gi, states, flat):
        setups = [tuple(flat[4 * i:4 * i + 4]) for i in range(2 * DN_GROUP)]
        return _dn_group_steps(states, [setups[:DN_GROUP], setups[DN_GROUP:]], (of_s, ob_s), group_chunks(gi))

    def flatten(setups):
        return [x for st in setups for x in st]

    def body(gi, carry):
        nxt, states = _interleave(setup_gen(gi + 1), steps_gen(gi, list(carry[:2]), carry[2:]))
        return (*states, *flatten(nxt))

    zero = jnp.zeros((DN_DK, DN_DV), F32)
    (first,) = _interleave(setup_gen(0))
    carry = lax.fori_loop(0, ngroups - 1, body, (zero, zero, *flatten(first)))
    _interleave(steps_gen(ngroups - 1, list(carry[:2]), carry[2:]))

    o = of_s[...] + ob_s[...]
    o = o * lax.rsqrt(jnp.mean(o * o, axis=-1, keepdims=True) + RMS_EPS) * gn_ref[...]
    gate = zg_ref[0].astype(F32)
    o_ref[0] = (o * _silu(gate)).astype(o_ref.dtype)


def deltanet_mixer(z3, ab3, w_conv, alog_p, dtb_p, g_norm):
    b, l, _ = z3.shape
    blk = Z_DNQKV // LANES
    gblk = Z_DNGATE // LANES
    hh = DN_HEADS

    def zspec(off):
        return pl.BlockSpec((1, l, LANES), lambda i, h: (i, 0, off + h))

    def wspec(seg):
        return pl.BlockSpec((3, LANES), lambda i, h: (0, seg * hh + h))

    vec = pl.BlockSpec((1, 1, LANES), lambda i, h: (h, 0, 0))
    seq = pltpu.VMEM((l, LANES), F32)
    return pl.pallas_call(
        functools.partial(_dn_kernel, nchunks=l // DN_CHUNK),
        grid=(b, hh),
        in_specs=[zspec(blk), zspec(blk + hh), zspec(blk + 2 * hh), zspec(gblk),
                  pl.BlockSpec((1, l, LANES), lambda i, h: (i, 0, h)),
                  wspec(0), wspec(1), wspec(2), vec, vec,
                  pl.BlockSpec((1, LANES), lambda i, h: (0, 0))],
        out_specs=pl.BlockSpec((1, l, LANES), lambda i, h: (i, 0, h)),
        out_shape=jax.ShapeDtypeStruct((b, l, hh * DN_DV), BF16),
        scratch_shapes=[seq] * 7 + [pltpu.VMEM((l + 2 * CONV_PAD, LANES), F32)],
        compiler_params=_params("parallel", "arbitrary"),
        name="deltanet_mixer",
    )(z3, z3, z3, z3, ab3, w_conv, w_conv, w_conv, alog_p, dtb_p, g_norm.reshape(1, -1))


def _rotate_half_cols(w):
    half = w.shape[-1] // 2
    return jnp.concatenate([-w[..., half:], w[..., :half]], axis=-1)


def _prep_w_in(w_in):
    cuts = np.cumsum([0, 1536, 1536, MLA_Q_RANK, MLA_KV_RANK, MLA_ROPE, 1536, 512, 16])
    dnab = w_in[:, cuts[7]:cuts[8]]
    w16 = w_in.astype(BF16)
    na, hy, cq, ckv, kr, dnqkv, dngate = (w16[:, cuts[i]:cuts[i + 1]] for i in range(7))
    gates = w16[:, cuts[8]:]
    pad = jnp.zeros((w_in.shape[0], Z_GATE - Z_KR - 2 * MLA_ROPE), BF16)
    main = jnp.concatenate([na, hy, dnqkv, dngate, cq, ckv, kr, _rotate_half_cols(kr), pad, 0.5 * gates], axis=1)
    ab = dnab.reshape(-1, 2, 2, DN_HEADS)
    ab = jnp.transpose(ab, (0, 3, 1, 2)).reshape(-1, DN_HEADS, 4)
    ab = jnp.pad(ab, ((0, 0), (0, 0), (0, LANES - 4))).reshape(-1, DN_HEADS * LANES)
    return main, ab.astype(BF16)


def _prep_w_uq(w_uq):
    k = w_uq.shape[0]
    w = w_uq.reshape(k, MLA_HEADS, MLA_NOPE + MLA_ROPE)
    rope_w = w[:, :, MLA_NOPE:]
    return jnp.concatenate([w, _rotate_half_cols(rope_w)], axis=-1).reshape(k, MLA_HEADS * MLA_HW).astype(BF16)


def _head_lane_vec(p):
    v = jnp.zeros((DN_HEADS, 1, LANES), F32)
    v = v.at[:, 0, 0].set(p[0].astype(F32))
    return v.at[:, 0, 2].set(p[1].astype(F32))


def _rope_table(l):
    half = MLA_ROPE // 2
    inv = ROPE_THETA ** (-jnp.arange(half, dtype=F32) / half)
    ang = jnp.arange(l, dtype=F32)[:, None] * inv[None, :]
    cos, sin = jnp.cos(ang), jnp.sin(ang)
    return jnp.concatenate([cos, cos, sin, sin], axis=-1)


def _split_const(x):
    hi = x.astype(BF16)
    lo = (x - hi.astype(np.float64)).astype(BF16)
    return jnp.asarray(hi), jnp.asarray(lo)


def _pick(n, pref):
    for t in pref:
        if n % t == 0:
            return t
    return n


def trunk(x_parts, l, norm_mix, w_in, na_rpb, hy_short, hy_skip, hy_w1, hy_b1, hy_w2, hy_b2, hy_w3,
          mla_g_q, mla_g_kv, mla_w_uq, mla_w_ukv, dn_conv, dn_a_log, dn_dt_bias, dn_g_norm,
          w_branch, w_out, norm_mlp, w_up, w_down, norm_final):
    d = x_parts[0].shape[1]
    part_rows = tuple(p.shape[0] for p in x_parts)
    m = sum(part_rows)
    b = m // l
    depth = w_in.shape[0]
    common = math.gcd(*part_rows)
    tm_big = _pick(common, (1024, 512, 256, 128))
    tm_mid = _pick(common, (512, 256, 128))
    fa, ga, fb, fbi = _fft_tables(l)
    n1 = 2 * l // FFT_N2
    tables = (jnp.asarray(fa[:, :, :n1 // 2], BF16), jnp.asarray(ga[:, :n1 // 2, :], BF16),
              jnp.asarray(fb, BF16), jnp.asarray(fbi, BF16))
    fa_split, fb_split = _split_const(fa[:, :, :n1 // 2]), _split_const(fb)
    cs_tab = _rope_table(l)
    x = tuple(x_parts)
    for layer in range(depth):
        last = layer == depth - 1
        w_main, w_ab = _prep_w_in(w_in[layer])
        z = norm_matmul(x, norm_mix[layer], w_main, BF16, tm_big, 1024)
        ab = norm_matmul(x, norm_mix[layer], w_ab, F32, tm_big, DN_HEADS * LANES)
        z3 = z.reshape(b, l, Z_COLS)
        br_a = neighbourhood_attention(z3, na_bias_table(na_rpb[layer]), _pick(l // GRID_W, (8, 4, 2, 1)))
        taps = hyena_filters(l, hy_w1[layer], hy_b1[layer], hy_w2[layer], hy_b2[layer], hy_w3[layer])
        kf = filter_spectrum(taps, fa_split, fb_split)
        br_b = hyena_mixer(z3, hy_short[layer], hy_skip[layer], kf, tables)
        br_c = mla_mixer(z, b, l, mla_g_q[layer], mla_g_kv[layer], _prep_w_uq(mla_w_uq[layer]),
                         mla_w_ukv[layer].astype(BF16), cs_tab, tm_mid, _pick(l, (512, 256, 128)))
        br_d = deltanet_mixer(z3, ab.reshape(b, l, -1), dn_conv[layer], _head_lane_vec(dn_a_log[layer]),
                              _head_lane_vec(dn_dt_bias[layer]), dn_g_norm[layer])
        branches = [t.reshape(m, BRANCH_W) for t in (br_a, br_b, br_c, br_d)]
        merged = gated_merge(z, branches, (0.5 * w_branch[layer]).astype(BF16), tm_mid)
        x_mid = matmul_residual(merged, w_out[layer].astype(BF16), x, tm_mid)
        x = tuple(mlp_block(x_mid, norm_mlp[layer], w_up[layer].astype(BF16), w_down[layer].astype(BF16), norm_final,
                            last, tm_mid, 1024, part_rows if last else (m,)))
    return x


def kernel(x_prompt, x_sample, norm_mix, w_in, na_rpb, hy_short, hy_skip, hy_w1, hy_b1, hy_w2, hy_b2, hy_w3,
           mla_g_q, mla_g_kv, mla_w_uq, mla_w_ukv, dn_conv, dn_a_log, dn_dt_bias, dn_g_norm,
           w_branch, w_out, norm_mlp, w_up, w_down, norm_final):
    assert x_prompt.shape[1:] == x_sample.shape[1:]
    l, d = x_prompt.shape[1:]
    y_prompt, y_sample = trunk((x_prompt.reshape(-1, d), x_sample.reshape(-1, d)), l, norm_mix, w_in, na_rpb,
                               hy_short, hy_skip, hy_w1, hy_b1, hy_w2, hy_b2, hy_w3, mla_g_q, mla_g_kv, mla_w_uq,
                               mla_w_ukv, dn_conv, dn_a_log, dn_dt_bias, dn_g_norm, w_branch, w_out, norm_mlp,
                               w_up, w_down, norm_final)
    return (y_prompt.reshape(x_prompt.shape), y_sample.reshape(x_sample.shape))
```

```python
import functools
import math

import jax
import jax.numpy as jnp
import numpy as np
from jax import lax
from jax.experimental import pallas as pl
from jax.experimental.pallas import tpu as pltpu

F32 = jnp.float32
BF16 = jnp.bfloat16
HIGHEST = lax.Precision.HIGHEST

VMEM_LIMIT_BYTES = 56 * 1024 * 1024
LANES = 128

D_MODEL = 2048
RMS_EPS = 1e-6
GRID_W = 64
N_BRANCH = 4
BRANCH_W = 512
NA_HEADS = 8
NA_HEAD_DIM = 64
NA_WIN_R = 8
NA_WIN_C = 16
HY_WIDTH = 512
HY_POS_BANDS = 16
HY_FILT_HIDDEN = 64
HY_FAST_DECAY = 0.3
HY_SLOW_DECAY = 1.5
HY_DECAY_TARGET = 1e-2
MLA_HEADS = 4
MLA_Q_RANK = 512
MLA_KV_RANK = 256
MLA_NOPE = 128
MLA_ROPE = 64
MLA_V = 128
ROPE_THETA = 10000.0
DN_HEADS = 4
DN_DK = 128
DN_DV = 128
DN_CHUNK = 64
DN_GROUP = 8
D_FF = 4 * D_MODEL

Z_NA = 0
Z_HY = 1536
Z_DNQKV = 3072
Z_DNGATE = 4608
Z_CQ = 5120
Z_CKV = 5632
Z_KR = 5888
Z_GATE = 6144
Z_COLS = Z_GATE + N_BRANCH * D_MODEL

FFT_N2 = 128
FFT_UNROLL = 8
FFT_GROUP = 4
FFT_PITCH_X = FFT_N2 + 8
FFT_PITCH_Y = 2 * FFT_N2 + 8


def _params(*sem):
    return pltpu.CompilerParams(dimension_semantics=sem, vmem_limit_bytes=VMEM_LIMIT_BYTES)


def _silu(x):
    return 0.5 * x * (jnp.tanh(0.5 * x) + 1.0)


def _rms_bf16(x, g):
    xf = x.astype(F32)
    y = xf * lax.rsqrt(jnp.mean(xf * xf, axis=-1, keepdims=True) + RMS_EPS)
    return (y * g).astype(BF16)


def _part_tiles(parts, tm):
    assert all(p.shape[0] % tm == 0 for p in parts)
    return tuple(p.shape[0] // tm for p in parts)


def _part_specs(tiles, block):
    specs, start = [], 0
    for count in tiles:
        specs.append(pl.BlockSpec(block, functools.partial(
            lambda i, j, start, count: (jnp.clip(i - start, 0, count - 1), 0), start=start, count=count)))
        start += count
    return specs


def _for_owning_part(tiles, refs, fn):
    if len(refs) == 1:
        fn(refs[0])
        return
    i = pl.program_id(0)
    start = 0
    for ref, count in zip(refs, tiles):
        pl.when((i >= start) & (i < start + count))(functools.partial(fn, ref))
        start += count


def _norm_mm_kernel(*refs, tiles):
    x_refs = refs[:len(tiles)]
    g_ref, w_ref, o_ref, h_ref = refs[len(tiles):]

    @pl.when(pl.program_id(1) == 0)
    def _():
        def norm(x_ref):
            h_ref[...] = _rms_bf16(x_ref[...], g_ref[...])

        _for_owning_part(tiles, x_refs, norm)

    o_ref[...] = jnp.dot(h_ref[...], w_ref[...], preferred_element_type=F32).astype(o_ref.dtype)


def norm_matmul(x_parts, g, w, out_dtype, tm, tn):
    k, n = w.shape
    tiles = _part_tiles(x_parts, tm)
    return pl.pallas_call(
        functools.partial(_norm_mm_kernel, tiles=tiles),
        grid=(sum(tiles), n // tn),
        in_specs=_part_specs(tiles, (tm, k)) + [pl.BlockSpec((1, k), lambda i, j: (0, 0)),
                                                 pl.BlockSpec((k, tn), lambda i, j: (0, j))],
        out_specs=pl.BlockSpec((tm, tn), lambda i, j: (i, j)),
        out_shape=jax.ShapeDtypeStruct((sum(tiles) * tm, n), out_dtype),
        scratch_shapes=[pltpu.VMEM((tm, k), BF16)],
        compiler_params=_params("parallel", "arbitrary"),
        name="norm_matmul",
    )(*x_parts, g.reshape(1, k), w)


def _mm_res_kernel(*refs, tiles):
    a_ref, w_ref = refs[:2]
    r_refs = refs[2:2 + len(tiles)]
    o_ref = refs[-1]
    y = jnp.dot(a_ref[...], w_ref[...], preferred_element_type=F32)

    def add(r_ref):
        o_ref[...] = r_ref[...] + y

    _for_owning_part(tiles, r_refs, add)


def matmul_residual(a, w, r_parts, tm):
    m, k = a.shape
    n = w.shape[1]
    tiles = _part_tiles(r_parts, tm)
    return pl.pallas_call(
        functools.partial(_mm_res_kernel, tiles=tiles),
        grid=(m // tm, 1),
        in_specs=[pl.BlockSpec((tm, k), lambda i, j: (i, 0)),
                  pl.BlockSpec((k, n), lambda i, j: (0, 0))] + _part_specs(tiles, (tm, n)),
        out_specs=pl.BlockSpec((tm, n), lambda i, j: (i, 0)),
        out_shape=jax.ShapeDtypeStruct((m, n), F32),
        compiler_params=_params("parallel", "arbitrary"),
        name="matmul_residual",
    )(a, w, *r_parts)


def _mlp_kernel(x_ref, g_ref, wu_ref, wd_ref, gf_ref, *refs, final_norm, tiles):
    o_refs, h_ref = refs[:-1], refs[-1]
    j = pl.program_id(1)

    def run(o_ref):
        @pl.when(j == 0)
        def _():
            x = x_ref[...]
            h_ref[...] = _rms_bf16(x, g_ref[...])
            o_ref[...] = x

        u = jnp.dot(h_ref[...], wu_ref[...], preferred_element_type=F32)
        a = jnp.square(jnp.maximum(u, 0.0)).astype(BF16)
        o_ref[...] += jnp.dot(a, wd_ref[...], preferred_element_type=F32)

        if final_norm:
            @pl.when(j == pl.num_programs(1) - 1)
            def _():
                y = o_ref[...]
                o_ref[...] = y * lax.rsqrt(jnp.mean(y * y, axis=-1, keepdims=True) + RMS_EPS) * gf_ref[...]

    _for_owning_part(tiles, o_refs, run)


def mlp_block(x, g, w_up, w_down, g_final, final_norm, tm, tf, out_rows):
    m, d = x.shape
    f = w_up.shape[1]
    assert sum(out_rows) == m and all(r % tm == 0 for r in out_rows)
    tiles = tuple(r // tm for r in out_rows)
    return pl.pallas_call(
        functools.partial(_mlp_kernel, final_norm=final_norm, tiles=tiles),
        grid=(m // tm, f // tf),
        in_specs=[pl.BlockSpec((tm, d), lambda i, j: (i, 0)),
                  pl.BlockSpec((1, d), lambda i, j: (0, 0)),
                  pl.BlockSpec((d, tf), lambda i, j: (0, j)),
                  pl.BlockSpec((tf, d), lambda i, j: (j, 0)),
                  pl.BlockSpec((1, d), lambda i, j: (0, 0))],
        out_specs=_part_specs(tiles, (tm, d)),
        out_shape=[jax.ShapeDtypeStruct((r, d), F32) for r in out_rows],
        scratch_shapes=[pltpu.VMEM((tm, d), BF16)],
        compiler_params=_params("parallel" if len(tiles) == 1 else "arbitrary", "arbitrary"),
        name="mlp_block",
    )(x, g.reshape(1, d), w_up, w_down, g_final.reshape(1, d))


def _merge_kernel(gate_ref, ba_ref, bb_ref, bc_ref, bd_ref, wb_ref, o_ref, acc_ref):
    n = pl.program_id(1)

    def contribution(b_ref):
        y = jnp.dot(b_ref[...], wb_ref[0], preferred_element_type=F32)
        return (jnp.tanh(gate_ref[...].astype(F32)) + 1.0) * y

    @pl.when(n == 0)
    def _():
        acc_ref[...] = contribution(ba_ref)

    @pl.when(n == 1)
    def _():
        acc_ref[...] += contribution(bb_ref)

    @pl.when(n == 2)
    def _():
        acc_ref[...] += contribution(bc_ref)

    @pl.when(n == 3)
    def _():
        o_ref[...] = (acc_ref[...] + contribution(bd_ref)).astype(o_ref.dtype)


def gated_merge(z, branches, w_branch, tm):
    m = z.shape[0]
    gate_blk0 = Z_GATE // D_MODEL
    br_spec = pl.BlockSpec((tm, BRANCH_W), lambda i, n: (i, 0))
    return pl.pallas_call(
        _merge_kernel,
        grid=(m // tm, N_BRANCH),
        in_specs=[pl.BlockSpec((tm, D_MODEL), lambda i, n: (i, gate_blk0 + n)),
                  br_spec, br_spec, br_spec, br_spec,
                  pl.BlockSpec((1, BRANCH_W, D_MODEL), lambda i, n: (n, 0, 0))],
        out_specs=pl.BlockSpec((tm, D_MODEL), lambda i, n: (i, 0)),
        out_shape=jax.ShapeDtypeStruct((m, D_MODEL), BF16),
        scratch_shapes=[pltpu.VMEM((tm, D_MODEL), F32)],
        compiler_params=_params("parallel", "arbitrary"),
        name="gated_merge",
    )(z, *branches, w_branch)


NA_ROWS_LOCKSTEP = 2
NA_MASK = -1e30


def na_bias_table(rpb):
    q = np.arange(GRID_W)
    kc = np.arange(GRID_W)
    cs = np.clip(q - NA_WIN_C // 2, 0, GRID_W - NA_WIN_C)
    ok = (kc[None, :] >= cs[:, None]) & (kc[None, :] < cs[:, None] + NA_WIN_C)
    dcol = np.clip(kc[None, :] - q[:, None] + NA_WIN_C - 1, 0, 2 * NA_WIN_C - 2)
    drow = np.arange(NA_WIN_R)[None, :] - np.arange(NA_WIN_R)[:, None] + NA_WIN_R - 1
    t = rpb.astype(F32)[:, drow]
    t = t[:, :, :, dcol]
    t = jnp.transpose(t, (1, 0, 3, 2, 4))
    t = jnp.where(jnp.asarray(ok)[None, None, :, None, :], t, NA_MASK)
    return t.reshape(NA_WIN_R, NA_HEADS, GRID_W, NA_WIN_R * GRID_W)


def _na_kernel(q_ref, k_ref, v_ref, bias_ref, o_ref, *, rows_per_step, rows):
    rblk = pl.program_id(1)
    win = NA_WIN_R * GRID_W
    scale = NA_HEAD_DIM ** -0.5

    pair_w = 2 * NA_HEAD_DIM
    low = lax.broadcasted_iota(jnp.int32, (GRID_W, pair_w), 1) < NA_HEAD_DIM

    def rows_body(it, carry):
        inst = []
        q, kb, vb, off, qstart = [], [], [], [], []
        for t in range(NA_ROWS_LOCKSTEP):
            rl = it * NA_ROWS_LOCKSTEP + t
            r = rblk * rows_per_step + rl
            rs = jnp.clip(r - NA_WIN_R // 2, 0, rows - NA_WIN_R)
            off.append(r - rs)
            qstart.append(pl.multiple_of(rl * GRID_W, GRID_W))
            kstart = pl.multiple_of(rs * GRID_W, GRID_W)
            q.append(q_ref[0, pl.ds(qstart[t], GRID_W), :])
            kb.append(k_ref[0, pl.ds(kstart, win), :])
            vb.append(v_ref[0, pl.ds(kstart, win), :])
            inst += [(t, h) for h in range(NA_HEADS)]

        def pair(x, h):
            return x[:, (h // 2) * pair_w:(h // 2 + 1) * pair_w]

        qm = [jnp.where(low if h % 2 == 0 else ~low, pair(q[t], h), jnp.zeros((), BF16)) for t, h in inst]
        s = [lax.dot_general(qm[i], pair(kb[t], h), (((1,), (1,)), ((), ())), preferred_element_type=F32)
             for i, (t, h) in enumerate(inst)]
        s = [s[i] * scale + bias_ref[off[t], h] for i, (t, h) in enumerate(inst)]
        p = [jnp.exp(x - jnp.max(x, axis=-1, keepdims=True)) for x in s]
        l = [jnp.sum(x, axis=-1, keepdims=True) for x in p]
        o = [jnp.dot(p[i].astype(BF16), pair(vb[t], h), preferred_element_type=F32) / l[i]
             for i, (t, h) in enumerate(inst)]
        for t in range(NA_ROWS_LOCKSTEP):
            base = t * NA_HEADS
            outs = [jnp.where(low, o[base + h], o[base + h + 1]) for h in range(0, NA_HEADS, 2)]
            o_ref[0, pl.ds(qstart[t], GRID_W), :] = jnp.concatenate(outs, axis=1).astype(o_ref.dtype)
        return carry

    lax.fori_loop(0, rows_per_step // NA_ROWS_LOCKSTEP, rows_body, 0)


def neighbourhood_attention(z3, bias, rows_per_step):
    b, l, _ = z3.shape
    rows = l // GRID_W
    w = NA_HEADS * NA_HEAD_DIM
    blk = Z_NA // w
    return pl.pallas_call(
        functools.partial(_na_kernel, rows_per_step=rows_per_step, rows=rows),
        grid=(b, rows // rows_per_step),
        in_specs=[pl.BlockSpec((1, rows_per_step * GRID_W, w), lambda i, r: (i, r, blk)),
                  pl.BlockSpec((1, l, w), lambda i, r: (i, 0, blk + 1)),
                  pl.BlockSpec((1, l, w), lambda i, r: (i, 0, blk + 2)),
                  pl.BlockSpec(bias.shape, lambda i, r: (0, 0, 0, 0))],
        out_specs=pl.BlockSpec((1, rows_per_step * GRID_W, w), lambda i, r: (i, r, 0)),
        out_shape=jax.ShapeDtypeStruct((b, l, w), BF16),
        compiler_params=_params("parallel", "arbitrary"),
        name="neighbourhood_attention",
    )(z3, z3, z3, bias)


def _fft_tables(l):
    n = 2 * l
    n2 = FFT_N2
    n1 = n // n2
    a = np.arange(n1)
    b = np.arange(n2)
    k1 = np.arange(n1)
    t = n2 * a[None, :] + b[:, None]
    th = 2.0 * np.pi * (k1[None, :, None] * t[:, None, :] % n) / n
    fa = np.concatenate([np.cos(th), -np.sin(th)], axis=1)
    th_t = np.transpose(th, (0, 2, 1))
    ga = np.concatenate([np.cos(th_t), -np.sin(th_t)], axis=2)
    ph = 2.0 * np.pi * (np.outer(b, b) % n2) / n2
    cr, ci = np.cos(ph), -np.sin(ph)
    fb = np.block([[cr, -ci], [ci, cr]])
    fbi = np.block([[cr, ci], [-ci, cr]])
    return fa, ga, fb, fbi


def _filter_fft_kernel(hf_ref, hb_ref, fa_hi_ref, fa_lo_ref, fb_hi_ref, fb_lo_ref, o_ref, y_ref, *, n1):
    n2 = FFT_N2
    na = n1 // 2
    inv_n = 1.0 / (n1 * n2)

    def transform(src_ref, combine):
        def step_a(b, carry):
            xb = _split_bf16(src_ref[pl.ds(b, na, stride=n2), :])
            r = _dot_split((fa_hi_ref[b], fa_lo_ref[b]), xb)
            y_ref[pl.ds(b, n1, stride=2 * n2), :] = r[:n1]
            y_ref[pl.ds(b + n2, n1, stride=2 * n2), :] = r[n1:]
            return carry

        lax.fori_loop(0, n2, step_a, 0, unroll=2)

        def step_b(k1, carry):
            r0 = pl.multiple_of(k1 * 2 * n2, 2 * n2)
            x = _dot_split((fb_hi_ref[...], fb_lo_ref[...]), _split_bf16(y_ref[pl.ds(r0, 2 * n2), :]))
            combine(r0, x * inv_n)
            return carry

        lax.fori_loop(0, n1, step_b, 0, unroll=2)

    def store(r0, x):
        o_ref[pl.ds(r0, 2 * n2), :] = x

    def add_conjugate(r0, x):
        o_ref[pl.ds(r0, n2), :] += x[:n2]
        o_ref[pl.ds(r0 + n2, n2), :] -= x[n2:]

    transform(hf_ref, store)
    transform(hb_ref, add_conjugate)


def filter_spectrum(h, fa, fb):
    l = h.shape[0]
    c = h.shape[1] // 2
    n = 2 * l
    n1 = n // FFT_N2
    nch = c // LANES
    return pl.pallas_call(
        functools.partial(_filter_fft_kernel, n1=n1),
        grid=(nch,),
        in_specs=[pl.BlockSpec((l, LANES), lambda i: (0, i)),
                  pl.BlockSpec((l, LANES), lambda i: (0, nch + i)),
                  pl.BlockSpec(fa[0].shape, lambda i: (0, 0, 0)),
                  pl.BlockSpec(fa[1].shape, lambda i: (0, 0, 0)),
                  pl.BlockSpec(fb[0].shape, lambda i: (0, 0)),
                  pl.BlockSpec(fb[1].shape, lambda i: (0, 0))],
        out_specs=pl.BlockSpec((2 * n, LANES), lambda i: (0, i)),
        out_shape=jax.ShapeDtypeStruct((2 * n, c), F32),
        scratch_shapes=[pltpu.VMEM((2 * n, LANES), F32)],
        compiler_params=_params("parallel"),
        name="hyena_filter_spectrum",
    )(h, h, *fa, *fb)


def _hyena_filter_kernel(z_ref, w1_ref, b1_ref, w2_ref, b2_ref, w3_ref, dec_ref, o_ref):
    hid = jnp.sin(jnp.dot(z_ref[...], w1_ref[...], precision=HIGHEST, preferred_element_type=F32) + b1_ref[...])
    hid = jnp.sin(jnp.dot(hid, w2_ref[...], precision=HIGHEST, preferred_element_type=F32) + b2_ref[...])
    h = jnp.dot(hid, w3_ref[...], precision=HIGHEST, preferred_element_type=F32)
    dec = dec_ref[...]
    hf = h[:, :HY_WIDTH] * dec
    hb = h[:, HY_WIDTH:] * dec
    norm = (jnp.sum(jnp.abs(hf), axis=0, keepdims=True) + jnp.sum(jnp.abs(hb), axis=0, keepdims=True)) + RMS_EPS
    o_ref[:, :HY_WIDTH] = hf / norm
    o_ref[:, HY_WIDTH:] = hb / norm


def hyena_filters(l, w1, b1, w2, b2, w3):
    t = jnp.linspace(0.0, 1.0, l, dtype=F32)[:, None]
    w = 2.0 * math.pi * jnp.arange(l, dtype=F32)[:, None] / l
    bands = jnp.linspace(1e-4, HY_POS_BANDS - 1, HY_POS_BANDS, dtype=F32)[None, :]
    z = jnp.concatenate([t, jnp.cos(bands * w), -jnp.sin(bands * w)], axis=-1)
    pad = LANES - z.shape[1]
    z = jnp.pad(z, ((0, 0), (0, pad)))
    w1p = jnp.pad(w1.astype(F32), ((0, pad), (0, 0)))
    max_decay = math.log(HY_DECAY_TARGET) / HY_FAST_DECAY
    min_decay = math.log(HY_DECAY_TARGET) / HY_SLOW_DECAY
    deltas = jnp.abs(jnp.linspace(min_decay, max_decay, HY_WIDTH, dtype=F32))
    dec = jnp.exp(-t * deltas[None, :])
    return pl.pallas_call(
        _hyena_filter_kernel,
        out_shape=jax.ShapeDtypeStruct((l, 2 * HY_WIDTH), F32),
        compiler_params=pltpu.CompilerParams(vmem_limit_bytes=VMEM_LIMIT_BYTES),
        name="hyena_filter_ffn",
    )(z, w1p, b1.reshape(1, -1).astype(F32), w2.astype(F32), b2.reshape(1, -1).astype(F32), w3.astype(F32), dec)


CONV_PAD = 8


def _dwconv3(x, w_ref, pad_ref):
    n = x.shape[0]
    zeros = jnp.zeros((CONV_PAD, x.shape[1]), F32)
    pad_ref[0:CONV_PAD, :] = zeros
    pad_ref[CONV_PAD + n:2 * CONV_PAD + n, :] = zeros
    pad_ref[CONV_PAD:CONV_PAD + n, :] = x
    w = w_ref[...].astype(F32)
    return (pad_ref[CONV_PAD - 1:CONV_PAD - 1 + n, :] * w[0:1] + x * w[1:2]
            + pad_ref[CONV_PAD + 1:CONV_PAD + 1 + n, :] * w[2:3])


def _hyena_kernel(x1_ref, x2_ref, v_ref, w1_ref, w2_ref, wv_ref, skip_ref, kf_ref, fa_ref, ga_ref, fb_ref, fbi_ref,
                  o_ref, xs_ref, y_ref, pad_ref, *, n1):
    n2 = FFT_N2
    na = n1 // 2
    pa, py = FFT_PITCH_X, FFT_PITCH_Y
    vg = _dwconv3(v_ref[0].astype(F32), wv_ref, pad_ref) * _dwconv3(x1_ref[0].astype(F32), w1_ref, pad_ref)
    for a in range(na):
        xs_ref[a * pa:a * pa + n2, :] = vg[a * n2:(a + 1) * n2]

    def fwd_a(b, carry):
        xb = xs_ref[pl.ds(b, na, stride=pa), :].astype(BF16)
        r = jnp.dot(fa_ref[b], xb, preferred_element_type=F32)
        y_ref[pl.ds(b, n1, stride=py), :] = r[:n1]
        y_ref[pl.ds(b + n2, n1, stride=py), :] = r[n1:]
        return carry

    lax.fori_loop(0, n2, fwd_a, 0, unroll=FFT_UNROLL)

    def mid(kg, carry):
        grp = range(FFT_GROUP)
        k1 = [kg * FFT_GROUP + t for t in grp]
        r0 = [pl.multiple_of(k * py, 8) for k in k1]
        f0 = [pl.multiple_of(k * 2 * n2, 2 * n2) for k in k1]
        fb = fb_ref[...]
        x = [jnp.dot(fb, y_ref[pl.ds(r0[t], 2 * n2), :].astype(BF16), preferred_element_type=F32) for t in grp]
        p = []
        for t in grp:
            xr, xi = x[t][:n2], x[t][n2:]
            kr = kf_ref[pl.ds(f0[t], n2), :]
            ki = kf_ref[pl.ds(f0[t] + n2, n2), :]
            p.append(jnp.concatenate([xr * kr - xi * ki, xr * ki + xi * kr], axis=0).astype(BF16))
        fbi = fbi_ref[...]
        q = [jnp.dot(fbi, p[t], preferred_element_type=F32) for t in grp]
        for t in grp:
            y_ref[pl.ds(r0[t], 2 * n2), :] = q[t]
        return carry

    lax.fori_loop(0, n1 // FFT_GROUP, mid, 0)

    def inv_a(b, carry):
        qb = jnp.concatenate([y_ref[pl.ds(b, n1, stride=py), :],
                              y_ref[pl.ds(b + n2, n1, stride=py), :]], axis=0).astype(BF16)
        xs_ref[pl.ds(b, na, stride=pa), :] = jnp.dot(ga_ref[b], qb, preferred_element_type=F32)
        return carry

    lax.fori_loop(0, n2, inv_a, 0, unroll=FFT_UNROLL)

    gate = _dwconv3(x2_ref[0].astype(F32), w2_ref, pad_ref)
    skip = skip_ref[...].astype(F32)
    for a in range(na):
        rows = slice(a * n2, (a + 1) * n2)
        o_ref[0, rows, :] = ((xs_ref[a * pa:a * pa + n2, :] + vg[rows] * skip) * gate[rows]).astype(o_ref.dtype)


def hyena_mixer(z3, w_short, skip, kf, tables):
    b, l, _ = z3.shape
    n1 = 2 * l // FFT_N2
    fa, ga, fb, fbi = tables
    nch = HY_WIDTH // LANES
    blk = Z_HY // LANES

    def zspec(seg):
        return pl.BlockSpec((1, l, LANES), lambda c, i: (i, 0, blk + seg * nch + c))

    def wspec(seg):
        return pl.BlockSpec((3, LANES), lambda c, i: (0, seg * nch + c))

    return pl.pallas_call(
        functools.partial(_hyena_kernel, n1=n1),
        grid=(nch, b),
        in_specs=[zspec(0), zspec(1), zspec(2), wspec(0), wspec(1), wspec(2),
                  pl.BlockSpec((1, LANES), lambda c, i: (0, c)),
                  pl.BlockSpec((4 * l, LANES), lambda c, i: (0, c)),
                  pl.BlockSpec(fa.shape, lambda c, i: (0, 0, 0)),
                  pl.BlockSpec(ga.shape, lambda c, i: (0, 0, 0)),
                  pl.BlockSpec(fb.shape, lambda c, i: (0, 0)),
                  pl.BlockSpec(fbi.shape, lambda c, i: (0, 0))],
        out_specs=pl.BlockSpec((1, l, LANES), lambda c, i: (i, 0, c)),
        out_shape=jax.ShapeDtypeStruct((b, l, HY_WIDTH), BF16),
        scratch_shapes=[pltpu.VMEM((n1 // 2 * FFT_PITCH_X, LANES), F32), pltpu.VMEM((n1 * FFT_PITCH_Y, LANES), F32),
                        pltpu.VMEM((l + 2 * CONV_PAD, LANES), F32)],
        compiler_params=_params("parallel", "arbitrary"),
        name="hyena_mixer",
    )(z3, z3, z3, w_short, w_short, w_short, skip.reshape(1, -1), kf, fa, ga, fb, fbi)


MLA_HW = 2 * LANES
MLA_HEADS_PER_STEP = 2


def _rope_group(g, cs):
    prod = g * cs
    s = prod + pltpu.roll(prod, MLA_ROPE, 1)
    lane = lax.broadcasted_iota(jnp.int32, s.shape, 1)
    return jnp.where(lane < MLA_ROPE, s, 0.0)


def _mla_prep_kernel(cq_ref, ckv_ref, kr_ref, gq_ref, gkv_ref, wq_ref, wkv_ref, cs_ref, q_ref, k_ref, v_ref):
    cs = cs_ref[...]
    hq = _rms_bf16(cq_ref[...], gq_ref[...])
    q = jnp.dot(hq, wq_ref[...], preferred_element_type=F32)
    hkv = _rms_bf16(ckv_ref[...], gkv_ref[...])
    kv = jnp.dot(hkv, wkv_ref[...], preferred_element_type=F32)
    k_rope = _rope_group(kr_ref[...].astype(F32), cs).astype(BF16)
    lane = lax.broadcasted_iota(jnp.int32, (cs.shape[0], LANES), 1)
    ones_lane = jnp.where(lane == 0, 1.0, 0.0).astype(BF16)
    for h in range(MLA_HEADS):
        o = h * MLA_HW
        q_ref[:, o:o + LANES] = q[:, o:o + LANES].astype(BF16)
        q_ref[:, o + LANES:o + MLA_HW] = _rope_group(q[:, o + LANES:o + MLA_HW], cs).astype(BF16)
        k_ref[:, o:o + LANES] = kv[:, o:o + LANES].astype(BF16)
        k_ref[:, o + LANES:o + MLA_HW] = k_rope
        v_ref[:, o:o + LANES] = kv[:, o + LANES:o + MLA_HW].astype(BF16)
        v_ref[:, o + LANES:o + MLA_HW] = ones_lane


def _mla_attn_kernel(q_ref, k_ref, v_ref, o_ref):
    c = (MLA_NOPE + MLA_ROPE) ** -0.5 * math.log2(math.e)
    heads = range(MLA_HEADS_PER_STEP)
    s = [lax.dot_general(q_ref[0, :, h * MLA_HW:(h + 1) * MLA_HW], k_ref[0, :, h * MLA_HW:(h + 1) * MLA_HW],
                         (((1,), (1,)), ((), ())), preferred_element_type=F32) for h in heads]
    for h in heads:
        p = jnp.exp2((s[h] - jnp.max(s[h], axis=-1, keepdims=True)) * c)
        o = jnp.dot(p.astype(BF16), v_ref[0, :, h * MLA_HW:(h + 1) * MLA_HW], preferred_element_type=F32)
        o_ref[0, :, h * MLA_V:(h + 1) * MLA_V] = (o[:, :MLA_V] / o[:, MLA_V:MLA_V + 1]).astype(o_ref.dtype)


def mla_mixer(z, b, l, g_q, g_kv, wq_p, w_ukv, cs_tab, tm, tq):
    m = z.shape[0]
    lt = l // tm
    qp, kp, vp = pl.pallas_call(
        _mla_prep_kernel,
        grid=(m // tm,),
        in_specs=[pl.BlockSpec((tm, MLA_Q_RANK), lambda i: (i, Z_CQ // MLA_Q_RANK)),
                  pl.BlockSpec((tm, MLA_KV_RANK), lambda i: (i, Z_CKV // MLA_KV_RANK)),
                  pl.BlockSpec((tm, LANES), lambda i: (i, Z_KR // LANES)),
                  pl.BlockSpec((1, MLA_Q_RANK), lambda i: (0, 0)),
                  pl.BlockSpec((1, MLA_KV_RANK), lambda i: (0, 0)),
                  pl.BlockSpec(wq_p.shape, lambda i: (0, 0)),
                  pl.BlockSpec(w_ukv.shape, lambda i: (0, 0)),
                  pl.BlockSpec((tm, LANES), lambda i: (i % lt, 0))],
        out_specs=[pl.BlockSpec((tm, MLA_HEADS * MLA_HW), lambda i: (i, 0)),
                   pl.BlockSpec((tm, MLA_HEADS * MLA_HW), lambda i: (i, 0)),
                   pl.BlockSpec((tm, MLA_HEADS * MLA_HW), lambda i: (i, 0))],
        out_shape=[jax.ShapeDtypeStruct((m, MLA_HEADS * MLA_HW), BF16),
                   jax.ShapeDtypeStruct((m, MLA_HEADS * MLA_HW), BF16),
                   jax.ShapeDtypeStruct((m, MLA_HEADS * MLA_HW), BF16)],
        compiler_params=_params("parallel"),
        name="mla_prep",
    )(z, z, z, g_q.reshape(1, -1), g_kv.reshape(1, -1), wq_p, w_ukv, cs_tab)
    qp = qp.reshape(b, l, -1)
    kp = kp.reshape(b, l, -1)
    vp = vp.reshape(b, l, -1)
    hs = MLA_HEADS_PER_STEP
    return pl.pallas_call(
        _mla_attn_kernel,
        grid=(b, MLA_HEADS // hs, l // tq),
        in_specs=[pl.BlockSpec((1, tq, hs * MLA_HW), lambda i, h, t: (i, t, h)),
                  pl.BlockSpec((1, l, hs * MLA_HW), lambda i, h, t: (i, 0, h)),
                  pl.BlockSpec((1, l, hs * MLA_HW), lambda i, h, t: (i, 0, h))],
        out_specs=pl.BlockSpec((1, tq, hs * MLA_V), lambda i, h, t: (i, t, h)),
        out_shape=jax.ShapeDtypeStruct((b, l, MLA_HEADS * MLA_V), BF16),
        compiler_params=_params("parallel", "parallel", "arbitrary"),
        name="mla_attention",
    )(qp, kp, vp)


def _split_bf16(x):
    hi = x.astype(BF16)
    return hi, (x - hi.astype(F32)).astype(BF16)


def _dot_split(a, b):
    (ah, al), (bh, bl) = a, b
    return (jnp.dot(ah, bh, preferred_element_type=F32)
            + (jnp.dot(ah, bl, preferred_element_type=F32) + jnp.dot(al, bh, preferred_element_type=F32)))


def _tri_unit_inverse(mats):
    c = mats[0].shape[0]
    eye = (lax.broadcasted_iota(jnp.int32, (c, c), 0) == lax.broadcasted_iota(jnp.int32, (c, c), 1)).astype(F32)
    ps = [eye - a for a in mats]
    ms = list(mats)
    for level in range(int(math.log2(c)) - 1):
        if level == 0:
            splits = [_split_bf16(m) for m in ms]
            ms = [_dot_split(sp, sp) for sp in splits]
            yield
            ps = [_dot_split(_split_bf16(p), _split_bf16(eye + m)) for p, m in zip(ps, ms)]
        else:
            hs = [m.astype(BF16) for m in ms]
            ms = [jnp.dot(h, h, preferred_element_type=F32) for h in hs]
            yield
            fs = [(eye + m).astype(BF16) for m in ms]
            ps = [jnp.dot(ph, f, preferred_element_type=F32) + jnp.dot(pl_, f, preferred_element_type=F32)
                  for (ph, pl_), f in zip([_split_bf16(p) for p in ps], fs)]
        yield
    return ps


def _interleave(main, side=None):
    gens = [g for g in (main, side) if g is not None]
    results = [None] * len(gens)
    live = [True] * len(gens)
    while any(live):
        for idx, gen in enumerate(gens):
            if live[idx]:
                try:
                    next(gen)
                except StopIteration as stop:
                    results[idx] = stop.value
                    live[idx] = False
    return results


def _bdot(a, b):
    return jnp.dot(a.astype(BF16), b.astype(BF16), preferred_element_type=F32)


def _bdot_nt(a, b):
    return lax.dot_general(a.astype(BF16), b.astype(BF16), (((1,), (1,)), ((), ())), preferred_element_type=F32)


def _dn_group_setup(chunks, refs):
    q_s, k_s, v_s, g_s, b_s = refs
    cc = DN_CHUNK
    n = range(len(chunks))
    rev = [r for _, r in chunks]
    r0 = [pl.multiple_of(c * cc, cc) for c, _ in chunks]
    q = [q_s[pl.ds(r, cc), :] for r in r0]
    k = [k_s[pl.ds(r, cc), :] for r in r0]
    v = [v_s[pl.ds(r, cc), :] for r in r0]
    lane = [2 if r else 0 for r in rev]
    ri = lax.broadcasted_iota(jnp.int32, (cc, cc), 0)
    ci = lax.broadcasted_iota(jnp.int32, (cc, cc), 1)
    incl = [(ri <= ci) if r else (ri >= ci) for r in rev]
    strict = [(ri < ci) if r else (ri > ci) for r in rev]
    g = [g_s[pl.ds(r, cc), :] for r in r0]
    g1 = [x.astype(BF16) for x in g]
    r1 = [x - h.astype(F32) for x, h in zip(g, g1)]
    g2 = [x.astype(BF16) for x in r1]
    g3 = [(x - h.astype(F32)).astype(BF16) for x, h in zip(r1, g2)]
    ones = [m.astype(BF16) for m in incl]
    gc = [jnp.dot(ones[i], g1[i], preferred_element_type=F32) for i in n]
    gc = [gc[i] + jnp.dot(ones[i], g2[i], preferred_element_type=F32) for i in n]
    gc = [gc[i] + jnp.dot(ones[i], g3[i], preferred_element_type=F32) for i in n]
    yield
    gcol = [gc[i][:, lane[i]:lane[i] + 1] for i in n]
    grow = [jnp.transpose(gc[i])[lane[i]:lane[i] + 1, :] for i in n]
    beta = [b_s[pl.ds(r0[i], cc), :][:, lane[i] + 1:lane[i] + 2] for i in n]
    g_last = [gcol[i][0:1, :] if rev[i] else gcol[i][cc - 1:cc, :] for i in n]
    decay = [jnp.where(incl[i], jnp.exp(jnp.where(incl[i], gcol[i] - grow[i], 0.0)), 0.0) for i in n]
    e_g = [jnp.exp(x) for x in gcol]
    kb = [k[i] * beta[i] for i in n]
    a_both = [_bdot_nt(jnp.concatenate([kb[i], q[i]], axis=0), k[i]) for i in n]
    yield
    a_kk = [jnp.where(strict[i], a_both[i][:cc] * decay[i], 0.0) for i in n]
    a_qk = [jnp.where(incl[i], a_both[i][cc:] * decay[i], 0.0) for i in n]
    t_inv = yield from _tri_unit_inverse(a_kk)
    uw = [_bdot(t_inv[i], jnp.concatenate([v[i] * beta[i], kb[i] * e_g[i]], axis=1)) for i in n]
    yield
    k_dec = [k[i] * jnp.exp(g_last[i] - gcol[i]) for i in n]
    wq = [jnp.concatenate([uw[i][:, DN_DV:], q[i] * e_g[i]], axis=0).astype(BF16) for i in n]
    ak = [jnp.concatenate([a_qk[i], jnp.transpose(k_dec[i])], axis=0).astype(BF16) for i in n]
    return [(wq[i], ak[i], uw[i][:, :DN_DV], jnp.exp(g_last[i])) for i in n]


def _dn_group_steps(states, setups, out_refs, chunks):
    cc = DN_CHUNK
    dirs = range(len(states))
    for t in range(len(setups[0])):
        st = [setups[d][t] for d in dirs]
        r = [jnp.dot(st[d][0], states[d].astype(BF16), preferred_element_type=F32) for d in dirs]
        yield
        v_new = [st[d][2] - r[d][:cc] for d in dirs]
        r2 = [jnp.dot(st[d][1], v_new[d].astype(BF16), preferred_element_type=F32) for d in dirs]
        yield
        states = [states[d] * st[d][3] + r2[d][cc:] for d in dirs]
        for d in dirs:
            out_refs[d][pl.ds(pl.multiple_of(chunks[d][t] * cc, cc), cc), :] = r[d][cc:] + r2[d][:cc]
    return states


def _dn_kernel(zq_ref, zk_ref, zv_ref, zg_ref, ab_ref, wq_ref, wk_ref, wv_ref, alog_ref, dtb_ref, gn_ref, o_ref,
               q_s, k_s, v_s, g_s, b_s, of_s, ob_s, pad_ref, *, nchunks):
    def act(z_ref, w_ref):
        c = _dwconv3(z_ref[0].astype(F32), w_ref, pad_ref)
        return _silu(c)

    def l2n(x):
        return x * lax.rsqrt(jnp.sum(x * x, axis=-1, keepdims=True) + RMS_EPS)

    q_s[...] = l2n(act(zq_ref, wq_ref)) * (DN_DK ** -0.5)
    k_s[...] = l2n(act(zk_ref, wk_ref))
    v_s[...] = act(zv_ref, wv_ref)
    ab = ab_ref[0]
    x = ab + dtb_ref[0]
    softplus = jnp.maximum(x, 0.0) + jnp.log1p(jnp.exp(-jnp.abs(x)))
    g_s[...] = -jnp.exp(alog_ref[0]) * softplus
    b_s[...] = jax.nn.sigmoid(ab)
    refs = (q_s, k_s, v_s, g_s, b_s)

    ngroups = nchunks // DN_GROUP

    def group_chunks(gi):
        fwd = [gi * DN_GROUP + t for t in range(DN_GROUP)]
        return fwd, [nchunks - 1 - c for c in fwd]

    def setup_gen(gi):
        fwd, bwd = group_chunks(gi)
        return _dn_group_setup([(c, False) for c in fwd] + [(c, True) for c in bwd], refs)

    def steps_gen(gi, states, flat):
        setups = [tuple(flat[4 * i:4 * i + 4]) for i in range(2 * DN_GROUP)]
        return _dn_group_steps(states, [setups[:DN_GROUP], setups[DN_GROUP:]], (of_s, ob_s), group_chunks(gi))

    def flatten(setups):
        return [x for st in setups for x in st]

    def body(gi, carry):
        nxt, states = _interleave(setup_gen(gi + 1), steps_gen(gi, list(carry[:2]), carry[2:]))
        return (*states, *flatten(nxt))

    zero = jnp.zeros((DN_DK, DN_DV), F32)
    (first,) = _interleave(setup_gen(0))
    carry = lax.fori_loop(0, ngroups - 1, body, (zero, zero, *flatten(first)))
    _interleave(steps_gen(ngroups - 1, list(carry[:2]), carry[2:]))

    o = of_s[...] + ob_s[...]
    o = o * lax.rsqrt(jnp.mean(o * o, axis=-1, keepdims=True) + RMS_EPS) * gn_ref[...]
    gate = zg_ref[0].astype(F32)
    o_ref[0] = (o * _silu(gate)).astype(o_ref.dtype)


def deltanet_mixer(z3, ab3, w_conv, alog_p, dtb_p, g_norm):
    b, l, _ = z3.shape
    blk = Z_DNQKV // LANES
    gblk = Z_DNGATE // LANES
    hh = DN_HEADS

    def zspec(off):
        return pl.BlockSpec((1, l, LANES), lambda i, h: (i, 0, off + h))

    def wspec(seg):
        return pl.BlockSpec((3, LANES), lambda i, h: (0, seg * hh + h))

    vec = pl.BlockSpec((1, 1, LANES), lambda i, h: (h, 0, 0))
    seq = pltpu.VMEM((l, LANES), F32)
    return pl.pallas_call(
        functools.partial(_dn_kernel, nchunks=l // DN_CHUNK),
        grid=(b, hh),
        in_specs=[zspec(blk), zspec(blk + hh), zspec(blk + 2 * hh), zspec(gblk),
                  pl.BlockSpec((1, l, LANES), lambda i, h: (i, 0, h)),
                  wspec(0), wspec(1), wspec(2), vec, vec,
                  pl.BlockSpec((1, LANES), lambda i, h: (0, 0))],
        out_specs=pl.BlockSpec((1, l, LANES), lambda i, h: (i, 0, h)),
        out_shape=jax.ShapeDtypeStruct((b, l, hh * DN_DV), BF16),
        scratch_shapes=[seq] * 7 + [pltpu.VMEM((l + 2 * CONV_PAD, LANES), F32)],
        compiler_params=_params("parallel", "arbitrary"),
        name="deltanet_mixer",
    )(z3, z3, z3, z3, ab3, w_conv, w_conv, w_conv, alog_p, dtb_p, g_norm.reshape(1, -1))


def _rotate_half_cols(w):
    half = w.shape[-1] // 2
    return jnp.concatenate([-w[..., half:], w[..., :half]], axis=-1)


def _prep_w_in(w_in):
    cuts = np.cumsum([0, 1536, 1536, MLA_Q_RANK, MLA_KV_RANK, MLA_ROPE, 1536, 512, 16])
    dnab = w_in[:, cuts[7]:cuts[8]]
    w16 = w_in.astype(BF16)
    na, hy, cq, ckv, kr, dnqkv, dngate = (w16[:, cuts[i]:cuts[i + 1]] for i in range(7))
    gates = w16[:, cuts[8]:]
    pad = jnp.zeros((w_in.shape[0], Z_GATE - Z_KR - 2 * MLA_ROPE), BF16)
    main = jnp.concatenate([na, hy, dnqkv, dngate, cq, ckv, kr, _rotate_half_cols(kr), pad, 0.5 * gates], axis=1)
    ab = dnab.reshape(-1, 2, 2, DN_HEADS)
    ab = jnp.transpose(ab, (0, 3, 1, 2)).reshape(-1, DN_HEADS, 4)
    ab = jnp.pad(ab, ((0, 0), (0, 0), (0, LANES - 4))).reshape(-1, DN_HEADS * LANES)
    return main, ab.astype(BF16)


def _prep_w_uq(w_uq):
    k = w_uq.shape[0]
    w = w_uq.reshape(k, MLA_HEADS, MLA_NOPE + MLA_ROPE)
    rope_w = w[:, :, MLA_NOPE:]
    return jnp.concatenate([w, _rotate_half_cols(rope_w)], axis=-1).reshape(k, MLA_HEADS * MLA_HW).astype(BF16)


def _head_lane_vec(p):
    v = jnp.zeros((DN_HEADS, 1, LANES), F32)
    v = v.at[:, 0, 0].set(p[0].astype(F32))
    return v.at[:, 0, 2].set(p[1].astype(F32))


def _rope_table(l):
    half = MLA_ROPE // 2
    inv = ROPE_THETA ** (-jnp.arange(half, dtype=F32) / half)
    ang = jnp.arange(l, dtype=F32)[:, None] * inv[None, :]
    cos, sin = jnp.cos(ang), jnp.sin(ang)
    return jnp.concatenate([cos, cos, sin, sin], axis=-1)


def _split_const(x):
    hi = x.astype(BF16)
    lo = (x - hi.astype(np.float64)).astype(BF16)
    return jnp.asarray(hi), jnp.asarray(lo)


def _pick(n, pref):
    for t in pref:
        if n % t == 0:
            return t
    return n


def trunk(x_parts, l, norm_mix, w_in, na_rpb, hy_short, hy_skip, hy_w1, hy_b1, hy_w2, hy_b2, hy_w3,
          mla_g_q, mla_g_kv, mla_w_uq, mla_w_ukv, dn_conv, dn_a_log, dn_dt_bias, dn_g_norm,
          w_branch, w_out, norm_mlp, w_up, w_down, norm_final):
    d = x_parts[0].shape[1]
    part_rows = tuple(p.shape[0] for p in x_parts)
    m = sum(part_rows)
    b = m // l
    depth = w_in.shape[0]
    common = math.gcd(*part_rows)
    tm_big = _pick(common, (1024, 512, 256, 128))
    tm_mid = _pick(common, (512, 256, 128))
    fa, ga, fb, fbi = _fft_tables(l)
    n1 = 2 * l // FFT_N2
    tables = (jnp.asarray(fa[:, :, :n1 // 2], BF16), jnp.asarray(ga[:, :n1 // 2, :], BF16),
              jnp.asarray(fb, BF16), jnp.asarray(fbi, BF16))
    fa_split, fb_split = _split_const(fa[:, :, :n1 // 2]), _split_const(fb)
    cs_tab = _rope_table(l)
    x = tuple(x_parts)
    for layer in range(depth):
        last = layer == depth - 1
        w_main, w_ab = _prep_w_in(w_in[layer])
        z = norm_matmul(x, norm_mix[layer], w_main, BF16, tm_big, 1024)
        ab = norm_matmul(x, norm_mix[layer], w_ab, F32, tm_big, DN_HEADS * LANES)
        z3 = z.reshape(b, l, Z_COLS)
        br_a = neighbourhood_attention(z3, na_bias_table(na_rpb[layer]), _pick(l // GRID_W, (8, 4, 2, 1)))
        taps = hyena_filters(l, hy_w1[layer], hy_b1[layer], hy_w2[layer], hy_b2[layer], hy_w3[layer])
        kf = filter_spectrum(taps, fa_split, fb_split)
        br_b = hyena_mixer(z3, hy_short[layer], hy_skip[layer], kf, tables)
        br_c = mla_mixer(z, b, l, mla_g_q[layer], mla_g_kv[layer], _prep_w_uq(mla_w_uq[layer]),
                         mla_w_ukv[layer].astype(BF16), cs_tab, tm_mid, _pick(l, (512, 256, 128)))
        br_d = deltanet_mixer(z3, ab.reshape(b, l, -1), dn_conv[layer], _head_lane_vec(dn_a_log[layer]),
                              _head_lane_vec(dn_dt_bias[layer]), dn_g_norm[layer])
        branches = [t.reshape(m, BRANCH_W) for t in (br_a, br_b, br_c, br_d)]
        merged = gated_merge(z, branches, (0.5 * w_branch[layer]).astype(BF16), tm_mid)
        x_mid = matmul_residual(merged, w_out[layer].astype(BF16), x, tm_mid)
        x = tuple(mlp_block(x_mid, norm_mlp[layer], w_up[layer].astype(BF16), w_down[layer].astype(BF16), norm_final,
                            last, tm_mid, 1024, part_rows if last else (m,)))
    return x


def kernel(x_prompt, x_sample, norm_mix, w_in, na_rpb, hy_short, hy_skip, hy_w1, hy_b1, hy_w2, hy_b2, hy_w3,
           mla_g_q, mla_g_kv, mla_w_uq, mla_w_ukv, dn_conv, dn_a_log, dn_dt_bias, dn_g_norm,
           w_branch, w_out, norm_mlp, w_up, w_down, norm_final):
    assert x_prompt.shape[1:] == x_sample.shape[1:]
    l, d = x_prompt.shape[1:]
    y_prompt, y_sample = trunk((x_prompt.reshape(-1, d), x_sample.reshape(-1, d)), l, norm_mix, w_in, na_rpb,
                               hy_short, hy_skip, hy_w1, hy_b1, hy_w2, hy_b2, hy_w3, mla_g_q, mla_g_kv, mla_w_uq,
                               mla_w_ukv, dn_conv, dn_a_log, dn_dt_bias, dn_g_norm, w_branch, w_out, norm_mlp,
                               w_up, w_down, norm_final)
    return (y_prompt.reshape(x_prompt.shape), y_sample.reshape(x_sample.shape))
```

```python
import functools
import math

import jax
import jax.numpy as jnp
import numpy as np
from jax import lax
from jax.experimental import pallas as pl
from jax.experimental.pallas import tpu as pltpu

F32 = jnp.float32
BF16 = jnp.bfloat16
HIGHEST = lax.Precision.HIGHEST

VMEM_LIMIT_BYTES = 56 * 1024 * 1024
LANES = 128

D_MODEL = 2048
RMS_EPS = 1e-6
GRID_W = 64
N_BRANCH = 4
BRANCH_W = 512
NA_HEADS = 8
NA_HEAD_DIM = 64
NA_WIN_R = 8
NA_WIN_C = 16
HY_WIDTH = 512
HY_POS_BANDS = 16
HY_FILT_HIDDEN = 64
HY_FAST_DECAY = 0.3
HY_SLOW_DECAY = 1.5
HY_DECAY_TARGET = 1e-2
MLA_HEADS = 4
MLA_Q_RANK = 512
MLA_KV_RANK = 256
MLA_NOPE = 128
MLA_ROPE = 64
MLA_V = 128
ROPE_THETA = 10000.0
DN_HEADS = 4
DN_DK = 128
DN_DV = 128
DN_CHUNK = 64
DN_GROUP = 16
D_FF = 4 * D_MODEL

Z_NA = 0
Z_HY = 1536
Z_DNQKV = 3072
Z_DNGATE = 4608
Z_CQ = 5120
Z_CKV = 5632
Z_KR = 5888
Z_GATE = 6144
Z_COLS = Z_GATE + N_BRANCH * D_MODEL

FFT_N2 = 128
FFT_UNROLL = 8
FFT_GROUP = 8
FFT_PITCH_X = FFT_N2 + 8
FFT_PITCH_Y = 2 * FFT_N2 + 8


def _params(*sem):
    return pltpu.CompilerParams(dimension_semantics=sem, vmem_limit_bytes=VMEM_LIMIT_BYTES)


def _silu(x):
    return 0.5 * x * (jnp.tanh(0.5 * x) + 1.0)


def _rms_bf16(x, g):
    xf = x.astype(F32)
    y = xf * lax.rsqrt(jnp.mean(xf * xf, axis=-1, keepdims=True) + RMS_EPS)
    return (y * g).astype(BF16)


def _part_tiles(parts, tm):
    assert all(p.shape[0] % tm == 0 for p in parts)
    return tuple(p.shape[0] // tm for p in parts)


def _part_specs(tiles, block):
    specs, start = [], 0
    for count in tiles:
        specs.append(pl.BlockSpec(block, functools.partial(
            lambda i, j, start, count: (jnp.clip(i - start, 0, count - 1), 0), start=start, count=count)))
        start += count
    return specs


def _for_owning_part(tiles, refs, fn):
    if len(refs) == 1:
        fn(refs[0])
        return
    i = pl.program_id(0)
    start = 0
    for ref, count in zip(refs, tiles):
        pl.when((i >= start) & (i < start + count))(functools.partial(fn, ref))
        start += count


def _norm_mm_kernel(*refs, tiles):
    x_refs = refs[:len(tiles)]
    g_ref, w_ref, o_ref, h_ref = refs[len(tiles):]

    @pl.when(pl.program_id(1) == 0)
    def _():
        def norm(x_ref):
            h_ref[...] = _rms_bf16(x_ref[...], g_ref[...])

        _for_owning_part(tiles, x_refs, norm)

    o_ref[...] = jnp.dot(h_ref[...], w_ref[...], preferred_element_type=F32).astype(o_ref.dtype)


def norm_matmul(x_parts, g, w, out_dtype, tm, tn):
    k, n = w.shape
    tiles = _part_tiles(x_parts, tm)
    return pl.pallas_call(
        functools.partial(_norm_mm_kernel, tiles=tiles),
        grid=(sum(tiles), n // tn),
        in_specs=_part_specs(tiles, (tm, k)) + [pl.BlockSpec((1, k), lambda i, j: (0, 0)),
                                                 pl.BlockSpec((k, tn), lambda i, j: (0, j))],
        out_specs=pl.BlockSpec((tm, tn), lambda i, j: (i, j)),
        out_shape=jax.ShapeDtypeStruct((sum(tiles) * tm, n), out_dtype),
        scratch_shapes=[pltpu.VMEM((tm, k), BF16)],
        compiler_params=_params("parallel", "arbitrary"),
        name="norm_matmul",
    )(*x_parts, g.reshape(1, k), w)


def _mm_res_kernel(*refs, tiles):
    a_ref, w_ref = refs[:2]
    r_refs = refs[2:2 + len(tiles)]
    o_ref = refs[-1]
    y = jnp.dot(a_ref[...], w_ref[...], preferred_element_type=F32)

    def add(r_ref):
        o_ref[...] = r_ref[...] + y

    _for_owning_part(tiles, r_refs, add)


def matmul_residual(a, w, r_parts, tm):
    m, k = a.shape
    n = w.shape[1]
    tiles = _part_tiles(r_parts, tm)
    return pl.pallas_call(
        functools.partial(_mm_res_kernel, tiles=tiles),
        grid=(m // tm, 1),
        in_specs=[pl.BlockSpec((tm, k), lambda i, j: (i, 0)),
                  pl.BlockSpec((k, n), lambda i, j: (0, 0))] + _part_specs(tiles, (tm, n)),
        out_specs=pl.BlockSpec((tm, n), lambda i, j: (i, 0)),
        out_shape=jax.ShapeDtypeStruct((m, n), F32),
        compiler_params=_params("parallel", "arbitrary"),
        name="matmul_residual",
    )(a, w, *r_parts)


def _mlp_kernel(x_ref, g_ref, wu_ref, wd_ref, gf_ref, *refs, final_norm, tiles):
    o_refs, h_ref = refs[:-1], refs[-1]
    j = pl.program_id(1)

    def run(o_ref):
        @pl.when(j == 0)
        def _():
            x = x_ref[...]
            h_ref[...] = _rms_bf16(x, g_ref[...])
            o_ref[...] = x

        u = jnp.dot(h_ref[...], wu_ref[...], preferred_element_type=F32)
        a = jnp.square(jnp.maximum(u, 0.0)).astype(BF16)
        o_ref[...] += jnp.dot(a, wd_ref[...], preferred_element_type=F32)

        if final_norm:
            @pl.when(j == pl.num_programs(1) - 1)
            def _():
                y = o_ref[...]
                o_ref[...] = y * lax.rsqrt(jnp.mean(y * y, axis=-1, keepdims=True) + RMS_EPS) * gf_ref[...]

    _for_owning_part(tiles, o_refs, run)


def mlp_block(x, g, w_up, w_down, g_final, final_norm, tm, tf, out_rows):
    m, d = x.shape
    f = w_up.shape[1]
    assert sum(out_rows) == m and all(r % tm == 0 for r in out_rows)
    tiles = tuple(r // tm for r in out_rows)
    return pl.pallas_call(
        functools.partial(_mlp_kernel, final_norm=final_norm, tiles=tiles),
        grid=(m // tm, f // tf),
        in_specs=[pl.BlockSpec((tm, d), lambda i, j: (i, 0)),
                  pl.BlockSpec((1, d), lambda i, j: (0, 0)),
                  pl.BlockSpec((d, tf), lambda i, j: (0, j)),
                  pl.BlockSpec((tf, d), lambda i, j: (j, 0)),
                  pl.BlockSpec((1, d), lambda i, j: (0, 0))],
        out_specs=_part_specs(tiles, (tm, d)),
        out_shape=[jax.ShapeDtypeStruct((r, d), F32) for r in out_rows],
        scratch_shapes=[pltpu.VMEM((tm, d), BF16)],
        compiler_params=_params("parallel" if len(tiles) == 1 else "arbitrary", "arbitrary"),
        name="mlp_block",
    )(x, g.reshape(1, d), w_up, w_down, g_final.reshape(1, d))


def _merge_kernel(gate_ref, ba_ref, bb_ref, bc_ref, bd_ref, wb_ref, o_ref, acc_ref):
    n = pl.program_id(1)

    def contribution(b_ref):
        y = jnp.dot(b_ref[...], wb_ref[0], preferred_element_type=F32)
        return (jnp.tanh(gate_ref[...].astype(F32)) + 1.0) * y

    @pl.when(n == 0)
    def _():
        acc_ref[...] = contribution(ba_ref)

    @pl.when(n == 1)
    def _():
        acc_ref[...] += contribution(bb_ref)

    @pl.when(n == 2)
    def _():
        acc_ref[...] += contribution(bc_ref)

    @pl.when(n == 3)
    def _():
        o_ref[...] = (acc_ref[...] + contribution(bd_ref)).astype(o_ref.dtype)


def gated_merge(z, branches, w_branch, tm):
    m = z.shape[0]
    gate_blk0 = Z_GATE // D_MODEL
    br_spec = pl.BlockSpec((tm, BRANCH_W), lambda i, n: (i, 0))
    return pl.pallas_call(
        _merge_kernel,
        grid=(m // tm, N_BRANCH),
        in_specs=[pl.BlockSpec((tm, D_MODEL), lambda i, n: (i, gate_blk0 + n)),
                  br_spec, br_spec, br_spec, br_spec,
                  pl.BlockSpec((1, BRANCH_W, D_MODEL), lambda i, n: (n, 0, 0))],
        out_specs=pl.BlockSpec((tm, D_MODEL), lambda i, n: (i, 0)),
        out_shape=jax.ShapeDtypeStruct((m, D_MODEL), BF16),
        scratch_shapes=[pltpu.VMEM((tm, D_MODEL), F32)],
        compiler_params=_params("parallel", "arbitrary"),
        name="gated_merge",
    )(z, *branches, w_branch)


NA_ROWS_LOCKSTEP = 4
NA_MASK = -1e30


def na_bias_table(rpb):
    q = np.arange(GRID_W)
    kc = np.arange(GRID_W)
    cs = np.clip(q - NA_WIN_C // 2, 0, GRID_W - NA_WIN_C)
    ok = (kc[None, :] >= cs[:, None]) & (kc[None, :] < cs[:, None] + NA_WIN_C)
    dcol = np.clip(kc[None, :] - q[:, None] + NA_WIN_C - 1, 0, 2 * NA_WIN_C - 2)
    drow = np.arange(NA_WIN_R)[None, :] - np.arange(NA_WIN_R)[:, None] + NA_WIN_R - 1
    t = rpb.astype(F32)[:, drow]
    t = t[:, :, :, dcol]
    t = jnp.transpose(t, (1, 0, 3, 2, 4))
    t = jnp.where(jnp.asarray(ok)[None, None, :, None, :], t, NA_MASK)
    return t.reshape(NA_WIN_R, NA_HEADS, GRID_W, NA_WIN_R * GRID_W)


def _na_kernel(q_ref, k_ref, v_ref, bias_ref, o_ref, *, rows_per_step, rows):
    rblk = pl.program_id(1)
    win = NA_WIN_R * GRID_W
    scale = NA_HEAD_DIM ** -0.5

    pair_w = 2 * NA_HEAD_DIM
    low = lax.broadcasted_iota(jnp.int32, (GRID_W, pair_w), 1) < NA_HEAD_DIM

    def rows_body(it, carry):
        inst = []
        q, kb, vb, off, qstart = [], [], [], [], []
        for t in range(NA_ROWS_LOCKSTEP):
            rl = it * NA_ROWS_LOCKSTEP + t
            r = rblk * rows_per_step + rl
            rs = jnp.clip(r - NA_WIN_R // 2, 0, rows - NA_WIN_R)
            off.append(r - rs)
            qstart.append(pl.multiple_of(rl * GRID_W, GRID_W))
            kstart = pl.multiple_of(rs * GRID_W, GRID_W)
            q.append(q_ref[0, pl.ds(qstart[t], GRID_W), :])
            kb.append(k_ref[0, pl.ds(kstart, win), :])
            vb.append(v_ref[0, pl.ds(kstart, win), :])
            inst += [(t, h) for h in range(NA_HEADS)]

        def pair(x, h):
            return x[:, (h // 2) * pair_w:(h // 2 + 1) * pair_w]

        qm = [jnp.where(low if h % 2 == 0 else ~low, pair(q[t], h), jnp.zeros((), BF16)) for t, h in inst]
        s = [lax.dot_general(qm[i], pair(kb[t], h), (((1,), (1,)), ((), ())), preferred_element_type=F32)
             for i, (t, h) in enumerate(inst)]
        s = [s[i] * scale + bias_ref[off[t], h] for i, (t, h) in enumerate(inst)]
        p = [jnp.exp(x - jnp.max(x, axis=-1, keepdims=True)) for x in s]
        l = [jnp.sum(x, axis=-1, keepdims=True) for x in p]
        o = [jnp.dot(p[i].astype(BF16), pair(vb[t], h), preferred_element_type=F32) / l[i]
             for i, (t, h) in enumerate(inst)]
        for t in range(NA_ROWS_LOCKSTEP):
            base = t * NA_HEADS
            outs = [jnp.where(low, o[base + h], o[base + h + 1]) for h in range(0, NA_HEADS, 2)]
            o_ref[0, pl.ds(qstart[t], GRID_W), :] = jnp.concatenate(outs, axis=1).astype(o_ref.dtype)
        return carry

    lax.fori_loop(0, rows_per_step // NA_ROWS_LOCKSTEP, rows_body, 0)


def neighbourhood_attention(z3, bias, rows_per_step):
    b, l, _ = z3.shape
    rows = l // GRID_W
    w = NA_HEADS * NA_HEAD_DIM
    blk = Z_NA // w
    return pl.pallas_call(
        functools.partial(_na_kernel, rows_per_step=rows_per_step, rows=rows),
        grid=(b, rows // rows_per_step),
        in_specs=[pl.BlockSpec((1, rows_per_step * GRID_W, w), lambda i, r: (i, r, blk)),
                  pl.BlockSpec((1, l, w), lambda i, r: (i, 0, blk + 1)),
                  pl.BlockSpec((1, l, w), lambda i, r: (i, 0, blk + 2)),
                  pl.BlockSpec(bias.shape, lambda i, r: (0, 0, 0, 0))],
        out_specs=pl.BlockSpec((1, rows_per_step * GRID_W, w), lambda i, r: (i, r, 0)),
        out_shape=jax.ShapeDtypeStruct((b, l, w), BF16),
        compiler_params=_params("parallel", "arbitrary"),
        name="neighbourhood_attention",
    )(z3, z3, z3, bias)


def _fft_tables(l):
    n = 2 * l
    n2 = FFT_N2
    n1 = n // n2
    a = np.arange(n1)
    b = np.arange(n2)
    k1 = np.arange(n1)
    t = n2 * a[None, :] + b[:, None]
    th = 2.0 * np.pi * (k1[None, :, None] * t[:, None, :] % n) / n
    fa = np.concatenate([np.cos(th), -np.sin(th)], axis=1)
    th_t = np.transpose(th, (0, 2, 1))
    ga = np.concatenate([np.cos(th_t), -np.sin(th_t)], axis=2)
    ph = 2.0 * np.pi * (np.outer(b, b) % n2) / n2
    cr, ci = np.cos(ph), -np.sin(ph)
    fb = np.block([[cr, -ci], [ci, cr]])
    fbi = np.block([[cr, ci], [-ci, cr]])
    return fa, ga, fb, fbi


def _filter_fft_kernel(hf_ref, hb_ref, fa_hi_ref, fa_lo_ref, fb_hi_ref, fb_lo_ref, o_ref, y_ref, *, n1):
    n2 = FFT_N2
    na = n1 // 2
    inv_n = 1.0 / (n1 * n2)

    def transform(src_ref, combine):
        def step_a(b, carry):
            xb = _split_bf16(src_ref[pl.ds(b, na, stride=n2), :])
            r = _dot_split((fa_hi_ref[b], fa_lo_ref[b]), xb)
            y_ref[pl.ds(b, n1, stride=2 * n2), :] = r[:n1]
            y_ref[pl.ds(b + n2, n1, stride=2 * n2), :] = r[n1:]
            return carry

        lax.fori_loop(0, n2, step_a, 0, unroll=2)

        def step_b(k1, carry):
            r0 = pl.multiple_of(k1 * 2 * n2, 2 * n2)
            x = _dot_split((fb_hi_ref[...], fb_lo_ref[...]), _split_bf16(y_ref[pl.ds(r0, 2 * n2), :]))
            combine(r0, x * inv_n)
            return carry

        lax.fori_loop(0, n1, step_b, 0, unroll=2)

    def store(r0, x):
        o_ref[pl.ds(r0, 2 * n2), :] = x

    def add_conjugate(r0, x):
        o_ref[pl.ds(r0, n2), :] += x[:n2]
        o_ref[pl.ds(r0 + n2, n2), :] -= x[n2:]

    transform(hf_ref, store)
    transform(hb_ref, add_conjugate)


def filter_spectrum(h, fa, fb):
    l = h.shape[0]
    c = h.shape[1] // 2
    n = 2 * l
    n1 = n // FFT_N2
    nch = c // LANES
    return pl.pallas_call(
        functools.partial(_filter_fft_kernel, n1=n1),
        grid=(nch,),
        in_specs=[pl.BlockSpec((l, LANES), lambda i: (0, i)),
                  pl.BlockSpec((l, LANES), lambda i: (0, nch + i)),
                  pl.BlockSpec(fa[0].shape, lambda i: (0, 0, 0)),
                  pl.BlockSpec(fa[1].shape, lambda i: (0, 0, 0)),
                  pl.BlockSpec(fb[0].shape, lambda i: (0, 0)),
                  pl.BlockSpec(fb[1].shape, lambda i: (0, 0))],
        out_specs=pl.BlockSpec((2 * n, LANES), lambda i: (0, i)),
        out_shape=jax.ShapeDtypeStruct((2 * n, c), F32),
        scratch_shapes=[pltpu.VMEM((2 * n, LANES), F32)],
        compiler_params=_params("parallel"),
        name="hyena_filter_spectrum",
    )(h, h, *fa, *fb)


def _hyena_filter_kernel(z_ref, w1_ref, b1_ref, w2_ref, b2_ref, w3_ref, dec_ref, o_ref):
    hid = jnp.sin(jnp.dot(z_ref[...], w1_ref[...], precision=HIGHEST, preferred_element_type=F32) + b1_ref[...])
    hid = jnp.sin(jnp.dot(hid, w2_ref[...], precision=HIGHEST, preferred_element_type=F32) + b2_ref[...])
    h = jnp.dot(hid, w3_ref[...], precision=HIGHEST, preferred_element_type=F32)
    dec = dec_ref[...]
    hf = h[:, :HY_WIDTH] * dec
    hb = h[:, HY_WIDTH:] * dec
    norm = (jnp.sum(jnp.abs(hf), axis=0, keepdims=True) + jnp.sum(jnp.abs(hb), axis=0, keepdims=True)) + RMS_EPS
    o_ref[:, :HY_WIDTH] = hf / norm
    o_ref[:, HY_WIDTH:] = hb / norm


def hyena_filters(l, w1, b1, w2, b2, w3):
    t = jnp.linspace(0.0, 1.0, l, dtype=F32)[:, None]
    w = 2.0 * math.pi * jnp.arange(l, dtype=F32)[:, None] / l
    bands = jnp.linspace(1e-4, HY_POS_BANDS - 1, HY_POS_BANDS, dtype=F32)[None, :]
    z = jnp.concatenate([t, jnp.cos(bands * w), -jnp.sin(bands * w)], axis=-1)
    pad = LANES - z.shape[1]
    z = jnp.pad(z, ((0, 0), (0, pad)))
    w1p = jnp.pad(w1.astype(F32), ((0, pad), (0, 0)))
    max_decay = math.log(HY_DECAY_TARGET) / HY_FAST_DECAY
    min_decay = math.log(HY_DECAY_TARGET) / HY_SLOW_DECAY
    deltas = jnp.abs(jnp.linspace(min_decay, max_decay, HY_WIDTH, dtype=F32))
    dec = jnp.exp(-t * deltas[None, :])
    return pl.pallas_call(
        _hyena_filter_kernel,
        out_shape=jax.ShapeDtypeStruct((l, 2 * HY_WIDTH), F32),
        compiler_params=pltpu.CompilerParams(vmem_limit_bytes=VMEM_LIMIT_BYTES),
        name="hyena_filter_ffn",
    )(z, w1p, b1.reshape(1, -1).astype(F32), w2.astype(F32), b2.reshape(1, -1).astype(F32), w3.astype(F32), dec)


CONV_PAD = 8


def _dwconv3(x, w_ref, pad_ref):
    n = x.shape[0]
    zeros = jnp.zeros((CONV_PAD, x.shape[1]), F32)
    pad_ref[0:CONV_PAD, :] = zeros
    pad_ref[CONV_PAD + n:2 * CONV_PAD + n, :] = zeros
    pad_ref[CONV_PAD:CONV_PAD + n, :] = x
    w = w_ref[...].astype(F32)
    return (pad_ref[CONV_PAD - 1:CONV_PAD - 1 + n, :] * w[0:1] + x * w[1:2]
            + pad_ref[CONV_PAD + 1:CONV_PAD + 1 + n, :] * w[2:3])


def _hyena_kernel(x1_ref, x2_ref, v_ref, w1_ref, w2_ref, wv_ref, skip_ref, kf_ref, fa_ref, ga_ref, fb_ref, fbi_ref,
                  o_ref, xs_ref, y_ref, pad_ref, *, n1):
    n2 = FFT_N2
    na = n1 // 2
    pa, py = FFT_PITCH_X, FFT_PITCH_Y
    vg = _dwconv3(v_ref[0].astype(F32), wv_ref, pad_ref) * _dwconv3(x1_ref[0].astype(F32), w1_ref, pad_ref)
    for a in range(na):
        xs_ref[a * pa:a * pa + n2, :] = vg[a * n2:(a + 1) * n2]

    def fwd_a(b, carry):
        xb = xs_ref[pl.ds(b, na, stride=pa), :].astype(BF16)
        r = jnp.dot(fa_ref[b], xb, preferred_element_type=F32)
        y_ref[pl.ds(b, n1, stride=py), :] = r[:n1]
        y_ref[pl.ds(b + n2, n1, stride=py), :] = r[n1:]
        return carry

    lax.fori_loop(0, n2, fwd_a, 0, unroll=FFT_UNROLL)

    def mid(kg, carry):
        grp = range(FFT_GROUP)
        k1 = [kg * FFT_GROUP + t for t in grp]
        r0 = [pl.multiple_of(k * py, 8) for k in k1]
        f0 = [pl.multiple_of(k * 2 * n2, 2 * n2) for k in k1]
        fb = fb_ref[...]
        x = [jnp.dot(fb, y_ref[pl.ds(r0[t], 2 * n2), :].astype(BF16), preferred_element_type=F32) for t in grp]
        p = []
        for t in grp:
            xr, xi = x[t][:n2], x[t][n2:]
            kr = kf_ref[pl.ds(f0[t], n2), :]
            ki = kf_ref[pl.ds(f0[t] + n2, n2), :]
            p.append(jnp.concatenate([xr * kr - xi * ki, xr * ki + xi * kr], axis=0).astype(BF16))
        fbi = fbi_ref[...]
        q = [jnp.dot(fbi, p[t], preferred_element_type=F32) for t in grp]
        for t in grp:
            y_ref[pl.ds(r0[t], 2 * n2), :] = q[t]
        return carry

    lax.fori_loop(0, n1 // FFT_GROUP, mid, 0)

    def inv_a(b, carry):
        qb = jnp.concatenate([y_ref[pl.ds(b, n1, stride=py), :],
                              y_ref[pl.ds(b + n2, n1, stride=py), :]], axis=0).astype(BF16)
        xs_ref[pl.ds(b, na, stride=pa), :] = jnp.dot(ga_ref[b], qb, preferred_element_type=F32)
        return carry

    lax.fori_loop(0, n2, inv_a, 0, unroll=FFT_UNROLL)

    gate = _dwconv3(x2_ref[0].astype(F32), w2_ref, pad_ref)
    skip = skip_ref[...].astype(F32)
    for a in range(na):
        rows = slice(a * n2, (a + 1) * n2)
        o_ref[0, rows, :] = ((xs_ref[a * pa:a * pa + n2, :] + vg[rows] * skip) * gate[rows]).astype(o_ref.dtype)


def hyena_mixer(z3, w_short, skip, kf, tables):
    b, l, _ = z3.shape
    n1 = 2 * l // FFT_N2
    fa, ga, fb, fbi = tables
    nch = HY_WIDTH // LANES
    blk = Z_HY // LANES

    def zspec(seg):
        return pl.BlockSpec((1, l, LANES), lambda c, i: (i, 0, blk + seg * nch + c))

    def wspec(seg):
        return pl.BlockSpec((3, LANES), lambda c, i: (0, seg * nch + c))

    return pl.pallas_call(
        functools.partial(_hyena_kernel, n1=n1),
        grid=(nch, b),
        in_specs=[zspec(0), zspec(1), zspec(2), wspec(0), wspec(1), wspec(2),
                  pl.BlockSpec((1, LANES), lambda c, i: (0, c)),
                  pl.BlockSpec((4 * l, LANES), lambda c, i: (0, c)),
                  pl.BlockSpec(fa.shape, lambda c, i: (0, 0, 0)),
                  pl.BlockSpec(ga.shape, lambda c, i: (0, 0, 0)),
                  pl.BlockSpec(fb.shape, lambda c, i: (0, 0)),
                  pl.BlockSpec(fbi.shape, lambda c, i: (0, 0))],
        out_specs=pl.BlockSpec((1, l, LANES), lambda c, i: (i, 0, c)),
        out_shape=jax.ShapeDtypeStruct((b, l, HY_WIDTH), BF16),
        scratch_shapes=[pltpu.VMEM((n1 // 2 * FFT_PITCH_X, LANES), F32), pltpu.VMEM((n1 * FFT_PITCH_Y, LANES), F32),
                        pltpu.VMEM((l + 2 * CONV_PAD, LANES), F32)],
        compiler_params=_params("parallel", "arbitrary"),
        name="hyena_mixer",
    )(z3, z3, z3, w_short, w_short, w_short, skip.reshape(1, -1), kf, fa, ga, fb, fbi)


MLA_HW = 2 * LANES
MLA_HEADS_PER_STEP = 2


def _rope_group(g, cs):
    prod = g * cs
    s = prod + pltpu.roll(prod, MLA_ROPE, 1)
    lane = lax.broadcasted_iota(jnp.int32, s.shape, 1)
    return jnp.where(lane < MLA_ROPE, s, 0.0)


def _mla_prep_kernel(cq_ref, ckv_ref, kr_ref, gq_ref, gkv_ref, wq_ref, wkv_ref, cs_ref, q_ref, k_ref, v_ref):
    cs = cs_ref[...]
    hq = _rms_bf16(cq_ref[...], gq_ref[...])
    q = jnp.dot(hq, wq_ref[...], preferred_element_type=F32)
    hkv = _rms_bf16(ckv_ref[...], gkv_ref[...])
    kv = jnp.dot(hkv, wkv_ref[...], preferred_element_type=F32)
    k_rope = _rope_group(kr_ref[...].astype(F32), cs).astype(BF16)
    lane = lax.broadcasted_iota(jnp.int32, (cs.shape[0], LANES), 1)
    ones_lane = jnp.where(lane == 0, 1.0, 0.0).astype(BF16)
    for h in range(MLA_HEADS):
        o = h * MLA_HW
        q_ref[:, o:o + LANES] = q[:, o:o + LANES].astype(BF16)
        q_ref[:, o + LANES:o + MLA_HW] = _rope_group(q[:, o + LANES:o + MLA_HW], cs).astype(BF16)
        k_ref[:, o:o + LANES] = kv[:, o:o + LANES].astype(BF16)
        k_ref[:, o + LANES:o + MLA_HW] = k_rope
        v_ref[:, o:o + LANES] = kv[:, o + LANES:o + MLA_HW].astype(BF16)
        v_ref[:, o + LANES:o + MLA_HW] = ones_lane


def _mla_attn_kernel(q_ref, k_ref, v_ref, o_ref):
    c = (MLA_NOPE + MLA_ROPE) ** -0.5 * math.log2(math.e)
    heads = range(MLA_HEADS_PER_STEP)
    s = [lax.dot_general(q_ref[0, :, h * MLA_HW:(h + 1) * MLA_HW], k_ref[0, :, h * MLA_HW:(h + 1) * MLA_HW],
                         (((1,), (1,)), ((), ())), preferred_element_type=F32) for h in heads]
    for h in heads:
        p = jnp.exp2((s[h] - jnp.max(s[h], axis=-1, keepdims=True)) * c)
        o = jnp.dot(p.astype(BF16), v_ref[0, :, h * MLA_HW:(h + 1) * MLA_HW], preferred_element_type=F32)
        o_ref[0, :, h * MLA_V:(h + 1) * MLA_V] = (o[:, :MLA_V] / o[:, MLA_V:MLA_V + 1]).astype(o_ref.dtype)


def mla_mixer(z, b, l, g_q, g_kv, wq_p, w_ukv, cs_tab, tm, tq):
    m = z.shape[0]
    lt = l // tm
    qp, kp, vp = pl.pallas_call(
        _mla_prep_kernel,
        grid=(m // tm,),
        in_specs=[pl.BlockSpec((tm, MLA_Q_RANK), lambda i: (i, Z_CQ // MLA_Q_RANK)),
                  pl.BlockSpec((tm, MLA_KV_RANK), lambda i: (i, Z_CKV // MLA_KV_RANK)),
                  pl.BlockSpec((tm, LANES), lambda i: (i, Z_KR // LANES)),
                  pl.BlockSpec((1, MLA_Q_RANK), lambda i: (0, 0)),
                  pl.BlockSpec((1, MLA_KV_RANK), lambda i: (0, 0)),
                  pl.BlockSpec(wq_p.shape, lambda i: (0, 0)),
                  pl.BlockSpec(w_ukv.shape, lambda i: (0, 0)),
                  pl.BlockSpec((tm, LANES), lambda i: (i % lt, 0))],
        out_specs=[pl.BlockSpec((tm, MLA_HEADS * MLA_HW), lambda i: (i, 0)),
                   pl.BlockSpec((tm, MLA_HEADS * MLA_HW), lambda i: (i, 0)),
                   pl.BlockSpec((tm, MLA_HEADS * MLA_HW), lambda i: (i, 0))],
        out_shape=[jax.ShapeDtypeStruct((m, MLA_HEADS * MLA_HW), BF16),
                   jax.ShapeDtypeStruct((m, MLA_HEADS * MLA_HW), BF16),
                   jax.ShapeDtypeStruct((m, MLA_HEADS * MLA_HW), BF16)],
        compiler_params=_params("parallel"),
        name="mla_prep",
    )(z, z, z, g_q.reshape(1, -1), g_kv.reshape(1, -1), wq_p, w_ukv, cs_tab)
    qp = qp.reshape(b, l, -1)
    kp = kp.reshape(b, l, -1)
    vp = vp.reshape(b, l, -1)
    hs = MLA_HEADS_PER_STEP
    return pl.pallas_call(
        _mla_attn_kernel,
        grid=(b, MLA_HEADS // hs, l // tq),
        in_specs=[pl.BlockSpec((1, tq, hs * MLA_HW), lambda i, h, t: (i, t, h)),
                  pl.BlockSpec((1, l, hs * MLA_HW), lambda i, h, t: (i, 0, h)),
                  pl.BlockSpec((1, l, hs * MLA_HW), lambda i, h, t: (i, 0, h))],
        out_specs=pl.BlockSpec((1, tq, hs * MLA_V), lambda i, h, t: (i, t, h)),
        out_shape=jax.ShapeDtypeStruct((b, l, MLA_HEADS * MLA_V), BF16),
        compiler_params=_params("parallel", "parallel", "arbitrary"),
        name="mla_attention",
    )(qp, kp, vp)


def _split_bf16(x):
    hi = x.astype(BF16)
    return hi, (x - hi.astype(F32)).astype(BF16)


def _dot_split(a, b):
    (ah, al), (bh, bl) = a, b
    return (jnp.dot(ah, bh, preferred_element_type=F32)
            + (jnp.dot(ah, bl, preferred_element_type=F32) + jnp.dot(al, bh, preferred_element_type=F32)))


def _tri_unit_inverse(mats):
    c = mats[0].shape[0]
    eye = (lax.broadcasted_iota(jnp.int32, (c, c), 0) == lax.broadcasted_iota(jnp.int32, (c, c), 1)).astype(F32)
    ps = [eye - a for a in mats]
    ms = list(mats)
    for level in range(int(math.log2(c)) - 1):
        if level == 0:
            splits = [_split_bf16(m) for m in ms]
            ms = [_dot_split(sp, sp) for sp in splits]
            yield
            ps = [_dot_split(_split_bf16(p), _split_bf16(eye + m)) for p, m in zip(ps, ms)]
        else:
            hs = [m.astype(BF16) for m in ms]
            ms = [jnp.dot(h, h, preferred_element_type=F32) for h in hs]
            yield
            fs = [(eye + m).astype(BF16) for m in ms]
            ps = [jnp.dot(ph, f, preferred_element_type=F32) + jnp.dot(pl_, f, preferred_element_type=F32)
                  for (ph, pl_), f in zip([_split_bf16(p) for p in ps], fs)]
        yield
    return ps


def _interleave(main, side=None):
    gens = [g for g in (main, side) if g is not None]
    results = [None] * len(gens)
    live = [True] * len(gens)
    while any(live):
        for idx, gen in enumerate(gens):
            if live[idx]:
                try:
                    next(gen)
                except StopIteration as stop:
                    results[idx] = stop.value
                    live[idx] = False
    return results


def _bdot(a, b):
    return jnp.dot(a.astype(BF16), b.astype(BF16), preferred_element_type=F32)


def _bdot_nt(a, b):
    return lax.dot_general(a.astype(BF16), b.astype(BF16), (((1,), (1,)), ((), ())), preferred_element_type=F32)


def _dn_group_setup(chunks, refs):
    q_s, k_s, v_s, g_s, b_s = refs
    cc = DN_CHUNK
    n = range(len(chunks))
    rev = [r for _, r in chunks]
    r0 = [pl.multiple_of(c * cc, cc) for c, _ in chunks]
    q = [q_s[pl.ds(r, cc), :] for r in r0]
    k = [k_s[pl.ds(r, cc), :] for r in r0]
    v = [v_s[pl.ds(r, cc), :] for r in r0]
    lane = [2 if r else 0 for r in rev]
    ri = lax.broadcasted_iota(jnp.int32, (cc, cc), 0)
    ci = lax.broadcasted_iota(jnp.int32, (cc, cc), 1)
    incl = [(ri <= ci) if r else (ri >= ci) for r in rev]
    strict = [(ri < ci) if r else (ri > ci) for r in rev]
    g = [g_s[pl.ds(r, cc), :] for r in r0]
    g1 = [x.astype(BF16) for x in g]
    r1 = [x - h.astype(F32) for x, h in zip(g, g1)]
    g2 = [x.astype(BF16) for x in r1]
    g3 = [(x - h.astype(F32)).astype(BF16) for x, h in zip(r1, g2)]
    ones = [m.astype(BF16) for m in incl]
    gc = [jnp.dot(ones[i], g1[i], preferred_element_type=F32) for i in n]
    gc = [gc[i] + jnp.dot(ones[i], g2[i], preferred_element_type=F32) for i in n]
    gc = [gc[i] + jnp.dot(ones[i], g3[i], preferred_element_type=F32) for i in n]
    yield
    gcol = [gc[i][:, lane[i]:lane[i] + 1] for i in n]
    grow = [jnp.transpose(gc[i])[lane[i]:lane[i] + 1, :] for i in n]
    beta = [b_s[pl.ds(r0[i], cc), :][:, lane[i] + 1:lane[i] + 2] for i in n]
    g_last = [gcol[i][0:1, :] if rev[i] else gcol[i][cc - 1:cc, :] for i in n]
    decay = [jnp.where(incl[i], jnp.exp(jnp.where(incl[i], gcol[i] - grow[i], 0.0)), 0.0) for i in n]
    e_g = [jnp.exp(x) for x in gcol]
    kb = [k[i] * beta[i] for i in n]
    a_both = [_bdot_nt(jnp.concatenate([kb[i], q[i]], axis=0), k[i]) for i in n]
    yield
    a_kk = [jnp.where(strict[i], a_both[i][:cc] * decay[i], 0.0) for i in n]
    a_qk = [jnp.where(incl[i], a_both[i][cc:] * decay[i], 0.0) for i in n]
    t_inv = yield from _tri_unit_inverse(a_kk)
    uw = [_bdot(t_inv[i], jnp.concatenate([v[i] * beta[i], kb[i] * e_g[i]], axis=1)) for i in n]
    yield
    k_dec = [k[i] * jnp.exp(g_last[i] - gcol[i]) for i in n]
    wq = [jnp.concatenate([uw[i][:, DN_DV:], q[i] * e_g[i]], axis=0).astype(BF16) for i in n]
    ak = [jnp.concatenate([a_qk[i], jnp.transpose(k_dec[i])], axis=0).astype(BF16) for i in n]
    return [(wq[i], ak[i], uw[i][:, :DN_DV], jnp.exp(g_last[i])) for i in n]


def _dn_group_steps(states, setups, out_refs, chunks):
    cc = DN_CHUNK
    dirs = range(len(states))
    for t in range(len(setups[0])):
        st = [setups[d][t] for d in dirs]
        r = [jnp.dot(st[d][0], states[d].astype(BF16), preferred_element_type=F32) for d in dirs]
        yield
        v_new = [st[d][2] - r[d][:cc] for d in dirs]
        r2 = [jnp.dot(st[d][1], v_new[d].astype(BF16), preferred_element_type=F32) for d in dirs]
        yield
        states = [states[d] * st[d][3] + r2[d][cc:] for d in dirs]
        for d in dirs:
            out_refs[d][pl.ds(pl.multiple_of(chunks[d][t] * cc, cc), cc), :] = r[d][cc:] + r2[d][:cc]
    return states


def _dn_kernel(zq_ref, zk_ref, zv_ref, zg_ref, ab_ref, wq_ref, wk_ref, wv_ref, alog_ref, dtb_ref, gn_ref, o_ref,
               q_s, k_s, v_s, g_s, b_s, of_s, ob_s, pad_ref, *, nchunks):
    def act(z_ref, w_ref):
        c = _dwconv3(z_ref[0].astype(F32), w_ref, pad_ref)
        return _silu(c)

    def l2n(x):
        return x * lax.rsqrt(jnp.sum(x * x, axis=-1, keepdims=True) + RMS_EPS)

    q_s[...] = l2n(act(zq_ref, wq_ref)) * (DN_DK ** -0.5)
    k_s[...] = l2n(act(zk_ref, wk_ref))
    v_s[...] = act(zv_ref, wv_ref)
    ab = ab_ref[0]
    x = ab + dtb_ref[0]
    softplus = jnp.maximum(x, 0.0) + jnp.log1p(jnp.exp(-jnp.abs(x)))
    g_s[...] = -jnp.exp(alog_ref[0]) * softplus
    b_s[...] = jax.nn.sigmoid(ab)
    refs = (q_s, k_s, v_s, g_s, b_s)

    ngroups = nchunks // DN_GROUP

    def group_chunks(gi):
        fwd = [gi * DN_GROUP + t for t in range(DN_GROUP)]
        return fwd, [nchunks - 1 - c for c in fwd]

    def setup_gen(gi):
        fwd, bwd = group_chunks(gi)
        return _dn_group_setup([(c, False) for c in fwd] + [(c, True) for c in bwd], refs)

    def steps_gen(gi, states, flat):
        setups = [tuple(flat[4 * i:4 * i + 4]) for i in range(2 * DN_GROUP)]
        return _dn_group_steps(states, [setups[:DN_GROUP], setups[DN_GROUP:]], (of_s, ob_s), group_chunks(gi))

    def flatten(setups):
        return [x for st in setups for x in st]

    def body(gi, carry):
        nxt, states = _interleave(setup_gen(gi + 1), steps_gen(gi, list(carry[:2]), carry[2:]))
        return (*states, *flatten(nxt))

    zero = jnp.zeros((DN_DK, DN_DV), F32)
    (first,) = _interleave(setup_gen(0))
    carry = lax.fori_loop(0, ngroups - 1, body, (zero, zero, *flatten(first)))
    _interleave(steps_gen(ngroups - 1, list(carry[:2]), carry[2:]))

    o = of_s[...] + ob_s[...]
    o = o * lax.rsqrt(jnp.mean(o * o, axis=-1, keepdims=True) + RMS_EPS) * gn_ref[...]
    gate = zg_ref[0].astype(F32)
    o_ref[0] = (o * _silu(gate)).astype(o_ref.dtype)


def deltanet_mixer(z3, ab3, w_conv, alog_p, dtb_p, g_norm):
    b, l, _ = z3.shape
    blk = Z_DNQKV // LANES
    gblk = Z_DNGATE // LANES
    hh = DN_HEADS

    def zspec(off):
        return pl.BlockSpec((1, l, LANES), lambda i, h: (i, 0, off + h))

    def wspec(seg):
        return pl.BlockSpec((3, LANES), lambda i, h: (0, seg * hh + h))

    vec = pl.BlockSpec((1, 1, LANES), lambda i, h: (h, 0, 0))
    seq = pltpu.VMEM((l, LANES), F32)
    return pl.pallas_call(
        functools.partial(_dn_kernel, nchunks=l // DN_CHUNK),
        grid=(b, hh),
        in_specs=[zspec(blk), zspec(blk + hh), zspec(blk + 2 * hh), zspec(gblk),
                  pl.BlockSpec((1, l, LANES), lambda i, h: (i, 0, h)),
                  wspec(0), wspec(1), wspec(2), vec, vec,
                  pl.BlockSpec((1, LANES), lambda i, h: (0, 0))],
        out_specs=pl.BlockSpec((1, l, LANES), lambda i, h: (i, 0, h)),
        out_shape=jax.ShapeDtypeStruct((b, l, hh * DN_DV), BF16),
        scratch_shapes=[seq] * 7 + [pltpu.VMEM((l + 2 * CONV_PAD, LANES), F32)],
        compiler_params=_params("parallel", "arbitrary"),
        name="deltanet_mixer",
    )(z3, z3, z3, z3, ab3, w_conv, w_conv, w_conv, alog_p, dtb_p, g_norm.reshape(1, -1))


def _rotate_half_cols(w):
    half = w.shape[-1] // 2
    return jnp.concatenate([-w[..., half:], w[..., :half]], axis=-1)


def _prep_w_in(w_in):
    cuts = np.cumsum([0, 1536, 1536, MLA_Q_RANK, MLA_KV_RANK, MLA_ROPE, 1536, 512, 16])
    dnab = w_in[:, cuts[7]:cuts[8]]
    w16 = w_in.astype(BF16)
    na, hy, cq, ckv, kr, dnqkv, dngate = (w16[:, cuts[i]:cuts[i + 1]] for i in range(7))
    gates = w16[:, cuts[8]:]
    pad = jnp.zeros((w_in.shape[0], Z_GATE - Z_KR - 2 * MLA_ROPE), BF16)
    main = jnp.concatenate([na, hy, dnqkv, dngate, cq, ckv, kr, _rotate_half_cols(kr), pad, 0.5 * gates], axis=1)
    ab = dnab.reshape(-1, 2, 2, DN_HEADS)
    ab = jnp.transpose(ab, (0, 3, 1, 2)).reshape(-1, DN_HEADS, 4)
    ab = jnp.pad(ab, ((0, 0), (0, 0), (0, LANES - 4))).reshape(-1, DN_HEADS * LANES)
    return main, ab.astype(BF16)


def _prep_w_uq(w_uq):
    k = w_uq.shape[0]
    w = w_uq.reshape(k, MLA_HEADS, MLA_NOPE + MLA_ROPE)
    rope_w = w[:, :, MLA_NOPE:]
    return jnp.concatenate([w, _rotate_half_cols(rope_w)], axis=-1).reshape(k, MLA_HEADS * MLA_HW).astype(BF16)


def _head_lane_vec(p):
    v = jnp.zeros((DN_HEADS, 1, LANES), F32)
    v = v.at[:, 0, 0].set(p[0].astype(F32))
    return v.at[:, 0, 2].set(p[1].astype(F32))


def _rope_table(l):
    half = MLA_ROPE // 2
    inv = ROPE_THETA ** (-jnp.arange(half, dtype=F32) / half)
    ang = jnp.arange(l, dtype=F32)[:, None] * inv[None, :]
    cos, sin = jnp.cos(ang), jnp.sin(ang)
    return jnp.concatenate([cos, cos, sin, sin], axis=-1)


def _split_const(x):
    hi = x.astype(BF16)
    lo = (x - hi.astype(np.float64)).astype(BF16)
    return jnp.asarray(hi), jnp.asarray(lo)


def _pick(n, pref):
    for t in pref:
        if n % t == 0:
            return t
    return n


def trunk(x_parts, l, norm_mix, w_in, na_rpb, hy_short, hy_skip, hy_w1, hy_b1, hy_w2, hy_b2, hy_w3,
          mla_g_q, mla_g_kv, mla_w_uq, mla_w_ukv, dn_conv, dn_a_log, dn_dt_bias, dn_g_norm,
          w_branch, w_out, norm_mlp, w_up, w_down, norm_final):
    d = x_parts[0].shape[1]
    part_rows = tuple(p.shape[0] for p in x_parts)
    m = sum(part_rows)
    b = m // l
    depth = w_in.shape[0]
    common = math.gcd(*part_rows)
    tm_big = _pick(common, (1024, 512, 256, 128))
    tm_mid = _pick(common, (512, 256, 128))
    fa, ga, fb, fbi = _fft_tables(l)
    n1 = 2 * l // FFT_N2
    tables = (jnp.asarray(fa[:, :, :n1 // 2], BF16), jnp.asarray(ga[:, :n1 // 2, :], BF16),
              jnp.asarray(fb, BF16), jnp.asarray(fbi, BF16))
    fa_split, fb_split = _split_const(fa[:, :, :n1 // 2]), _split_const(fb)
    cs_tab = _rope_table(l)
    x = tuple(x_parts)
    for layer in range(depth):
        last = layer == depth - 1
        w_main, w_ab = _prep_w_in(w_in[layer])
        z = norm_matmul(x, norm_mix[layer], w_main, BF16, tm_big, 1024)
        ab = norm_matmul(x, norm_mix[layer], w_ab, F32, tm_big, DN_HEADS * LANES)
        z3 = z.reshape(b, l, Z_COLS)
        br_a = neighbourhood_attention(z3, na_bias_table(na_rpb[layer]), _pick(l // GRID_W, (8, 4, 2, 1)))
        taps = hyena_filters(l, hy_w1[layer], hy_b1[layer], hy_w2[layer], hy_b2[layer], hy_w3[layer])
        kf = filter_spectrum(taps, fa_split, fb_split)
        br_b = hyena_mixer(z3, hy_short[layer], hy_skip[layer], kf, tables)
        br_c = mla_mixer(z, b, l, mla_g_q[layer], mla_g_kv[layer], _prep_w_uq(mla_w_uq[layer]),
                         mla_w_ukv[layer].astype(BF16), cs_tab, tm_mid, _pick(l, (512, 256, 128)))
        br_d = deltanet_mixer(z3, ab.reshape(b, l, -1), dn_conv[layer], _head_lane_vec(dn_a_log[layer]),
                              _head_lane_vec(dn_dt_bias[layer]), dn_g_norm[layer])
        branches = [t.reshape(m, BRANCH_W) for t in (br_a, br_b, br_c, br_d)]
        merged = gated_merge(z, branches, (0.5 * w_branch[layer]).astype(BF16), tm_mid)
        x_mid = matmul_residual(merged, w_out[layer].astype(BF16), x, tm_mid)
        x = tuple(mlp_block(x_mid, norm_mlp[layer], w_up[layer].astype(BF16), w_down[layer].astype(BF16), norm_final,
                            last, tm_mid, 1024, part_rows if last else (m,)))
    return x


def kernel(x_prompt, x_sample, norm_mix, w_in, na_rpb, hy_short, hy_skip, hy_w1, hy_b1, hy_w2, hy_b2, hy_w3,
           mla_g_q, mla_g_kv, mla_w_uq, mla_w_ukv, dn_conv, dn_a_log, dn_dt_bias, dn_g_norm,
           w_branch, w_out, norm_mlp, w_up, w_down, norm_final):
    assert x_prompt.shape[1:] == x_sample.shape[1:]
    l, d = x_prompt.shape[1:]
    y_prompt, y_sample = trunk((x_prompt.reshape(-1, d), x_sample.reshape(-1, d)), l, norm_mix, w_in, na_rpb,
                               hy_short, hy_skip, hy_w1, hy_b1, hy_w2, hy_b2, hy_w3, mla_g_q, mla_g_kv, mla_w_uq,
                               mla_w_ukv, dn_conv, dn_a_log, dn_dt_bias, dn_g_norm, w_branch, w_out, norm_mlp,
                               w_up, w_down, norm_final)
    return (y_prompt.reshape(x_prompt.shape), y_sample.reshape(x_sample.shape))
```

```python
import functools
import math

import jax
import jax.numpy as jnp
import numpy as np
from jax import lax
from jax.experimental import pallas as pl
from jax.experimental.pallas import tpu as pltpu

F32 = jnp.float32
BF16 = jnp.bfloat16
HIGHEST = lax.Precision.HIGHEST

VMEM_LIMIT_BYTES = 56 * 1024 * 1024
LANES = 128

D_MODEL = 2048
RMS_EPS = 1e-6
GRID_W = 64
N_BRANCH = 4
BRANCH_W = 512
NA_HEADS = 8
NA_HEAD_DIM = 64
NA_WIN_R = 8
NA_WIN_C = 16
HY_WIDTH = 512
HY_POS_BANDS = 16
HY_FILT_HIDDEN = 64
HY_FAST_DECAY = 0.3
HY_SLOW_DECAY = 1.5
HY_DECAY_TARGET = 1e-2
MLA_HEADS = 4
MLA_Q_RANK = 512
MLA_KV_RANK = 256
MLA_NOPE = 128
MLA_ROPE = 64
MLA_V = 128
ROPE_THETA = 10000.0
DN_HEADS = 4
DN_DK = 128
DN_DV = 128
DN_CHUNK = 64
DN_GROUP = 8
D_FF = 4 * D_MODEL

Z_NA = 0
Z_HY = 1536
Z_DNQKV = 3072
Z_DNGATE = 4608
Z_CQ = 5120
Z_CKV = 5632
Z_KR = 5888
Z_GATE = 6144
Z_COLS = Z_GATE + N_BRANCH * D_MODEL

FFT_N2 = 128
FFT_UNROLL = 8
FFT_GROUP = 16
FFT_PITCH_X = FFT_N2 + 8
FFT_PITCH_Y = 2 * FFT_N2 + 8


def _params(*sem):
    return pltpu.CompilerParams(dimension_semantics=sem, vmem_limit_bytes=VMEM_LIMIT_BYTES)


def _silu(x):
    return 0.5 * x * (jnp.tanh(0.5 * x) + 1.0)


def _rms_bf16(x, g):
    xf = x.astype(F32)
    y = xf * lax.rsqrt(jnp.mean(xf * xf, axis=-1, keepdims=True) + RMS_EPS)
    return (y * g).astype(BF16)


def _part_tiles(parts, tm):
    assert all(p.shape[0] % tm == 0 for p in parts)
    return tuple(p.shape[0] // tm for p in parts)


def _part_specs(tiles, block):
    specs, start = [], 0
    for count in tiles:
        specs.append(pl.BlockSpec(block, functools.partial(
            lambda i, j, start, count: (jnp.clip(i - start, 0, count - 1), 0), start=start, count=count)))
        start += count
    return specs


def _for_owning_part(tiles, refs, fn):
    if len(refs) == 1:
        fn(refs[0])
        return
    i = pl.program_id(0)
    start = 0
    for ref, count in zip(refs, tiles):
        pl.when((i >= start) & (i < start + count))(functools.partial(fn, ref))
        start += count


def _norm_mm_kernel(*refs, tiles):
    x_refs = refs[:len(tiles)]
    g_ref, w_ref, o_ref, h_ref = refs[len(tiles):]

    @pl.when(pl.program_id(1) == 0)
    def _():
        def norm(x_ref):
            h_ref[...] = _rms_bf16(x_ref[...], g_ref[...])

        _for_owning_part(tiles, x_refs, norm)

    o_ref[...] = jnp.dot(h_ref[...], w_ref[...], preferred_element_type=F32).astype(o_ref.dtype)


def norm_matmul(x_parts, g, w, out_dtype, tm, tn):
    k, n = w.shape
    tiles = _part_tiles(x_parts, tm)
    return pl.pallas_call(
        functools.partial(_norm_mm_kernel, tiles=tiles),
        grid=(sum(tiles), n // tn),
        in_specs=_part_specs(tiles, (tm, k)) + [pl.BlockSpec((1, k), lambda i, j: (0, 0)),
                                                 pl.BlockSpec((k, tn), lambda i, j: (0, j))],
        out_specs=pl.BlockSpec((tm, tn), lambda i, j: (i, j)),
        out_shape=jax.ShapeDtypeStruct((sum(tiles) * tm, n), out_dtype),
        scratch_shapes=[pltpu.VMEM((tm, k), BF16)],
        compiler_params=_params("parallel", "arbitrary"),
        name="norm_matmul",
    )(*x_parts, g.reshape(1, k), w)


def _mm_res_kernel(*refs, tiles):
    a_ref, w_ref = refs[:2]
    r_refs = refs[2:2 + len(tiles)]
    o_ref = refs[-1]
    y = jnp.dot(a_ref[...], w_ref[...], preferred_element_type=F32)

    def add(r_ref):
        o_ref[...] = r_ref[...] + y

    _for_owning_part(tiles, r_refs, add)


def matmul_residual(a, w, r_parts, tm):
    m, k = a.shape
    n = w.shape[1]
    tiles = _part_tiles(r_parts, tm)
    return pl.pallas_call(
        functools.partial(_mm_res_kernel, tiles=tiles),
        grid=(m // tm, 1),
        in_specs=[pl.BlockSpec((tm, k), lambda i, j: (i, 0)),
                  pl.BlockSpec((k, n), lambda i, j: (0, 0))] + _part_specs(tiles, (tm, n)),
        out_specs=pl.BlockSpec((tm, n), lambda i, j: (i, 0)),
        out_shape=jax.ShapeDtypeStruct((m, n), F32),
        compiler_params=_params("parallel", "arbitrary"),
        name="matmul_residual",
    )(a, w, *r_parts)


def _mlp_kernel(x_ref, g_ref, wu_ref, wd_ref, gf_ref, *refs, final_norm, tiles):
    o_refs, h_ref = refs[:-1], refs[-1]
    j = pl.program_id(1)

    def run(o_ref):
        @pl.when(j == 0)
        def _():
            x = x_ref[...]
            h_ref[...] = _rms_bf16(x, g_ref[...])
            o_ref[...] = x

        u = jnp.dot(h_ref[...], wu_ref[...], preferred_element_type=F32)
        a = jnp.square(jnp.maximum(u, 0.0)).astype(BF16)
        o_ref[...] += jnp.dot(a, wd_ref[...], preferred_element_type=F32)

        if final_norm:
            @pl.when(j == pl.num_programs(1) - 1)
            def _():
                y = o_ref[...]
                o_ref[...] = y * lax.rsqrt(jnp.mean(y * y, axis=-1, keepdims=True) + RMS_EPS) * gf_ref[...]

    _for_owning_part(tiles, o_refs, run)


def mlp_block(x, g, w_up, w_down, g_final, final_norm, tm, tf, out_rows):
    m, d = x.shape
    f = w_up.shape[1]
    assert sum(out_rows) == m and all(r % tm == 0 for r in out_rows)
    tiles = tuple(r // tm for r in out_rows)
    return pl.pallas_call(
        functools.partial(_mlp_kernel, final_norm=final_norm, tiles=tiles),
        grid=(m // tm, f // tf),
        in_specs=[pl.BlockSpec((tm, d), lambda i, j: (i, 0)),
                  pl.BlockSpec((1, d), lambda i, j: (0, 0)),
                  pl.BlockSpec((d, tf), lambda i, j: (0, j)),
                  pl.BlockSpec((tf, d), lambda i, j: (j, 0)),
                  pl.BlockSpec((1, d), lambda i, j: (0, 0))],
        out_specs=_part_specs(tiles, (tm, d)),
        out_shape=[jax.ShapeDtypeStruct((r, d), F32) for r in out_rows],
        scratch_shapes=[pltpu.VMEM((tm, d), BF16)],
        compiler_params=_params("parallel" if len(tiles) == 1 else "arbitrary", "arbitrary"),
        name="mlp_block",
    )(x, g.reshape(1, d), w_up, w_down, g_final.reshape(1, d))


def _merge_kernel(gate_ref, ba_ref, bb_ref, bc_ref, bd_ref, wb_ref, o_ref, acc_ref):
    n = pl.program_id(1)

    def contribution(b_ref):
        y = jnp.dot(b_ref[...], wb_ref[0], preferred_element_type=F32)
        return (jnp.tanh(gate_ref[...].astype(F32)) + 1.0) * y

    @pl.when(n == 0)
    def _():
        acc_ref[...] = contribution(ba_ref)

    @pl.when(n == 1)
    def _():
        acc_ref[...] += contribution(bb_ref)

    @pl.when(n == 2)
    def _():
        acc_ref[...] += contribution(bc_ref)

    @pl.when(n == 3)
    def _():
        o_ref[...] = (acc_ref[...] + contribution(bd_ref)).astype(o_ref.dtype)


def gated_merge(z, branches, w_branch, tm):
    m = z.shape[0]
    gate_blk0 = Z_GATE // D_MODEL
    br_spec = pl.BlockSpec((tm, BRANCH_W), lambda i, n: (i, 0))
    return pl.pallas_call(
        _merge_kernel,
        grid=(m // tm, N_BRANCH),
        in_specs=[pl.BlockSpec((tm, D_MODEL), lambda i, n: (i, gate_blk0 + n)),
                  br_spec, br_spec, br_spec, br_spec,
                  pl.BlockSpec((1, BRANCH_W, D_MODEL), lambda i, n: (n, 0, 0))],
        out_specs=pl.BlockSpec((tm, D_MODEL), lambda i, n: (i, 0)),
        out_shape=jax.ShapeDtypeStruct((m, D_MODEL), BF16),
        scratch_shapes=[pltpu.VMEM((tm, D_MODEL), F32)],
        compiler_params=_params("parallel", "arbitrary"),
        name="gated_merge",
    )(z, *branches, w_branch)


NA_ROWS_LOCKSTEP = 8
NA_MASK = -1e30


def na_bias_table(rpb):
    q = np.arange(GRID_W)
    kc = np.arange(GRID_W)
    cs = np.clip(q - NA_WIN_C // 2, 0, GRID_W - NA_WIN_C)
    ok = (kc[None, :] >= cs[:, None]) & (kc[None, :] < cs[:, None] + NA_WIN_C)
    dcol = np.clip(kc[None, :] - q[:, None] + NA_WIN_C - 1, 0, 2 * NA_WIN_C - 2)
    drow = np.arange(NA_WIN_R)[None, :] - np.arange(NA_WIN_R)[:, None] + NA_WIN_R - 1
    t = rpb.astype(F32)[:, drow]
    t = t[:, :, :, dcol]
    t = jnp.transpose(t, (1, 0, 3, 2, 4))
    t = jnp.where(jnp.asarray(ok)[None, None, :, None, :], t, NA_MASK)
    return t.reshape(NA_WIN_R, NA_HEADS, GRID_W, NA_WIN_R * GRID_W)


def _na_kernel(q_ref, k_ref, v_ref, bias_ref, o_ref, *, rows_per_step, rows):
    rblk = pl.program_id(1)
    win = NA_WIN_R * GRID_W
    scale = NA_HEAD_DIM ** -0.5

    pair_w = 2 * NA_HEAD_DIM
    low = lax.broadcasted_iota(jnp.int32, (GRID_W, pair_w), 1) < NA_HEAD_DIM

    def rows_body(it, carry):
        inst = []
        q, kb, vb, off, qstart = [], [], [], [], []
        for t in range(NA_ROWS_LOCKSTEP):
            rl = it * NA_ROWS_LOCKSTEP + t
            r = rblk * rows_per_step + rl
            rs = jnp.clip(r - NA_WIN_R // 2, 0, rows - NA_WIN_R)
            off.append(r - rs)
            qstart.append(pl.multiple_of(rl * GRID_W, GRID_W))
            kstart = pl.multiple_of(rs * GRID_W, GRID_W)
            q.append(q_ref[0, pl.ds(qstart[t], GRID_W), :])
            kb.append(k_ref[0, pl.ds(kstart, win), :])
            vb.append(v_ref[0, pl.ds(kstart, win), :])
            inst += [(t, h) for h in range(NA_HEADS)]

        def pair(x, h):
            return x[:, (h // 2) * pair_w:(h // 2 + 1) * pair_w]

        qm = [jnp.where(low if h % 2 == 0 else ~low, pair(q[t], h), jnp.zeros((), BF16)) for t, h in inst]
        s = [lax.dot_general(qm[i], pair(kb[t], h), (((1,), (1,)), ((), ())), preferred_element_type=F32)
             for i, (t, h) in enumerate(inst)]
        s = [s[i] * scale + bias_ref[off[t], h] for i, (t, h) in enumerate(inst)]
        p = [jnp.exp(x - jnp.max(x, axis=-1, keepdims=True)) for x in s]
        l = [jnp.sum(x, axis=-1, keepdims=True) for x in p]
        o = [jnp.dot(p[i].astype(BF16), pair(vb[t], h), preferred_element_type=F32) / l[i]
             for i, (t, h) in enumerate(inst)]
        for t in range(NA_ROWS_LOCKSTEP):
            base = t * NA_HEADS
            outs = [jnp.where(low, o[base + h], o[base + h + 1]) for h in range(0, NA_HEADS, 2)]
            o_ref[0, pl.ds(qstart[t], GRID_W), :] = jnp.concatenate(outs, axis=1).astype(o_ref.dtype)
        return carry

    lax.fori_loop(0, rows_per_step // NA_ROWS_LOCKSTEP, rows_body, 0)


def neighbourhood_attention(z3, bias, rows_per_step):
    b, l, _ = z3.shape
    rows = l // GRID_W
    w = NA_HEADS * NA_HEAD_DIM
    blk = Z_NA // w
    return pl.pallas_call(
        functools.partial(_na_kernel, rows_per_step=rows_per_step, rows=rows),
        grid=(b, rows // rows_per_step),
        in_specs=[pl.BlockSpec((1, rows_per_step * GRID_W, w), lambda i, r: (i, r, blk)),
                  pl.BlockSpec((1, l, w), lambda i, r: (i, 0, blk + 1)),
                  pl.BlockSpec((1, l, w), lambda i, r: (i, 0, blk + 2)),
                  pl.BlockSpec(bias.shape, lambda i, r: (0, 0, 0, 0))],
        out_specs=pl.BlockSpec((1, rows_per_step * GRID_W, w), lambda i, r: (i, r, 0)),
        out_shape=jax.ShapeDtypeStruct((b, l, w), BF16),
        compiler_params=_params("parallel", "arbitrary"),
        name="neighbourhood_attention",
    )(z3, z3, z3, bias)


def _fft_tables(l):
    n = 2 * l
    n2 = FFT_N2
    n1 = n // n2
    a = np.arange(n1)
    b = np.arange(n2)
    k1 = np.arange(n1)
    t = n2 * a[None, :] + b[:, None]
    th = 2.0 * np.pi * (k1[None, :, None] * t[:, None, :] % n) / n
    fa = np.concatenate([np.cos(th), -np.sin(th)], axis=1)
    th_t = np.transpose(th, (0, 2, 1))
    ga = np.concatenate([np.cos(th_t), -np.sin(th_t)], axis=2)
    ph = 2.0 * np.pi * (np.outer(b, b) % n2) / n2
    cr, ci = np.cos(ph), -np.sin(ph)
    fb = np.block([[cr, -ci], [ci, cr]])
    fbi = np.block([[cr, ci], [-ci, cr]])
    return fa, ga, fb, fbi


def _filter_fft_kernel(hf_ref, hb_ref, fa_hi_ref, fa_lo_ref, fb_hi_ref, fb_lo_ref, o_ref, y_ref, *, n1):
    n2 = FFT_N2
    na = n1 // 2
    inv_n = 1.0 / (n1 * n2)

    def transform(src_ref, combine):
        def step_a(b, carry):
            xb = _split_bf16(src_ref[pl.ds(b, na, stride=n2), :])
            r = _dot_split((fa_hi_ref[b], fa_lo_ref[b]), xb)
            y_ref[pl.ds(b, n1, stride=2 * n2), :] = r[:n1]
            y_ref[pl.ds(b + n2, n1, stride=2 * n2), :] = r[n1:]
            return carry

        lax.fori_loop(0, n2, step_a, 0, unroll=8)

        def step_b(k1, carry):
            r0 = pl.multiple_of(k1 * 2 * n2, 2 * n2)
            x = _dot_split((fb_hi_ref[...], fb_lo_ref[...]), _split_bf16(y_ref[pl.ds(r0, 2 * n2), :]))
            combine(r0, x * inv_n)
            return carry

        lax.fori_loop(0, n1, step_b, 0, unroll=4)

    def store(r0, x):
        o_ref[pl.ds(r0, 2 * n2), :] = x

    def add_conjugate(r0, x):
        o_ref[pl.ds(r0, n2), :] += x[:n2]
        o_ref[pl.ds(r0 + n2, n2), :] -= x[n2:]

    transform(hf_ref, store)
    transform(hb_ref, add_conjugate)


def filter_spectrum(h, fa, fb):
    l = h.shape[0]
    c = h.shape[1] // 2
    n = 2 * l
    n1 = n // FFT_N2
    nch = c // LANES
    return pl.pallas_call(
        functools.partial(_filter_fft_kernel, n1=n1),
        grid=(nch,),
        in_specs=[pl.BlockSpec((l, LANES), lambda i: (0, i)),
                  pl.BlockSpec((l, LANES), lambda i: (0, nch + i)),
                  pl.BlockSpec(fa[0].shape, lambda i: (0, 0, 0)),
                  pl.BlockSpec(fa[1].shape, lambda i: (0, 0, 0)),
                  pl.BlockSpec(fb[0].shape, lambda i: (0, 0)),
                  pl.BlockSpec(fb[1].shape, lambda i: (0, 0))],
        out_specs=pl.BlockSpec((2 * n, LANES), lambda i: (0, i)),
        out_shape=jax.ShapeDtypeStruct((2 * n, c), F32),
        scratch_shapes=[pltpu.VMEM((2 * n, LANES), F32)],
        compiler_params=_params("parallel"),
        name="hyena_filter_spectrum",
    )(h, h, *fa, *fb)


def _hyena_filter_kernel(z_ref, w1_ref, b1_ref, w2_ref, b2_ref, w3_ref, dec_ref, o_ref):
    hid = jnp.sin(jnp.dot(z_ref[...], w1_ref[...], precision=HIGHEST, preferred_element_type=F32) + b1_ref[...])
    hid = jnp.sin(jnp.dot(hid, w2_ref[...], precision=HIGHEST, preferred_element_type=F32) + b2_ref[...])
    h = jnp.dot(hid, w3_ref[...], precision=HIGHEST, preferred_element_type=F32)
    dec = dec_ref[...]
    hf = h[:, :HY_WIDTH] * dec
    hb = h[:, HY_WIDTH:] * dec
    norm = (jnp.sum(jnp.abs(hf), axis=0, keepdims=True) + jnp.sum(jnp.abs(hb), axis=0, keepdims=True)) + RMS_EPS
    o_ref[:, :HY_WIDTH] = hf / norm
    o_ref[:, HY_WIDTH:] = hb / norm


def hyena_filters(l, w1, b1, w2, b2, w3):
    t = jnp.linspace(0.0, 1.0, l, dtype=F32)[:, None]
    w = 2.0 * math.pi * jnp.arange(l, dtype=F32)[:, None] / l
    bands = jnp.linspace(1e-4, HY_POS_BANDS - 1, HY_POS_BANDS, dtype=F32)[None, :]
    z = jnp.concatenate([t, jnp.cos(bands * w), -jnp.sin(bands * w)], axis=-1)
    pad = LANES - z.shape[1]
    z = jnp.pad(z, ((0, 0), (0, pad)))
    w1p = jnp.pad(w1.astype(F32), ((0, pad), (0, 0)))
    max_decay = math.log(HY_DECAY_TARGET) / HY_FAST_DECAY
    min_decay = math.log(HY_DECAY_TARGET) / HY_SLOW_DECAY
    deltas = jnp.abs(jnp.linspace(min_decay, max_decay, HY_WIDTH, dtype=F32))
    dec = jnp.exp(-t * deltas[None, :])
    return pl.pallas_call(
        _hyena_filter_kernel,
        out_shape=jax.ShapeDtypeStruct((l, 2 * HY_WIDTH), F32),
        compiler_params=pltpu.CompilerParams(vmem_limit_bytes=VMEM_LIMIT_BYTES),
        name="hyena_filter_ffn",
    )(z, w1p, b1.reshape(1, -1).astype(F32), w2.astype(F32), b2.reshape(1, -1).astype(F32), w3.astype(F32), dec)


CONV_PAD = 8


def _dwconv3(x, w_ref, pad_ref):
    n = x.shape[0]
    zeros = jnp.zeros((CONV_PAD, x.shape[1]), F32)
    pad_ref[0:CONV_PAD, :] = zeros
    pad_ref[CONV_PAD + n:2 * CONV_PAD + n, :] = zeros
    pad_ref[CONV_PAD:CONV_PAD + n, :] = x
    w = w_ref[...].astype(F32)
    return (pad_ref[CONV_PAD - 1:CONV_PAD - 1 + n, :] * w[0:1] + x * w[1:2]
            + pad_ref[CONV_PAD + 1:CONV_PAD + 1 + n, :] * w[2:3])


def _hyena_kernel(x1_ref, x2_ref, v_ref, w1_ref, w2_ref, wv_ref, skip_ref, kf_ref, fa_ref, ga_ref, fb_ref, fbi_ref,
                  o_ref, xs_ref, y_ref, pad_ref, *, n1):
    n2 = FFT_N2
    na = n1 // 2
    pa, py = FFT_PITCH_X, FFT_PITCH_Y
    vg = _dwconv3(v_ref[0].astype(F32), wv_ref, pad_ref) * _dwconv3(x1_ref[0].astype(F32), w1_ref, pad_ref)
    for a in range(na):
        xs_ref[a * pa:a * pa + n2, :] = vg[a * n2:(a + 1) * n2]

    def fwd_a(b, carry):
        xb = xs_ref[pl.ds(b, na, stride=pa), :].astype(BF16)
        r = jnp.dot(fa_ref[b], xb, preferred_element_type=F32)
        y_ref[pl.ds(b, n1, stride=py), :] = r[:n1]
        y_ref[pl.ds(b + n2, n1, stride=py), :] = r[n1:]
        return carry

    lax.fori_loop(0, n2, fwd_a, 0, unroll=FFT_UNROLL)

    def mid(kg, carry):
        grp = range(FFT_GROUP)
        k1 = [kg * FFT_GROUP + t for t in grp]
        r0 = [pl.multiple_of(k * py, 8) for k in k1]
        f0 = [pl.multiple_of(k * 2 * n2, 2 * n2) for k in k1]
        fb = fb_ref[...]
        x = [jnp.dot(fb, y_ref[pl.ds(r0[t], 2 * n2), :].astype(BF16), preferred_element_type=F32) for t in grp]
        p = []
        for t in grp:
            xr, xi = x[t][:n2], x[t][n2:]
            kr = kf_ref[pl.ds(f0[t], n2), :]
            ki = kf_ref[pl.ds(f0[t] + n2, n2), :]
            p.append(jnp.concatenate([xr * kr - xi * ki, xr * ki + xi * kr], axis=0).astype(BF16))
        fbi = fbi_ref[...]
        q = [jnp.dot(fbi, p[t], preferred_element_type=F32) for t in grp]
        for t in grp:
            y_ref[pl.ds(r0[t], 2 * n2), :] = q[t]
        return carry

    lax.fori_loop(0, n1 // FFT_GROUP, mid, 0)

    def inv_a(b, carry):
        qb = jnp.concatenate([y_ref[pl.ds(b, n1, stride=py), :],
                              y_ref[pl.ds(b + n2, n1, stride=py), :]], axis=0).astype(BF16)
        xs_ref[pl.ds(b, na, stride=pa), :] = jnp.dot(ga_ref[b], qb, preferred_element_type=F32)
        return carry

    lax.fori_loop(0, n2, inv_a, 0, unroll=FFT_UNROLL)

    gate = _dwconv3(x2_ref[0].astype(F32), w2_ref, pad_ref)
    skip = skip_ref[...].astype(F32)
    for a in range(na):
        rows = slice(a * n2, (a + 1) * n2)
        o_ref[0, rows, :] = ((xs_ref[a * pa:a * pa + n2, :] + vg[rows] * skip) * gate[rows]).astype(o_ref.dtype)


def hyena_mixer(z3, w_short, skip, kf, tables):
    b, l, _ = z3.shape
    n1 = 2 * l // FFT_N2
    fa, ga, fb, fbi = tables
    nch = HY_WIDTH // LANES
    blk = Z_HY // LANES

    def zspec(seg):
        return pl.BlockSpec((1, l, LANES), lambda c, i: (i, 0, blk + seg * nch + c))

    def wspec(seg):
        return pl.BlockSpec((3, LANES), lambda c, i: (0, seg * nch + c))

    return pl.pallas_call(
        functools.partial(_hyena_kernel, n1=n1),
        grid=(nch, b),
        in_specs=[zspec(0), zspec(1), zspec(2), wspec(0), wspec(1), wspec(2),
                  pl.BlockSpec((1, LANES), lambda c, i: (0, c)),
                  pl.BlockSpec((4 * l, LANES), lambda c, i: (0, c)),
                  pl.BlockSpec(fa.shape, lambda c, i: (0, 0, 0)),
                  pl.BlockSpec(ga.shape, lambda c, i: (0, 0, 0)),
                  pl.BlockSpec(fb.shape, lambda c, i: (0, 0)),
                  pl.BlockSpec(fbi.shape, lambda c, i: (0, 0))],
        out_specs=pl.BlockSpec((1, l, LANES), lambda c, i: (i, 0, c)),
        out_shape=jax.ShapeDtypeStruct((b, l, HY_WIDTH), BF16),
        scratch_shapes=[pltpu.VMEM((n1 // 2 * FFT_PITCH_X, LANES), F32), pltpu.VMEM((n1 * FFT_PITCH_Y, LANES), F32),
                        pltpu.VMEM((l + 2 * CONV_PAD, LANES), F32)],
        compiler_params=_params("parallel", "arbitrary"),
        name="hyena_mixer",
    )(z3, z3, z3, w_short, w_short, w_short, skip.reshape(1, -1), kf, fa, ga, fb, fbi)


MLA_HW = 2 * LANES
MLA_HEADS_PER_STEP = 2


def _rope_group(g, cs):
    prod = g * cs
    s = prod + pltpu.roll(prod, MLA_ROPE, 1)
    lane = lax.broadcasted_iota(jnp.int32, s.shape, 1)
    return jnp.where(lane < MLA_ROPE, s, 0.0)


def _mla_prep_kernel(cq_ref, ckv_ref, kr_ref, gq_ref, gkv_ref, wq_ref, wkv_ref, cs_ref, q_ref, k_ref, v_ref):
    cs = cs_ref[...]
    hq = _rms_bf16(cq_ref[...], gq_ref[...])
    q = jnp.dot(hq, wq_ref[...], preferred_element_type=F32)
    hkv = _rms_bf16(ckv_ref[...], gkv_ref[...])
    kv = jnp.dot(hkv, wkv_ref[...], preferred_element_type=F32)
    k_rope = _rope_group(kr_ref[...].astype(F32), cs).astype(BF16)
    lane = lax.broadcasted_iota(jnp.int32, (cs.shape[0], LANES), 1)
    ones_lane = jnp.where(lane == 0, 1.0, 0.0).astype(BF16)
    for h in range(MLA_HEADS):
        o = h * MLA_HW
        q_ref[:, o:o + LANES] = q[:, o:o + LANES].astype(BF16)
        q_ref[:, o + LANES:o + MLA_HW] = _rope_group(q[:, o + LANES:o + MLA_HW], cs).astype(BF16)
        k_ref[:, o:o + LANES] = kv[:, o:o + LANES].astype(BF16)
        k_ref[:, o + LANES:o + MLA_HW] = k_rope
        v_ref[:, o:o + LANES] = kv[:, o + LANES:o + MLA_HW].astype(BF16)
        v_ref[:, o + LANES:o + MLA_HW] = ones_lane


def _mla_attn_kernel(q_ref, k_ref, v_ref, o_ref):
    c = (MLA_NOPE + MLA_ROPE) ** -0.5 * math.log2(math.e)
    heads = range(MLA_HEADS_PER_STEP)
    s = [lax.dot_general(q_ref[0, :, h * MLA_HW:(h + 1) * MLA_HW], k_ref[0, :, h * MLA_HW:(h + 1) * MLA_HW],
                         (((1,), (1,)), ((), ())), preferred_element_type=F32) for h in heads]
    for h in heads:
        p = jnp.exp2((s[h] - jnp.max(s[h], axis=-1, keepdims=True)) * c)
        o = jnp.dot(p.astype(BF16), v_ref[0, :, h * MLA_HW:(h + 1) * MLA_HW], preferred_element_type=F32)
        o_ref[0, :, h * MLA_V:(h + 1) * MLA_V] = (o[:, :MLA_V] / o[:, MLA_V:MLA_V + 1]).astype(o_ref.dtype)


def mla_mixer(z, b, l, g_q, g_kv, wq_p, w_ukv, cs_tab, tm, tq):
    m = z.shape[0]
    lt = l // tm
    qp, kp, vp = pl.pallas_call(
        _mla_prep_kernel,
        grid=(m // tm,),
        in_specs=[pl.BlockSpec((tm, MLA_Q_RANK), lambda i: (i, Z_CQ // MLA_Q_RANK)),
                  pl.BlockSpec((tm, MLA_KV_RANK), lambda i: (i, Z_CKV // MLA_KV_RANK)),
                  pl.BlockSpec((tm, LANES), lambda i: (i, Z_KR // LANES)),
                  pl.BlockSpec((1, MLA_Q_RANK), lambda i: (0, 0)),
                  pl.BlockSpec((1, MLA_KV_RANK), lambda i: (0, 0)),
                  pl.BlockSpec(wq_p.shape, lambda i: (0, 0)),
                  pl.BlockSpec(w_ukv.shape, lambda i: (0, 0)),
                  pl.BlockSpec((tm, LANES), lambda i: (i % lt, 0))],
        out_specs=[pl.BlockSpec((tm, MLA_HEADS * MLA_HW), lambda i: (i, 0)),
                   pl.BlockSpec((tm, MLA_HEADS * MLA_HW), lambda i: (i, 0)),
                   pl.BlockSpec((tm, MLA_HEADS * MLA_HW), lambda i: (i, 0))],
        out_shape=[jax.ShapeDtypeStruct((m, MLA_HEADS * MLA_HW), BF16),
                   jax.ShapeDtypeStruct((m, MLA_HEADS * MLA_HW), BF16),
                   jax.ShapeDtypeStruct((m, MLA_HEADS * MLA_HW), BF16)],
        compiler_params=_params("parallel"),
        name="mla_prep",
    )(z, z, z, g_q.reshape(1, -1), g_kv.reshape(1, -1), wq_p, w_ukv, cs_tab)
    qp = qp.reshape(b, l, -1)
    kp = kp.reshape(b, l, -1)
    vp = vp.reshape(b, l, -1)
    hs = MLA_HEADS_PER_STEP
    return pl.pallas_call(
        _mla_attn_kernel,
        grid=(b, MLA_HEADS // hs, l // tq),
        in_specs=[pl.BlockSpec((1, tq, hs * MLA_HW), lambda i, h, t: (i, t, h)),
                  pl.BlockSpec((1, l, hs * MLA_HW), lambda i, h, t: (i, 0, h)),
                  pl.BlockSpec((1, l, hs * MLA_HW), lambda i, h, t: (i, 0, h))],
        out_specs=pl.BlockSpec((1, tq, hs * MLA_V), lambda i, h, t: (i, t, h)),
        out_shape=jax.ShapeDtypeStruct((b, l, MLA_HEADS * MLA_V), BF16),
        compiler_params=_params("parallel", "parallel", "arbitrary"),
        name="mla_attention",
    )(qp, kp, vp)


def _split_bf16(x):
    hi = x.astype(BF16)
    return hi, (x - hi.astype(F32)).astype(BF16)


def _dot_split(a, b):
    (ah, al), (bh, bl) = a, b
    return (jnp.dot(ah, bh, preferred_element_type=F32)
            + (jnp.dot(ah, bl, preferred_element_type=F32) + jnp.dot(al, bh, preferred_element_type=F32)))


def _tri_unit_inverse(mats):
    c = mats[0].shape[0]
    eye = (lax.broadcasted_iota(jnp.int32, (c, c), 0) == lax.broadcasted_iota(jnp.int32, (c, c), 1)).astype(F32)
    ps = [eye - a for a in mats]
    ms = list(mats)
    for level in range(int(math.log2(c)) - 1):
        if level == 0:
            splits = [_split_bf16(m) for m in ms]
            ms = [_dot_split(sp, sp) for sp in splits]
            yield
            ps = [_dot_split(_split_bf16(p), _split_bf16(eye + m)) for p, m in zip(ps, ms)]
        else:
            hs = [m.astype(BF16) for m in ms]
            ms = [jnp.dot(h, h, preferred_element_type=F32) for h in hs]
            yield
            fs = [(eye + m).astype(BF16) for m in ms]
            ps = [jnp.dot(ph, f, preferred_element_type=F32) + jnp.dot(pl_, f, preferred_element_type=F32)
                  for (ph, pl_), f in zip([_split_bf16(p) for p in ps], fs)]
        yield
    return ps


def _interleave(main, side=None):
    gens = [g for g in (main, side) if g is not None]
    results = [None] * len(gens)
    live = [True] * len(gens)
    while any(live):
        for idx, gen in enumerate(gens):
            if live[idx]:
                try:
                    next(gen)
                except StopIteration as stop:
                    results[idx] = stop.value
                    live[idx] = False
    return results


def _bdot(a, b):
    return jnp.dot(a.astype(BF16), b.astype(BF16), preferred_element_type=F32)


def _bdot_nt(a, b):
    return lax.dot_general(a.astype(BF16), b.astype(BF16), (((1,), (1,)), ((), ())), preferred_element_type=F32)


def _dn_group_setup(chunks, refs):
    q_s, k_s, v_s, g_s, b_s = refs
    cc = DN_CHUNK
    n = range(len(chunks))
    rev = [r for _, r in chunks]
    r0 = [pl.multiple_of(c * cc, cc) for c, _ in chunks]
    q = [q_s[pl.ds(r, cc), :] for r in r0]
    k = [k_s[pl.ds(r, cc), :] for r in r0]
    v = [v_s[pl.ds(r, cc), :] for r in r0]
    lane = [2 if r else 0 for r in rev]
    ri = lax.broadcasted_iota(jnp.int32, (cc, cc), 0)
    ci = lax.broadcasted_iota(jnp.int32, (cc, cc), 1)
    incl = [(ri <= ci) if r else (ri >= ci) for r in rev]
    strict = [(ri < ci) if r else (ri > ci) for r in rev]
    g = [g_s[pl.ds(r, cc), :] for r in r0]
    g1 = [x.astype(BF16) for x in g]
    r1 = [x - h.astype(F32) for x, h in zip(g, g1)]
    g2 = [x.astype(BF16) for x in r1]
    g3 = [(x - h.astype(F32)).astype(BF16) for x, h in zip(r1, g2)]
    ones = [m.astype(BF16) for m in incl]
    gc = [jnp.dot(ones[i], g1[i], preferred_element_type=F32) for i in n]
    gc = [gc[i] + jnp.dot(ones[i], g2[i], preferred_element_type=F32) for i in n]
    gc = [gc[i] + jnp.dot(ones[i], g3[i], preferred_element_type=F32) for i in n]
    yield
    gcol = [gc[i][:, lane[i]:lane[i] + 1] for i in n]
    grow = [jnp.transpose(gc[i])[lane[i]:lane[i] + 1, :] for i in n]
    beta = [b_s[pl.ds(r0[i], cc), :][:, lane[i] + 1:lane[i] + 2] for i in n]
    g_last = [gcol[i][0:1, :] if rev[i] else gcol[i][cc - 1:cc, :] for i in n]
    decay = [jnp.where(incl[i], jnp.exp(jnp.where(incl[i], gcol[i] - grow[i], 0.0)), 0.0) for i in n]
    e_g = [jnp.exp(x) for x in gcol]
    kb = [k[i] * beta[i] for i in n]
    a_both = [_bdot_nt(jnp.concatenate([kb[i], q[i]], axis=0), k[i]) for i in n]
    yield
    a_kk = [jnp.where(strict[i], a_both[i][:cc] * decay[i], 0.0) for i in n]
    a_qk = [jnp.where(incl[i], a_both[i][cc:] * decay[i], 0.0) for i in n]
    t_inv = yield from _tri_unit_inverse(a_kk)
    uw = [_bdot(t_inv[i], jnp.concatenate([v[i] * beta[i], kb[i] * e_g[i]], axis=1)) for i in n]
    yield
    k_dec = [k[i] * jnp.exp(g_last[i] - gcol[i]) for i in n]
    wq = [jnp.concatenate([uw[i][:, DN_DV:], q[i] * e_g[i]], axis=0).astype(BF16) for i in n]
    ak = [jnp.concatenate([a_qk[i], jnp.transpose(k_dec[i])], axis=0).astype(BF16) for i in n]
    return [(wq[i], ak[i], uw[i][:, :DN_DV], jnp.exp(g_last[i])) for i in n]


def _dn_group_steps(states, setups, out_refs, chunks):
    cc = DN_CHUNK
    dirs = range(len(states))
    for t in range(len(setups[0])):
        st = [setups[d][t] for d in dirs]
        r = [jnp.dot(st[d][0], states[d].astype(BF16), preferred_element_type=F32) for d in dirs]
        yield
        v_new = [st[d][2] - r[d][:cc] for d in dirs]
        r2 = [jnp.dot(st[d][1], v_new[d].astype(BF16), preferred_element_type=F32) for d in dirs]
        yield
        states = [states[d] * st[d][3] + r2[d][cc:] for d in dirs]
        for d in dirs:
            out_refs[d][pl.ds(pl.multiple_of(chunks[d][t] * cc, cc), cc), :] = r[d][cc:] + r2[d][:cc]
    return states


def _dn_kernel(zq_ref, zk_ref, zv_ref, zg_ref, ab_ref, wq_ref, wk_ref, wv_ref, alog_ref, dtb_ref, gn_ref, o_ref,
               q_s, k_s, v_s, g_s, b_s, of_s, ob_s, pad_ref, *, nchunks):
    def act(z_ref, w_ref):
        c = _dwconv3(z_ref[0].astype(F32), w_ref, pad_ref)
        return _silu(c)

    def l2n(x):
        return x * lax.rsqrt(jnp.sum(x * x, axis=-1, keepdims=True) + RMS_EPS)

    q_s[...] = l2n(act(zq_ref, wq_ref)) * (DN_DK ** -0.5)
    k_s[...] = l2n(act(zk_ref, wk_ref))
    v_s[...] = act(zv_ref, wv_ref)
    ab = ab_ref[0]
    x = ab + dtb_ref[0]
    softplus = jnp.maximum(x, 0.0) + jnp.log1p(jnp.exp(-jnp.abs(x)))
    g_s[...] = -jnp.exp(alog_ref[0]) * softplus
    b_s[...] = jax.nn.sigmoid(ab)
    refs = (q_s, k_s, v_s, g_s, b_s)

    ngroups = nchunks // DN_GROUP

    def group_chunks(gi):
        fwd = [gi * DN_GROUP + t for t in range(DN_GROUP)]
        return fwd, [nchunks - 1 - c for c in fwd]

    def setup_gen(gi):
        fwd, bwd = group_chunks(gi)
        return _dn_group_setup([(c, False) for c in fwd] + [(c, True) for c in bwd], refs)

    def steps_gen(gi, states, flat):
        setups = [tuple(flat[4 * i:4 * i + 4]) for i in range(2 * DN_GROUP)]
        return _dn_group_steps(states, [setups[:DN_GROUP], setups[DN_GROUP:]], (of_s, ob_s), group_chunks(gi))

    def flatten(setups):
        return [x for st in setups for x in st]

    def body(gi, carry):
        nxt, states = _interleave(setup_gen(gi + 1), steps_gen(gi, list(carry[:2]), carry[2:]))
        return (*states, *flatten(nxt))

    zero = jnp.zeros((DN_DK, DN_DV), F32)
    (first,) = _interleave(setup_gen(0))
    carry = lax.fori_loop(0, ngroups - 1, body, (zero, zero, *flatten(first)))
    _interleave(steps_gen(ngroups - 1, list(carry[:2]), carry[2:]))

    o = of_s[...] + ob_s[...]
    o = o * lax.rsqrt(jnp.mean(o * o, axis=-1, keepdims=True) + RMS_EPS) * gn_ref[...]
    gate = zg_ref[0].astype(F32)
    o_ref[0] = (o * _silu(gate)).astype(o_ref.dtype)


def deltanet_mixer(z3, ab3, w_conv, alog_p, dtb_p, g_norm):
    b, l, _ = z3.shape
    blk = Z_DNQKV // LANES
    gblk = Z_DNGATE // LANES
    hh = DN_HEADS

    def zspec(off):
        return pl.BlockSpec((1, l, LANES), lambda i, h: (i, 0, off + h))

    def wspec(seg):
        return pl.BlockSpec((3, LANES), lambda i, h: (0, seg * hh + h))

    vec = pl.BlockSpec((1, 1, LANES), lambda i, h: (h, 0, 0))
    seq = pltpu.VMEM((l, LANES), F32)
    return pl.pallas_call(
        functools.partial(_dn_kernel, nchunks=l // DN_CHUNK),
        grid=(b, hh),
        in_specs=[zspec(blk), zspec(blk + hh), zspec(blk + 2 * hh), zspec(gblk),
                  pl.BlockSpec((1, l, LANES), lambda i, h: (i, 0, h)),
                  wspec(0), wspec(1), wspec(2), vec, vec,
                  pl.BlockSpec((1, LANES), lambda i, h: (0, 0))],
        out_specs=pl.BlockSpec((1, l, LANES), lambda i, h: (i, 0, h)),
        out_shape=jax.ShapeDtypeStruct((b, l, hh * DN_DV), BF16),
        scratch_shapes=[seq] * 7 + [pltpu.VMEM((l + 2 * CONV_PAD, LANES), F32)],
        compiler_params=_params("parallel", "arbitrary"),
        name="deltanet_mixer",
    )(z3, z3, z3, z3, ab3, w_conv, w_conv, w_conv, alog_p, dtb_p, g_norm.reshape(1, -1))


def _rotate_half_cols(w):
    half = w.shape[-1] // 2
    return jnp.concatenate([-w[..., half:], w[..., :half]], axis=-1)


def _prep_w_in(w_in):
    cuts = np.cumsum([0, 1536, 1536, MLA_Q_RANK, MLA_KV_RANK, MLA_ROPE, 1536, 512, 16])
    dnab = w_in[:, cuts[7]:cuts[8]]
    w16 = w_in.astype(BF16)
    na, hy, cq, ckv, kr, dnqkv, dngate = (w16[:, cuts[i]:cuts[i + 1]] for i in range(7))
    gates = w16[:, cuts[8]:]
    pad = jnp.zeros((w_in.shape[0], Z_GATE - Z_KR - 2 * MLA_ROPE), BF16)
    main = jnp.concatenate([na, hy, dnqkv, dngate, cq, ckv, kr, _rotate_half_cols(kr), pad, 0.5 * gates], axis=1)
    ab = dnab.reshape(-1, 2, 2, DN_HEADS)
    ab = jnp.transpose(ab, (0, 3, 1, 2)).reshape(-1, DN_HEADS, 4)
    ab = jnp.pad(ab, ((0, 0), (0, 0), (0, LANES - 4))).reshape(-1, DN_HEADS * LANES)
    return main, ab.astype(BF16)


def _prep_w_uq(w_uq):
    k = w_uq.shape[0]
    w = w_uq.reshape(k, MLA_HEADS, MLA_NOPE + MLA_ROPE)
    rope_w = w[:, :, MLA_NOPE:]
    return jnp.concatenate([w, _rotate_half_cols(rope_w)], axis=-1).reshape(k, MLA_HEADS * MLA_HW).astype(BF16)


def _head_lane_vec(p):
    v = jnp.zeros((DN_HEADS, 1, LANES), F32)
    v = v.at[:, 0, 0].set(p[0].astype(F32))
    return v.at[:, 0, 2].set(p[1].astype(F32))


def _rope_table(l):
    half = MLA_ROPE // 2
    inv = ROPE_THETA ** (-jnp.arange(half, dtype=F32) / half)
    ang = jnp.arange(l, dtype=F32)[:, None] * inv[None, :]
    cos, sin = jnp.cos(ang), jnp.sin(ang)
    return jnp.concatenate([cos, cos, sin, sin], axis=-1)


def _split_const(x):
    hi = x.astype(BF16)
    lo = (x - hi.astype(np.float64)).astype(BF16)
    return jnp.asarray(hi), jnp.asarray(lo)


def _pick(n, pref):
    for t in pref:
        if n % t == 0:
            return t
    return n


def trunk(x_parts, l, norm_mix, w_in, na_rpb, hy_short, hy_skip, hy_w1, hy_b1, hy_w2, hy_b2, hy_w3,
          mla_g_q, mla_g_kv, mla_w_uq, mla_w_ukv, dn_conv, dn_a_log, dn_dt_bias, dn_g_norm,
          w_branch, w_out, norm_mlp, w_up, w_down, norm_final):
    d = x_parts[0].shape[1]
    part_rows = tuple(p.shape[0] for p in x_parts)
    m = sum(part_rows)
    b = m // l
    depth = w_in.shape[0]
    common = math.gcd(*part_rows)
    tm_big = _pick(common, (1024, 512, 256, 128))
    tm_mid = _pick(common, (512, 256, 128))
    fa, ga, fb, fbi = _fft_tables(l)
    n1 = 2 * l // FFT_N2
    tables = (jnp.asarray(fa[:, :, :n1 // 2], BF16), jnp.asarray(ga[:, :n1 // 2, :], BF16),
              jnp.asarray(fb, BF16), jnp.asarray(fbi, BF16))
    fa_split, fb_split = _split_const(fa[:, :, :n1 // 2]), _split_const(fb)
    cs_tab = _rope_table(l)
    x = tuple(x_parts)
    for layer in range(depth):
        last = layer == depth - 1
        w_main, w_ab = _prep_w_in(w_in[layer])
        z = norm_matmul(x, norm_mix[layer], w_main, BF16, tm_big, 1024)
        ab = norm_matmul(x, norm_mix[layer], w_ab, F32, tm_big, DN_HEADS * LANES)
        z3 = z.reshape(b, l, Z_COLS)
        br_a = neighbourhood_attention(z3, na_bias_table(na_rpb[layer]), _pick(l // GRID_W, (8, 4, 2, 1)))
        taps = hyena_filters(l, hy_w1[layer], hy_b1[layer], hy_w2[layer], hy_b2[layer], hy_w3[layer])
        kf = filter_spectrum(taps, fa_split, fb_split)
        br_b = hyena_mixer(z3, hy_short[layer], hy_skip[layer], kf, tables)
        br_c = mla_mixer(z, b, l, mla_g_q[layer], mla_g_kv[layer], _prep_w_uq(mla_w_uq[layer]),
                         mla_w_ukv[layer].astype(BF16), cs_tab, tm_mid, _pick(l, (512, 256, 128)))
        br_d = deltanet_mixer(z3, ab.reshape(b, l, -1), dn_conv[layer], _head_lane_vec(dn_a_log[layer]),
                              _head_lane_vec(dn_dt_bias[layer]), dn_g_norm[layer])
        branches = [t.reshape(m, BRANCH_W) for t in (br_a, br_b, br_c, br_d)]
        merged = gated_merge(z, branches, (0.5 * w_branch[layer]).astype(BF16), tm_mid)
        x_mid = matmul_residual(merged, w_out[layer].astype(BF16), x, tm_mid)
        x = tuple(mlp_block(x_mid, norm_mlp[layer], w_up[layer].astype(BF16), w_down[layer].astype(BF16), norm_final,
                            last, tm_mid, 1024, part_rows if last else (m,)))
    return x


def kernel(x_prompt, x_sample, norm_mix, w_in, na_rpb, hy_short, hy_skip, hy_w1, hy_b1, hy_w2, hy_b2, hy_w3,
           mla_g_q, mla_g_kv, mla_w_uq, mla_w_ukv, dn_conv, dn_a_log, dn_dt_bias, dn_g_norm,
           w_branch, w_out, norm_mlp, w_up, w_down, norm_final):
    assert x_prompt.shape[1:] == x_sample.shape[1:]
    l, d = x_prompt.shape[1:]
    y_prompt, y_sample = trunk((x_prompt.reshape(-1, d), x_sample.reshape(-1, d)), l, norm_mix, w_in, na_rpb,
                               hy_short, hy_skip, hy_w1, hy_b1, hy_w2, hy_b2, hy_w3, mla_g_q, mla_g_kv, mla_w_uq,
                               mla_w_ukv, dn_conv, dn_a_log, dn_dt_bias, dn_g_norm, w_branch, w_out, norm_mlp,
                               w_up, w_down, norm_final)
    return (y_prompt.reshape(x_prompt.shape), y_sample.reshape(x_sample.shape))
```

```python
import functools
import math

import jax
import jax.numpy as jnp
import numpy as np
from jax import lax
from jax.experimental import pallas as pl
from jax.experimental.pallas import tpu as pltpu

F32 = jnp.float32
BF16 = jnp.bfloat16
HIGHEST = lax.Precision.HIGHEST

VMEM_LIMIT_BYTES = 56 * 1024 * 1024
LANES = 128

D_MODEL = 2048
RMS_EPS = 1e-6
GRID_W = 64
N_BRANCH = 4
BRANCH_W = 512
NA_HEADS = 8
NA_HEAD_DIM = 64
NA_WIN_R = 8
NA_WIN_C = 16
HY_WIDTH = 512
HY_POS_BANDS = 16
HY_FILT_HIDDEN = 64
HY_FAST_DECAY = 0.3
HY_SLOW_DECAY = 1.5
HY_DECAY_TARGET = 1e-2
MLA_HEADS = 4
MLA_Q_RANK = 512
MLA_KV_RANK = 256
MLA_NOPE = 128
MLA_ROPE = 64
MLA_V = 128
ROPE_THETA = 10000.0
DN_HEADS = 4
DN_DK = 128
DN_DV = 128
DN_CHUNK = 64
DN_GROUP = 8
D_FF = 4 * D_MODEL

Z_NA = 0
Z_HY = 1536
Z_DNQKV = 3072
Z_DNGATE = 4608
Z_CQ = 5120
Z_CKV = 5632
Z_KR = 5888
Z_GATE = 6144
Z_COLS = Z_GATE + N_BRANCH * D_MODEL

FFT_N2 = 128
FFT_UNROLL = 16
FFT_GROUP = 16
FFT_PITCH_X = FFT_N2 + 8
FFT_PITCH_Y = 2 * FFT_N2 + 8


def _params(*sem):
    return pltpu.CompilerParams(dimension_semantics=sem, vmem_limit_bytes=VMEM_LIMIT_BYTES)


def _silu(x):
    return 0.5 * x * (jnp.tanh(0.5 * x) + 1.0)


def _rms_bf16(x, g):
    xf = x.astype(F32)
    y = xf * lax.rsqrt(jnp.mean(xf * xf, axis=-1, keepdims=True) + RMS_EPS)
    return (y * g).astype(BF16)


def _part_tiles(parts, tm):
    assert all(p.shape[0] % tm == 0 for p in parts)
    return tuple(p.shape[0] // tm for p in parts)


def _part_specs(tiles, block):
    specs, start = [], 0
    for count in tiles:
        specs.append(pl.BlockSpec(block, functools.partial(
            lambda i, j, start, count: (jnp.clip(i - start, 0, count - 1), 0), start=start, count=count)))
        start += count
    return specs


def _for_owning_part(tiles, refs, fn):
    if len(refs) == 1:
        fn(refs[0])
        return
    i = pl.program_id(0)
    start = 0
    for ref, count in zip(refs, tiles):
        pl.when((i >= start) & (i < start + count))(functools.partial(fn, ref))
        start += count


def _norm_mm_kernel(*refs, tiles):
    x_refs = refs[:len(tiles)]
    g_ref, w_ref, o_ref, h_ref = refs[len(tiles):]

    @pl.when(pl.program_id(1) == 0)
    def _():
        def norm(x_ref):
            h_ref[...] = _rms_bf16(x_ref[...], g_ref[...])

        _for_owning_part(tiles, x_refs, norm)

    o_ref[...] = jnp.dot(h_ref[...], w_ref[...], preferred_element_type=F32).astype(o_ref.dtype)


def norm_matmul(x_parts, g, w, out_dtype, tm, tn):
    k, n = w.shape
    tiles = _part_tiles(x_parts, tm)
    return pl.pallas_call(
        functools.partial(_norm_mm_kernel, tiles=tiles),
        grid=(sum(tiles), n // tn),
        in_specs=_part_specs(tiles, (tm, k)) + [pl.BlockSpec((1, k), lambda i, j: (0, 0)),
                                                 pl.BlockSpec((k, tn), lambda i, j: (0, j))],
        out_specs=pl.BlockSpec((tm, tn), lambda i, j: (i, j)),
        out_shape=jax.ShapeDtypeStruct((sum(tiles) * tm, n), out_dtype),
        scratch_shapes=[pltpu.VMEM((tm, k), BF16)],
        compiler_params=_params("parallel", "arbitrary"),
        name="norm_matmul",
    )(*x_parts, g.reshape(1, k), w)


def _mm_res_kernel(*refs, tiles):
    a_ref, w_ref = refs[:2]
    r_refs = refs[2:2 + len(tiles)]
    o_ref = refs[-1]
    y = jnp.dot(a_ref[...], w_ref[...], preferred_element_type=F32)

    def add(r_ref):
        o_ref[...] = r_ref[...] + y

    _for_owning_part(tiles, r_refs, add)


def matmul_residual(a, w, r_parts, tm):
    m, k = a.shape
    n = w.shape[1]
    tiles = _part_tiles(r_parts, tm)
    return pl.pallas_call(
        functools.partial(_mm_res_kernel, tiles=tiles),
        grid=(m // tm, 1),
        in_specs=[pl.BlockSpec((tm, k), lambda i, j: (i, 0)),
                  pl.BlockSpec((k, n), lambda i, j: (0, 0))] + _part_specs(tiles, (tm, n)),
        out_specs=pl.BlockSpec((tm, n), lambda i, j: (i, 0)),
        out_shape=jax.ShapeDtypeStruct((m, n), F32),
        compiler_params=_params("parallel", "arbitrary"),
        name="matmul_residual",
    )(a, w, *r_parts)


def _mlp_kernel(x_ref, g_ref, wu_ref, wd_ref, gf_ref, *refs, final_norm, tiles):
    o_refs, h_ref = refs[:-1], refs[-1]
    j = pl.program_id(1)

    def run(o_ref):
        @pl.when(j == 0)
        def _():
            x = x_ref[...]
            h_ref[...] = _rms_bf16(x, g_ref[...])
            o_ref[...] = x

        u = jnp.dot(h_ref[...], wu_ref[...], preferred_element_type=F32)
        a = jnp.square(jnp.maximum(u, 0.0)).astype(BF16)
        o_ref[...] += jnp.dot(a, wd_ref[...], preferred_element_type=F32)

        if final_norm:
            @pl.when(j == pl.num_programs(1) - 1)
            def _():
                y = o_ref[...]
                o_ref[...] = y * lax.rsqrt(jnp.mean(y * y, axis=-1, keepdims=True) + RMS_EPS) * gf_ref[...]

    _for_owning_part(tiles, o_refs, run)


def mlp_block(x, g, w_up, w_down, g_final, final_norm, tm, tf, out_rows):
    m, d = x.shape
    f = w_up.shape[1]
    assert sum(out_rows) == m and all(r % tm == 0 for r in out_rows)
    tiles = tuple(r // tm for r in out_rows)
    return pl.pallas_call(
        functools.partial(_mlp_kernel, final_norm=final_norm, tiles=tiles),
        grid=(m // tm, f // tf),
        in_specs=[pl.BlockSpec((tm, d), lambda i, j: (i, 0)),
                  pl.BlockSpec((1, d), lambda i, j: (0, 0)),
                  pl.BlockSpec((d, tf), lambda i, j: (0, j)),
                  pl.BlockSpec((tf, d), lambda i, j: (j, 0)),
                  pl.BlockSpec((1, d), lambda i, j: (0, 0))],
        out_specs=_part_specs(tiles, (tm, d)),
        out_shape=[jax.ShapeDtypeStruct((r, d), F32) for r in out_rows],
        scratch_shapes=[pltpu.VMEM((tm, d), BF16)],
        compiler_params=_params("parallel" if len(tiles) == 1 else "arbitrary", "arbitrary"),
        name="mlp_block",
    )(x, g.reshape(1, d), w_up, w_down, g_final.reshape(1, d))


def _merge_kernel(gate_ref, ba_ref, bb_ref, bc_ref, bd_ref, wb_ref, o_ref, acc_ref):
    n = pl.program_id(1)

    def contribution(b_ref):
        y = jnp.dot(b_ref[...], wb_ref[0], preferred_element_type=F32)
        return (jnp.tanh(gate_ref[...].astype(F32)) + 1.0) * y

    @pl.when(n == 0)
    def _():
        acc_ref[...] = contribution(ba_ref)

    @pl.when(n == 1)
    def _():
        acc_ref[...] += contribution(bb_ref)

    @pl.when(n == 2)
    def _():
        acc_ref[...] += contribution(bc_ref)

    @pl.when(n == 3)
    def _():
        o_ref[...] = (acc_ref[...] + contribution(bd_ref)).astype(o_ref.dtype)


def gated_merge(z, branches, w_branch, tm):
    m = z.shape[0]
    gate_blk0 = Z_GATE // D_MODEL
    br_spec = pl.BlockSpec((tm, BRANCH_W), lambda i, n: (i, 0))
    return pl.pallas_call(
        _merge_kernel,
        grid=(m // tm, N_BRANCH),
        in_specs=[pl.BlockSpec((tm, D_MODEL), lambda i, n: (i, gate_blk0 + n)),
                  br_spec, br_spec, br_spec, br_spec,
                  pl.BlockSpec((1, BRANCH_W, D_MODEL), lambda i, n: (n, 0, 0))],
        out_specs=pl.BlockSpec((tm, D_MODEL), lambda i, n: (i, 0)),
        out_shape=jax.ShapeDtypeStruct((m, D_MODEL), BF16),
        scratch_shapes=[pltpu.VMEM((tm, D_MODEL), F32)],
        compiler_params=_params("parallel", "arbitrary"),
        name="gated_merge",
    )(z, *branches, w_branch)


NA_ROWS_LOCKSTEP = 8
NA_MASK = -1e30


def na_bias_table(rpb):
    q = np.arange(GRID_W)
    kc = np.arange(GRID_W)
    cs = np.clip(q - NA_WIN_C // 2, 0, GRID_W - NA_WIN_C)
    ok = (kc[None, :] >= cs[:, None]) & (kc[None, :] < cs[:, None] + NA_WIN_C)
    dcol = np.clip(kc[None, :] - q[:, None] + NA_WIN_C - 1, 0, 2 * NA_WIN_C - 2)
    drow = np.arange(NA_WIN_R)[None, :] - np.arange(NA_WIN_R)[:, None] + NA_WIN_R - 1
    t = rpb.astype(F32)[:, drow]
    t = t[:, :, :, dcol]
    t = jnp.transpose(t, (1, 0, 3, 2, 4))
    t = jnp.where(jnp.asarray(ok)[None, None, :, None, :], t, NA_MASK)
    return t.reshape(NA_WIN_R, NA_HEADS, GRID_W, NA_WIN_R * GRID_W)


def _na_kernel(q_ref, k_ref, v_ref, bias_ref, o_ref, *, rows_per_step, rows):
    rblk = pl.program_id(1)
    win = NA_WIN_R * GRID_W
    scale = NA_HEAD_DIM ** -0.5

    pair_w = 2 * NA_HEAD_DIM
    low = lax.broadcasted_iota(jnp.int32, (GRID_W, pair_w), 1) < NA_HEAD_DIM

    def rows_body(it, carry):
        inst = []
        q, kb, vb, off, qstart = [], [], [], [], []
        for t in range(NA_ROWS_LOCKSTEP):
            rl = it * NA_ROWS_LOCKSTEP + t
            r = rblk * rows_per_step + rl
            rs = jnp.clip(r - NA_WIN_R // 2, 0, rows - NA_WIN_R)
            off.append(r - rs)
            qstart.append(pl.multiple_of(rl * GRID_W, GRID_W))
            kstart = pl.multiple_of(rs * GRID_W, GRID_W)
            q.append(q_ref[0, pl.ds(qstart[t], GRID_W), :])
            kb.append(k_ref[0, pl.ds(kstart, win), :])
            vb.append(v_ref[0, pl.ds(kstart, win), :])
            inst += [(t, h) for h in range(NA_HEADS)]

        def pair(x, h):
            return x[:, (h // 2) * pair_w:(h // 2 + 1) * pair_w]

        qm = [jnp.where(low if h % 2 == 0 else ~low, pair(q[t], h), jnp.zeros((), BF16)) for t, h in inst]
        s = [lax.dot_general(qm[i], pair(kb[t], h), (((1,), (1,)), ((), ())), preferred_element_type=F32)
             for i, (t, h) in enumerate(inst)]
        s = [s[i] * scale + bias_ref[off[t], h] for i, (t, h) in enumerate(inst)]
        p = [jnp.exp(x - jnp.max(x, axis=-1, keepdims=True)) for x in s]
        l = [jnp.sum(x, axis=-1, keepdims=True) for x in p]
        o = [jnp.dot(p[i].astype(BF16), pair(vb[t], h), preferred_element_type=F32) / l[i]
             for i, (t, h) in enumerate(inst)]
        for t in range(NA_ROWS_LOCKSTEP):
            base = t * NA_HEADS
            outs = [jnp.where(low, o[base + h], o[base + h + 1]) for h in range(0, NA_HEADS, 2)]
            o_ref[0, pl.ds(qstart[t], GRID_W), :] = jnp.concatenate(outs, axis=1).astype(o_ref.dtype)
        return carry

    lax.fori_loop(0, rows_per_step // NA_ROWS_LOCKSTEP, rows_body, 0)


def neighbourhood_attention(z3, bias, rows_per_step):
    b, l, _ = z3.shape
    rows = l // GRID_W
    w = NA_HEADS * NA_HEAD_DIM
    blk = Z_NA // w
    return pl.pallas_call(
        functools.partial(_na_kernel, rows_per_step=rows_per_step, rows=rows),
        grid=(b, rows // rows_per_step),
        in_specs=[pl.BlockSpec((1, rows_per_step * GRID_W, w), lambda i, r: (i, r, blk)),
                  pl.BlockSpec((1, l, w), lambda i, r: (i, 0, blk + 1)),
                  pl.BlockSpec((1, l, w), lambda i, r: (i, 0, blk + 2)),
                  pl.BlockSpec(bias.shape, lambda i, r: (0, 0, 0, 0))],
        out_specs=pl.BlockSpec((1, rows_per_step * GRID_W, w), lambda i, r: (i, r, 0)),
        out_shape=jax.ShapeDtypeStruct((b, l, w), BF16),
        compiler_params=_params("parallel", "arbitrary"),
        name="neighbourhood_attention",
    )(z3, z3, z3, bias)


def _fft_tables(l):
    n = 2 * l
    n2 = FFT_N2
    n1 = n // n2
    a = np.arange(n1)
    b = np.arange(n2)
    k1 = np.arange(n1)
    t = n2 * a[None, :] + b[:, None]
    th = 2.0 * np.pi * (k1[None, :, None] * t[:, None, :] % n) / n
    fa = np.concatenate([np.cos(th), -np.sin(th)], axis=1)
    th_t = np.transpose(th, (0, 2, 1))
    ga = np.concatenate([np.cos(th_t), -np.sin(th_t)], axis=2)
    ph = 2.0 * np.pi * (np.outer(b, b) % n2) / n2
    cr, ci = np.cos(ph), -np.sin(ph)
    fb = np.block([[cr, -ci], [ci, cr]])
    fbi = np.block([[cr, ci], [-ci, cr]])
    return fa, ga, fb, fbi


def _filter_fft_kernel(hf_ref, hb_ref, fa_hi_ref, fa_lo_ref, fb_hi_ref, fb_lo_ref, o_ref, y_ref, *, n1):
    n2 = FFT_N2
    na = n1 // 2
    inv_n = 1.0 / (n1 * n2)

    def transform(src_ref, combine):
        def step_a(b, carry):
            xb = _split_bf16(src_ref[pl.ds(b, na, stride=n2), :])
            r = _dot_split((fa_hi_ref[b], fa_lo_ref[b]), xb)
            y_ref[pl.ds(b, n1, stride=2 * n2), :] = r[:n1]
            y_ref[pl.ds(b + n2, n1, stride=2 * n2), :] = r[n1:]
            return carry

        lax.fori_loop(0, n2, step_a, 0, unroll=16)

        def step_b(k1, carry):
            r0 = pl.multiple_of(k1 * 2 * n2, 2 * n2)
            x = _dot_split((fb_hi_ref[...], fb_lo_ref[...]), _split_bf16(y_ref[pl.ds(r0, 2 * n2), :]))
            combine(r0, x * inv_n)
            return carry

        lax.fori_loop(0, n1, step_b, 0, unroll=8)

    def store(r0, x):
        o_ref[pl.ds(r0, 2 * n2), :] = x

    def add_conjugate(r0, x):
        o_ref[pl.ds(r0, n2), :] += x[:n2]
        o_ref[pl.ds(r0 + n2, n2), :] -= x[n2:]

    transform(hf_ref, store)
    transform(hb_ref, add_conjugate)


def filter_spectrum(h, fa, fb):
    l = h.shape[0]
    c = h.shape[1] // 2
    n = 2 * l
    n1 = n // FFT_N2
    nch = c // LANES
    return pl.pallas_call(
        functools.partial(_filter_fft_kernel, n1=n1),
        grid=(nch,),
        in_specs=[pl.BlockSpec((l, LANES), lambda i: (0, i)),
                  pl.BlockSpec((l, LANES), lambda i: (0, nch + i)),
                  pl.BlockSpec(fa[0].shape, lambda i: (0, 0, 0)),
                  pl.BlockSpec(fa[1].shape, lambda i: (0, 0, 0)),
                  pl.BlockSpec(fb[0].shape, lambda i: (0, 0)),
                  pl.BlockSpec(fb[1].shape, lambda i: (0, 0))],
        out_specs=pl.BlockSpec((2 * n, LANES), lambda i: (0, i)),
        out_shape=jax.ShapeDtypeStruct((2 * n, c), F32),
        scratch_shapes=[pltpu.VMEM((2 * n, LANES), F32)],
        compiler_params=_params("parallel"),
        name="hyena_filter_spectrum",
    )(h, h, *fa, *fb)


def _hyena_filter_kernel(z_ref, w1_ref, b1_ref, w2_ref, b2_ref, w3_ref, dec_ref, o_ref):
    hid = jnp.sin(jnp.dot(z_ref[...], w1_ref[...], precision=HIGHEST, preferred_element_type=F32) + b1_ref[...])
    hid = jnp.sin(jnp.dot(hid, w2_ref[...], precision=HIGHEST, preferred_element_type=F32) + b2_ref[...])
    h = jnp.dot(hid, w3_ref[...], precision=HIGHEST, preferred_element_type=F32)
    dec = dec_ref[...]
    hf = h[:, :HY_WIDTH] * dec
    hb = h[:, HY_WIDTH:] * dec
    norm = (jnp.sum(jnp.abs(hf), axis=0, keepdims=True) + jnp.sum(jnp.abs(hb), axis=0, keepdims=True)) + RMS_EPS
    o_ref[:, :HY_WIDTH] = hf / norm
    o_ref[:, HY_WIDTH:] = hb / norm


def hyena_filters(l, w1, b1, w2, b2, w3):
    t = jnp.linspace(0.0, 1.0, l, dtype=F32)[:, None]
    w = 2.0 * math.pi * jnp.arange(l, dtype=F32)[:, None] / l
    bands = jnp.linspace(1e-4, HY_POS_BANDS - 1, HY_POS_BANDS, dtype=F32)[None, :]
    z = jnp.concatenate([t, jnp.cos(bands * w), -jnp.sin(bands * w)], axis=-1)
    pad = LANES - z.shape[1]
    z = jnp.pad(z, ((0, 0), (0, pad)))
    w1p = jnp.pad(w1.astype(F32), ((0, pad), (0, 0)))
    max_decay = math.log(HY_DECAY_TARGET) / HY_FAST_DECAY
    min_decay = math.log(HY_DECAY_TARGET) / HY_SLOW_DECAY
    deltas = jnp.abs(jnp.linspace(min_decay, max_decay, HY_WIDTH, dtype=F32))
    dec = jnp.exp(-t * deltas[None, :])
    return pl.pallas_call(
        _hyena_filter_kernel,
        out_shape=jax.ShapeDtypeStruct((l, 2 * HY_WIDTH), F32),
        compiler_params=pltpu.CompilerParams(vmem_limit_bytes=VMEM_LIMIT_BYTES),
        name="hyena_filter_ffn",
    )(z, w1p, b1.reshape(1, -1).astype(F32), w2.astype(F32), b2.reshape(1, -1).astype(F32), w3.astype(F32), dec)


CONV_PAD = 8


def _dwconv3(x, w_ref, pad_ref):
    n = x.shape[0]
    zeros = jnp.zeros((CONV_PAD, x.shape[1]), F32)
    pad_ref[0:CONV_PAD, :] = zeros
    pad_ref[CONV_PAD + n:2 * CONV_PAD + n, :] = zeros
    pad_ref[CONV_PAD:CONV_PAD + n, :] = x
    w = w_ref[...].astype(F32)
    return (pad_ref[CONV_PAD - 1:CONV_PAD - 1 + n, :] * w[0:1] + x * w[1:2]
            + pad_ref[CONV_PAD + 1:CONV_PAD + 1 + n, :] * w[2:3])


def _hyena_kernel(x1_ref, x2_ref, v_ref, w1_ref, w2_ref, wv_ref, skip_ref, kf_ref, fa_ref, ga_ref, fb_ref, fbi_ref,
                  o_ref, xs_ref, y_ref, pad_ref, *, n1):
    n2 = FFT_N2
    na = n1 // 2
    pa, py = FFT_PITCH_X, FFT_PITCH_Y
    vg = _dwconv3(v_ref[0].astype(F32), wv_ref, pad_ref) * _dwconv3(x1_ref[0].astype(F32), w1_ref, pad_ref)
    for a in range(na):
        xs_ref[a * pa:a * pa + n2, :] = vg[a * n2:(a + 1) * n2]

    def fwd_a(b, carry):
        xb = xs_ref[pl.ds(b, na, stride=pa), :].astype(BF16)
        r = jnp.dot(fa_ref[b], xb, preferred_element_type=F32)
        y_ref[pl.ds(b, n1, stride=py), :] = r[:n1]
        y_ref[pl.ds(b + n2, n1, stride=py), :] = r[n1:]
        return carry

    lax.fori_loop(0, n2, fwd_a, 0, unroll=FFT_UNROLL)

    def mid(kg, carry):
        grp = range(FFT_GROUP)
        k1 = [kg * FFT_GROUP + t for t in grp]
        r0 = [pl.multiple_of(k * py, 8) for k in k1]
        f0 = [pl.multiple_of(k * 2 * n2, 2 * n2) for k in k1]
        fb = fb_ref[...]
        x = [jnp.dot(fb, y_ref[pl.ds(r0[t], 2 * n2), :].astype(BF16), preferred_element_type=F32) for t in grp]
        p = []
        for t in grp:
            xr, xi = x[t][:n2], x[t][n2:]
            kr = kf_ref[pl.ds(f0[t], n2), :]
            ki = kf_ref[pl.ds(f0[t] + n2, n2), :]
            p.append(jnp.concatenate([xr * kr - xi * ki, xr * ki + xi * kr], axis=0).astype(BF16))
        fbi = fbi_ref[...]
        q = [jnp.dot(fbi, p[t], preferred_element_type=F32) for t in grp]
        for t in grp:
            y_ref[pl.ds(r0[t], 2 * n2), :] = q[t]
        return carry

    lax.fori_loop(0, n1 // FFT_GROUP, mid, 0)

    def inv_a(b, carry):
        qb = jnp.concatenate([y_ref[pl.ds(b, n1, stride=py), :],
                              y_ref[pl.ds(b + n2, n1, stride=py), :]], axis=0).astype(BF16)
        xs_ref[pl.ds(b, na, stride=pa), :] = jnp.dot(ga_ref[b], qb, preferred_element_type=F32)
        return carry

    lax.fori_loop(0, n2, inv_a, 0, unroll=FFT_UNROLL)

    gate = _dwconv3(x2_ref[0].astype(F32), w2_ref, pad_ref)
    skip = skip_ref[...].astype(F32)
    for a in range(na):
        rows = slice(a * n2, (a + 1) * n2)
        o_ref[0, rows, :] = ((xs_ref[a * pa:a * pa + n2, :] + vg[rows] * skip) * gate[rows]).astype(o_ref.dtype)


def hyena_mixer(z3, w_short, skip, kf, tables):
    b, l, _ = z3.shape
    n1 = 2 * l // FFT_N2
    fa, ga, fb, fbi = tables
    nch = HY_WIDTH // LANES
    blk = Z_HY // LANES

    def zspec(seg):
        return pl.BlockSpec((1, l, LANES), lambda c, i: (i, 0, blk + seg * nch + c))

    def wspec(seg):
        return pl.BlockSpec((3, LANES), lambda c, i: (0, seg * nch + c))

    return pl.pallas_call(
        functools.partial(_hyena_kernel, n1=n1),
        grid=(nch, b),
        in_specs=[zspec(0), zspec(1), zspec(2), wspec(0), wspec(1), wspec(2),
                  pl.BlockSpec((1, LANES), lambda c, i: (0, c)),
                  pl.BlockSpec((4 * l, LANES), lambda c, i: (0, c)),
                  pl.BlockSpec(fa.shape, lambda c, i: (0, 0, 0)),
                  pl.BlockSpec(ga.shape, lambda c, i: (0, 0, 0)),
                  pl.BlockSpec(fb.shape, lambda c, i: (0, 0)),
                  pl.BlockSpec(fbi.shape, lambda c, i: (0, 0))],
        out_specs=pl.BlockSpec((1, l, LANES), lambda c, i: (i, 0, c)),
        out_shape=jax.ShapeDtypeStruct((b, l, HY_WIDTH), BF16),
        scratch_shapes=[pltpu.VMEM((n1 // 2 * FFT_PITCH_X, LANES), F32), pltpu.VMEM((n1 * FFT_PITCH_Y, LANES), F32),
                        pltpu.VMEM((l + 2 * CONV_PAD, LANES), F32)],
        compiler_params=_params("parallel", "arbitrary"),
        name="hyena_mixer",
    )(z3, z3, z3, w_short, w_short, w_short, skip.reshape(1, -1), kf, fa, ga, fb, fbi)


MLA_HW = 2 * LANES
MLA_HEADS_PER_STEP = 2


def _rope_group(g, cs):
    prod = g * cs
    s = prod + pltpu.roll(prod, MLA_ROPE, 1)
    lane = lax.broadcasted_iota(jnp.int32, s.shape, 1)
    return jnp.where(lane < MLA_ROPE, s, 0.0)


def _mla_prep_kernel(cq_ref, ckv_ref, kr_ref, gq_ref, gkv_ref, wq_ref, wkv_ref, cs_ref, q_ref, k_ref, v_ref):
    cs = cs_ref[...]
    hq = _rms_bf16(cq_ref[...], gq_ref[...])
    q = jnp.dot(hq, wq_ref[...], preferred_element_type=F32)
    hkv = _rms_bf16(ckv_ref[...], gkv_ref[...])
    kv = jnp.dot(hkv, wkv_ref[...], preferred_element_type=F32)
    k_rope = _rope_group(kr_ref[...].astype(F32), cs).astype(BF16)
    lane = lax.broadcasted_iota(jnp.int32, (cs.shape[0], LANES), 1)
    ones_lane = jnp.where(lane == 0, 1.0, 0.0).astype(BF16)
    for h in range(MLA_HEADS):
        o = h * MLA_HW
        q_ref[:, o:o + LANES] = q[:, o:o + LANES].astype(BF16)
        q_ref[:, o + LANES:o + MLA_HW] = _rope_group(q[:, o + LANES:o + MLA_HW], cs).astype(BF16)
        k_ref[:, o:o + LANES] = kv[:, o:o + LANES].astype(BF16)
        k_ref[:, o + LANES:o + MLA_HW] = k_rope
        v_ref[:, o:o + LANES] = kv[:, o + LANES:o + MLA_HW].astype(BF16)
        v_ref[:, o + LANES:o + MLA_HW] = ones_lane


def _mla_attn_kernel(q_ref, k_ref, v_ref, o_ref):
    c = (MLA_NOPE + MLA_ROPE) ** -0.5 * math.log2(math.e)
    heads = range(MLA_HEADS_PER_STEP)
    s = [lax.dot_general(q_ref[0, :, h * MLA_HW:(h + 1) * MLA_HW], k_ref[0, :, h * MLA_HW:(h + 1) * MLA_HW],
                         (((1,), (1,)), ((), ())), preferred_element_type=F32) for h in heads]
    for h in heads:
        p = jnp.exp2((s[h] - jnp.max(s[h], axis=-1, keepdims=True)) * c)
        o = jnp.dot(p.astype(BF16), v_ref[0, :, h * MLA_HW:(h + 1) * MLA_HW], preferred_element_type=F32)
        o_ref[0, :, h * MLA_V:(h + 1) * MLA_V] = (o[:, :MLA_V] / o[:, MLA_V:MLA_V + 1]).astype(o_ref.dtype)


def mla_mixer(z, b, l, g_q, g_kv, wq_p, w_ukv, cs_tab, tm, tq):
    m = z.shape[0]
    lt = l // tm
    qp, kp, vp = pl.pallas_call(
        _mla_prep_kernel,
        grid=(m // tm,),
        in_specs=[pl.BlockSpec((tm, MLA_Q_RANK), lambda i: (i, Z_CQ // MLA_Q_RANK)),
                  pl.BlockSpec((tm, MLA_KV_RANK), lambda i: (i, Z_CKV // MLA_KV_RANK)),
                  pl.BlockSpec((tm, LANES), lambda i: (i, Z_KR // LANES)),
                  pl.BlockSpec((1, MLA_Q_RANK), lambda i: (0, 0)),
                  pl.BlockSpec((1, MLA_KV_RANK), lambda i: (0, 0)),
                  pl.BlockSpec(wq_p.shape, lambda i: (0, 0)),
                  pl.BlockSpec(w_ukv.shape, lambda i: (0, 0)),
                  pl.BlockSpec((tm, LANES), lambda i: (i % lt, 0))],
        out_specs=[pl.BlockSpec((tm, MLA_HEADS * MLA_HW), lambda i: (i, 0)),
                   pl.BlockSpec((tm, MLA_HEADS * MLA_HW), lambda i: (i, 0)),
                   pl.BlockSpec((tm, MLA_HEADS * MLA_HW), lambda i: (i, 0))],
        out_shape=[jax.ShapeDtypeStruct((m, MLA_HEADS * MLA_HW), BF16),
                   jax.ShapeDtypeStruct((m, MLA_HEADS * MLA_HW), BF16),
                   jax.ShapeDtypeStruct((m, MLA_HEADS * MLA_HW), BF16)],
        compiler_params=_params("parallel"),
        name="mla_prep",
    )(z, z, z, g_q.reshape(1, -1), g_kv.reshape(1, -1), wq_p, w_ukv, cs_tab)
    qp = qp.reshape(b, l, -1)
    kp = kp.reshape(b, l, -1)
    vp = vp.reshape(b, l, -1)
    hs = MLA_HEADS_PER_STEP
    return pl.pallas_call(
        _mla_attn_kernel,
        grid=(b, MLA_HEADS // hs, l // tq),
        in_specs=[pl.BlockSpec((1, tq, hs * MLA_HW), lambda i, h, t: (i, t, h)),
                  pl.BlockSpec((1, l, hs * MLA_HW), lambda i, h, t: (i, 0, h)),
                  pl.BlockSpec((1, l, hs * MLA_HW), lambda i, h, t: (i, 0, h))],
        out_specs=pl.BlockSpec((1, tq, hs * MLA_V), lambda i, h, t: (i, t, h)),
        out_shape=jax.ShapeDtypeStruct((b, l, MLA_HEADS * MLA_V), BF16),
        compiler_params=_params("parallel", "parallel", "arbitrary"),
        name="mla_attention",
    )(qp, kp, vp)


def _split_bf16(x):
    hi = x.astype(BF16)
    return hi, (x - hi.astype(F32)).astype(BF16)


def _dot_split(a, b):
    (ah, al), (bh, bl) = a, b
    return (jnp.dot(ah, bh, preferred_element_type=F32)
            + (jnp.dot(ah, bl, preferred_element_type=F32) + jnp.dot(al, bh, preferred_element_type=F32)))


def _tri_unit_inverse(mats):
    c = mats[0].shape[0]
    eye = (lax.broadcasted_iota(jnp.int32, (c, c), 0) == lax.broadcasted_iota(jnp.int32, (c, c), 1)).astype(F32)
    ps = [eye - a for a in mats]
    ms = list(mats)
    for level in range(int(math.log2(c)) - 1):
        if level == 0:
            splits = [_split_bf16(m) for m in ms]
            ms = [_dot_split(sp, sp) for sp in splits]
            yield
            ps = [_dot_split(_split_bf16(p), _split_bf16(eye + m)) for p, m in zip(ps, ms)]
        else:
            hs = [m.astype(BF16) for m in ms]
            ms = [jnp.dot(h, h, preferred_element_type=F32) for h in hs]
            yield
            fs = [(eye + m).astype(BF16) for m in ms]
            ps = [jnp.dot(ph, f, preferred_element_type=F32) + jnp.dot(pl_, f, preferred_element_type=F32)
                  for (ph, pl_), f in zip([_split_bf16(p) for p in ps], fs)]
        yield
    return ps


def _interleave(main, side=None):
    gens = [g for g in (main, side) if g is not None]
    results = [None] * len(gens)
    live = [True] * len(gens)
    while any(live):
        for idx, gen in enumerate(gens):
            if live[idx]:
                try:
                    next(gen)
                except StopIteration as stop:
                    results[idx] = stop.value
                    live[idx] = False
    return results


def _bdot(a, b):
    return jnp.dot(a.astype(BF16), b.astype(BF16), preferred_element_type=F32)


def _bdot_nt(a, b):
    return lax.dot_general(a.astype(BF16), b.astype(BF16), (((1,), (1,)), ((), ())), preferred_element_type=F32)


def _dn_group_setup(chunks, refs):
    q_s, k_s, v_s, g_s, b_s = refs
    cc = DN_CHUNK
    n = range(len(chunks))
    rev = [r for _, r in chunks]
    r0 = [pl.multiple_of(c * cc, cc) for c, _ in chunks]
    q = [q_s[pl.ds(r, cc), :] for r in r0]
    k = [k_s[pl.ds(r, cc), :] for r in r0]
    v = [v_s[pl.ds(r, cc), :] for r in r0]
    lane = [2 if r else 0 for r in rev]
    ri = lax.broadcasted_iota(jnp.int32, (cc, cc), 0)
    ci = lax.broadcasted_iota(jnp.int32, (cc, cc), 1)
    incl = [(ri <= ci) if r else (ri >= ci) for r in rev]
    strict = [(ri < ci) if r else (ri > ci) for r in rev]
    g = [g_s[pl.ds(r, cc), :] for r in r0]
    g1 = [x.astype(BF16) for x in g]
    r1 = [x - h.astype(F32) for x, h in zip(g, g1)]
    g2 = [x.astype(BF16) for x in r1]
    g3 = [(x - h.astype(F32)).astype(BF16) for x, h in zip(r1, g2)]
    ones = [m.astype(BF16) for m in incl]
    gc = [jnp.dot(ones[i], g1[i], preferred_element_type=F32) for i in n]
    gc = [gc[i] + jnp.dot(ones[i], g2[i], preferred_element_type=F32) for i in n]
    gc = [gc[i] + jnp.dot(ones[i], g3[i], preferred_element_type=F32) for i in n]
    yield
    gcol = [gc[i][:, lane[i]:lane[i] + 1] for i in n]
    grow = [jnp.transpose(gc[i])[lane[i]:lane[i] + 1, :] for i in n]
    beta = [b_s[pl.ds(r0[i], cc), :][:, lane[i] + 1:lane[i] + 2] for i in n]
    g_last = [gcol[i][0:1, :] if rev[i] else gcol[i][cc - 1:cc, :] for i in n]
    decay = [jnp.where(incl[i], jnp.exp(jnp.where(incl[i], gcol[i] - grow[i], 0.0)), 0.0) for i in n]
    e_g = [jnp.exp(x) for x in gcol]
    kb = [k[i] * beta[i] for i in n]
    a_both = [_bdot_nt(jnp.concatenate([kb[i], q[i]], axis=0), k[i]) for i in n]
    yield
    a_kk = [jnp.where(strict[i], a_both[i][:cc] * decay[i], 0.0) for i in n]
    a_qk = [jnp.where(incl[i], a_both[i][cc:] * decay[i], 0.0) for i in n]
    t_inv = yield from _tri_unit_inverse(a_kk)
    uw = [_bdot(t_inv[i], jnp.concatenate([v[i] * beta[i], kb[i] * e_g[i]], axis=1)) for i in n]
    yield
    k_dec = [k[i] * jnp.exp(g_last[i] - gcol[i]) for i in n]
    wq = [jnp.concatenate([uw[i][:, DN_DV:], q[i] * e_g[i]], axis=0).astype(BF16) for i in n]
    ak = [jnp.concatenate([a_qk[i], jnp.transpose(k_dec[i])], axis=0).astype(BF16) for i in n]
    return [(wq[i], ak[i], uw[i][:, :DN_DV], jnp.exp(g_last[i])) for i in n]


def _dn_group_steps(states, setups, out_refs, chunks):
    cc = DN_CHUNK
    dirs = range(len(states))
    for t in range(len(setups[0])):
        st = [setups[d][t] for d in dirs]
        r = [jnp.dot(st[d][0], states[d].astype(BF16), preferred_element_type=F32) for d in dirs]
        yield
        v_new = [st[d][2] - r[d][:cc] for d in dirs]
        r2 = [jnp.dot(st[d][1], v_new[d].astype(BF16), preferred_element_type=F32) for d in dirs]
        yield
        states = [states[d] * st[d][3] + r2[d][cc:] for d in dirs]
        for d in dirs:
            out_refs[d][pl.ds(pl.multiple_of(chunks[d][t] * cc, cc), cc), :] = r[d][cc:] + r2[d][:cc]
    return states


def _dn_kernel(zq_ref, zk_ref, zv_ref, zg_ref, ab_ref, wq_ref, wk_ref, wv_ref, alog_ref, dtb_ref, gn_ref, o_ref,
               q_s, k_s, v_s, g_s, b_s, of_s, ob_s, pad_ref, *, nchunks):
    def act(z_ref, w_ref):
        c = _dwconv3(z_ref[0].astype(F32), w_ref, pad_ref)
        return _silu(c)

    def l2n(x):
        return x * lax.rsqrt(jnp.sum(x * x, axis=-1, keepdims=True) + RMS_EPS)

    q_s[...] = l2n(act(zq_ref, wq_ref)) * (DN_DK ** -0.5)
    k_s[...] = l2n(act(zk_ref, wk_ref))
    v_s[...] = act(zv_ref, wv_ref)
    ab = ab_ref[0]
    x = ab + dtb_ref[0]
    softplus = jnp.maximum(x, 0.0) + jnp.log1p(jnp.exp(-jnp.abs(x)))
    g_s[...] = -jnp.exp(alog_ref[0]) * softplus
    b_s[...] = jax.nn.sigmoid(ab)
    refs = (q_s, k_s, v_s, g_s, b_s)

    ngroups = nchunks // DN_GROUP

    def group_chunks(gi):
        fwd = [gi * DN_GROUP + t for t in range(DN_GROUP)]
        return fwd, [nchunks - 1 - c for c in fwd]

    def setup_gen(gi):
        fwd, bwd = group_chunks(gi)
        return _dn_group_setup([(c, False) for c in fwd] + [(c, True) for c in bwd], refs)

    def steps_gen(gi, states, flat):
        setups = [tuple(flat[4 * i:4 * i + 4]) for i in range(2 * DN_GROUP)]
        return _dn_group_steps(states, [setups[:DN_GROUP], setups[DN_GROUP:]], (of_s, ob_s), group_chunks(gi))

    def flatten(setups):
        return [x for st in setups for x in st]

    def body(gi, carry):
        nxt, states = _interleave(setup_gen(gi + 1), steps_gen(gi, list(carry[:2]), carry[2:]))
        return (*states, *flatten(nxt))

    zero = jnp.zeros((DN_DK, DN_DV), F32)
    (first,) = _interleave(setup_gen(0))
    carry = lax.fori_loop(0, ngroups - 1, body, (zero, zero, *flatten(first)))
    _interleave(steps_gen(ngroups - 1, list(carry[:2]), carry[2:]))

    o = of_s[...] + ob_s[...]
    o = o * lax.rsqrt(jnp.mean(o * o, axis=-1, keepdims=True) + RMS_EPS) * gn_ref[...]
    gate = zg_ref[0].astype(F32)
    o_ref[0] = (o * _silu(gate)).astype(o_ref.dtype)


def deltanet_mixer(z3, ab3, w_conv, alog_p, dtb_p, g_norm):
    b, l, _ = z3.shape
    blk = Z_DNQKV // LANES
    gblk = Z_DNGATE // LANES
    hh = DN_HEADS

    def zspec(off):
        return pl.BlockSpec((1, l, LANES), lambda i, h: (i, 0, off + h))

    def wspec(seg):
        return pl.BlockSpec((3, LANES), lambda i, h: (0, seg * hh + h))

    vec = pl.BlockSpec((1, 1, LANES), lambda i, h: (h, 0, 0))
    seq = pltpu.VMEM((l, LANES), F32)
    return pl.pallas_call(
        functools.partial(_dn_kernel, nchunks=l // DN_CHUNK),
        grid=(b, hh),
        in_specs=[zspec(blk), zspec(blk + hh), zspec(blk + 2 * hh), zspec(gblk),
                  pl.BlockSpec((1, l, LANES), lambda i, h: (i, 0, h)),
                  wspec(0), wspec(1), wspec(2), vec, vec,
                  pl.BlockSpec((1, LANES), lambda i, h: (0, 0))],
        out_specs=pl.BlockSpec((1, l, LANES), lambda i, h: (i, 0, h)),
        out_shape=jax.ShapeDtypeStruct((b, l, hh * DN_DV), BF16),
        scratch_shapes=[seq] * 7 + [pltpu.VMEM((l + 2 * CONV_PAD, LANES), F32)],
        compiler_params=_params("parallel", "arbitrary"),
        name="deltanet_mixer",
    )(z3, z3, z3, z3, ab3, w_conv, w_conv, w_conv, alog_p, dtb_p, g_norm.reshape(1, -1))


def _rotate_half_cols(w):
    half = w.shape[-1] // 2
    return jnp.concatenate([-w[..., half:], w[..., :half]], axis=-1)


def _prep_w_in(w_in):
    cuts = np.cumsum([0, 1536, 1536, MLA_Q_RANK, MLA_KV_RANK, MLA_ROPE, 1536, 512, 16])
    dnab = w_in[:, cuts[7]:cuts[8]]
    w16 = w_in.astype(BF16)
    na, hy, cq, ckv, kr, dnqkv, dngate = (w16[:, cuts[i]:cuts[i + 1]] for i in range(7))
    gates = w16[:, cuts[8]:]
    pad = jnp.zeros((w_in.shape[0], Z_GATE - Z_KR - 2 * MLA_ROPE), BF16)
    main = jnp.concatenate([na, hy, dnqkv, dngate, cq, ckv, kr, _rotate_half_cols(kr), pad, 0.5 * gates], axis=1)
    ab = dnab.reshape(-1, 2, 2, DN_HEADS)
    ab = jnp.transpose(ab, (0, 3, 1, 2)).reshape(-1, DN_HEADS, 4)
    ab = jnp.pad(ab, ((0, 0), (0, 0), (0, LANES - 4))).reshape(-1, DN_HEADS * LANES)
    return main, ab.astype(BF16)


def _prep_w_uq(w_uq):
    k = w_uq.shape[0]
    w = w_uq.reshape(k, MLA_HEADS, MLA_NOPE + MLA_ROPE)
    rope_w = w[:, :, MLA_NOPE:]
    return jnp.concatenate([w, _rotate_half_cols(rope_w)], axis=-1).reshape(k, MLA_HEADS * MLA_HW).astype(BF16)


def _head_lane_vec(p):
    v = jnp.zeros((DN_HEADS, 1, LANES), F32)
    v = v.at[:, 0, 0].set(p[0].astype(F32))
    return v.at[:, 0, 2].set(p[1].astype(F32))


def _rope_table(l):
    half = MLA_ROPE // 2
    inv = ROPE_THETA ** (-jnp.arange(half, dtype=F32) / half)
    ang = jnp.arange(l, dtype=F32)[:, None] * inv[None, :]
    cos, sin = jnp.cos(ang), jnp.sin(ang)
    return jnp.concatenate([cos, cos, sin, sin], axis=-1)


def _split_const(x):
    hi = x.astype(BF16)
    lo = (x - hi.astype(np.float64)).astype(BF16)
    return jnp.asarray(hi), jnp.asarray(lo)


def _pick(n, pref):
    for t in pref:
        if n % t == 0:
            return t
    return n


def trunk(x_parts, l, norm_mix, w_in, na_rpb, hy_short, hy_skip, hy_w1, hy_b1, hy_w2, hy_b2, hy_w3,
          mla_g_q, mla_g_kv, mla_w_uq, mla_w_ukv, dn_conv, dn_a_log, dn_dt_bias, dn_g_norm,
          w_branch, w_out, norm_mlp, w_up, w_down, norm_final):
    d = x_parts[0].shape[1]
    part_rows = tuple(p.shape[0] for p in x_parts)
    m = sum(part_rows)
    b = m // l
    depth = w_in.shape[0]
    common = math.gcd(*part_rows)
    tm_big = _pick(common, (1024, 512, 256, 128))
    tm_mid = _pick(common, (512, 256, 128))
    fa, ga, fb, fbi = _fft_tables(l)
    n1 = 2 * l // FFT_N2
    tables = (jnp.asarray(fa[:, :, :n1 // 2], BF16), jnp.asarray(ga[:, :n1 // 2, :], BF16),
              jnp.asarray(fb, BF16), jnp.asarray(fbi, BF16))
    fa_split, fb_split = _split_const(fa[:, :, :n1 // 2]), _split_const(fb)
    cs_tab = _rope_table(l)
    x = tuple(x_parts)
    for layer in range(depth):
        last = layer == depth - 1
        w_main, w_ab = _prep_w_in(w_in[layer])
        z = norm_matmul(x, norm_mix[layer], w_main, BF16, tm_big, 2048 if len(x) == 1 else 1024)
        ab = norm_matmul(x, norm_mix[layer], w_ab, F32, tm_big, DN_HEADS * LANES)
        z3 = z.reshape(b, l, Z_COLS)
        br_a = neighbourhood_attention(z3, na_bias_table(na_rpb[layer]), _pick(l // GRID_W, (8, 4, 2, 1)))
        taps = hyena_filters(l, hy_w1[layer], hy_b1[layer], hy_w2[layer], hy_b2[layer], hy_w3[layer])
        kf = filter_spectrum(taps, fa_split, fb_split)
        br_b = hyena_mixer(z3, hy_short[layer], hy_skip[layer], kf, tables)
        br_c = mla_mixer(z, b, l, mla_g_q[layer], mla_g_kv[layer], _prep_w_uq(mla_w_uq[layer]),
                         mla_w_ukv[layer].astype(BF16), cs_tab, tm_mid, _pick(l, (512, 256, 128)))
        br_d = deltanet_mixer(z3, ab.reshape(b, l, -1), dn_conv[layer], _head_lane_vec(dn_a_log[layer]),
                              _head_lane_vec(dn_dt_bias[layer]), dn_g_norm[layer])
        branches = [t.reshape(m, BRANCH_W) for t in (br_a, br_b, br_c, br_d)]
        merged = gated_merge(z, branches, (0.5 * w_branch[layer]).astype(BF16), tm_mid)
        x_mid = matmul_residual(merged, w_out[layer].astype(BF16), x, tm_mid)
        x = tuple(mlp_block(x_mid, norm_mlp[layer], w_up[layer].astype(BF16), w_down[layer].astype(BF16), norm_final,
                            last, tm_mid, 1024, part_rows if last else (m,)))
    return x


def kernel(x_prompt, x_sample, norm_mix, w_in, na_rpb, hy_short, hy_skip, hy_w1, hy_b1, hy_w2, hy_b2, hy_w3,
           mla_g_q, mla_g_kv, mla_w_uq, mla_w_ukv, dn_conv, dn_a_log, dn_dt_bias, dn_g_norm,
           w_branch, w_out, norm_mlp, w_up, w_down, norm_final):
    assert x_prompt.shape[1:] == x_sample.shape[1:]
    l, d = x_prompt.shape[1:]
    y_prompt, y_sample = trunk((x_prompt.reshape(-1, d), x_sample.reshape(-1, d)), l, norm_mix, w_in, na_rpb,
                               hy_short, hy_skip, hy_w1, hy_b1, hy_w2, hy_b2, hy_w3, mla_g_q, mla_g_kv, mla_w_uq,
                               mla_w_ukv, dn_conv, dn_a_log, dn_dt_bias, dn_g_norm, w_branch, w_out, norm_mlp,
                               w_up, w_down, norm_final)
    return (y_prompt.reshape(x_prompt.shape), y_sample.reshape(x_sample.shape))
```

```python
import functools
import math

import jax
import jax.numpy as jnp
import numpy as np
from jax import lax
from jax.experimental import pallas as pl
from jax.experimental.pallas import tpu as pltpu

F32 = jnp.float32
BF16 = jnp.bfloat16
HIGHEST = lax.Precision.HIGHEST

VMEM_LIMIT_BYTES = 56 * 1024 * 1024
LANES = 128

D_MODEL = 2048
RMS_EPS = 1e-6
GRID_W = 64
N_BRANCH = 4
BRANCH_W = 512
NA_HEADS = 8
NA_HEAD_DIM = 64
NA_WIN_R = 8
NA_WIN_C = 16
HY_WIDTH = 512
HY_POS_BANDS = 16
HY_FILT_HIDDEN = 64
HY_FAST_DECAY = 0.3
HY_SLOW_DECAY = 1.5
HY_DECAY_TARGET = 1e-2
MLA_HEADS = 4
MLA_Q_RANK = 512
MLA_KV_RANK = 256
MLA_NOPE = 128
MLA_ROPE = 64
MLA_V = 128
ROPE_THETA = 10000.0
DN_HEADS = 4
DN_DK = 128
DN_DV = 128
DN_CHUNK = 64
DN_GROUP = 8
D_FF = 4 * D_MODEL

Z_NA = 0
Z_HY = 1536
Z_DNQKV = 3072
Z_DNGATE = 4608
Z_CQ = 5120
Z_CKV = 5632
Z_KR = 5888
Z_GATE = 6144
Z_COLS = Z_GATE + N_BRANCH * D_MODEL

FFT_N2 = 128
FFT_UNROLL = 16
FFT_GROUP = 16
FFT_PITCH_X = FFT_N2 + 8
FFT_PITCH_Y = 2 * FFT_N2 + 8


def _params(*sem):
    return pltpu.CompilerParams(dimension_semantics=sem, vmem_limit_bytes=VMEM_LIMIT_BYTES)


def _silu(x):
    return 0.5 * x * (jnp.tanh(0.5 * x) + 1.0)


def _rms_bf16(x, g):
    xf = x.astype(F32)
    y = xf * lax.rsqrt(jnp.mean(xf * xf, axis=-1, keepdims=True) + RMS_EPS)
    return (y * g).astype(BF16)


def _part_tiles(parts, tm):
    assert all(p.shape[0] % tm == 0 for p in parts)
    return tuple(p.shape[0] // tm for p in parts)


def _part_specs(tiles, block):
    specs, start = [], 0
    for count in tiles:
        specs.append(pl.BlockSpec(block, functools.partial(
            lambda i, j, start, count: (jnp.clip(i - start, 0, count - 1), 0), start=start, count=count)))
        start += count
    return specs


def _for_owning_part(tiles, refs, fn):
    if len(refs) == 1:
        fn(refs[0])
        return
    i = pl.program_id(0)
    start = 0
    for ref, count in zip(refs, tiles):
        pl.when((i >= start) & (i < start + count))(functools.partial(fn, ref))
        start += count


def _norm_mm_kernel(*refs, tiles):
    x_refs = refs[:len(tiles)]
    g_ref, w_ref, o_ref, h_ref = refs[len(tiles):]

    @pl.when(pl.program_id(1) == 0)
    def _():
        def norm(x_ref):
            h_ref[...] = _rms_bf16(x_ref[...], g_ref[...])

        _for_owning_part(tiles, x_refs, norm)

    o_ref[...] = jnp.dot(h_ref[...], w_ref[...], preferred_element_type=F32).astype(o_ref.dtype)


def norm_matmul(x_parts, g, w, out_dtype, tm, tn):
    k, n = w.shape
    tiles = _part_tiles(x_parts, tm)
    return pl.pallas_call(
        functools.partial(_norm_mm_kernel, tiles=tiles),
        grid=(sum(tiles), n // tn),
        in_specs=_part_specs(tiles, (tm, k)) + [pl.BlockSpec((1, k), lambda i, j: (0, 0)),
                                                 pl.BlockSpec((k, tn), lambda i, j: (0, j))],
        out_specs=pl.BlockSpec((tm, tn), lambda i, j: (i, j)),
        out_shape=jax.ShapeDtypeStruct((sum(tiles) * tm, n), out_dtype),
        scratch_shapes=[pltpu.VMEM((tm, k), BF16)],
        compiler_params=_params("parallel", "arbitrary"),
        name="norm_matmul",
    )(*x_parts, g.reshape(1, k), w)


def _mm_res_kernel(*refs, tiles):
    a_ref, w_ref = refs[:2]
    r_refs = refs[2:2 + len(tiles)]
    o_ref = refs[-1]
    y = jnp.dot(a_ref[...], w_ref[...], preferred_element_type=F32)

    def add(r_ref):
        o_ref[...] = r_ref[...] + y

    _for_owning_part(tiles, r_refs, add)


def matmul_residual(a, w, r_parts, tm):
    m, k = a.shape
    n = w.shape[1]
    tiles = _part_tiles(r_parts, tm)
    return pl.pallas_call(
        functools.partial(_mm_res_kernel, tiles=tiles),
        grid=(m // tm, 1),
        in_specs=[pl.BlockSpec((tm, k), lambda i, j: (i, 0)),
                  pl.BlockSpec((k, n), lambda i, j: (0, 0))] + _part_specs(tiles, (tm, n)),
        out_specs=pl.BlockSpec((tm, n), lambda i, j: (i, 0)),
        out_shape=jax.ShapeDtypeStruct((m, n), F32),
        compiler_params=_params("parallel", "arbitrary"),
        name="matmul_residual",
    )(a, w, *r_parts)


def _mlp_kernel(x_ref, g_ref, wu_ref, wd_ref, gf_ref, *refs, final_norm, tiles):
    o_refs, h_ref = refs[:-1], refs[-1]
    j = pl.program_id(1)

    def run(o_ref):
        @pl.when(j == 0)
        def _():
            x = x_ref[...]
            h_ref[...] = _rms_bf16(x, g_ref[...])
            o_ref[...] = x

        u = jnp.dot(h_ref[...], wu_ref[...], preferred_element_type=F32)
        a = jnp.square(jnp.maximum(u, 0.0)).astype(BF16)
        o_ref[...] += jnp.dot(a, wd_ref[...], preferred_element_type=F32)

        if final_norm:
            @pl.when(j == pl.num_programs(1) - 1)
            def _():
                y = o_ref[...]
                o_ref[...] = y * lax.rsqrt(jnp.mean(y * y, axis=-1, keepdims=True) + RMS_EPS) * gf_ref[...]

    _for_owning_part(tiles, o_refs, run)


def mlp_block(x, g, w_up, w_down, g_final, final_norm, tm, tf, out_rows):
    m, d = x.shape
    f = w_up.shape[1]
    assert sum(out_rows) == m and all(r % tm == 0 for r in out_rows)
    tiles = tuple(r // tm for r in out_rows)
    return pl.pallas_call(
        functools.partial(_mlp_kernel, final_norm=final_norm, tiles=tiles),
        grid=(m // tm, f // tf),
        in_specs=[pl.BlockSpec((tm, d), lambda i, j: (i, 0)),
                  pl.BlockSpec((1, d), lambda i, j: (0, 0)),
                  pl.BlockSpec((d, tf), lambda i, j: (0, j)),
                  pl.BlockSpec((tf, d), lambda i, j: (j, 0)),
                  pl.BlockSpec((1, d), lambda i, j: (0, 0))],
        out_specs=_part_specs(tiles, (tm, d)),
        out_shape=[jax.ShapeDtypeStruct((r, d), F32) for r in out_rows],
        scratch_shapes=[pltpu.VMEM((tm, d), BF16)],
        compiler_params=_params("parallel" if len(tiles) == 1 else "arbitrary", "arbitrary"),
        name="mlp_block",
    )(x, g.reshape(1, d), w_up, w_down, g_final.reshape(1, d))


def _merge_kernel(gate_ref, ba_ref, bb_ref, bc_ref, bd_ref, wb_ref, o_ref, acc_ref):
    n = pl.program_id(1)

    def contribution(b_ref):
        y = jnp.dot(b_ref[...], wb_ref[0], preferred_element_type=F32)
        return (jnp.tanh(gate_ref[...].astype(F32)) + 1.0) * y

    @pl.when(n == 0)
    def _():
        acc_ref[...] = contribution(ba_ref)

    @pl.when(n == 1)
    def _():
        acc_ref[...] += contribution(bb_ref)

    @pl.when(n == 2)
    def _():
        acc_ref[...] += contribution(bc_ref)

    @pl.when(n == 3)
    def _():
        o_ref[...] = (acc_ref[...] + contribution(bd_ref)).astype(o_ref.dtype)


def gated_merge(z, branches, w_branch, tm):
    m = z.shape[0]
    gate_blk0 = Z_GATE // D_MODEL
    br_spec = pl.BlockSpec((tm, BRANCH_W), lambda i, n: (i, 0))
    return pl.pallas_call(
        _merge_kernel,
        grid=(m // tm, N_BRANCH),
        in_specs=[pl.BlockSpec((tm, D_MODEL), lambda i, n: (i, gate_blk0 + n)),
                  br_spec, br_spec, br_spec, br_spec,
                  pl.BlockSpec((1, BRANCH_W, D_MODEL), lambda i, n: (n, 0, 0))],
        out_specs=pl.BlockSpec((tm, D_MODEL), lambda i, n: (i, 0)),
        out_shape=jax.ShapeDtypeStruct((m, D_MODEL), BF16),
        scratch_shapes=[pltpu.VMEM((tm, D_MODEL), F32)],
        compiler_params=_params("parallel", "arbitrary"),
        name="gated_merge",
    )(z, *branches, w_branch)


NA_ROWS_LOCKSTEP = 8
NA_MASK = -1e30


def na_bias_table(rpb):
    q = np.arange(GRID_W)
    kc = np.arange(GRID_W)
    cs = np.clip(q - NA_WIN_C // 2, 0, GRID_W - NA_WIN_C)
    ok = (kc[None, :] >= cs[:, None]) & (kc[None, :] < cs[:, None] + NA_WIN_C)
    dcol = np.clip(kc[None, :] - q[:, None] + NA_WIN_C - 1, 0, 2 * NA_WIN_C - 2)
    drow = np.arange(NA_WIN_R)[None, :] - np.arange(NA_WIN_R)[:, None] + NA_WIN_R - 1
    t = rpb.astype(F32)[:, drow]
    t = t[:, :, :, dcol]
    t = jnp.transpose(t, (1, 0, 3, 2, 4))
    t = jnp.where(jnp.asarray(ok)[None, None, :, None, :], t, NA_MASK)
    return t.reshape(NA_WIN_R, NA_HEADS, GRID_W, NA_WIN_R * GRID_W)


def _na_kernel(q_ref, k_ref, v_ref, bias_ref, o_ref, *, rows_per_step, rows):
    rblk = pl.program_id(1)
    win = NA_WIN_R * GRID_W
    scale = NA_HEAD_DIM ** -0.5

    pair_w = 2 * NA_HEAD_DIM
    low = lax.broadcasted_iota(jnp.int32, (GRID_W, pair_w), 1) < NA_HEAD_DIM

    def rows_body(it, carry):
        inst = []
        q, kb, vb, off, qstart = [], [], [], [], []
        for t in range(NA_ROWS_LOCKSTEP):
            rl = it * NA_ROWS_LOCKSTEP + t
            r = rblk * rows_per_step + rl
            rs = jnp.clip(r - NA_WIN_R // 2, 0, rows - NA_WIN_R)
            off.append(r - rs)
            qstart.append(pl.multiple_of(rl * GRID_W, GRID_W))
            kstart = pl.multiple_of(rs * GRID_W, GRID_W)
            q.append(q_ref[0, pl.ds(qstart[t], GRID_W), :])
            kb.append(k_ref[0, pl.ds(kstart, win), :])
            vb.append(v_ref[0, pl.ds(kstart, win), :])
            inst += [(t, h) for h in range(NA_HEADS)]

        def pair(x, h):
            return x[:, (h // 2) * pair_w:(h // 2 + 1) * pair_w]

        qm = [jnp.where(low if h % 2 == 0 else ~low, pair(q[t], h), jnp.zeros((), BF16)) for t, h in inst]
        s = [lax.dot_general(qm[i], pair(kb[t], h), (((1,), (1,)), ((), ())), preferred_element_type=F32)
             for i, (t, h) in enumerate(inst)]
        s = [s[i] * scale + bias_ref[off[t], h] for i, (t, h) in enumerate(inst)]
        p = [jnp.exp(x - jnp.max(x, axis=-1, keepdims=True)) for x in s]
        l = [jnp.sum(x, axis=-1, keepdims=True) for x in p]
        o = [jnp.dot(p[i].astype(BF16), pair(vb[t], h), preferred_element_type=F32) / l[i]
             for i, (t, h) in enumerate(inst)]
        for t in range(NA_ROWS_LOCKSTEP):
            base = t * NA_HEADS
            outs = [jnp.where(low, o[base + h], o[base + h + 1]) for h in range(0, NA_HEADS, 2)]
            o_ref[0, pl.ds(qstart[t], GRID_W), :] = jnp.concatenate(outs, axis=1).astype(o_ref.dtype)
        return carry

    lax.fori_loop(0, rows_per_step // NA_ROWS_LOCKSTEP, rows_body, 0)


def neighbourhood_attention(z3, bias, rows_per_step):
    b, l, _ = z3.shape
    rows = l // GRID_W
    w = NA_HEADS * NA_HEAD_DIM
    blk = Z_NA // w
    return pl.pallas_call(
        functools.partial(_na_kernel, rows_per_step=rows_per_step, rows=rows),
        grid=(b, rows // rows_per_step),
        in_specs=[pl.BlockSpec((1, rows_per_step * GRID_W, w), lambda i, r: (i, r, blk)),
                  pl.BlockSpec((1, l, w), lambda i, r: (i, 0, blk + 1)),
                  pl.BlockSpec((1, l, w), lambda i, r: (i, 0, blk + 2)),
                  pl.BlockSpec(bias.shape, lambda i, r: (0, 0, 0, 0))],
        out_specs=pl.BlockSpec((1, rows_per_step * GRID_W, w), lambda i, r: (i, r, 0)),
        out_shape=jax.ShapeDtypeStruct((b, l, w), BF16),
        compiler_params=_params("parallel", "arbitrary"),
        name="neighbourhood_attention",
    )(z3, z3, z3, bias)


def _fft_tables(l):
    n = 2 * l
    n2 = FFT_N2
    n1 = n // n2
    a = np.arange(n1)
    b = np.arange(n2)
    k1 = np.arange(n1)
    t = n2 * a[None, :] + b[:, None]
    th = 2.0 * np.pi * (k1[None, :, None] * t[:, None, :] % n) / n
    fa = np.concatenate([np.cos(th), -np.sin(th)], axis=1)
    th_t = np.transpose(th, (0, 2, 1))
    ga = np.concatenate([np.cos(th_t), -np.sin(th_t)], axis=2)
    ph = 2.0 * np.pi * (np.outer(b, b) % n2) / n2
    cr, ci = np.cos(ph), -np.sin(ph)
    fb = np.block([[cr, -ci], [ci, cr]])
    fbi = np.block([[cr, ci], [-ci, cr]])
    return fa, ga, fb, fbi


def _filter_fft_kernel(hf_ref, hb_ref, fa_hi_ref, fa_lo_ref, fb_hi_ref, fb_lo_ref, o_ref, y_ref, *, n1):
    n2 = FFT_N2
    na = n1 // 2
    inv_n = 1.0 / (n1 * n2)

    def transform(src_ref, combine):
        def step_a(b, carry):
            xb = _split_bf16(src_ref[pl.ds(b, na, stride=n2), :])
            r = _dot_split((fa_hi_ref[b], fa_lo_ref[b]), xb)
            y_ref[pl.ds(b, n1, stride=2 * n2), :] = r[:n1]
            y_ref[pl.ds(b + n2, n1, stride=2 * n2), :] = r[n1:]
            return carry

        lax.fori_loop(0, n2, step_a, 0, unroll=16)

        def step_b(k1, carry):
            r0 = pl.multiple_of(k1 * 2 * n2, 2 * n2)
            x = _dot_split((fb_hi_ref[...], fb_lo_ref[...]), _split_bf16(y_ref[pl.ds(r0, 2 * n2), :]))
            combine(r0, x * inv_n)
            return carry

        lax.fori_loop(0, n1, step_b, 0, unroll=8)

    def store(r0, x):
        o_ref[pl.ds(r0, 2 * n2), :] = x

    def add_conjugate(r0, x):
        o_ref[pl.ds(r0, n2), :] += x[:n2]
        o_ref[pl.ds(r0 + n2, n2), :] -= x[n2:]

    transform(hf_ref, store)
    transform(hb_ref, add_conjugate)


def filter_spectrum(h, fa, fb):
    l = h.shape[0]
    c = h.shape[1] // 2
    n = 2 * l
    n1 = n // FFT_N2
    nch = c // LANES
    return pl.pallas_call(
        functools.partial(_filter_fft_kernel, n1=n1),
        grid=(nch,),
        in_specs=[pl.BlockSpec((l, LANES), lambda i: (0, i)),
                  pl.BlockSpec((l, LANES), lambda i: (0, nch + i)),
                  pl.BlockSpec(fa[0].shape, lambda i: (0, 0, 0)),
                  pl.BlockSpec(fa[1].shape, lambda i: (0, 0, 0)),
                  pl.BlockSpec(fb[0].shape, lambda i: (0, 0)),
                  pl.BlockSpec(fb[1].shape, lambda i: (0, 0))],
        out_specs=pl.BlockSpec((2 * n, LANES), lambda i: (0, i)),
        out_shape=jax.ShapeDtypeStruct((2 * n, c), F32),
        scratch_shapes=[pltpu.VMEM((2 * n, LANES), F32)],
        compiler_params=_params("parallel"),
        name="hyena_filter_spectrum",
    )(h, h, *fa, *fb)


def _hyena_filter_kernel(z_ref, w1_ref, b1_ref, w2_ref, b2_ref, w3_ref, dec_ref, o_ref):
    hid = jnp.sin(jnp.dot(z_ref[...], w1_ref[...], precision=HIGHEST, preferred_element_type=F32) + b1_ref[...])
    hid = jnp.sin(jnp.dot(hid, w2_ref[...], precision=HIGHEST, preferred_element_type=F32) + b2_ref[...])
    h = jnp.dot(hid, w3_ref[...], precision=HIGHEST, preferred_element_type=F32)
    dec = dec_ref[...]
    hf = h[:, :HY_WIDTH] * dec
    hb = h[:, HY_WIDTH:] * dec
    norm = (jnp.sum(jnp.abs(hf), axis=0, keepdims=True) + jnp.sum(jnp.abs(hb), axis=0, keepdims=True)) + RMS_EPS
    o_ref[:, :HY_WIDTH] = hf / norm
    o_ref[:, HY_WIDTH:] = hb / norm


def hyena_filters(l, w1, b1, w2, b2, w3):
    t = jnp.linspace(0.0, 1.0, l, dtype=F32)[:, None]
    w = 2.0 * math.pi * jnp.arange(l, dtype=F32)[:, None] / l
    bands = jnp.linspace(1e-4, HY_POS_BANDS - 1, HY_POS_BANDS, dtype=F32)[None, :]
    z = jnp.concatenate([t, jnp.cos(bands * w), -jnp.sin(bands * w)], axis=-1)
    pad = LANES - z.shape[1]
    z = jnp.pad(z, ((0, 0), (0, pad)))
    w1p = jnp.pad(w1.astype(F32), ((0, pad), (0, 0)))
    max_decay = math.log(HY_DECAY_TARGET) / HY_FAST_DECAY
    min_decay = math.log(HY_DECAY_TARGET) / HY_SLOW_DECAY
    deltas = jnp.abs(jnp.linspace(min_decay, max_decay, HY_WIDTH, dtype=F32))
    dec = jnp.exp(-t * deltas[None, :])
    return pl.pallas_call(
        _hyena_filter_kernel,
        out_shape=jax.ShapeDtypeStruct((l, 2 * HY_WIDTH), F32),
        compiler_params=pltpu.CompilerParams(vmem_limit_bytes=VMEM_LIMIT_BYTES),
        name="hyena_filter_ffn",
    )(z, w1p, b1.reshape(1, -1).astype(F32), w2.astype(F32), b2.reshape(1, -1).astype(F32), w3.astype(F32), dec)


CONV_PAD = 8


def _dwconv3(x, w_ref, pad_ref):
    n = x.shape[0]
    zeros = jnp.zeros((CONV_PAD, x.shape[1]), F32)
    pad_ref[0:CONV_PAD, :] = zeros
    pad_ref[CONV_PAD + n:2 * CONV_PAD + n, :] = zeros
    pad_ref[CONV_PAD:CONV_PAD + n, :] = x
    w = w_ref[...].astype(F32)
    return (pad_ref[CONV_PAD - 1:CONV_PAD - 1 + n, :] * w[0:1] + x * w[1:2]
            + pad_ref[CONV_PAD + 1:CONV_PAD + 1 + n, :] * w[2:3])


def _hyena_kernel(x1_ref, x2_ref, v_ref, w1_ref, w2_ref, wv_ref, skip_ref, kf_ref, fa_ref, ga_ref, fb_ref, fbi_ref,
                  o_ref, xs_ref, y_ref, pad_ref, *, n1):
    n2 = FFT_N2
    na = n1 // 2
    pa, py = FFT_PITCH_X, FFT_PITCH_Y
    vg = _dwconv3(v_ref[0].astype(F32), wv_ref, pad_ref) * _dwconv3(x1_ref[0].astype(F32), w1_ref, pad_ref)
    for a in range(na):
        xs_ref[a * pa:a * pa + n2, :] = vg[a * n2:(a + 1) * n2]

    def fwd_a(b, carry):
        xb = xs_ref[pl.ds(b, na, stride=pa), :].astype(BF16)
        r = jnp.dot(fa_ref[b], xb, preferred_element_type=F32)
        y_ref[pl.ds(b, n1, stride=py), :] = r[:n1]
        y_ref[pl.ds(b + n2, n1, stride=py), :] = r[n1:]
        return carry

    lax.fori_loop(0, n2, fwd_a, 0, unroll=FFT_UNROLL)

    def mid(kg, carry):
        grp = range(FFT_GROUP)
        k1 = [kg * FFT_GROUP + t for t in grp]
        r0 = [pl.multiple_of(k * py, 8) for k in k1]
        f0 = [pl.multiple_of(k * 2 * n2, 2 * n2) for k in k1]
        fb = fb_ref[...]
        x = [jnp.dot(fb, y_ref[pl.ds(r0[t], 2 * n2), :].astype(BF16), preferred_element_type=F32) for t in grp]
        p = []
        for t in grp:
            xr, xi = x[t][:n2], x[t][n2:]
            kr = kf_ref[pl.ds(f0[t], n2), :]
            ki = kf_ref[pl.ds(f0[t] + n2, n2), :]
            p.append(jnp.concatenate([xr * kr - xi * ki, xr * ki + xi * kr], axis=0).astype(BF16))
        fbi = fbi_ref[...]
        q = [jnp.dot(fbi, p[t], preferred_element_type=F32) for t in grp]
        for t in grp:
            y_ref[pl.ds(r0[t], 2 * n2), :] = q[t]
        return carry

    lax.fori_loop(0, n1 // FFT_GROUP, mid, 0)

    def inv_a(b, carry):
        qb = jnp.concatenate([y_ref[pl.ds(b, n1, stride=py), :],
                              y_ref[pl.ds(b + n2, n1, stride=py), :]], axis=0).astype(BF16)
        xs_ref[pl.ds(b, na, stride=pa), :] = jnp.dot(ga_ref[b], qb, preferred_element_type=F32)
        return carry

    lax.fori_loop(0, n2, inv_a, 0, unroll=FFT_UNROLL)

    gate = _dwconv3(x2_ref[0].astype(F32), w2_ref, pad_ref)
    skip = skip_ref[...].astype(F32)
    for a in range(na):
        rows = slice(a * n2, (a + 1) * n2)
        o_ref[0, rows, :] = ((xs_ref[a * pa:a * pa + n2, :] + vg[rows] * skip) * gate[rows]).astype(o_ref.dtype)


def hyena_mixer(z3, w_short, skip, kf, tables):
    b, l, _ = z3.shape
    n1 = 2 * l // FFT_N2
    fa, ga, fb, fbi = tables
    nch = HY_WIDTH // LANES
    blk = Z_HY // LANES

    def zspec(seg):
        return pl.BlockSpec((1, l, LANES), lambda c, i: (i, 0, blk + seg * nch + c))

    def wspec(seg):
        return pl.BlockSpec((3, LANES), lambda c, i: (0, seg * nch + c))

    return pl.pallas_call(
        functools.partial(_hyena_kernel, n1=n1),
        grid=(nch, b),
        in_specs=[zspec(0), zspec(1), zspec(2), wspec(0), wspec(1), wspec(2),
                  pl.BlockSpec((1, LANES), lambda c, i: (0, c)),
                  pl.BlockSpec((4 * l, LANES), lambda c, i: (0, c)),
                  pl.BlockSpec(fa.shape, lambda c, i: (0, 0, 0)),
                  pl.BlockSpec(ga.shape, lambda c, i: (0, 0, 0)),
                  pl.BlockSpec(fb.shape, lambda c, i: (0, 0)),
                  pl.BlockSpec(fbi.shape, lambda c, i: (0, 0))],
        out_specs=pl.BlockSpec((1, l, LANES), lambda c, i: (i, 0, c)),
        out_shape=jax.ShapeDtypeStruct((b, l, HY_WIDTH), BF16),
        scratch_shapes=[pltpu.VMEM((n1 // 2 * FFT_PITCH_X, LANES), F32), pltpu.VMEM((n1 * FFT_PITCH_Y, LANES), F32),
                        pltpu.VMEM((l + 2 * CONV_PAD, LANES), F32)],
        compiler_params=_params("parallel", "arbitrary"),
        name="hyena_mixer",
    )(z3, z3, z3, w_short, w_short, w_short, skip.reshape(1, -1), kf, fa, ga, fb, fbi)


MLA_HW = 2 * LANES
MLA_HEADS_PER_STEP = 2


def _rope_group(g, cs):
    prod = g * cs
    s = prod + pltpu.roll(prod, MLA_ROPE, 1)
    lane = lax.broadcasted_iota(jnp.int32, s.shape, 1)
    return jnp.where(lane < MLA_ROPE, s, 0.0)


def _mla_prep_kernel(cq_ref, ckv_ref, kr_ref, gq_ref, gkv_ref, wq_ref, wkv_ref, cs_ref, q_ref, k_ref, v_ref):
    cs = cs_ref[...]
    hq = _rms_bf16(cq_ref[...], gq_ref[...])
    q = jnp.dot(hq, wq_ref[...], preferred_element_type=F32)
    hkv = _rms_bf16(ckv_ref[...], gkv_ref[...])
    kv = jnp.dot(hkv, wkv_ref[...], preferred_element_type=F32)
    k_rope = _rope_group(kr_ref[...].astype(F32), cs).astype(BF16)
    lane = lax.broadcasted_iota(jnp.int32, (cs.shape[0], LANES), 1)
    ones_lane = jnp.where(lane == 0, 1.0, 0.0).astype(BF16)
    for h in range(MLA_HEADS):
        o = h * MLA_HW
        q_ref[:, o:o + LANES] = q[:, o:o + LANES].astype(BF16)
        q_ref[:, o + LANES:o + MLA_HW] = _rope_group(q[:, o + LANES:o + MLA_HW], cs).astype(BF16)
        k_ref[:, o:o + LANES] = kv[:, o:o + LANES].astype(BF16)
        k_ref[:, o + LANES:o + MLA_HW] = k_rope
        v_ref[:, o:o + LANES] = kv[:, o + LANES:o + MLA_HW].astype(BF16)
        v_ref[:, o + LANES:o + MLA_HW] = ones_lane


def _mla_attn_kernel(q_ref, k_ref, v_ref, o_ref):
    c = (MLA_NOPE + MLA_ROPE) ** -0.5 * math.log2(math.e)
    heads = range(MLA_HEADS_PER_STEP)
    s = [lax.dot_general(q_ref[0, :, h * MLA_HW:(h + 1) * MLA_HW], k_ref[0, :, h * MLA_HW:(h + 1) * MLA_HW],
                         (((1,), (1,)), ((), ())), preferred_element_type=F32) for h in heads]
    for h in heads:
        p = jnp.exp2((s[h] - jnp.max(s[h], axis=-1, keepdims=True)) * c)
        o = jnp.dot(p.astype(BF16), v_ref[0, :, h * MLA_HW:(h + 1) * MLA_HW], preferred_element_type=F32)
        o_ref[0, :, h * MLA_V:(h + 1) * MLA_V] = (o[:, :MLA_V] / o[:, MLA_V:MLA_V + 1]).astype(o_ref.dtype)


def mla_mixer(z, b, l, g_q, g_kv, wq_p, w_ukv, cs_tab, tm, tq):
    m = z.shape[0]
    lt = l // tm
    qp, kp, vp = pl.pallas_call(
        _mla_prep_kernel,
        grid=(m // tm,),
        in_specs=[pl.BlockSpec((tm, MLA_Q_RANK), lambda i: (i, Z_CQ // MLA_Q_RANK)),
                  pl.BlockSpec((tm, MLA_KV_RANK), lambda i: (i, Z_CKV // MLA_KV_RANK)),
                  pl.BlockSpec((tm, LANES), lambda i: (i, Z_KR // LANES)),
                  pl.BlockSpec((1, MLA_Q_RANK), lambda i: (0, 0)),
                  pl.BlockSpec((1, MLA_KV_RANK), lambda i: (0, 0)),
                  pl.BlockSpec(wq_p.shape, lambda i: (0, 0)),
                  pl.BlockSpec(w_ukv.shape, lambda i: (0, 0)),
                  pl.BlockSpec((tm, LANES), lambda i: (i % lt, 0))],
        out_specs=[pl.BlockSpec((tm, MLA_HEADS * MLA_HW), lambda i: (i, 0)),
                   pl.BlockSpec((tm, MLA_HEADS * MLA_HW), lambda i: (i, 0)),
                   pl.BlockSpec((tm, MLA_HEADS * MLA_HW), lambda i: (i, 0))],
        out_shape=[jax.ShapeDtypeStruct((m, MLA_HEADS * MLA_HW), BF16),
                   jax.ShapeDtypeStruct((m, MLA_HEADS * MLA_HW), BF16),
                   jax.ShapeDtypeStruct((m, MLA_HEADS * MLA_HW), BF16)],
        compiler_params=_params("parallel"),
        name="mla_prep",
    )(z, z, z, g_q.reshape(1, -1), g_kv.reshape(1, -1), wq_p, w_ukv, cs_tab)
    qp = qp.reshape(b, l, -1)
    kp = kp.reshape(b, l, -1)
    vp = vp.reshape(b, l, -1)
    hs = MLA_HEADS_PER_STEP
    return pl.pallas_call(
        _mla_attn_kernel,
        grid=(b, MLA_HEADS // hs, l // tq),
        in_specs=[pl.BlockSpec((1, tq, hs * MLA_HW), lambda i, h, t: (i, t, h)),
                  pl.BlockSpec((1, l, hs * MLA_HW), lambda i, h, t: (i, 0, h)),
                  pl.BlockSpec((1, l, hs * MLA_HW), lambda i, h, t: (i, 0, h))],
        out_specs=pl.BlockSpec((1, tq, hs * MLA_V), lambda i, h, t: (i, t, h)),
        out_shape=jax.ShapeDtypeStruct((b, l, MLA_HEADS * MLA_V), BF16),
        compiler_params=_params("parallel", "parallel", "arbitrary"),
        name="mla_attention",
    )(qp, kp, vp)


def _split_bf16(x):
    hi = x.astype(BF16)
    return hi, (x - hi.astype(F32)).astype(BF16)


def _dot_split(a, b):
    (ah, al), (bh, bl) = a, b
    return (jnp.dot(ah, bh, preferred_element_type=F32)
            + (jnp.dot(ah, bl, preferred_element_type=F32) + jnp.dot(al, bh, preferred_element_type=F32)))


def _tri_unit_inverse(mats):
    c = mats[0].shape[0]
    eye = (lax.broadcasted_iota(jnp.int32, (c, c), 0) == lax.broadcasted_iota(jnp.int32, (c, c), 1)).astype(F32)
    ps = [eye - a for a in mats]
    ms = list(mats)
    for _ in range(int(math.log2(c)) - 1):
        splits = [_split_bf16(m) for m in ms]
        ms = [_dot_split(sp, sp) for sp in splits]
        yield
        ps = [_dot_split(_split_bf16(p), _split_bf16(eye + m)) for p, m in zip(ps, ms)]
        yield
    return ps


def _interleave(main, side=None):
    gens = [g for g in (main, side) if g is not None]
    results = [None] * len(gens)
    live = [True] * len(gens)
    while any(live):
        for idx, gen in enumerate(gens):
            if live[idx]:
                try:
                    next(gen)
                except StopIteration as stop:
                    results[idx] = stop.value
                    live[idx] = False
    return results


def _bdot(a, b):
    return jnp.dot(a.astype(BF16), b.astype(BF16), preferred_element_type=F32)


def _bdot_nt(a, b):
    return lax.dot_general(a.astype(BF16), b.astype(BF16), (((1,), (1,)), ((), ())), preferred_element_type=F32)


def _dn_group_setup(chunks, refs):
    q_s, k_s, v_s, g_s, b_s = refs
    cc = DN_CHUNK
    n = range(len(chunks))
    rev = [r for _, r in chunks]
    r0 = [pl.multiple_of(c * cc, cc) for c, _ in chunks]
    q = [q_s[pl.ds(r, cc), :] for r in r0]
    k = [k_s[pl.ds(r, cc), :] for r in r0]
    v = [v_s[pl.ds(r, cc), :] for r in r0]
    lane = [2 if r else 0 for r in rev]
    ri = lax.broadcasted_iota(jnp.int32, (cc, cc), 0)
    ci = lax.broadcasted_iota(jnp.int32, (cc, cc), 1)
    incl = [(ri <= ci) if r else (ri >= ci) for r in rev]
    strict = [(ri < ci) if r else (ri > ci) for r in rev]
    g = [g_s[pl.ds(r, cc), :] for r in r0]
    g1 = [x.astype(BF16) for x in g]
    r1 = [x - h.astype(F32) for x, h in zip(g, g1)]
    g2 = [x.astype(BF16) for x in r1]
    g3 = [(x - h.astype(F32)).astype(BF16) for x, h in zip(r1, g2)]
    ones = [m.astype(BF16) for m in incl]
    gc = [jnp.dot(ones[i], g1[i], preferred_element_type=F32) for i in n]
    gc = [gc[i] + jnp.dot(ones[i], g2[i], preferred_element_type=F32) for i in n]
    gc = [gc[i] + jnp.dot(ones[i], g3[i], preferred_element_type=F32) for i in n]
    yield
    gcol = [gc[i][:, lane[i]:lane[i] + 1] for i in n]
    grow = [jnp.transpose(gc[i])[lane[i]:lane[i] + 1, :] for i in n]
    beta = [b_s[pl.ds(r0[i], cc), :][:, lane[i] + 1:lane[i] + 2] for i in n]
    g_last = [gcol[i][0:1, :] if rev[i] else gcol[i][cc - 1:cc, :] for i in n]
    decay = [jnp.where(incl[i], jnp.exp(jnp.where(incl[i], gcol[i] - grow[i], 0.0)), 0.0) for i in n]
    e_g = [jnp.exp(x) for x in gcol]
    kb = [k[i] * beta[i] for i in n]
    a_both = [_bdot_nt(jnp.concatenate([kb[i], q[i]], axis=0), k[i]) for i in n]
    yield
    a_kk = [jnp.where(strict[i], a_both[i][:cc] * decay[i], 0.0) for i in n]
    a_qk = [jnp.where(incl[i], a_both[i][cc:] * decay[i], 0.0) for i in n]
    t_inv = yield from _tri_unit_inverse(a_kk)
    uw = [_bdot(t_inv[i], jnp.concatenate([v[i] * beta[i], kb[i] * e_g[i]], axis=1)) for i in n]
    yield
    k_dec = [k[i] * jnp.exp(g_last[i] - gcol[i]) for i in n]
    wq = [jnp.concatenate([uw[i][:, DN_DV:], q[i] * e_g[i]], axis=0).astype(BF16) for i in n]
    ak = [jnp.concatenate([a_qk[i], jnp.transpose(k_dec[i])], axis=0).astype(BF16) for i in n]
    return [(wq[i], ak[i], uw[i][:, :DN_DV], jnp.exp(g_last[i])) for i in n]


def _dn_group_steps(states, setups, out_refs, chunks):
    cc = DN_CHUNK
    dirs = range(len(states))
    for t in range(len(setups[0])):
        st = [setups[d][t] for d in dirs]
        r = [jnp.dot(st[d][0], states[d].astype(BF16), preferred_element_type=F32) for d in dirs]
        yield
        v_new = [st[d][2] - r[d][:cc] for d in dirs]
        r2 = [jnp.dot(st[d][1], v_new[d].astype(BF16), preferred_element_type=F32) for d in dirs]
        yield
        states = [states[d] * st[d][3] + r2[d][cc:] for d in dirs]
        for d in dirs:
            out_refs[d][pl.ds(pl.multiple_of(chunks[d][t] * cc, cc), cc), :] = r[d][cc:] + r2[d][:cc]
    return states


def _dn_kernel(zq_ref, zk_ref, zv_ref, zg_ref, ab_ref, wq_ref, wk_ref, wv_ref, alog_ref, dtb_ref, gn_ref, o_ref,
               q_s, k_s, v_s, g_s, b_s, of_s, ob_s, pad_ref, *, nchunks):
    def act(z_ref, w_ref):
        c = _dwconv3(z_ref[0].astype(F32), w_ref, pad_ref)
        return _silu(c)

    def l2n(x):
        return x * lax.rsqrt(jnp.sum(x * x, axis=-1, keepdims=True) + RMS_EPS)

    q_s[...] = l2n(act(zq_ref, wq_ref)) * (DN_DK ** -0.5)
    k_s[...] = l2n(act(zk_ref, wk_ref))
    v_s[...] = act(zv_ref, wv_ref)
    ab = ab_ref[0]
    x = ab + dtb_ref[0]
    softplus = jnp.maximum(x, 0.0) + jnp.log1p(jnp.exp(-jnp.abs(x)))
    g_s[...] = -jnp.exp(alog_ref[0]) * softplus
    b_s[...] = jax.nn.sigmoid(ab)
    refs = (q_s, k_s, v_s, g_s, b_s)

    ngroups = nchunks // DN_GROUP

    def group_chunks(gi):
        fwd = [gi * DN_GROUP + t for t in range(DN_GROUP)]
        return fwd, [nchunks - 1 - c for c in fwd]

    def setup_gen(gi):
        fwd, bwd = group_chunks(gi)
        return _dn_group_setup([(c, False) for c in fwd] + [(c, True) for c in bwd], refs)

    def steps_gen(gi, states, flat):
        setups = [tuple(flat[4 * i:4 * i + 4]) for i in range(2 * DN_GROUP)]
        return _dn_group_steps(states, [setups[:DN_GROUP], setups[DN_GROUP:]], (of_s, ob_s), group_chunks(gi))

    def flatten(setups):
        return [x for st in setups for x in st]

    def body(gi, carry):
        nxt, states = _interleave(setup_gen(gi + 1), steps_gen(gi, list(carry[:2]), carry[2:]))
        return (*states, *flatten(nxt))

    zero = jnp.zeros((DN_DK, DN_DV), F32)
    (first,) = _interleave(setup_gen(0))
    carry = lax.fori_loop(0, ngroups - 1, body, (zero, zero, *flatten(first)))
    _interleave(steps_gen(ngroups - 1, list(carry[:2]), carry[2:]))

    o = of_s[...] + ob_s[...]
    o = o * lax.rsqrt(jnp.mean(o * o, axis=-1, keepdims=True) + RMS_EPS) * gn_ref[...]
    gate = zg_ref[0].astype(F32)
    o_ref[0] = (o * _silu(gate)).astype(o_ref.dtype)


def deltanet_mixer(z3, ab3, w_conv, alog_p, dtb_p, g_norm):
    b, l, _ = z3.shape
    blk = Z_DNQKV // LANES
    gblk = Z_DNGATE // LANES
    hh = DN_HEADS

    def zspec(off):
        return pl.BlockSpec((1, l, LANES), lambda i, h: (i, 0, off + h))

    def wspec(seg):
        return pl.BlockSpec((3, LANES), lambda i, h: (0, seg * hh + h))

    vec = pl.BlockSpec((1, 1, LANES), lambda i, h: (h, 0, 0))
    seq = pltpu.VMEM((l, LANES), F32)
    return pl.pallas_call(
        functools.partial(_dn_kernel, nchunks=l // DN_CHUNK),
        grid=(b, hh),
        in_specs=[zspec(blk), zspec(blk + hh), zspec(blk + 2 * hh), zspec(gblk),
                  pl.BlockSpec((1, l, LANES), lambda i, h: (i, 0, h)),
                  wspec(0), wspec(1), wspec(2), vec, vec,
                  pl.BlockSpec((1, LANES), lambda i, h: (0, 0))],
        out_specs=pl.BlockSpec((1, l, LANES), lambda i, h: (i, 0, h)),
        out_shape=jax.ShapeDtypeStruct((b, l, hh * DN_DV), BF16),
        scratch_shapes=[seq] * 7 + [pltpu.VMEM((l + 2 * CONV_PAD, LANES), F32)],
        compiler_params=_params("parallel", "arbitrary"),
        name="deltanet_mixer",
    )(z3, z3, z3, z3, ab3, w_conv, w_conv, w_conv, alog_p, dtb_p, g_norm.reshape(1, -1))


def _rotate_half_cols(w):
    half = w.shape[-1] // 2
    return jnp.concatenate([-w[..., half:], w[..., :half]], axis=-1)


def _prep_w_in(w_in):
    cuts = np.cumsum([0, 1536, 1536, MLA_Q_RANK, MLA_KV_RANK, MLA_ROPE, 1536, 512, 16])
    dnab = w_in[:, cuts[7]:cuts[8]]
    w16 = w_in.astype(BF16)
    na, hy, cq, ckv, kr, dnqkv, dngate = (w16[:, cuts[i]:cuts[i + 1]] for i in range(7))
    gates = w16[:, cuts[8]:]
    pad = jnp.zeros((w_in.shape[0], Z_GATE - Z_KR - 2 * MLA_ROPE), BF16)
    main = jnp.concatenate([na, hy, dnqkv, dngate, cq, ckv, kr, _rotate_half_cols(kr), pad, 0.5 * gates], axis=1)
    ab = dnab.reshape(-1, 2, 2, DN_HEADS)
    ab = jnp.transpose(ab, (0, 3, 1, 2)).reshape(-1, DN_HEADS, 4)
    ab = jnp.pad(ab, ((0, 0), (0, 0), (0, LANES - 4))).reshape(-1, DN_HEADS * LANES)
    return main, ab.astype(BF16)


def _prep_w_uq(w_uq):
    k = w_uq.shape[0]
    w = w_uq.reshape(k, MLA_HEADS, MLA_NOPE + MLA_ROPE)
    rope_w = w[:, :, MLA_NOPE:]
    return jnp.concatenate([w, _rotate_half_cols(rope_w)], axis=-1).reshape(k, MLA_HEADS * MLA_HW).astype(BF16)


def _head_lane_vec(p):
    v = jnp.zeros((DN_HEADS, 1, LANES), F32)
    v = v.at[:, 0, 0].set(p[0].astype(F32))
    return v.at[:, 0, 2].set(p[1].astype(F32))


def _rope_table(l):
    half = MLA_ROPE // 2
    inv = ROPE_THETA ** (-jnp.arange(half, dtype=F32) / half)
    ang = jnp.arange(l, dtype=F32)[:, None] * inv[None, :]
    cos, sin = jnp.cos(ang), jnp.sin(ang)
    return jnp.concatenate([cos, cos, sin, sin], axis=-1)


def _split_const(x):
    hi = x.astype(BF16)
    lo = (x - hi.astype(np.float64)).astype(BF16)
    return jnp.asarray(hi), jnp.asarray(lo)


def _pick(n, pref):
    for t in pref:
        if n % t == 0:
            return t
    return n


def trunk(x_parts, l, norm_mix, w_in, na_rpb, hy_short, hy_skip, hy_w1, hy_b1, hy_w2, hy_b2, hy_w3,
          mla_g_q, mla_g_kv, mla_w_uq, mla_w_ukv, dn_conv, dn_a_log, dn_dt_bias, dn_g_norm,
          w_branch, w_out, norm_mlp, w_up, w_down, norm_final):
    d = x_parts[0].shape[1]
    part_rows = tuple(p.shape[0] for p in x_parts)
    m = sum(part_rows)
    b = m // l
    depth = w_in.shape[0]
    common = math.gcd(*part_rows)
    tm_big = _pick(common, (1024, 512, 256, 128))
    tm_mid = _pick(common, (512, 256, 128))
    fa, ga, fb, fbi = _fft_tables(l)
    n1 = 2 * l // FFT_N2
    tables = (jnp.asarray(fa[:, :, :n1 // 2], BF16), jnp.asarray(ga[:, :n1 // 2, :], BF16),
              jnp.asarray(fb, BF16), jnp.asarray(fbi, BF16))
    fa_split, fb_split = _split_const(fa[:, :, :n1 // 2]), _split_const(fb)
    cs_tab = _rope_table(l)
    x = tuple(x_parts)
    for layer in range(depth):
        last = layer == depth - 1
        w_main, w_ab = _prep_w_in(w_in[layer])
        z = norm_matmul(x, norm_mix[layer], w_main, BF16, tm_big, 2048 if len(x) == 1 else 1024)
        ab = norm_matmul(x, norm_mix[layer], w_ab, F32, tm_big, DN_HEADS * LANES)
        z3 = z.reshape(b, l, Z_COLS)
        br_a = neighbourhood_attention(z3, na_bias_table(na_rpb[layer]), _pick(l // GRID_W, (8, 4, 2, 1)))
        taps = hyena_filters(l, hy_w1[layer], hy_b1[layer], hy_w2[layer], hy_b2[layer], hy_w3[layer])
        kf = filter_spectrum(taps, fa_split, fb_split)
        br_b = hyena_mixer(z3, hy_short[layer], hy_skip[layer], kf, tables)
        br_c = mla_mixer(z, b, l, mla_g_q[layer], mla_g_kv[layer], _prep_w_uq(mla_w_uq[layer]),
                         mla_w_ukv[layer].astype(BF16), cs_tab, tm_mid, _pick(l, (512, 256, 128)))
        br_d = deltanet_mixer(z3, ab.reshape(b, l, -1), dn_conv[layer], _head_lane_vec(dn_a_log[layer]),
                              _head_lane_vec(dn_dt_bias[layer]), dn_g_norm[layer])
        branches = [t.reshape(m, BRANCH_W) for t in (br_a, br_b, br_c, br_d)]
        merged = gated_merge(z, branches, (0.5 * w_branch[layer]).astype(BF16), tm_mid)
        x_mid = matmul_residual(merged, w_out[layer].astype(BF16), x, tm_mid)
        x = tuple(mlp_block(x_mid, norm_mlp[layer], w_up[layer].astype(BF16), w_down[layer].astype(BF16), norm_final,
                            last, tm_mid, 1024, part_rows if last else (m,)))
    return x


def kernel(x_prompt, x_sample, norm_mix, w_in, na_rpb, hy_short, hy_skip, hy_w1, hy_b1, hy_w2, hy_b2, hy_w3,
           mla_g_q, mla_g_kv, mla_w_uq, mla_w_ukv, dn_conv, dn_a_log, dn_dt_bias, dn_g_norm,
           w_branch, w_out, norm_mlp, w_up, w_down, norm_final):
    assert x_prompt.shape[1:] == x_sample.shape[1:]
    l, d = x_prompt.shape[1:]
    y_prompt, y_sample = trunk((x_prompt.reshape(-1, d), x_sample.reshape(-1, d)), l, norm_mix, w_in, na_rpb,
                               hy_short, hy_skip, hy_w1, hy_b1, hy_w2, hy_b2, hy_w3, mla_g_q, mla_g_kv, mla_w_uq,
                               mla_w_ukv, dn_conv, dn_a_log, dn_dt_bias, dn_g_norm, w_branch, w_out, norm_mlp,
                               w_up, w_down, norm_final)
    return (y_prompt.reshape(x_prompt.shape), y_sample.reshape(x_sample.shape))
```

```python
import functools
import math

import jax
import jax.numpy as jnp
import numpy as np
from jax import lax
from jax.experimental import pallas as pl
from jax.experimental.pallas import tpu as pltpu

F32 = jnp.float32
BF16 = jnp.bfloat16
HIGHEST = lax.Precision.HIGHEST

VMEM_LIMIT_BYTES = 56 * 1024 * 1024
LANES = 128

D_MODEL = 2048
RMS_EPS = 1e-6
GRID_W = 64
N_BRANCH = 4
BRANCH_W = 512
NA_HEADS = 8
NA_HEAD_DIM = 64
NA_WIN_R = 8
NA_WIN_C = 16
HY_WIDTH = 512
HY_POS_BANDS = 16
HY_FILT_HIDDEN = 64
HY_FAST_DECAY = 0.3
HY_SLOW_DECAY = 1.5
HY_DECAY_TARGET = 1e-2
MLA_HEADS = 4
MLA_Q_RANK = 512
MLA_KV_RANK = 256
MLA_NOPE = 128
MLA_ROPE = 64
MLA_V = 128
ROPE_THETA = 10000.0
DN_HEADS = 4
DN_DK = 128
DN_DV = 128
DN_CHUNK = 64
DN_GROUP = 8
D_FF = 4 * D_MODEL

Z_NA = 0
Z_HY = 1536
Z_DNQKV = 3072
Z_DNGATE = 4608
Z_CQ = 5120
Z_CKV = 5632
Z_KR = 5888
Z_GATE = 6144
Z_COLS = Z_GATE + N_BRANCH * D_MODEL

FFT_N2 = 128
FFT_UNROLL = 16
FFT_GROUP = 16
FFT_PITCH_X = FFT_N2 + 8
FFT_PITCH_Y = 2 * FFT_N2 + 8


def _params(*sem):
    return pltpu.CompilerParams(dimension_semantics=sem, vmem_limit_bytes=VMEM_LIMIT_BYTES)


def _silu(x):
    return 0.5 * x * (jnp.tanh(0.5 * x) + 1.0)


def _rms_bf16(x, g):
    xf = x.astype(F32)
    y = xf * lax.rsqrt(jnp.mean(xf * xf, axis=-1, keepdims=True) + RMS_EPS)
    return (y * g).astype(BF16)


def _part_tiles(parts, tm):
    assert all(p.shape[0] % tm == 0 for p in parts)
    return tuple(p.shape[0] // tm for p in parts)


def _part_specs(tiles, block):
    specs, start = [], 0
    for count in tiles:
        specs.append(pl.BlockSpec(block, functools.partial(
            lambda i, j, start, count: (jnp.clip(i - start, 0, count - 1), 0), start=start, count=count)))
        start += count
    return specs


def _for_owning_part(tiles, refs, fn):
    if len(refs) == 1:
        fn(refs[0])
        return
    i = pl.program_id(0)
    start = 0
    for ref, count in zip(refs, tiles):
        pl.when((i >= start) & (i < start + count))(functools.partial(fn, ref))
        start += count


def _norm_mm_kernel(*refs, tiles):
    x_refs = refs[:len(tiles)]
    g_ref, w_ref, o_ref, h_ref = refs[len(tiles):]

    @pl.when(pl.program_id(1) == 0)
    def _():
        def norm(x_ref):
            h_ref[...] = _rms_bf16(x_ref[...], g_ref[...])

        _for_owning_part(tiles, x_refs, norm)

    o_ref[...] = jnp.dot(h_ref[...], w_ref[...], preferred_element_type=F32).astype(o_ref.dtype)


def norm_matmul(x_parts, g, w, out_dtype, tm, tn):
    k, n = w.shape
    tiles = _part_tiles(x_parts, tm)
    return pl.pallas_call(
        functools.partial(_norm_mm_kernel, tiles=tiles),
        grid=(sum(tiles), n // tn),
        in_specs=_part_specs(tiles, (tm, k)) + [pl.BlockSpec((1, k), lambda i, j: (0, 0)),
                                                 pl.BlockSpec((k, tn), lambda i, j: (0, j))],
        out_specs=pl.BlockSpec((tm, tn), lambda i, j: (i, j)),
        out_shape=jax.ShapeDtypeStruct((sum(tiles) * tm, n), out_dtype),
        scratch_shapes=[pltpu.VMEM((tm, k), BF16)],
        compiler_params=_params("parallel", "arbitrary"),
        name="norm_matmul",
    )(*x_parts, g.reshape(1, k), w)


def _mm_res_kernel(*refs, tiles):
    a_ref, w_ref = refs[:2]
    r_refs = refs[2:2 + len(tiles)]
    o_ref = refs[-1]
    y = jnp.dot(a_ref[...], w_ref[...], preferred_element_type=F32)

    def add(r_ref):
        o_ref[...] = r_ref[...] + y

    _for_owning_part(tiles, r_refs, add)


def matmul_residual(a, w, r_parts, tm):
    m, k = a.shape
    n = w.shape[1]
    tiles = _part_tiles(r_parts, tm)
    return pl.pallas_call(
        functools.partial(_mm_res_kernel, tiles=tiles),
        grid=(m // tm, 1),
        in_specs=[pl.BlockSpec((tm, k), lambda i, j: (i, 0)),
                  pl.BlockSpec((k, n), lambda i, j: (0, 0))] + _part_specs(tiles, (tm, n)),
        out_specs=pl.BlockSpec((tm, n), lambda i, j: (i, 0)),
        out_shape=jax.ShapeDtypeStruct((m, n), F32),
        compiler_params=_params("parallel", "arbitrary"),
        name="matmul_residual",
    )(a, w, *r_parts)


def _mlp_kernel(x_ref, g_ref, wu_ref, wd_ref, gf_ref, *refs, final_norm, tiles):
    o_refs, h_ref = refs[:-1], refs[-1]
    j = pl.program_id(1)

    def run(o_ref):
        @pl.when(j == 0)
        def _():
            x = x_ref[...]
            h_ref[...] = _rms_bf16(x, g_ref[...])
            o_ref[...] = x

        u = jnp.dot(h_ref[...], wu_ref[...], preferred_element_type=F32)
        a = jnp.square(jnp.maximum(u, 0.0)).astype(BF16)
        o_ref[...] += jnp.dot(a, wd_ref[...], preferred_element_type=F32)

        if final_norm:
            @pl.when(j == pl.num_programs(1) - 1)
            def _():
                y = o_ref[...]
                o_ref[...] = y * lax.rsqrt(jnp.mean(y * y, axis=-1, keepdims=True) + RMS_EPS) * gf_ref[...]

    _for_owning_part(tiles, o_refs, run)


def mlp_block(x, g, w_up, w_down, g_final, final_norm, tm, tf, out_rows):
    m, d = x.shape
    f = w_up.shape[1]
    assert sum(out_rows) == m and all(r % tm == 0 for r in out_rows)
    tiles = tuple(r // tm for r in out_rows)
    return pl.pallas_call(
        functools.partial(_mlp_kernel, final_norm=final_norm, tiles=tiles),
        grid=(m // tm, f // tf),
        in_specs=[pl.BlockSpec((tm, d), lambda i, j: (i, 0)),
                  pl.BlockSpec((1, d), lambda i, j: (0, 0)),
                  pl.BlockSpec((d, tf), lambda i, j: (0, j)),
                  pl.BlockSpec((tf, d), lambda i, j: (j, 0)),
                  pl.BlockSpec((1, d), lambda i, j: (0, 0))],
        out_specs=_part_specs(tiles, (tm, d)),
        out_shape=[jax.ShapeDtypeStruct((r, d), F32) for r in out_rows],
        scratch_shapes=[pltpu.VMEM((tm, d), BF16)],
        compiler_params=_params("parallel" if len(tiles) == 1 else "arbitrary", "arbitrary"),
        name="mlp_block",
    )(x, g.reshape(1, d), w_up, w_down, g_final.reshape(1, d))


def _merge_kernel(gate_ref, ba_ref, bb_ref, bc_ref, bd_ref, wb_ref, o_ref, acc_ref):
    n = pl.program_id(1)

    def contribution(b_ref):
        y = jnp.dot(b_ref[...], wb_ref[0], preferred_element_type=F32)
        return (jnp.tanh(gate_ref[...].astype(F32)) + 1.0) * y

    @pl.when(n == 0)
    def _():
        acc_ref[...] = contribution(ba_ref)

    @pl.when(n == 1)
    def _():
        acc_ref[...] += contribution(bb_ref)

    @pl.when(n == 2)
    def _():
        acc_ref[...] += contribution(bc_ref)

    @pl.when(n == 3)
    def _():
        o_ref[...] = (acc_ref[...] + contribution(bd_ref)).astype(o_ref.dtype)


def gated_merge(z, branches, w_branch, tm):
    m = z.shape[0]
    gate_blk0 = Z_GATE // D_MODEL
    br_spec = pl.BlockSpec((tm, BRANCH_W), lambda i, n: (i, 0))
    return pl.pallas_call(
        _merge_kernel,
        grid=(m // tm, N_BRANCH),
        in_specs=[pl.BlockSpec((tm, D_MODEL), lambda i, n: (i, gate_blk0 + n)),
                  br_spec, br_spec, br_spec, br_spec,
                  pl.BlockSpec((1, BRANCH_W, D_MODEL), lambda i, n: (n, 0, 0))],
        out_specs=pl.BlockSpec((tm, D_MODEL), lambda i, n: (i, 0)),
        out_shape=jax.ShapeDtypeStruct((m, D_MODEL), BF16),
        scratch_shapes=[pltpu.VMEM((tm, D_MODEL), F32)],
        compiler_params=_params("parallel", "arbitrary"),
        name="gated_merge",
    )(z, *branches, w_branch)


NA_ROWS_LOCKSTEP = 8
NA_MASK = -1e30


def na_bias_table(rpb):
    q = np.arange(GRID_W)
    kc = np.arange(GRID_W)
    cs = np.clip(q - NA_WIN_C // 2, 0, GRID_W - NA_WIN_C)
    ok = (kc[None, :] >= cs[:, None]) & (kc[None, :] < cs[:, None] + NA_WIN_C)
    dcol = np.clip(kc[None, :] - q[:, None] + NA_WIN_C - 1, 0, 2 * NA_WIN_C - 2)
    drow = np.arange(NA_WIN_R)[None, :] - np.arange(NA_WIN_R)[:, None] + NA_WIN_R - 1
    t = rpb.astype(F32)[:, drow]
    t = t[:, :, :, dcol]
    t = jnp.transpose(t, (1, 0, 3, 2, 4))
    t = jnp.where(jnp.asarray(ok)[None, None, :, None, :], t, NA_MASK)
    return t.reshape(NA_WIN_R, NA_HEADS, GRID_W, NA_WIN_R * GRID_W)


def _na_kernel(q_ref, k_ref, v_ref, bias_ref, o_ref, *, rows_per_step, rows):
    rblk = pl.program_id(1)
    win = NA_WIN_R * GRID_W
    scale = NA_HEAD_DIM ** -0.5

    pair_w = 2 * NA_HEAD_DIM
    low = lax.broadcasted_iota(jnp.int32, (GRID_W, pair_w), 1) < NA_HEAD_DIM

    def rows_body(it, carry):
        inst = []
        q, kb, vb, off, qstart = [], [], [], [], []
        for t in range(NA_ROWS_LOCKSTEP):
            rl = it * NA_ROWS_LOCKSTEP + t
            r = rblk * rows_per_step + rl
            rs = jnp.clip(r - NA_WIN_R // 2, 0, rows - NA_WIN_R)
            off.append(r - rs)
            qstart.append(pl.multiple_of(rl * GRID_W, GRID_W))
            kstart = pl.multiple_of(rs * GRID_W, GRID_W)
            q.append(q_ref[0, pl.ds(qstart[t], GRID_W), :])
            kb.append(k_ref[0, pl.ds(kstart, win), :])
            vb.append(v_ref[0, pl.ds(kstart, win), :])
            inst += [(t, h) for h in range(NA_HEADS)]

        def pair(x, h):
            return x[:, (h // 2) * pair_w:(h // 2 + 1) * pair_w]

        qm = [jnp.where(low if h % 2 == 0 else ~low, pair(q[t], h), jnp.zeros((), BF16)) for t, h in inst]
        s = [lax.dot_general(qm[i], pair(kb[t], h), (((1,), (1,)), ((), ())), preferred_element_type=F32)
             for i, (t, h) in enumerate(inst)]
        s = [s[i] * scale + bias_ref[off[t], h] for i, (t, h) in enumerate(inst)]
        p = [jnp.exp(x - jnp.max(x, axis=-1, keepdims=True)) for x in s]
        l = [jnp.sum(x, axis=-1, keepdims=True) for x in p]
        o = [jnp.dot(p[i].astype(BF16), pair(vb[t], h), preferred_element_type=F32) / l[i]
             for i, (t, h) in enumerate(inst)]
        for t in range(NA_ROWS_LOCKSTEP):
            base = t * NA_HEADS
            outs = [jnp.where(low, o[base + h], o[base + h + 1]) for h in range(0, NA_HEADS, 2)]
            o_ref[0, pl.ds(qstart[t], GRID_W), :] = jnp.concatenate(outs, axis=1).astype(o_ref.dtype)
        return carry

    lax.fori_loop(0, rows_per_step // NA_ROWS_LOCKSTEP, rows_body, 0)


def neighbourhood_attention(z3, bias, rows_per_step):
    b, l, _ = z3.shape
    rows = l // GRID_W
    w = NA_HEADS * NA_HEAD_DIM
    blk = Z_NA // w
    return pl.pallas_call(
        functools.partial(_na_kernel, rows_per_step=rows_per_step, rows=rows),
        grid=(b, rows // rows_per_step),
        in_specs=[pl.BlockSpec((1, rows_per_step * GRID_W, w), lambda i, r: (i, r, blk)),
                  pl.BlockSpec((1, l, w), lambda i, r: (i, 0, blk + 1)),
                  pl.BlockSpec((1, l, w), lambda i, r: (i, 0, blk + 2)),
                  pl.BlockSpec(bias.shape, lambda i, r: (0, 0, 0, 0))],
        out_specs=pl.BlockSpec((1, rows_per_step * GRID_W, w), lambda i, r: (i, r, 0)),
        out_shape=jax.ShapeDtypeStruct((b, l, w), BF16),
        compiler_params=_params("parallel", "arbitrary"),
        name="neighbourhood_attention",
    )(z3, z3, z3, bias)


def _fft_tables(l):
    n = 2 * l
    n2 = FFT_N2
    n1 = n // n2
    a = np.arange(n1)
    b = np.arange(n2)
    k1 = np.arange(n1)
    t = n2 * a[None, :] + b[:, None]
    th = 2.0 * np.pi * (k1[None, :, None] * t[:, None, :] % n) / n
    fa = np.concatenate([np.cos(th), -np.sin(th)], axis=1)
    th_t = np.transpose(th, (0, 2, 1))
    ga = np.concatenate([np.cos(th_t), -np.sin(th_t)], axis=2)
    ph = 2.0 * np.pi * (np.outer(b, b) % n2) / n2
    cr, ci = np.cos(ph), -np.sin(ph)
    fb = np.block([[cr, -ci], [ci, cr]])
    fbi = np.block([[cr, ci], [-ci, cr]])
    return fa, ga, fb, fbi


def _filter_fft_kernel(hf_ref, hb_ref, fa_hi_ref, fa_lo_ref, fb_hi_ref, fb_lo_ref, o_ref, y_ref, *, n1):
    n2 = FFT_N2
    na = n1 // 2
    inv_n = 1.0 / (n1 * n2)

    def transform(src_ref, combine):
        def step_a(b, carry):
            xb = _split_bf16(src_ref[pl.ds(b, na, stride=n2), :])
            r = _dot_split((fa_hi_ref[b], fa_lo_ref[b]), xb)
            y_ref[pl.ds(b, n1, stride=2 * n2), :] = r[:n1]
            y_ref[pl.ds(b + n2, n1, stride=2 * n2), :] = r[n1:]
            return carry

        lax.fori_loop(0, n2, step_a, 0, unroll=16)

        def step_b(k1, carry):
            r0 = pl.multiple_of(k1 * 2 * n2, 2 * n2)
            x = _dot_split((fb_hi_ref[...], fb_lo_ref[...]), _split_bf16(y_ref[pl.ds(r0, 2 * n2), :]))
            combine(r0, x * inv_n)
            return carry

        lax.fori_loop(0, n1, step_b, 0, unroll=8)

    def store(r0, x):
        o_ref[pl.ds(r0, 2 * n2), :] = x

    def add_conjugate(r0, x):
        o_ref[pl.ds(r0, n2), :] += x[:n2]
        o_ref[pl.ds(r0 + n2, n2), :] -= x[n2:]

    transform(hf_ref, store)
    transform(hb_ref, add_conjugate)


def filter_spectrum(h, fa, fb):
    l = h.shape[0]
    c = h.shape[1] // 2
    n = 2 * l
    n1 = n // FFT_N2
    nch = c // LANES
    return pl.pallas_call(
        functools.partial(_filter_fft_kernel, n1=n1),
        grid=(nch,),
        in_specs=[pl.BlockSpec((l, LANES), lambda i: (0, i)),
                  pl.BlockSpec((l, LANES), lambda i: (0, nch + i)),
                  pl.BlockSpec(fa[0].shape, lambda i: (0, 0, 0)),
                  pl.BlockSpec(fa[1].shape, lambda i: (0, 0, 0)),
                  pl.BlockSpec(fb[0].shape, lambda i: (0, 0)),
                  pl.BlockSpec(fb[1].shape, lambda i: (0, 0))],
        out_specs=pl.BlockSpec((2 * n, LANES), lambda i: (0, i)),
        out_shape=jax.ShapeDtypeStruct((2 * n, c), F32),
        scratch_shapes=[pltpu.VMEM((2 * n, LANES), F32)],
        compiler_params=_params("parallel"),
        name="hyena_filter_spectrum",
    )(h, h, *fa, *fb)


def _hyena_filter_kernel(z_ref, w1_ref, b1_ref, w2_ref, b2_ref, w3_ref, dec_ref, o_ref):
    hid = jnp.sin(jnp.dot(z_ref[...], w1_ref[...], precision=HIGHEST, preferred_element_type=F32) + b1_ref[...])
    hid = jnp.sin(jnp.dot(hid, w2_ref[...], precision=HIGHEST, preferred_element_type=F32) + b2_ref[...])
    h = jnp.dot(hid, w3_ref[...], precision=HIGHEST, preferred_element_type=F32)
    dec = dec_ref[...]
    hf = h[:, :HY_WIDTH] * dec
    hb = h[:, HY_WIDTH:] * dec
    norm = (jnp.sum(jnp.abs(hf), axis=0, keepdims=True) + jnp.sum(jnp.abs(hb), axis=0, keepdims=True)) + RMS_EPS
    o_ref[:, :HY_WIDTH] = hf / norm
    o_ref[:, HY_WIDTH:] = hb / norm


def hyena_filters(l, w1, b1, w2, b2, w3):
    t = jnp.linspace(0.0, 1.0, l, dtype=F32)[:, None]
    w = 2.0 * math.pi * jnp.arange(l, dtype=F32)[:, None] / l
    bands = jnp.linspace(1e-4, HY_POS_BANDS - 1, HY_POS_BANDS, dtype=F32)[None, :]
    z = jnp.concatenate([t, jnp.cos(bands * w), -jnp.sin(bands * w)], axis=-1)
    pad = LANES - z.shape[1]
    z = jnp.pad(z, ((0, 0), (0, pad)))
    w1p = jnp.pad(w1.astype(F32), ((0, pad), (0, 0)))
    max_decay = math.log(HY_DECAY_TARGET) / HY_FAST_DECAY
    min_decay = math.log(HY_DECAY_TARGET) / HY_SLOW_DECAY
    deltas = jnp.abs(jnp.linspace(min_decay, max_decay, HY_WIDTH, dtype=F32))
    dec = jnp.exp(-t * deltas[None, :])
    return pl.pallas_call(
        _hyena_filter_kernel,
        out_shape=jax.ShapeDtypeStruct((l, 2 * HY_WIDTH), F32),
        compiler_params=pltpu.CompilerParams(vmem_limit_bytes=VMEM_LIMIT_BYTES),
        name="hyena_filter_ffn",
    )(z, w1p, b1.reshape(1, -1).astype(F32), w2.astype(F32), b2.reshape(1, -1).astype(F32), w3.astype(F32), dec)


CONV_PAD = 8


def _dwconv3(x, w_ref, pad_ref):
    n = x.shape[0]
    zeros = jnp.zeros((CONV_PAD, x.shape[1]), F32)
    pad_ref[0:CONV_PAD, :] = zeros
    pad_ref[CONV_PAD + n:2 * CONV_PAD + n, :] = zeros
    pad_ref[CONV_PAD:CONV_PAD + n, :] = x
    w = w_ref[...].astype(F32)
    return (pad_ref[CONV_PAD - 1:CONV_PAD - 1 + n, :] * w[0:1] + x * w[1:2]
            + pad_ref[CONV_PAD + 1:CONV_PAD + 1 + n, :] * w[2:3])


def _hyena_kernel(x1_ref, x2_ref, v_ref, w1_ref, w2_ref, wv_ref, skip_ref, kf_ref, fa_ref, ga_ref, fb_ref, fbi_ref,
                  o_ref, xs_ref, y_ref, pad_ref, *, n1):
    n2 = FFT_N2
    na = n1 // 2
    pa, py = FFT_PITCH_X, FFT_PITCH_Y
    vg = _dwconv3(v_ref[0].astype(F32), wv_ref, pad_ref) * _dwconv3(x1_ref[0].astype(F32), w1_ref, pad_ref)
    for a in range(na):
        xs_ref[a * pa:a * pa + n2, :] = vg[a * n2:(a + 1) * n2]

    def fwd_a(b, carry):
        xb = xs_ref[pl.ds(b, na, stride=pa), :].astype(BF16)
        r = jnp.dot(fa_ref[b], xb, preferred_element_type=F32)
        y_ref[pl.ds(b, n1, stride=py), :] = r[:n1]
        y_ref[pl.ds(b + n2, n1, stride=py), :] = r[n1:]
        return carry

    lax.fori_loop(0, n2, fwd_a, 0, unroll=FFT_UNROLL)

    def mid(kg, carry):
        grp = range(FFT_GROUP)
        k1 = [kg * FFT_GROUP + t for t in grp]
        r0 = [pl.multiple_of(k * py, 8) for k in k1]
        f0 = [pl.multiple_of(k * 2 * n2, 2 * n2) for k in k1]
        fb = fb_ref[...]
        x = [jnp.dot(fb, y_ref[pl.ds(r0[t], 2 * n2), :].astype(BF16), preferred_element_type=F32) for t in grp]
        p = []
        for t in grp:
            xr, xi = x[t][:n2], x[t][n2:]
            kr = kf_ref[pl.ds(f0[t], n2), :]
            ki = kf_ref[pl.ds(f0[t] + n2, n2), :]
            p.append(jnp.concatenate([xr * kr - xi * ki, xr * ki + xi * kr], axis=0).astype(BF16))
        fbi = fbi_ref[...]
        q = [jnp.dot(fbi, p[t], preferred_element_type=F32) for t in grp]
        for t in grp:
            y_ref[pl.ds(r0[t], 2 * n2), :] = q[t]
        return carry

    lax.fori_loop(0, n1 // FFT_GROUP, mid, 0)

    def inv_a(b, carry):
        qb = jnp.concatenate([y_ref[pl.ds(b, n1, stride=py), :],
                              y_ref[pl.ds(b + n2, n1, stride=py), :]], axis=0).astype(BF16)
        xs_ref[pl.ds(b, na, stride=pa), :] = jnp.dot(ga_ref[b], qb, preferred_element_type=F32)
        return carry

    lax.fori_loop(0, n2, inv_a, 0, unroll=FFT_UNROLL)

    gate = _dwconv3(x2_ref[0].astype(F32), w2_ref, pad_ref)
    skip = skip_ref[...].astype(F32)
    for a in range(na):
        rows = slice(a * n2, (a + 1) * n2)
        o_ref[0, rows, :] = ((xs_ref[a * pa:a * pa + n2, :] + vg[rows] * skip) * gate[rows]).astype(o_ref.dtype)


def hyena_mixer(z3, w_short, skip, kf, tables):
    b, l, _ = z3.shape
    n1 = 2 * l // FFT_N2
    fa, ga, fb, fbi = tables
    nch = HY_WIDTH // LANES
    blk = Z_HY // LANES

    def zspec(seg):
        return pl.BlockSpec((1, l, LANES), lambda c, i: (i, 0, blk + seg * nch + c))

    def wspec(seg):
        return pl.BlockSpec((3, LANES), lambda c, i: (0, seg * nch + c))

    return pl.pallas_call(
        functools.partial(_hyena_kernel, n1=n1),
        grid=(nch, b),
        in_specs=[zspec(0), zspec(1), zspec(2), wspec(0), wspec(1), wspec(2),
                  pl.BlockSpec((1, LANES), lambda c, i: (0, c)),
                  pl.BlockSpec((4 * l, LANES), lambda c, i: (0, c)),
                  pl.BlockSpec(fa.shape, lambda c, i: (0, 0, 0)),
                  pl.BlockSpec(ga.shape, lambda c, i: (0, 0, 0)),
                  pl.BlockSpec(fb.shape, lambda c, i: (0, 0)),
                  pl.BlockSpec(fbi.shape, lambda c, i: (0, 0))],
        out_specs=pl.BlockSpec((1, l, LANES), lambda c, i: (i, 0, c)),
        out_shape=jax.ShapeDtypeStruct((b, l, HY_WIDTH), BF16),
        scratch_shapes=[pltpu.VMEM((n1 // 2 * FFT_PITCH_X, LANES), F32), pltpu.VMEM((n1 * FFT_PITCH_Y, LANES), F32),
                        pltpu.VMEM((l + 2 * CONV_PAD, LANES), F32)],
        compiler_params=_params("parallel", "arbitrary"),
        name="hyena_mixer",
    )(z3, z3, z3, w_short, w_short, w_short, skip.reshape(1, -1), kf, fa, ga, fb, fbi)


MLA_HW = 2 * LANES
MLA_HEADS_PER_STEP = 2


def _rope_group(g, cs):
    prod = g * cs
    s = prod + pltpu.roll(prod, MLA_ROPE, 1)
    lane = lax.broadcasted_iota(jnp.int32, s.shape, 1)
    return jnp.where(lane < MLA_ROPE, s, 0.0)


def _mla_prep_kernel(cq_ref, ckv_ref, kr_ref, gq_ref, gkv_ref, wq_ref, wkv_ref, cs_ref, q_ref, k_ref, v_ref):
    cs = cs_ref[...]
    hq = _rms_bf16(cq_ref[...], gq_ref[...])
    q = jnp.dot(hq, wq_ref[...], preferred_element_type=F32)
    hkv = _rms_bf16(ckv_ref[...], gkv_ref[...])
    kv = jnp.dot(hkv, wkv_ref[...], preferred_element_type=F32)
    k_rope = _rope_group(kr_ref[...].astype(F32), cs).astype(BF16)
    lane = lax.broadcasted_iota(jnp.int32, (cs.shape[0], LANES), 1)
    ones_lane = jnp.where(lane == 0, 1.0, 0.0).astype(BF16)
    for h in range(MLA_HEADS):
        o = h * MLA_HW
        q_ref[:, o:o + LANES] = q[:, o:o + LANES].astype(BF16)
        q_ref[:, o + LANES:o + MLA_HW] = _rope_group(q[:, o + LANES:o + MLA_HW], cs).astype(BF16)
        k_ref[:, o:o + LANES] = kv[:, o:o + LANES].astype(BF16)
        k_ref[:, o + LANES:o + MLA_HW] = k_rope
        v_ref[:, o:o + LANES] = kv[:, o + LANES:o + MLA_HW].astype(BF16)
        v_ref[:, o + LANES:o + MLA_HW] = ones_lane


def _mla_attn_kernel(q_ref, k_ref, v_ref, o_ref):
    c = (MLA_NOPE + MLA_ROPE) ** -0.5 * math.log2(math.e)
    heads = range(MLA_HEADS_PER_STEP)
    s = [lax.dot_general(q_ref[0, :, h * MLA_HW:(h + 1) * MLA_HW], k_ref[0, :, h * MLA_HW:(h + 1) * MLA_HW],
                         (((1,), (1,)), ((), ())), preferred_element_type=F32) for h in heads]
    for h in heads:
        p = jnp.exp2((s[h] - jnp.max(s[h], axis=-1, keepdims=True)) * c)
        o = jnp.dot(p.astype(BF16), v_ref[0, :, h * MLA_HW:(h + 1) * MLA_HW], preferred_element_type=F32)
        o_ref[0, :, h * MLA_V:(h + 1) * MLA_V] = (o[:, :MLA_V] / o[:, MLA_V:MLA_V + 1]).astype(o_ref.dtype)


def mla_mixer(z, b, l, g_q, g_kv, wq_p, w_ukv, cs_tab, tm, tq):
    m = z.shape[0]
    lt = l // tm
    qp, kp, vp = pl.pallas_call(
        _mla_prep_kernel,
        grid=(m // tm,),
        in_specs=[pl.BlockSpec((tm, MLA_Q_RANK), lambda i: (i, Z_CQ // MLA_Q_RANK)),
                  pl.BlockSpec((tm, MLA_KV_RANK), lambda i: (i, Z_CKV // MLA_KV_RANK)),
                  pl.BlockSpec((tm, LANES), lambda i: (i, Z_KR // LANES)),
                  pl.BlockSpec((1, MLA_Q_RANK), lambda i: (0, 0)),
                  pl.BlockSpec((1, MLA_KV_RANK), lambda i: (0, 0)),
                  pl.BlockSpec(wq_p.shape, lambda i: (0, 0)),
                  pl.BlockSpec(w_ukv.shape, lambda i: (0, 0)),
                  pl.BlockSpec((tm, LANES), lambda i: (i % lt, 0))],
        out_specs=[pl.BlockSpec((tm, MLA_HEADS * MLA_HW), lambda i: (i, 0)),
                   pl.BlockSpec((tm, MLA_HEADS * MLA_HW), lambda i: (i, 0)),
                   pl.BlockSpec((tm, MLA_HEADS * MLA_HW), lambda i: (i, 0))],
        out_shape=[jax.ShapeDtypeStruct((m, MLA_HEADS * MLA_HW), BF16),
                   jax.ShapeDtypeStruct((m, MLA_HEADS * MLA_HW), BF16),
                   jax.ShapeDtypeStruct((m, MLA_HEADS * MLA_HW), BF16)],
        compiler_params=_params("parallel"),
        name="mla_prep",
    )(z, z, z, g_q.reshape(1, -1), g_kv.reshape(1, -1), wq_p, w_ukv, cs_tab)
    qp = qp.reshape(b, l, -1)
    kp = kp.reshape(b, l, -1)
    vp = vp.reshape(b, l, -1)
    hs = MLA_HEADS_PER_STEP
    return pl.pallas_call(
        _mla_attn_kernel,
        grid=(b, MLA_HEADS // hs, l // tq),
        in_specs=[pl.BlockSpec((1, tq, hs * MLA_HW), lambda i, h, t: (i, t, h)),
                  pl.BlockSpec((1, l, hs * MLA_HW), lambda i, h, t: (i, 0, h)),
                  pl.BlockSpec((1, l, hs * MLA_HW), lambda i, h, t: (i, 0, h))],
        out_specs=pl.BlockSpec((1, tq, hs * MLA_V), lambda i, h, t: (i, t, h)),
        out_shape=jax.ShapeDtypeStruct((b, l, MLA_HEADS * MLA_V), BF16),
        compiler_params=_params("parallel", "parallel", "arbitrary"),
        name="mla_attention",
    )(qp, kp, vp)


def _split_bf16(x):
    hi = x.astype(BF16)
    return hi, (x - hi.astype(F32)).astype(BF16)


def _dot_split(a, b):
    (ah, al), (bh, bl) = a, b
    return (jnp.dot(ah, bh, preferred_element_type=F32)
            + (jnp.dot(ah, bl, preferred_element_type=F32) + jnp.dot(al, bh, preferred_element_type=F32)))


def _tri_unit_inverse(mats):
    c = mats[0].shape[0]
    eye = (lax.broadcasted_iota(jnp.int32, (c, c), 0) == lax.broadcasted_iota(jnp.int32, (c, c), 1)).astype(F32)
    ps = [eye - a for a in mats]
    ms = list(mats)
    for _ in range(int(math.log2(c)) - 1):
        splits = [_split_bf16(m) for m in ms]
        ms = [_dot_split(sp, sp) for sp in splits]
        yield
        ps = [_dot_split(_split_bf16(p), _split_bf16(eye + m)) for p, m in zip(ps, ms)]
        yield
    return ps


def _interleave(main, side=None):
    gens = [g for g in (main, side) if g is not None]
    results = [None] * len(gens)
    live = [True] * len(gens)
    while any(live):
        for idx, gen in enumerate(gens):
            if live[idx]:
                try:
                    next(gen)
                except StopIteration as stop:
                    results[idx] = stop.value
                    live[idx] = False
    return results


def _bdot(a, b):
    return jnp.dot(a.astype(BF16), b.astype(BF16), preferred_element_type=F32)


def _bdot_nt(a, b):
    return lax.dot_general(a.astype(BF16), b.astype(BF16), (((1,), (1,)), ((), ())), preferred_element_type=F32)


def _dn_group_setup(chunks, refs):
    q_s, k_s, v_s, g_s, b_s = refs
    cc = DN_CHUNK
    n = range(len(chunks))
    rev = [r for _, r in chunks]
    r0 = [pl.multiple_of(c * cc, cc) for c, _ in chunks]
    q = [q_s[pl.ds(r, cc), :] for r in r0]
    k = [k_s[pl.ds(r, cc), :] for r in r0]
    v = [v_s[pl.ds(r, cc), :] for r in r0]
    lane = [2 if r else 0 for r in rev]
    ri = lax.broadcasted_iota(jnp.int32, (cc, cc), 0)
    ci = lax.broadcasted_iota(jnp.int32, (cc, cc), 1)
    incl = [(ri <= ci) if r else (ri >= ci) for r in rev]
    strict = [(ri < ci) if r else (ri > ci) for r in rev]
    g = [g_s[pl.ds(r, cc), :] for r in r0]
    g1 = [x.astype(BF16) for x in g]
    r1 = [x - h.astype(F32) for x, h in zip(g, g1)]
    g2 = [x.astype(BF16) for x in r1]
    g3 = [(x - h.astype(F32)).astype(BF16) for x, h in zip(r1, g2)]
    ones = [m.astype(BF16) for m in incl]
    gc = [jnp.dot(ones[i], g1[i], preferred_element_type=F32) for i in n]
    gc = [gc[i] + jnp.dot(ones[i], g2[i], preferred_element_type=F32) for i in n]
    gc = [gc[i] + jnp.dot(ones[i], g3[i], preferred_element_type=F32) for i in n]
    yield
    gcol = [gc[i][:, lane[i]:lane[i] + 1] for i in n]
    grow = [jnp.transpose(gc[i])[lane[i]:lane[i] + 1, :] for i in n]
    beta = [b_s[pl.ds(r0[i], cc), :][:, lane[i] + 1:lane[i] + 2] for i in n]
    g_last = [gcol[i][0:1, :] if rev[i] else gcol[i][cc - 1:cc, :] for i in n]
    decay = [jnp.where(incl[i], jnp.exp(jnp.where(incl[i], gcol[i] - grow[i], 0.0)), 0.0) for i in n]
    e_g = [jnp.exp(x) for x in gcol]
    kb = [k[i] * beta[i] for i in n]
    a_both = [_bdot_nt(jnp.concatenate([kb[i], q[i]], axis=0), k[i]) for i in n]
    yield
    a_kk = [jnp.where(strict[i], a_both[i][:cc] * decay[i], 0.0) for i in n]
    a_qk = [jnp.where(incl[i], a_both[i][cc:] * decay[i], 0.0) for i in n]
    t_inv = yield from _tri_unit_inverse(a_kk)
    uw = [_bdot(t_inv[i], jnp.concatenate([v[i] * beta[i], kb[i] * e_g[i]], axis=1)) for i in n]
    yield
    k_dec = [k[i] * jnp.exp(g_last[i] - gcol[i]) for i in n]
    wq = [jnp.concatenate([uw[i][:, DN_DV:], q[i] * e_g[i]], axis=0).astype(BF16) for i in n]
    ak = [jnp.concatenate([a_qk[i], jnp.transpose(k_dec[i])], axis=0).astype(BF16) for i in n]
    return [(wq[i], ak[i], uw[i][:, :DN_DV], jnp.exp(g_last[i])) for i in n]


def _dn_group_steps(states, setups, out_refs, chunks):
    cc = DN_CHUNK
    dirs = range(len(states))
    for t in range(len(setups[0])):
        st = [setups[d][t] for d in dirs]
        r = [jnp.dot(st[d][0], states[d].astype(BF16), preferred_element_type=F32) for d in dirs]
        yield
        v_new = [st[d][2] - r[d][:cc] for d in dirs]
        r2 = [jnp.dot(st[d][1], v_new[d].astype(BF16), preferred_element_type=F32) for d in dirs]
        yield
        states = [states[d] * st[d][3] + r2[d][cc:] for d in dirs]
        for d in dirs:
            out_refs[d][pl.ds(pl.multiple_of(chunks[d][t] * cc, cc), cc), :] = r[d][cc:] + r2[d][:cc]
    return states


def _dn_kernel(zq_ref, zk_ref, zv_ref, zg_ref, ab_ref, wq_ref, wk_ref, wv_ref, alog_ref, dtb_ref, gn_ref, o_ref,
               q_s, k_s, v_s, g_s, b_s, of_s, ob_s, pad_ref, *, nchunks):
    def act(z_ref, w_ref):
        c = _dwconv3(z_ref[0].astype(F32), w_ref, pad_ref)
        return _silu(c)

    def l2n(x):
        return x * lax.rsqrt(jnp.sum(x * x, axis=-1, keepdims=True) + RMS_EPS)

    q_s[...] = l2n(act(zq_ref, wq_ref)) * (DN_DK ** -0.5)
    k_s[...] = l2n(act(zk_ref, wk_ref))
    v_s[...] = act(zv_ref, wv_ref)
    ab = ab_ref[0]
    x = ab + dtb_ref[0]
    softplus = jnp.maximum(x, 0.0) + jnp.log1p(jnp.exp(-jnp.abs(x)))
    g_s[...] = -jnp.exp(alog_ref[0]) * softplus
    b_s[...] = jax.nn.sigmoid(ab)
    refs = (q_s, k_s, v_s, g_s, b_s)

    ngroups = nchunks // DN_GROUP

    def group_chunks(gi):
        fwd = [gi * DN_GROUP + t for t in range(DN_GROUP)]
        return fwd, [nchunks - 1 - c for c in fwd]

    def setup_gen(gi):
        fwd, bwd = group_chunks(gi)
        return _dn_group_setup([(c, False) for c in fwd] + [(c, True) for c in bwd], refs)

    def steps_gen(gi, states, flat):
        setups = [tuple(flat[4 * i:4 * i + 4]) for i in range(2 * DN_GROUP)]
        return _dn_group_steps(states, [setups[:DN_GROUP], setups[DN_GROUP:]], (of_s, ob_s), group_chunks(gi))

    def flatten(setups):
        return [x for st in setups for x in st]

    def body(gi, carry):
        nxt, states = _interleave(setup_gen(gi + 1), steps_gen(gi, list(carry[:2]), carry[2:]))
        return (*states, *flatten(nxt))

    zero = jnp.zeros((DN_DK, DN_DV), F32)
    (first,) = _interleave(setup_gen(0))
    carry = lax.fori_loop(0, ngroups - 1, body, (zero, zero, *flatten(first)))
    _interleave(steps_gen(ngroups - 1, list(carry[:2]), carry[2:]))

    o = of_s[...] + ob_s[...]
    o = o * lax.rsqrt(jnp.mean(o * o, axis=-1, keepdims=True) + RMS_EPS) * gn_ref[...]
    gate = zg_ref[0].astype(F32)
    o_ref[0] = (o * _silu(gate)).astype(o_ref.dtype)


def deltanet_mixer(z3, ab3, w_conv, alog_p, dtb_p, g_norm):
    b, l, _ = z3.shape
    blk = Z_DNQKV // LANES
    gblk = Z_DNGATE // LANES
    hh = DN_HEADS

    def zspec(off):
        return pl.BlockSpec((1, l, LANES), lambda i, h: (i, 0, off + h))

    def wspec(seg):
        return pl.BlockSpec((3, LANES), lambda i, h: (0, seg * hh + h))

    vec = pl.BlockSpec((1, 1, LANES), lambda i, h: (h, 0, 0))
    seq = pltpu.VMEM((l, LANES), F32)
    return pl.pallas_call(
        functools.partial(_dn_kernel, nchunks=l // DN_CHUNK),
        grid=(b, hh),
        in_specs=[zspec(blk), zspec(blk + hh), zspec(blk + 2 * hh), zspec(gblk),
                  pl.BlockSpec((1, l, LANES), lambda i, h: (i, 0, h)),
                  wspec(0), wspec(1), wspec(2), vec, vec,
                  pl.BlockSpec((1, LANES), lambda i, h: (0, 0))],
        out_specs=pl.BlockSpec((1, l, LANES), lambda i, h: (i, 0, h)),
        out_shape=jax.ShapeDtypeStruct((b, l, hh * DN_DV), BF16),
        scratch_shapes=[seq] * 7 + [pltpu.VMEM((l + 2 * CONV_PAD, LANES), F32)],
        compiler_params=_params("parallel", "arbitrary"),
        name="deltanet_mixer",
    )(z3, z3, z3, z3, ab3, w_conv, w_conv, w_conv, alog_p, dtb_p, g_norm.reshape(1, -1))


def _rotate_half_cols(w):
    half = w.shape[-1] // 2
    return jnp.concatenate([-w[..., half:], w[..., :half]], axis=-1)


def _prep_w_in(w_in):
    cuts = np.cumsum([0, 1536, 1536, MLA_Q_RANK, MLA_KV_RANK, MLA_ROPE, 1536, 512, 16])
    dnab = w_in[:, cuts[7]:cuts[8]]
    w16 = w_in.astype(BF16)
    na, hy, cq, ckv, kr, dnqkv, dngate = (w16[:, cuts[i]:cuts[i + 1]] for i in range(7))
    gates = w16[:, cuts[8]:]
    pad = jnp.zeros((w_in.shape[0], Z_GATE - Z_KR - 2 * MLA_ROPE), BF16)
    main = jnp.concatenate([na, hy, dnqkv, dngate, cq, ckv, kr, _rotate_half_cols(kr), pad, 0.5 * gates], axis=1)
    ab = dnab.reshape(-1, 2, 2, DN_HEADS)
    ab = jnp.transpose(ab, (0, 3, 1, 2)).reshape(-1, DN_HEADS, 4)
    ab = jnp.pad(ab, ((0, 0), (0, 0), (0, LANES - 4))).reshape(-1, DN_HEADS * LANES)
    return main, ab.astype(BF16)


def _prep_w_uq(w_uq):
    k = w_uq.shape[0]
    w = w_uq.reshape(k, MLA_HEADS, MLA_NOPE + MLA_ROPE)
    rope_w = w[:, :, MLA_NOPE:]
    return jnp.concatenate([w, _rotate_half_cols(rope_w)], axis=-1).reshape(k, MLA_HEADS * MLA_HW).astype(BF16)


def _head_lane_vec(p):
    v = jnp.zeros((DN_HEADS, 1, LANES), F32)
    v = v.at[:, 0, 0].set(p[0].astype(F32))
    return v.at[:, 0, 2].set(p[1].astype(F32))


def _rope_table(l):
    half = MLA_ROPE // 2
    inv = ROPE_THETA ** (-jnp.arange(half, dtype=F32) / half)
    ang = jnp.arange(l, dtype=F32)[:, None] * inv[None, :]
    cos, sin = jnp.cos(ang), jnp.sin(ang)
    return jnp.concatenate([cos, cos, sin, sin], axis=-1)


def _split_const(x):
    hi = x.astype(BF16)
    lo = (x - hi.astype(np.float64)).astype(BF16)
    return jnp.asarray(hi), jnp.asarray(lo)


def _pick(n, pref):
    for t in pref:
        if n % t == 0:
            return t
    return n


def trunk(x_parts, l, norm_mix, w_in, na_rpb, hy_short, hy_skip, hy_w1, hy_b1, hy_w2, hy_b2, hy_w3,
          mla_g_q, mla_g_kv, mla_w_uq, mla_w_ukv, dn_conv, dn_a_log, dn_dt_bias, dn_g_norm,
          w_branch, w_out, norm_mlp, w_up, w_down, norm_final):
    d = x_parts[0].shape[1]
    part_rows = tuple(p.shape[0] for p in x_parts)
    m = sum(part_rows)
    b = m // l
    depth = w_in.shape[0]
    common = math.gcd(*part_rows)
    tm_big = _pick(common, (1024, 512, 256, 128))
    tm_mid = _pick(common, (512, 256, 128))
    fa, ga, fb, fbi = _fft_tables(l)
    n1 = 2 * l // FFT_N2
    tables = (jnp.asarray(fa[:, :, :n1 // 2], BF16), jnp.asarray(ga[:, :n1 // 2, :], BF16),
              jnp.asarray(fb, BF16), jnp.asarray(fbi, BF16))
    fa_split, fb_split = _split_const(fa[:, :, :n1 // 2]), _split_const(fb)
    cs_tab = _rope_table(l)
    x = tuple(x_parts)
    for layer in range(depth):
        last = layer == depth - 1
        w_main, w_ab = _prep_w_in(w_in[layer])
        z = norm_matmul(x, norm_mix[layer], w_main, BF16, tm_big if len(x) == 1 else tm_mid, 2048)
        ab = norm_matmul(x, norm_mix[layer], w_ab, F32, tm_big, DN_HEADS * LANES)
        z3 = z.reshape(b, l, Z_COLS)
        br_a = neighbourhood_attention(z3, na_bias_table(na_rpb[layer]), _pick(l // GRID_W, (8, 4, 2, 1)))
        taps = hyena_filters(l, hy_w1[layer], hy_b1[layer], hy_w2[layer], hy_b2[layer], hy_w3[layer])
        kf = filter_spectrum(taps, fa_split, fb_split)
        br_b = hyena_mixer(z3, hy_short[layer], hy_skip[layer], kf, tables)
        br_c = mla_mixer(z, b, l, mla_g_q[layer], mla_g_kv[layer], _prep_w_uq(mla_w_uq[layer]),
                         mla_w_ukv[layer].astype(BF16), cs_tab, tm_mid, _pick(l, (512, 256, 128)))
        br_d = deltanet_mixer(z3, ab.reshape(b, l, -1), dn_conv[layer], _head_lane_vec(dn_a_log[layer]),
                              _head_lane_vec(dn_dt_bias[layer]), dn_g_norm[layer])
        branches = [t.reshape(m, BRANCH_W) for t in (br_a, br_b, br_c, br_d)]
        merged = gated_merge(z, branches, (0.5 * w_branch[layer]).astype(BF16), tm_mid)
        x_mid = matmul_residual(merged, w_out[layer].astype(BF16), x, tm_mid)
        x = tuple(mlp_block(x_mid, norm_mlp[layer], w_up[layer].astype(BF16), w_down[layer].astype(BF16), norm_final,
                            last, tm_mid, 1024, part_rows if last else (m,)))
    return x


def kernel(x_prompt, x_sample, norm_mix, w_in, na_rpb, hy_short, hy_skip, hy_w1, hy_b1, hy_w2, hy_b2, hy_w3,
           mla_g_q, mla_g_kv, mla_w_uq, mla_w_ukv, dn_conv, dn_a_log, dn_dt_bias, dn_g_norm,
           w_branch, w_out, norm_mlp, w_up, w_down, norm_final):
    assert x_prompt.shape[1:] == x_sample.shape[1:]
    l, d = x_prompt.shape[1:]
    y_prompt, y_sample = trunk((x_prompt.reshape(-1, d), x_sample.reshape(-1, d)), l, norm_mix, w_in, na_rpb,
                               hy_short, hy_skip, hy_w1, hy_b1, hy_w2, hy_b2, hy_w3, mla_g_q, mla_g_kv, mla_w_uq,
                               mla_w_ukv, dn_conv, dn_a_log, dn_dt_bias, dn_g_norm, w_branch, w_out, norm_mlp,
                               w_up, w_down, norm_final)
    return (y_prompt.reshape(x_prompt.shape), y_sample.reshape(x_sample.shape))
```
